```python
import math
import jax, jax.numpy as jnp
from jax import lax
import numpy as np

D_MODEL = 1024
BATCH = 16
SEQ = 2048
DEPTH = 2

SSM_WIDTH = 512
SSM_GROUP = 16
SSM_GROUPS = SSM_WIDTH // SSM_GROUP
SSM_STATE = 64
DT_MIN = 0.001
DT_MAX = 0.1
N_HEADS = 8
N_KV_HEADS = 2
HEAD_DIM = 64
Q_PER_KV = N_HEADS // N_KV_HEADS
ATTN_WIDTH = N_HEADS * HEAD_DIM
KV_WIDTH = N_KV_HEADS * HEAD_DIM
WINDOW = 128
BLOCK = 128
SPLITS = (SSM_WIDTH,
          SSM_WIDTH + ATTN_WIDTH,
          SSM_WIDTH + ATTN_WIDTH + KV_WIDTH,
          SSM_WIDTH + ATTN_WIDTH + 2 * KV_WIDTH,
          SSM_WIDTH + ATTN_WIDTH + 2 * KV_WIDTH + D_MODEL)
IN_COLS = SSM_WIDTH + ATTN_WIDTH + 2 * KV_WIDTH + 2 * D_MODEL
N_GROUPS = 4
EXPERTS_PER_GROUP = 4
N_EXPERTS = N_GROUPS * EXPERTS_PER_GROUP
TOP_K = 2
D_EXPERT = 256
EPS = 1e-6
NEG = -1e30

kernel_name = "hybrid_s5_swa_sink_hmoe"


def rmsnorm(x, g):
    xf = x.astype(jnp.float32)
    y = xf * lax.rsqrt(jnp.mean(xf * xf, axis=-1, keepdims=True) + EPS) * g.astype(jnp.float32)
    return y.astype(x.dtype)


def head_rmsnorm(t, g):
    tf = t.astype(jnp.float32)
    return tf * lax.rsqrt(jnp.mean(tf * tf, axis=-1, keepdims=True) + EPS) * g.astype(jnp.float32)


def alibi_slopes():
    return jnp.exp2(-8.0 * jnp.arange(1, N_HEADS + 1, dtype=jnp.float32) / N_HEADS)


def s5_mixer(u, lam_re, lam_im, log_dt, b_re, b_im, c_re, c_im, d_skip):
    f32 = jnp.float32
    bsz, L, _ = u.shape
    uf = u.astype(f32).reshape(bsz, L, SSM_GROUPS, SSM_GROUP)
    lam = lax.complex(lam_re.astype(f32), lam_im.astype(f32))
    dt = jnp.exp(log_dt.astype(f32))[:, None]
    lam_bar = jnp.exp(lam * dt)
    b = lax.complex(b_re.astype(f32), b_im.astype(f32))
    b_bar = ((lam_bar - 1.0) / lam)[..., None] * b
    bu = jnp.einsum('gph,blgh->blgp', b_bar, uf)
    a = jnp.broadcast_to(lam_bar, bu.shape)

    def combine(e1, e2):
        a1, s1 = e1
        a2, s2 = e2
        return a1 * a2, a2 * s1 + s2

    _, states = lax.associative_scan(combine, (a, bu), axis=1)
    c = lax.complex(c_re.astype(f32), c_im.astype(f32))
    y = jnp.einsum('ghp,blgp->blgh', c, states).real
    y = y + d_skip.astype(f32).reshape(SSM_GROUPS, SSM_GROUP) * uf
    return y.reshape(bsz, L, SSM_WIDTH).astype(u.dtype)


def with_prev_block(t):
    pad = ((0, 0), (1, 0)) + ((0, 0),) * (t.ndim - 2)
    prev = jnp.pad(t[:, :-1], pad)
    return jnp.concatenate([prev, t], axis=2)


def sliding_window_attention(q, k, v, q_gain, k_gain, sinks):
    f32 = jnp.float32
    bsz, L, _ = q.shape
    nb = L // BLOCK
    qh = head_rmsnorm(q.reshape(bsz, L, N_HEADS, HEAD_DIM), q_gain)
    kh = head_rmsnorm(k.reshape(bsz, L, N_KV_HEADS, HEAD_DIM), k_gain)
    vh = v.astype(f32).reshape(bsz, L, N_KV_HEADS, HEAD_DIM)
    qb = qh.reshape(bsz, nb, BLOCK, N_KV_HEADS, Q_PER_KV, HEAD_DIM)
    kw = with_prev_block(kh.reshape(bsz, nb, BLOCK, N_KV_HEADS, HEAD_DIM))
    vw = with_prev_block(vh.reshape(bsz, nb, BLOCK, N_KV_HEADS, HEAD_DIM))
    scores = jnp.einsum('bnqkgd,bnskd->bnkgqs', qb, kw) * (HEAD_DIM ** -0.5)
    qpos = jnp.arange(BLOCK)[:, None] + BLOCK
    kpos = jnp.arange(2 * BLOCK)[None, :]
    dist = qpos - kpos
    blk = jnp.arange(nb)[:, None, None]
    valid = (dist >= 0) & (dist < WINDOW)
    valid = valid[None] & ((blk > 0) | (kpos[None] >= BLOCK))
    slopes = alibi_slopes().reshape(N_KV_HEADS, Q_PER_KV)
    scores = scores - slopes[:, :, None, None] * dist.astype(f32)
    scores = jnp.where(valid[None, :, None, None], scores, NEG)
    sink = sinks.astype(f32).reshape(N_KV_HEADS, Q_PER_KV)[None, None, :, :, None, None]
    m = jnp.maximum(jnp.max(scores, axis=-1, keepdims=True), sink)
    p = jnp.exp(scores - m)
    denom = jnp.sum(p, axis=-1, keepdims=True) + jnp.exp(sink - m)
    out = jnp.einsum('bnkgqs,bnskd->bnqkgd', p / denom, vw)
    return out.reshape(bsz, L, ATTN_WIDTH).astype(q.dtype)


def hierarchical_moe(h, w_coarse, w_fine, w_up, w_down):
    f32 = jnp.float32
    bsz, L, D = h.shape
    t = h.reshape(-1, D)
    n_tok = t.shape[0]
    coarse = jax.nn.softmax((t @ w_coarse).astype(f32), axis=-1)
    g_prob, g_idx = lax.top_k(coarse, 1)
    g_idx = g_idx[:, 0]
    fine = (t @ w_fine).astype(f32).reshape(n_tok, N_GROUPS, EXPERTS_PER_GROUP)
    fine = fine[jnp.arange(n_tok), g_idx]
    top_vals, top_idx = lax.top_k(fine, TOP_K)
    top_w = jax.nn.softmax(top_vals, axis=-1) * g_prob
    expert_id = g_idx[:, None] * EXPERTS_PER_GROUP + top_idx
    gates = jnp.sum(jax.nn.one_hot(expert_id, N_EXPERTS, dtype=f32) * top_w[..., None], axis=1)
    gu = jnp.einsum('nd,edf->nef', t, w_up)
    g_part, u_part = jnp.split(gu, 2, axis=-1)
    act = jax.nn.silu(g_part) * u_part * gates[..., None].astype(t.dtype)
    out = jnp.einsum('nef,efd->nd', act, w_down)
    return out.reshape(bsz, L, D)


def setup_inputs(seed: int = 0) -> dict:
    key = jax.random.key(seed)
    ks = jax.random.split(key, 24)
    f32 = jnp.float32

    def normal(k, shape):
        return jax.random.normal(k, shape, f32)

    def dense(k, shape, fan_in):
        return normal(k, shape) * fan_in ** -0.5

    n_idx = jnp.arange(SSM_STATE, dtype=f32)
    gp = (DEPTH, SSM_GROUPS, SSM_STATE)
    return {
        "x": normal(ks[0], (BATCH, SEQ, D_MODEL)),
        "norm_mix": 1.0 + 0.02 * normal(ks[1], (DEPTH, D_MODEL)),
        "w_in": dense(ks[2], (DEPTH, D_MODEL, IN_COLS), D_MODEL),
        "ssm_lam_re": -0.5 + 0.01 * normal(ks[3], gp),
        "ssm_lam_im": math.pi * n_idx + 0.01 * normal(ks[4], gp),
        "ssm_log_dt": jax.random.uniform(ks[5], (DEPTH, SSM_GROUPS), f32, math.log(DT_MIN), math.log(DT_MAX)),
        "ssm_b_re": dense(ks[6], (DEPTH, SSM_GROUPS, SSM_STATE, SSM_GROUP), 2 * SSM_GROUP),
        "ssm_b_im": dense(ks[7], (DEPTH, SSM_GROUPS, SSM_STATE, SSM_GROUP), 2 * SSM_GROUP),
        "ssm_c_re": dense(ks[8], (DEPTH, SSM_GROUPS, SSM_GROUP, SSM_STATE), SSM_STATE),
        "ssm_c_im": dense(ks[9], (DEPTH, SSM_GROUPS, SSM_GROUP, SSM_STATE), SSM_STATE),
        "ssm_d": normal(ks[10], (DEPTH, SSM_WIDTH)),
        "w_ssm_glu_val": dense(ks[11], (DEPTH, SSM_WIDTH, D_MODEL), SSM_WIDTH),
        "w_ssm_glu_gate": dense(ks[12], (DEPTH, SSM_WIDTH, D_MODEL), SSM_WIDTH),
        "q_norm": 1.0 + 0.02 * normal(ks[13], (DEPTH, HEAD_DIM)),
        "k_norm": 1.0 + 0.02 * normal(ks[14], (DEPTH, HEAD_DIM)),
        "attn_sinks": normal(ks[15], (DEPTH, N_HEADS)),
        "w_attn_up": dense(ks[16], (DEPTH, ATTN_WIDTH, D_MODEL), ATTN_WIDTH),
        "w_out": dense(ks[17], (DEPTH, D_MODEL, D_MODEL), D_MODEL),
        "norm_ffn": 1.0 + 0.02 * normal(ks[18], (DEPTH, D_MODEL)),
        "w_coarse": dense(ks[19], (DEPTH, D_MODEL, N_GROUPS), D_MODEL),
        "w_fine": dense(ks[20], (DEPTH, D_MODEL, N_EXPERTS), D_MODEL),
        "w_expert_up": dense(ks[21], (DEPTH, N_EXPERTS, D_MODEL, 2 * D_EXPERT), D_MODEL),
        "w_expert_down": dense(ks[22], (DEPTH, N_EXPERTS, D_EXPERT, D_MODEL), D_EXPERT),
    }


def reference(x, norm_mix, w_in, ssm_lam_re, ssm_lam_im, ssm_log_dt, ssm_b_re, ssm_b_im,
              ssm_c_re, ssm_c_im, ssm_d, w_ssm_glu_val, w_ssm_glu_gate, q_norm, k_norm,
              attn_sinks, w_attn_up, w_out, norm_ffn, w_coarse, w_fine, w_expert_up,
              w_expert_down):
    for i in range(DEPTH):
        h = rmsnorm(x, norm_mix[i])
        proj = h @ w_in[i]
        u, q, k, v, gate_s, gate_a = jnp.split(proj, SPLITS, axis=-1)
        y_s = s5_mixer(u, ssm_lam_re[i], ssm_lam_im[i], ssm_log_dt[i], ssm_b_re[i], ssm_b_im[i],
                       ssm_c_re[i], ssm_c_im[i], ssm_d[i])
        z = jax.nn.gelu(y_s)
        branch_s = (z @ w_ssm_glu_val[i]) * jax.nn.sigmoid(z @ w_ssm_glu_gate[i])
        y_a = sliding_window_attention(q, k, v, q_norm[i], k_norm[i], attn_sinks[i])
        branch_a = y_a @ w_attn_up[i]
        merged = jax.nn.sigmoid(gate_s) * branch_s + jax.nn.sigmoid(gate_a) * branch_a
        x = x + merged @ w_out[i]
        h2 = rmsnorm(x, norm_ffn[i])
        x = x + hierarchical_moe(h2, w_coarse[i], w_fine[i], w_expert_up[i], w_expert_down[i])
    return x
```

```python
import functools
import math

import jax
import jax.numpy as jnp
from jax import lax
from jax.experimental import pallas as pl
from jax.experimental.pallas import tpu as pltpu

F32 = jnp.float32
BF16 = jnp.bfloat16

D_MODEL = 1024
SSM_WIDTH = 512
SSM_GROUP = 16
SSM_GROUPS = 32
SSM_STATE = 64
N_HEADS = 8
N_KV_HEADS = 2
HEAD_DIM = 64
ATTN_WIDTH = 512
KV_WIDTH = 128
BLOCK = 128
N_GROUPS = 4
EXPERTS_PER_GROUP = 4
N_EXPERTS = 16
D_EXPERT = 256
EPS = 1e-6
NEG = -1e30

LANES = 128
SLABS = SSM_WIDTH // LANES
GROUPS_PER_SLAB = LANES // SSM_GROUP
SLAB_STATE = GROUPS_PER_SLAB * SSM_STATE
ROUTER_LANES = 128
FINE_OFF = N_GROUPS

VMEM_LIMIT = 56 * 1024 * 1024


def _cparams(sem):
    return pltpu.CompilerParams(dimension_semantics=sem, vmem_limit_bytes=VMEM_LIMIT)


PROJ_TM = 512


def _proj_body(x_ref, g_ref, w_ref, u_ref, q_ref, k_ref, v_ref, gs_ref, ga_ref):
    x = x_ref[...]
    ms = jnp.mean(x * x, axis=-1, keepdims=True)
    h = (x * lax.rsqrt(ms + EPS) * g_ref[...]).astype(BF16)

    def seg(a, b):
        return jnp.dot(h, w_ref[:, a:b], preferred_element_type=F32)

    o = 0
    u_ref[...] = seg(o, o + SSM_WIDTH)
    o += SSM_WIDTH
    q_ref[...] = seg(o, o + ATTN_WIDTH).astype(BF16)
    o += ATTN_WIDTH
    k_ref[...] = seg(o, o + KV_WIDTH).astype(BF16)
    o += KV_WIDTH
    v_ref[...] = seg(o, o + KV_WIDTH).astype(BF16)
    o += KV_WIDTH
    gs_ref[...] = jax.nn.sigmoid(seg(o, o + D_MODEL)).astype(BF16)
    o += D_MODEL
    ga_ref[...] = jax.nn.sigmoid(seg(o, o + D_MODEL)).astype(BF16)


def _proj(x2, g, w):
    n = x2.shape[0]
    cols = w.shape[1]
    row = lambda c: pl.BlockSpec((PROJ_TM, c), lambda i: (i, 0))
    full = lambda a, b: pl.BlockSpec((a, b), lambda i: (0, 0))
    return pl.pallas_call(
        _proj_body,
        grid=(n // PROJ_TM,),
        in_specs=[row(D_MODEL), full(1, D_MODEL), full(D_MODEL, cols)],
        out_specs=[row(SSM_WIDTH), row(ATTN_WIDTH), row(KV_WIDTH), row(KV_WIDTH),
                   row(D_MODEL), row(D_MODEL)],
        out_shape=[jax.ShapeDtypeStruct((n, SSM_WIDTH), F32),
                   jax.ShapeDtypeStruct((n, ATTN_WIDTH), BF16),
                   jax.ShapeDtypeStruct((n, KV_WIDTH), BF16),
                   jax.ShapeDtypeStruct((n, KV_WIDTH), BF16),
                   jax.ShapeDtypeStruct((n, D_MODEL), BF16),
                   jax.ShapeDtypeStruct((n, D_MODEL), BF16)],
        compiler_params=_cparams(("arbitrary",)),
        name="proj",
    )(x2, g, w)


S5_TT = 32


def _s5_body(u_ref, bblk_ref, cblk_ref, lam_ref, d_ref, z_ref,
             ut_ref, y_ref, st_ref, carry_ref, *, bsz):
    rows = S5_TT * bsz

    @pl.when(pl.program_id(0) == 0)
    def _():
        carry_ref[...] = jnp.zeros_like(carry_ref)

    for b in range(bsz):
        for j in range(SLABS):
            ut_ref[j, pl.ds(b, S5_TT, stride=bsz), :] = u_ref[b, :, j * LANES:(j + 1) * LANES]

    for j in range(SLABS):
        uj = ut_ref[j]
        st_ref[...] = jnp.dot(uj.astype(BF16), bblk_ref[j], preferred_element_type=F32)
        a = jnp.broadcast_to(lam_ref[j, 0:1, :], (bsz, SLAB_STATE))
        bb = jnp.broadcast_to(lam_ref[j, 1:2, :], (bsz, SLAB_STATE))

        def step(t, c):
            cre, cim = c
            r0 = pl.multiple_of(t * bsz, bsz)
            bre = st_ref[pl.ds(r0, bsz), 0:SLAB_STATE]
            bim = st_ref[pl.ds(r0, bsz), SLAB_STATE:2 * SLAB_STATE]
            nre = a * cre - bb * cim + bre
            nim = a * cim + bb * cre + bim
            st_ref[pl.ds(r0, bsz), 0:SLAB_STATE] = nre
            st_ref[pl.ds(r0, bsz), SLAB_STATE:2 * SLAB_STATE] = nim
            return nre, nim

        c0 = (carry_ref[j, :, 0:SLAB_STATE], carry_ref[j, :, SLAB_STATE:2 * SLAB_STATE])
        cre, cim = lax.fori_loop(0, S5_TT, step, c0, unroll=4)
        carry_ref[j, :, 0:SLAB_STATE] = cre
        carry_ref[j, :, SLAB_STATE:2 * SLAB_STATE] = cim

        y = jnp.dot(st_ref[...].astype(BF16), cblk_ref[j], preferred_element_type=F32)
        y = y + d_ref[j] * uj
        y_ref[j] = jax.nn.gelu(y)

    for b in range(bsz):
        for j in range(SLABS):
            z_ref[b, :, j * LANES:(j + 1) * LANES] = (
                y_ref[j, pl.ds(b, S5_TT, stride=bsz), :].astype(BF16))


def _s5(u3, bblk, cblk, lam, dskip):
    bsz, seq, _ = u3.shape
    rows = S5_TT * bsz
    full = lambda *s: pl.BlockSpec(s, lambda i: (0,) * len(s))
    return pl.pallas_call(
        functools.partial(_s5_body, bsz=bsz),
        grid=(seq // S5_TT,),
        in_specs=[pl.BlockSpec((bsz, S5_TT, SSM_WIDTH), lambda i: (0, i, 0)),
                  full(SLABS, LANES, 2 * SLAB_STATE),
                  full(SLABS, 2 * SLAB_STATE, LANES),
                  full(SLABS, 2, SLAB_STATE),
                  full(SLABS, 1, LANES)],
        out_specs=pl.BlockSpec((bsz, S5_TT, SSM_WIDTH), lambda i: (0, i, 0)),
        out_shape=jax.ShapeDtypeStruct((bsz, seq, SSM_WIDTH), BF16),
        scratch_shapes=[pltpu.VMEM((SLABS, rows, LANES), F32),
                        pltpu.VMEM((SLABS, rows, LANES), F32),
                        pltpu.VMEM((rows, 2 * SLAB_STATE), F32),
                        pltpu.VMEM((SLABS, bsz, 2 * SLAB_STATE), F32)],
        compiler_params=_cparams(("arbitrary",)),
        name="s5",
    )(u3, bblk, cblk, lam, dskip)


def _s5_params(lam_re, lam_im, log_dt, b_re, b_im, c_re, c_im, d_skip):
    lam = lax.complex(lam_re.astype(F32), lam_im.astype(F32))
    dt = jnp.exp(log_dt.astype(F32))[:, None]
    lam_bar = jnp.exp(lam * dt)
    b = lax.complex(b_re.astype(F32), b_im.astype(F32))
    b_bar = ((lam_bar - 1.0) / lam)[..., None] * b
    eye = jnp.eye(GROUPS_PER_SLAB, dtype=F32)

    def in_blk(part):
        p4 = part.reshape(SLABS, GROUPS_PER_SLAB, SSM_STATE, SSM_GROUP)
        m = jnp.einsum('jgph,gk->jghkp', p4, eye)
        return m.reshape(SLABS, LANES, SLAB_STATE)

    bblk = jnp.concatenate([in_blk(b_bar.real), in_blk(b_bar.imag)], axis=-1).astype(BF16)

    def out_blk(part):
        p4 = part.reshape(SLABS, GROUPS_PER_SLAB, SSM_GROUP, SSM_STATE)
        m = jnp.einsum('jghp,gk->jgpkh', p4, eye)
        return m.reshape(SLABS, SLAB_STATE, LANES)

    cblk = jnp.concatenate([out_blk(c_re.astype(F32)), -out_blk(c_im.astype(F32))],
                           axis=1).astype(BF16)
    lam_k = jnp.stack([lam_bar.real.reshape(SLABS, SLAB_STATE),
                       lam_bar.imag.reshape(SLABS, SLAB_STATE)], axis=1)
    dsk = d_skip.astype(F32).reshape(SLABS, 1, LANES)
    return bblk, cblk, lam_k, dsk


def _pair_norm(t, gain2):
    lane = lax.broadcasted_iota(jnp.int32, (1, LANES), 1)
    sq = t * t
    s0 = jnp.sum(sq[:, :HEAD_DIM], axis=-1, keepdims=True)
    s1 = jnp.sum(sq[:, HEAD_DIM:], axis=-1, keepdims=True)
    ms = jnp.where(lane < HEAD_DIM, s0, s1) * (1.0 / HEAD_DIM)
    return t * lax.rsqrt(ms + EPS) * gain2


def _attn_body(sink_ref, q_ref, kc_ref, kp_ref, vc_ref, vp_ref, qg_ref, kg_ref, o_ref):
    n = pl.program_id(1)
    lane = lax.broadcasted_iota(jnp.int32, (1, LANES), 1)
    lo = lane < HEAD_DIM

    kcat = jnp.concatenate([kc_ref[0], kp_ref[0]], axis=0).astype(F32)
    kn = _pair_norm(kcat, kg_ref[...])
    kroll = pltpu.roll(kn, HEAD_DIM, axis=1)
    vcat = jnp.concatenate([vc_ref[0], vp_ref[0]], axis=0).astype(F32)
    vroll = pltpu.roll(vcat, HEAD_DIM, axis=1)
    zero = jnp.zeros_like(kn)
    k_var = [[jnp.where(lo, kn, zero).astype(BF16), jnp.where(lo, zero, kroll).astype(BF16)],
             [jnp.where(lo, kroll, zero).astype(BF16), jnp.where(lo, zero, kn).astype(BF16)]]
    v_var = [[jnp.where(lo, vcat, zero).astype(BF16), jnp.where(lo, zero, vroll).astype(BF16)],
             [jnp.where(lo, vroll, zero).astype(BF16), jnp.where(lo, zero, vcat).astype(BF16)]]

    qi = lax.broadcasted_iota(jnp.int32, (BLOCK, BLOCK), 0)
    si = lax.broadcasted_iota(jnp.int32, (BLOCK, BLOCK), 1)
    is_cur = si <= qi
    dist = jnp.where(is_cur, qi - si, qi - si + BLOCK).astype(F32)
    has_prev = n > 0

    for i in range(ATTN_WIDTH // LANES):
        kvh = (2 * i) // (N_HEADS // N_KV_HEADS)
        qn = _pair_norm(q_ref[0, :, i * LANES:(i + 1) * LANES].astype(F32), qg_ref[...])
        qn = (qn * (HEAD_DIM ** -0.5)).astype(BF16)
        acc = jnp.zeros((BLOCK, LANES), F32)
        for slot in range(2):
            h = 2 * i + slot
            slope = 2.0 ** (-8.0 * (h + 1) / N_HEADS)
            sc = lax.dot_general(qn, k_var[kvh][slot], (((1,), (1,)), ((), ())),
                                 preferred_element_type=F32)
            cur = sc[:, :BLOCK] - slope * dist
            prev = jnp.where(has_prev, sc[:, BLOCK:] - slope * dist, NEG)
            s = jnp.where(is_cur, cur, prev)
            sink = sink_ref[h]
            m = jnp.maximum(jnp.max(s, axis=-1, keepdims=True), sink)
            p = jnp.exp(s - m)
            denom = jnp.sum(p, axis=-1, keepdims=True) + jnp.exp(sink - m)
            pn = p / denom
            pcat = jnp.concatenate([jnp.where(is_cur, pn, 0.0), jnp.where(is_cur, 0.0, pn)],
                                   axis=1).astype(BF16)
            acc = acc + jnp.dot(pcat, v_var[kvh][slot], preferred_element_type=F32)
        o_ref[0, :, i * LANES:(i + 1) * LANES] = acc.astype(BF16)


def _attn(q3, k3, v3, q_gain, k_gain, sinks):
    bsz, seq, _ = q3.shape
    nb = seq // BLOCK
    cur = lambda b, n: (b, n, 0)
    prev = lambda b, n: (b, jnp.maximum(n - 1, 0), 0)
    qg2 = jnp.tile(q_gain.astype(F32), 2).reshape(1, LANES)
    kg2 = jnp.tile(k_gain.astype(F32), 2).reshape(1, LANES)
    gain_spec = pl.BlockSpec((1, LANES), lambda b, n: (0, 0))
    return pl.pallas_call(
        _attn_body,
        grid=(bsz, nb),
        in_specs=[pl.BlockSpec(memory_space=pltpu.SMEM),
                  pl.BlockSpec((1, BLOCK, ATTN_WIDTH), cur),
                  pl.BlockSpec((1, BLOCK, KV_WIDTH), cur),
                  pl.BlockSpec((1, BLOCK, KV_WIDTH), prev),
                  pl.BlockSpec((1, BLOCK, KV_WIDTH), cur),
                  pl.BlockSpec((1, BLOCK, KV_WIDTH), prev),
                  gain_spec, gain_spec],
        out_specs=pl.BlockSpec((1, BLOCK, ATTN_WIDTH), cur),
        out_shape=jax.ShapeDtypeStruct((bsz, seq, ATTN_WIDTH), BF16),
        compiler_params=_cparams(("arbitrary", "arbitrary")),
        name="attn",
    )(sinks.astype(F32), q3, k3, k3, v3, v3, qg2, kg2)


MIX_TM = 512


def _mix_body(z_ref, ya_ref, gs_ref, ga_ref, x_ref, wglu_ref, wup_ref, wout_ref, g2_ref, wr_ref,
              x1_ref, h2_ref, gate_ref):
    z = z_ref[...]
    bv = jnp.dot(z, wglu_ref[:, :D_MODEL], preferred_element_type=F32)
    bg = jnp.dot(z, wglu_ref[:, D_MODEL:], preferred_element_type=F32)
    bs = bv * jax.nn.sigmoid(bg)
    ba = jnp.dot(ya_ref[...], wup_ref[...], preferred_element_type=F32)
    merged = gs_ref[...].astype(F32) * bs + ga_ref[...].astype(F32) * ba
    x1 = x_ref[...] + jnp.dot(merged.astype(BF16), wout_ref[...], preferred_element_type=F32)
    x1_ref[...] = x1

    ms = jnp.mean(x1 * x1, axis=-1, keepdims=True)
    h2 = x1 * lax.rsqrt(ms + EPS) * g2_ref[...]
    h2_ref[...] = h2.astype(BF16)

    logits = jnp.dot(h2, wr_ref[...], preferred_element_type=F32,
                     precision=lax.Precision.HIGHEST)
    lane = lax.broadcasted_iota(jnp.int32, logits.shape, 1)
    ninf = -jnp.inf
    cm = jnp.where(lane < N_GROUPS, logits, ninf)
    cmax = jnp.max(cm, axis=-1, keepdims=True)
    g_prob = 1.0 / jnp.sum(jnp.exp(cm - cmax), axis=-1, keepdims=True)
    g_idx = jnp.min(jnp.where(cm == cmax, lane, ROUTER_LANES), axis=-1, keepdims=True)
    f0 = FINE_OFF + EXPERTS_PER_GROUP * g_idx
    fm = jnp.where((lane >= f0) & (lane < f0 + EXPERTS_PER_GROUP), logits, ninf)
    v1 = jnp.max(fm, axis=-1, keepdims=True)
    i1 = jnp.min(jnp.where(fm == v1, lane, ROUTER_LANES), axis=-1, keepdims=True)
    fm2 = jnp.where(lane == i1, ninf, fm)
    v2 = jnp.max(fm2, axis=-1, keepdims=True)
    i2 = jnp.min(jnp.where(fm2 == v2, lane, ROUTER_LANES), axis=-1, keepdims=True)
    e21 = jnp.exp(v2 - v1)
    w1 = g_prob / (1.0 + e21)
    w2 = w1 * e21
    gate_ref[...] = jnp.where(lane == i1, w1, jnp.where(lane == i2, w2, 0.0))


def _mix(z2, ya2, gs2, ga2, x2, wglu, wup, wout, g2, wr):
    n = x2.shape[0]
    row = lambda c: pl.BlockSpec((MIX_TM, c), lambda i: (i, 0))
    full = lambda a, b: pl.BlockSpec((a, b), lambda i: (0, 0))
    return pl.pallas_call(
        _mix_body,
        grid=(n // MIX_TM,),
        in_specs=[row(SSM_WIDTH), row(ATTN_WIDTH), row(D_MODEL), row(D_MODEL), row(D_MODEL),
                  full(SSM_WIDTH, 2 * D_MODEL), full(ATTN_WIDTH, D_MODEL), full(D_MODEL, D_MODEL),
                  full(1, D_MODEL), full(D_MODEL, ROUTER_LANES)],
        out_specs=[row(D_MODEL), row(D_MODEL), row(ROUTER_LANES)],
        out_shape=[jax.ShapeDtypeStruct((n, D_MODEL), F32),
                   jax.ShapeDtypeStruct((n, D_MODEL), BF16),
                   jax.ShapeDtypeStruct((n, ROUTER_LANES), F32)],
        compiler_params=_cparams(("arbitrary",)),
        name="mix",
    )(z2, ya2, gs2, ga2, x2, wglu, wup, wout, g2, wr)


MOE_TM = 1024


def _moe_body(h_ref, gate_ref, x1_ref, wu_ref, wd_ref, o_ref):
    e = pl.program_id(1)

    @pl.when(e == 0)
    def _():
        o_ref[...] = x1_ref[...]

    gu = jnp.dot(h_ref[...], wu_ref[0], preferred_element_type=F32)
    lane = lax.broadcasted_iota(jnp.int32, gate_ref.shape, 1)
    gate = jnp.sum(jnp.where(lane == FINE_OFF + e, gate_ref[...], 0.0), axis=-1, keepdims=True)
    act = jax.nn.silu(gu[:, :D_EXPERT]) * gu[:, D_EXPERT:] * gate
    o_ref[...] += jnp.dot(act.astype(BF16), wd_ref[0], preferred_element_type=F32)


def _moe(h2, gates, x1, wu, wd):
    n = h2.shape[0]
    row = lambda c: pl.BlockSpec((MOE_TM, c), lambda i, e: (i, 0))
    return pl.pallas_call(
        _moe_body,
        grid=(n // MOE_TM, N_EXPERTS),
        in_specs=[row(D_MODEL), row(ROUTER_LANES), row(D_MODEL),
                  pl.BlockSpec((1, D_MODEL, 2 * D_EXPERT), lambda i, e: (e, 0, 0)),
                  pl.BlockSpec((1, D_EXPERT, D_MODEL), lambda i, e: (e, 0, 0))],
        out_specs=row(D_MODEL),
        out_shape=jax.ShapeDtypeStruct((n, D_MODEL), F32),
        compiler_params=_cparams(("arbitrary", "arbitrary")),
        name="moe",
    )(h2, gates, x1, wu, wd)


def kernel(x, norm_mix, w_in, ssm_lam_re, ssm_lam_im, ssm_log_dt, ssm_b_re, ssm_b_im, ssm_c_re,
           ssm_c_im, ssm_d, w_ssm_glu_val, w_ssm_glu_gate, q_norm, k_norm, attn_sinks, w_attn_up,
           w_out, norm_ffn, w_coarse, w_fine, w_expert_up, w_expert_down):
    bsz, seq, d = x.shape
    n = bsz * seq
    depth = w_in.shape[0]
    x2 = x.reshape(n, d)
    for i in range(depth):
        u, q, k, v, gs, ga = _proj(x2, norm_mix[i].reshape(1, d), w_in[i].astype(BF16))
        bblk, cblk, lam_k, dsk = _s5_params(ssm_lam_re[i], ssm_lam_im[i], ssm_log_dt[i],
                                            ssm_b_re[i], ssm_b_im[i], ssm_c_re[i], ssm_c_im[i],
                                            ssm_d[i])
        z = _s5(u.reshape(bsz, seq, SSM_WIDTH), bblk, cblk, lam_k, dsk)
        ya = _attn(q.reshape(bsz, seq, ATTN_WIDTH), k.reshape(bsz, seq, KV_WIDTH),
                   v.reshape(bsz, seq, KV_WIDTH), q_norm[i], k_norm[i], attn_sinks[i])
        wglu = jnp.concatenate([w_ssm_glu_val[i], w_ssm_glu_gate[i]], axis=1).astype(BF16)
        wr = jnp.concatenate(
            [w_coarse[i], w_fine[i],
             jnp.zeros((d, ROUTER_LANES - N_GROUPS - N_EXPERTS), w_fine.dtype)], axis=1).astype(F32)
        x1, h2, gates = _mix(z.reshape(n, SSM_WIDTH), ya.reshape(n, ATTN_WIDTH), gs, ga, x2,
                             wglu, w_attn_up[i].astype(BF16), w_out[i].astype(BF16),
                             norm_ffn[i].reshape(1, d), wr)
        x2 = _moe(h2, gates, x1, w_expert_up[i].astype(BF16), w_expert_down[i].astype(BF16))
    return x2.reshape(bsz, seq, d)
```

```python
import functools
import math

import jax
import jax.numpy as jnp
from jax import lax
from jax.experimental import pallas as pl
from jax.experimental.pallas import tpu as pltpu

F32 = jnp.float32
BF16 = jnp.bfloat16

D_MODEL = 1024
SSM_WIDTH = 512
SSM_GROUP = 16
SSM_GROUPS = 32
SSM_STATE = 64
N_HEADS = 8
N_KV_HEADS = 2
HEAD_DIM = 64
ATTN_WIDTH = 512
KV_WIDTH = 128
BLOCK = 128
N_GROUPS = 4
EXPERTS_PER_GROUP = 4
N_EXPERTS = 16
D_EXPERT = 256
EPS = 1e-6
NEG = -1e30

LANES = 128
SLABS = SSM_WIDTH // LANES
GROUPS_PER_SLAB = LANES // SSM_GROUP
SLAB_STATE = GROUPS_PER_SLAB * SSM_STATE
ROUTER_LANES = 128
FINE_OFF = N_GROUPS
ROUTE_GROUP_OFF = 8

VMEM_LIMIT = 56 * 1024 * 1024


def _cparams(sem):
    return pltpu.CompilerParams(dimension_semantics=sem, vmem_limit_bytes=VMEM_LIMIT)


PROJ_TM = 512


def _proj_body(x_ref, g_ref, w_ref, u_ref, q_ref, k_ref, v_ref, gs_ref, ga_ref):
    x = x_ref[...]
    ms = jnp.mean(x * x, axis=-1, keepdims=True)
    h = (x * lax.rsqrt(ms + EPS) * g_ref[...]).astype(BF16)

    def seg(a, b):
        return jnp.dot(h, w_ref[:, a:b], preferred_element_type=F32)

    o = 0
    u_ref[...] = seg(o, o + SSM_WIDTH)
    o += SSM_WIDTH
    q_ref[...] = seg(o, o + ATTN_WIDTH).astype(BF16)
    o += ATTN_WIDTH
    k_ref[...] = seg(o, o + KV_WIDTH).astype(BF16)
    o += KV_WIDTH
    v_ref[...] = seg(o, o + KV_WIDTH).astype(BF16)
    o += KV_WIDTH
    gs_ref[...] = jax.nn.sigmoid(seg(o, o + D_MODEL)).astype(BF16)
    o += D_MODEL
    ga_ref[...] = jax.nn.sigmoid(seg(o, o + D_MODEL)).astype(BF16)


def _proj(x2, g, w):
    n = x2.shape[0]
    cols = w.shape[1]
    row = lambda c: pl.BlockSpec((PROJ_TM, c), lambda i: (i, 0))
    full = lambda a, b: pl.BlockSpec((a, b), lambda i: (0, 0))
    return pl.pallas_call(
        _proj_body,
        grid=(n // PROJ_TM,),
        in_specs=[row(D_MODEL), full(1, D_MODEL), full(D_MODEL, cols)],
        out_specs=[row(SSM_WIDTH), row(ATTN_WIDTH), row(KV_WIDTH), row(KV_WIDTH),
                   row(D_MODEL), row(D_MODEL)],
        out_shape=[jax.ShapeDtypeStruct((n, SSM_WIDTH), F32),
                   jax.ShapeDtypeStruct((n, ATTN_WIDTH), BF16),
                   jax.ShapeDtypeStruct((n, KV_WIDTH), BF16),
                   jax.ShapeDtypeStruct((n, KV_WIDTH), BF16),
                   jax.ShapeDtypeStruct((n, D_MODEL), BF16),
                   jax.ShapeDtypeStruct((n, D_MODEL), BF16)],
        compiler_params=_cparams(("arbitrary",)),
        name="proj",
    )(x2, g, w)


S5_TT = 32


def _s5_body(u_ref, bblk_ref, cblk_ref, lam_ref, d_ref, z_ref,
             ut_ref, y_ref, st_ref, carry_ref, *, bsz):
    rows = S5_TT * bsz

    @pl.when(pl.program_id(0) == 0)
    def _():
        carry_ref[...] = jnp.zeros_like(carry_ref)

    for b in range(bsz):
        for j in range(SLABS):
            ut_ref[j, pl.ds(b, S5_TT, stride=bsz), :] = u_ref[b, :, j * LANES:(j + 1) * LANES]

    for j in range(SLABS):
        uj = ut_ref[j]
        st_ref[...] = jnp.dot(uj.astype(BF16), bblk_ref[j], preferred_element_type=F32)
        a = jnp.broadcast_to(lam_ref[j, 0:1, :], (bsz, SLAB_STATE))
        bb = jnp.broadcast_to(lam_ref[j, 1:2, :], (bsz, SLAB_STATE))

        def step(t, c):
            cre, cim = c
            r0 = pl.multiple_of(t * bsz, bsz)
            bre = st_ref[pl.ds(r0, bsz), 0:SLAB_STATE]
            bim = st_ref[pl.ds(r0, bsz), SLAB_STATE:2 * SLAB_STATE]
            nre = a * cre - bb * cim + bre
            nim = a * cim + bb * cre + bim
            st_ref[pl.ds(r0, bsz), 0:SLAB_STATE] = nre
            st_ref[pl.ds(r0, bsz), SLAB_STATE:2 * SLAB_STATE] = nim
            return nre, nim

        c0 = (carry_ref[j, :, 0:SLAB_STATE], carry_ref[j, :, SLAB_STATE:2 * SLAB_STATE])
        cre, cim = lax.fori_loop(0, S5_TT, step, c0, unroll=4)
        carry_ref[j, :, 0:SLAB_STATE] = cre
        carry_ref[j, :, SLAB_STATE:2 * SLAB_STATE] = cim

        y = jnp.dot(st_ref[...].astype(BF16), cblk_ref[j], preferred_element_type=F32)
        y = y + d_ref[j] * uj
        y_ref[j] = jax.nn.gelu(y)

    for b in range(bsz):
        for j in range(SLABS):
            z_ref[b, :, j * LANES:(j + 1) * LANES] = (
                y_ref[j, pl.ds(b, S5_TT, stride=bsz), :].astype(BF16))


def _s5(u3, bblk, cblk, lam, dskip):
    bsz, seq, _ = u3.shape
    rows = S5_TT * bsz
    full = lambda *s: pl.BlockSpec(s, lambda i: (0,) * len(s))
    return pl.pallas_call(
        functools.partial(_s5_body, bsz=bsz),
        grid=(seq // S5_TT,),
        in_specs=[pl.BlockSpec((bsz, S5_TT, SSM_WIDTH), lambda i: (0, i, 0)),
                  full(SLABS, LANES, 2 * SLAB_STATE),
                  full(SLABS, 2 * SLAB_STATE, LANES),
                  full(SLABS, 2, SLAB_STATE),
                  full(SLABS, 1, LANES)],
        out_specs=pl.BlockSpec((bsz, S5_TT, SSM_WIDTH), lambda i: (0, i, 0)),
        out_shape=jax.ShapeDtypeStruct((bsz, seq, SSM_WIDTH), BF16),
        scratch_shapes=[pltpu.VMEM((SLABS, rows, LANES), F32),
                        pltpu.VMEM((SLABS, rows, LANES), F32),
                        pltpu.VMEM((rows, 2 * SLAB_STATE), F32),
                        pltpu.VMEM((SLABS, bsz, 2 * SLAB_STATE), F32)],
        compiler_params=_cparams(("arbitrary",)),
        name="s5",
    )(u3, bblk, cblk, lam, dskip)


def _s5_params(lam_re, lam_im, log_dt, b_re, b_im, c_re, c_im, d_skip):
    lam = lax.complex(lam_re.astype(F32), lam_im.astype(F32))
    dt = jnp.exp(log_dt.astype(F32))[:, None]
    lam_bar = jnp.exp(lam * dt)
    b = lax.complex(b_re.astype(F32), b_im.astype(F32))
    b_bar = ((lam_bar - 1.0) / lam)[..., None] * b
    eye = jnp.eye(GROUPS_PER_SLAB, dtype=F32)

    def in_blk(part):
        p4 = part.reshape(SLABS, GROUPS_PER_SLAB, SSM_STATE, SSM_GROUP)
        m = jnp.einsum('jgph,gk->jghkp', p4, eye)
        return m.reshape(SLABS, LANES, SLAB_STATE)

    bblk = jnp.concatenate([in_blk(b_bar.real), in_blk(b_bar.imag)], axis=-1).astype(BF16)

    def out_blk(part):
        p4 = part.reshape(SLABS, GROUPS_PER_SLAB, SSM_GROUP, SSM_STATE)
        m = jnp.einsum('jghp,gk->jgpkh', p4, eye)
        return m.reshape(SLABS, SLAB_STATE, LANES)

    cblk = jnp.concatenate([out_blk(c_re.astype(F32)), -out_blk(c_im.astype(F32))],
                           axis=1).astype(BF16)
    lam_k = jnp.stack([lam_bar.real.reshape(SLABS, SLAB_STATE),
                       lam_bar.imag.reshape(SLABS, SLAB_STATE)], axis=1)
    dsk = d_skip.astype(F32).reshape(SLABS, 1, LANES)
    return bblk, cblk, lam_k, dsk


def _pair_norm(t, gain2):
    lane = lax.broadcasted_iota(jnp.int32, (1, LANES), 1)
    sq = t * t
    s0 = jnp.sum(sq[:, :HEAD_DIM], axis=-1, keepdims=True)
    s1 = jnp.sum(sq[:, HEAD_DIM:], axis=-1, keepdims=True)
    ms = jnp.where(lane < HEAD_DIM, s0, s1) * (1.0 / HEAD_DIM)
    return t * lax.rsqrt(ms + EPS) * gain2


def _attn_body(sink_ref, q_ref, kc_ref, kp_ref, vc_ref, vp_ref, qg_ref, kg_ref, o_ref):
    n = pl.program_id(1)
    lane = lax.broadcasted_iota(jnp.int32, (1, LANES), 1)
    lo = lane < HEAD_DIM

    kcat = jnp.concatenate([kc_ref[0], kp_ref[0]], axis=0).astype(F32)
    kn = _pair_norm(kcat, kg_ref[...])
    kroll = pltpu.roll(kn, HEAD_DIM, axis=1)
    vcat = jnp.concatenate([vc_ref[0], vp_ref[0]], axis=0).astype(F32)
    vroll = pltpu.roll(vcat, HEAD_DIM, axis=1)
    zero = jnp.zeros_like(kn)
    k_var = [[jnp.where(lo, kn, zero).astype(BF16), jnp.where(lo, zero, kroll).astype(BF16)],
             [jnp.where(lo, kroll, zero).astype(BF16), jnp.where(lo, zero, kn).astype(BF16)]]
    v_var = [[jnp.where(lo, vcat, zero).astype(BF16), jnp.where(lo, zero, vroll).astype(BF16)],
             [jnp.where(lo, vroll, zero).astype(BF16), jnp.where(lo, zero, vcat).astype(BF16)]]

    qi = lax.broadcasted_iota(jnp.int32, (BLOCK, BLOCK), 0)
    si = lax.broadcasted_iota(jnp.int32, (BLOCK, BLOCK), 1)
    is_cur = si <= qi
    dist = jnp.where(is_cur, qi - si, qi - si + BLOCK).astype(F32)
    has_prev = n > 0

    for i in range(ATTN_WIDTH // LANES):
        kvh = (2 * i) // (N_HEADS // N_KV_HEADS)
        qn = _pair_norm(q_ref[0, :, i * LANES:(i + 1) * LANES].astype(F32), qg_ref[...])
        qn = (qn * (HEAD_DIM ** -0.5)).astype(BF16)
        acc = jnp.zeros((BLOCK, LANES), F32)
        for slot in range(2):
            h = 2 * i + slot
            slope = 2.0 ** (-8.0 * (h + 1) / N_HEADS)
            sc = lax.dot_general(qn, k_var[kvh][slot], (((1,), (1,)), ((), ())),
                                 preferred_element_type=F32)
            cur = sc[:, :BLOCK] - slope * dist
            prev = jnp.where(has_prev, sc[:, BLOCK:] - slope * dist, NEG)
            s = jnp.where(is_cur, cur, prev)
            sink = sink_ref[h]
            m = jnp.maximum(jnp.max(s, axis=-1, keepdims=True), sink)
            p = jnp.exp(s - m)
            denom = jnp.sum(p, axis=-1, keepdims=True) + jnp.exp(sink - m)
            pn = p / denom
            pcat = jnp.concatenate([jnp.where(is_cur, pn, 0.0), jnp.where(is_cur, 0.0, pn)],
                                   axis=1).astype(BF16)
            acc = acc + jnp.dot(pcat, v_var[kvh][slot], preferred_element_type=F32)
        o_ref[0, :, i * LANES:(i + 1) * LANES] = acc.astype(BF16)


def _attn(q3, k3, v3, q_gain, k_gain, sinks):
    bsz, seq, _ = q3.shape
    nb = seq // BLOCK
    cur = lambda b, n: (b, n, 0)
    prev = lambda b, n: (b, jnp.maximum(n - 1, 0), 0)
    qg2 = jnp.tile(q_gain.astype(F32), 2).reshape(1, LANES)
    kg2 = jnp.tile(k_gain.astype(F32), 2).reshape(1, LANES)
    gain_spec = pl.BlockSpec((1, LANES), lambda b, n: (0, 0))
    return pl.pallas_call(
        _attn_body,
        grid=(bsz, nb),
        in_specs=[pl.BlockSpec(memory_space=pltpu.SMEM),
                  pl.BlockSpec((1, BLOCK, ATTN_WIDTH), cur),
                  pl.BlockSpec((1, BLOCK, KV_WIDTH), cur),
                  pl.BlockSpec((1, BLOCK, KV_WIDTH), prev),
                  pl.BlockSpec((1, BLOCK, KV_WIDTH), cur),
                  pl.BlockSpec((1, BLOCK, KV_WIDTH), prev),
                  gain_spec, gain_spec],
        out_specs=pl.BlockSpec((1, BLOCK, ATTN_WIDTH), cur),
        out_shape=jax.ShapeDtypeStruct((bsz, seq, ATTN_WIDTH), BF16),
        compiler_params=_cparams(("arbitrary", "arbitrary")),
        name="attn",
    )(sinks.astype(F32), q3, k3, k3, v3, v3, qg2, kg2)


MIX_TM = 512


def _mix_body(z_ref, ya_ref, gs_ref, ga_ref, x_ref, wglu_ref, wup_ref, wout_ref, g2_ref, wr_ref,
              x1_ref, h2_ref, gate_ref):
    z = z_ref[...]
    bv = jnp.dot(z, wglu_ref[:, :D_MODEL], preferred_element_type=F32)
    bg = jnp.dot(z, wglu_ref[:, D_MODEL:], preferred_element_type=F32)
    bs = bv * jax.nn.sigmoid(bg)
    ba = jnp.dot(ya_ref[...], wup_ref[...], preferred_element_type=F32)
    merged = gs_ref[...].astype(F32) * bs + ga_ref[...].astype(F32) * ba
    x1 = x_ref[...] + jnp.dot(merged.astype(BF16), wout_ref[...], preferred_element_type=F32)
    x1_ref[...] = x1

    ms = jnp.mean(x1 * x1, axis=-1, keepdims=True)
    h2 = x1 * lax.rsqrt(ms + EPS) * g2_ref[...]
    h2_ref[...] = h2.astype(BF16)

    logits = jnp.dot(h2, wr_ref[...], preferred_element_type=F32,
                     precision=lax.Precision.HIGHEST)
    lane = lax.broadcasted_iota(jnp.int32, logits.shape, 1)
    ninf = -jnp.inf
    cm = jnp.where(lane < N_GROUPS, logits, ninf)
    cmax = jnp.max(cm, axis=-1, keepdims=True)
    g_prob = 1.0 / jnp.sum(jnp.exp(cm - cmax), axis=-1, keepdims=True)
    g_idx = jnp.min(jnp.where(cm == cmax, lane, ROUTER_LANES), axis=-1, keepdims=True)
    f0 = FINE_OFF + EXPERTS_PER_GROUP * g_idx
    fm = jnp.where((lane >= f0) & (lane < f0 + EXPERTS_PER_GROUP), logits, ninf)
    v1 = jnp.max(fm, axis=-1, keepdims=True)
    i1 = jnp.min(jnp.where(fm == v1, lane, ROUTER_LANES), axis=-1, keepdims=True)
    fm2 = jnp.where(lane == i1, ninf, fm)
    v2 = jnp.max(fm2, axis=-1, keepdims=True)
    i2 = jnp.min(jnp.where(fm2 == v2, lane, ROUTER_LANES), axis=-1, keepdims=True)
    e21 = jnp.exp(v2 - v1)
    w1 = g_prob / (1.0 + e21)
    w2 = w1 * e21
    gate_ref[...] = jnp.where(lane == i1 - f0, w1,
                              jnp.where(lane == i2 - f0, w2,
                                        jnp.where(lane == ROUTE_GROUP_OFF + g_idx, 1.0, 0.0)))


def _mix(z2, ya2, gs2, ga2, x2, wglu, wup, wout, g2, wr):
    n = x2.shape[0]
    row = lambda c: pl.BlockSpec((MIX_TM, c), lambda i: (i, 0))
    full = lambda a, b: pl.BlockSpec((a, b), lambda i: (0, 0))
    return pl.pallas_call(
        _mix_body,
        grid=(n // MIX_TM,),
        in_specs=[row(SSM_WIDTH), row(ATTN_WIDTH), row(D_MODEL), row(D_MODEL), row(D_MODEL),
                  full(SSM_WIDTH, 2 * D_MODEL), full(ATTN_WIDTH, D_MODEL), full(D_MODEL, D_MODEL),
                  full(1, D_MODEL), full(D_MODEL, ROUTER_LANES)],
        out_specs=[row(D_MODEL), row(D_MODEL), row(ROUTER_LANES)],
        out_shape=[jax.ShapeDtypeStruct((n, D_MODEL), F32),
                   jax.ShapeDtypeStruct((n, D_MODEL), BF16),
                   jax.ShapeDtypeStruct((n, ROUTER_LANES), F32)],
        compiler_params=_cparams(("arbitrary",)),
        name="mix",
    )(z2, ya2, gs2, ga2, x2, wglu, wup, wout, g2, wr)


SORT_TM = 256
MOE_TM = 512
TOK_ROWS = 8
H_WORDS = D_MODEL // 2
H_CHUNKS = H_WORDS // LANES
ROUTE_ROW = H_CHUNKS
ZFILL_TOK = MOE_TM + SORT_TM
HI_MASK = 0xFFFF0000


def _region_cap(n):
    cap = n + ZFILL_TOK
    return -(-cap // MOE_TM) * MOE_TM


def _tile_positions(route):
    tm = route.shape[0]
    lane = lax.broadcasted_iota(jnp.int32, route.shape, 1)
    onehot = jnp.where((lane >= ROUTE_GROUP_OFF) & (lane < ROUTE_GROUP_OFF + N_GROUPS), route, 0.0)
    ci = lax.broadcasted_iota(jnp.int32, (tm, tm), 0)
    cj = lax.broadcasted_iota(jnp.int32, (tm, tm), 1)
    earlier = (cj < ci).astype(BF16)
    rank = jnp.dot(earlier, onehot.astype(BF16), preferred_element_type=F32)
    cnt = jnp.sum(onehot, axis=0, keepdims=True)
    li = lax.broadcasted_iota(jnp.int32, (LANES, LANES), 0)
    lj = lax.broadcasted_iota(jnp.int32, (LANES, LANES), 1)
    base = jnp.dot(jnp.broadcast_to(cnt, (8, LANES)).astype(BF16), (li < lj).astype(BF16),
                   preferred_element_type=F32)[0:1]
    return onehot, rank, base, cnt


def _sort_body(h_ref, route_ref, hs_hbm, starts_ref, counts_ref, stage_ref, run_ref, sem):
    t = pl.program_id(0)
    nt = pl.num_programs(0)
    tm = SORT_TM
    slot_rows = 2 * tm * TOK_ROWS
    cap = hs_hbm.shape[0] // (N_GROUPS * TOK_ROWS)

    @pl.when(t == 0)
    def _():
        stage_ref[...] = jnp.zeros_like(stage_ref)
        for g in range(N_GROUPS):
            run_ref[g] = 0

    route = route_ref[...]
    onehot, rank, base, cnt = _tile_positions(route)
    w = (onehot * (rank + base)).astype(BF16)
    pos_row = lax.dot_general(jnp.ones((8, LANES), BF16), w, (((1,), (1,)), ((), ())),
                              preferred_element_type=F32)[0:1]
    ri = lax.broadcasted_iota(jnp.int32, (tm, tm), 0).astype(F32)
    perm = ri == pos_row
    sh = jnp.dot(perm.astype(BF16), h_ref[...], preferred_element_type=F32)
    sr = jnp.dot(perm.astype(F32), route, preferred_element_type=F32,
                 precision=lax.Precision.HIGHEST)
    hb = lax.bitcast_convert_type(sh, jnp.uint32)
    words = (hb[:, :H_WORDS] & jnp.uint32(HI_MASK)) | (hb[:, H_WORDS:] >> 16)

    slot0 = pl.multiple_of((t % 2) * slot_rows, TOK_ROWS)
    for s in range(H_CHUNKS):
        stage_ref[pl.ds(slot0 + s, tm, stride=TOK_ROWS), :] = words[:, s * LANES:(s + 1) * LANES]
    stage_ref[pl.ds(slot0 + ROUTE_ROW, tm, stride=TOK_ROWS), :] = lax.bitcast_convert_type(
        sr, jnp.uint32)

    def run_copy(src_row, dst_row, g):
        return pltpu.make_async_copy(stage_ref.at[pl.ds(src_row, tm * TOK_ROWS), :],
                                     hs_hbm.at[pl.ds(dst_row, tm * TOK_ROWS), :], sem.at[g])

    @pl.when(t > 0)
    def _():
        for g in range(N_GROUPS):
            run_copy(0, 0, g).wait()

    for g in range(N_GROUPS):
        c_g = cnt[0, ROUTE_GROUP_OFF + g].astype(jnp.int32)
        b_g = base[0, ROUTE_GROUP_OFF + g].astype(jnp.int32)
        start = run_ref[g]
        starts_ref[t * N_GROUPS + g] = start
        run_ref[g] = start + c_g
        src = pl.multiple_of(slot0 + b_g * TOK_ROWS, TOK_ROWS)
        dst = pl.multiple_of((g * cap + start) * TOK_ROWS, TOK_ROWS)
        run_copy(src, dst, g).start()

    @pl.when(t == nt - 1)
    def _():
        for g in range(N_GROUPS):
            run_copy(0, 0, g).wait()
        stage_ref[...] = jnp.zeros_like(stage_ref)
        zrows = ZFILL_TOK * TOK_ROWS
        for g in range(N_GROUPS):
            total = run_ref[g]
            counts_ref[g] = total
            dst = pl.multiple_of((g * cap + total) * TOK_ROWS, TOK_ROWS)
            pltpu.make_async_copy(stage_ref.at[pl.ds(0, zrows), :],
                                  hs_hbm.at[pl.ds(dst, zrows), :], sem.at[g]).start()
        for g in range(N_GROUPS):
            pltpu.make_async_copy(stage_ref.at[pl.ds(0, zrows), :],
                                  hs_hbm.at[pl.ds(0, zrows), :], sem.at[g]).wait()


def _sort(h2, route):
    n = h2.shape[0]
    nt = n // SORT_TM
    cap = _region_cap(n)
    assert 4 * SORT_TM >= ZFILL_TOK
    smem = pl.BlockSpec(memory_space=pltpu.SMEM)
    return pl.pallas_call(
        _sort_body,
        grid=(nt,),
        in_specs=[pl.BlockSpec((SORT_TM, D_MODEL), lambda i: (i, 0)),
                  pl.BlockSpec((SORT_TM, ROUTER_LANES), lambda i: (i, 0))],
        out_specs=[pl.BlockSpec(memory_space=pl.ANY), smem, smem],
        out_shape=[jax.ShapeDtypeStruct((N_GROUPS * cap * TOK_ROWS, LANES), jnp.uint32),
                   jax.ShapeDtypeStruct((nt * N_GROUPS,), jnp.int32),
                   jax.ShapeDtypeStruct((N_GROUPS,), jnp.int32)],
        scratch_shapes=[pltpu.VMEM((4 * SORT_TM * TOK_ROWS, LANES), jnp.uint32),
                        pltpu.SMEM((N_GROUPS,), jnp.int32),
                        pltpu.SemaphoreType.DMA((N_GROUPS,))],
        compiler_params=_cparams(("arbitrary",)),
        name="moe_sort",
    )(h2, route)


def _experts_body(blk_ref, grp_ref, valid_ref, hs_ref, wu_ref, wd_ref, ys_ref):
    t = pl.program_id(0)
    tm = MOE_TM

    @pl.when(valid_ref[t] > 0)
    def _():
        chunks = [hs_ref[pl.ds(s, tm, stride=TOK_ROWS), :] for s in range(H_CHUNKS + 1)]
        hi = [lax.bitcast_convert_type(c & jnp.uint32(HI_MASK), F32) for c in chunks[:H_CHUNKS]]
        lo = [lax.bitcast_convert_type(c << 16, F32) for c in chunks[:H_CHUNKS]]
        h = jnp.concatenate(hi + lo, axis=1).astype(BF16)
        route = lax.bitcast_convert_type(chunks[ROUTE_ROW], F32)
        gu = jnp.dot(h, wu_ref[0], preferred_element_type=F32)
        half = EXPERTS_PER_GROUP * D_EXPERT
        gate = jnp.concatenate(
            [jnp.broadcast_to(route[:, e:e + 1], (tm, D_EXPERT)) for e in range(EXPERTS_PER_GROUP)],
            axis=1)
        act = jax.nn.silu(gu[:, :half]) * gu[:, half:] * gate
        out = jnp.dot(act.astype(BF16), wd_ref[0], preferred_element_type=F32)
        for j in range(D_MODEL // LANES):
            ys_ref[pl.ds(j, tm, stride=TOK_ROWS), :] = out[:, j * LANES:(j + 1) * LANES]


def _experts(blk, grp, valid, hs, wu_g, wd_g):
    n_tiles = blk.shape[0]
    half = EXPERTS_PER_GROUP * D_EXPERT
    return pl.pallas_call(
        _experts_body,
        grid_spec=pltpu.PrefetchScalarGridSpec(
            num_scalar_prefetch=3,
            grid=(n_tiles,),
            in_specs=[pl.BlockSpec((MOE_TM * TOK_ROWS, LANES), lambda t, b, g, v: (b[t], 0)),
                      pl.BlockSpec((1, D_MODEL, 2 * half), lambda t, b, g, v: (g[t], 0, 0)),
                      pl.BlockSpec((1, half, D_MODEL), lambda t, b, g, v: (g[t], 0, 0))],
            out_specs=pl.BlockSpec((MOE_TM * TOK_ROWS, LANES), lambda t, b, g, v: (b[t], 0)),
        ),
        out_shape=jax.ShapeDtypeStruct(hs.shape, F32),
        compiler_params=_cparams(("arbitrary",)),
        name="moe_experts",
    )(blk, grp, valid, hs, wu_g, wd_g)


def _combine_body(starts_ref, x1_ref, route_ref, ys_hbm, o_ref, buf_ref, sem):
    t = pl.program_id(0)
    nt = pl.num_programs(0)
    tm = SORT_TM
    blk_rows = tm * TOK_ROWS
    slot_rows = N_GROUPS * blk_rows
    cap = ys_hbm.shape[0] // (N_GROUPS * TOK_ROWS)

    def fetch(step, slot, g):
        src = pl.multiple_of((g * cap + starts_ref[step * N_GROUPS + g]) * TOK_ROWS, TOK_ROWS)
        dst = pl.multiple_of(slot * slot_rows + g * blk_rows, TOK_ROWS)
        return pltpu.make_async_copy(ys_hbm.at[pl.ds(src, blk_rows), :],
                                     buf_ref.at[pl.ds(dst, blk_rows), :], sem.at[slot, g])

    @pl.when(t == 0)
    def _():
        for g in range(N_GROUPS):
            fetch(0, 0, g).start()

    @pl.when(t + 1 < nt)
    def _():
        for g in range(N_GROUPS):
            fetch(t + 1, (t + 1) % 2, g).start()

    onehot, rank, _, _ = _tile_positions(route_ref[...])
    own_rank = jnp.sum(onehot * rank, axis=1, keepdims=True)
    lane = lax.broadcasted_iota(jnp.int32, onehot.shape, 1)
    ri = lax.broadcasted_iota(jnp.int32, (tm, tm), 1).astype(F32)
    slot = t % 2
    acc = x1_ref[...]
    for g in range(N_GROUPS):
        fetch(t, slot, g).wait()
        in_g = jnp.sum(jnp.where(lane == ROUTE_GROUP_OFF + g, onehot, 0.0), axis=1, keepdims=True)
        sel = ((ri == own_rank) & (in_g > 0.5)).astype(BF16)
        r0 = pl.multiple_of(slot * slot_rows + g * blk_rows, TOK_ROWS)
        y = jnp.concatenate(
            [buf_ref[pl.ds(r0 + j, tm, stride=TOK_ROWS), :] for j in range(D_MODEL // LANES)],
            axis=1).astype(BF16)
        acc = acc + jnp.dot(sel, y, preferred_element_type=F32)
    o_ref[...] = acc


def _combine(starts, x1, route, ys):
    n = x1.shape[0]
    nt = n // SORT_TM
    return pl.pallas_call(
        _combine_body,
        grid_spec=pltpu.PrefetchScalarGridSpec(
            num_scalar_prefetch=1,
            grid=(nt,),
            in_specs=[pl.BlockSpec((SORT_TM, D_MODEL), lambda i, s: (i, 0)),
                      pl.BlockSpec((SORT_TM, ROUTER_LANES), lambda i, s: (i, 0)),
                      pl.BlockSpec(memory_space=pl.ANY)],
            out_specs=pl.BlockSpec((SORT_TM, D_MODEL), lambda i, s: (i, 0)),
            scratch_shapes=[pltpu.VMEM((2 * N_GROUPS * SORT_TM * TOK_ROWS, LANES), F32),
                            pltpu.SemaphoreType.DMA((2, N_GROUPS))],
        ),
        out_shape=jax.ShapeDtypeStruct((n, D_MODEL), F32),
        compiler_params=_cparams(("arbitrary",)),
        name="moe_combine",
    )(starts, x1, route, ys)


def _expert_tiles(counts, n):
    cap_blocks = _region_cap(n) // MOE_TM
    n_tiles = (n + N_GROUPS * SORT_TM) // MOE_TM + N_GROUPS
    per_group = (counts + SORT_TM + MOE_TM - 1) // MOE_TM
    ends = jnp.cumsum(per_group)
    total = ends[-1]
    t = jnp.minimum(jnp.arange(n_tiles, dtype=jnp.int32), total - 1)
    grp = jnp.sum((t[:, None] >= ends[None, :]).astype(jnp.int32), axis=1)
    first = ends - per_group
    blk = grp * cap_blocks + (t - first[grp])
    valid = (jnp.arange(n_tiles, dtype=jnp.int32) < total).astype(jnp.int32)
    return blk.astype(jnp.int32), grp.astype(jnp.int32), valid


def _group_weights(w_up, w_down):
    d = w_up.shape[1]
    wu = w_up.reshape(N_GROUPS, EXPERTS_PER_GROUP, d, 2, D_EXPERT)
    wu = wu.transpose(0, 2, 3, 1, 4).reshape(N_GROUPS, d, 2 * EXPERTS_PER_GROUP * D_EXPERT)
    wd = w_down.reshape(N_GROUPS, EXPERTS_PER_GROUP * D_EXPERT, d)
    return wu.astype(BF16), wd.astype(BF16)


def _moe(h2, route, x1, w_up, w_down):
    n = h2.shape[0]
    hs, starts, counts = _sort(h2, route)
    blk, grp, valid = _expert_tiles(counts, n)
    wu_g, wd_g = _group_weights(w_up, w_down)
    ys = _experts(blk, grp, valid, hs, wu_g, wd_g)
    return _combine(starts, x1, route, ys)


def kernel(x, norm_mix, w_in, ssm_lam_re, ssm_lam_im, ssm_log_dt, ssm_b_re, ssm_b_im, ssm_c_re,
           ssm_c_im, ssm_d, w_ssm_glu_val, w_ssm_glu_gate, q_norm, k_norm, attn_sinks, w_attn_up,
           w_out, norm_ffn, w_coarse, w_fine, w_expert_up, w_expert_down):
    bsz, seq, d = x.shape
    n = bsz * seq
    depth = w_in.shape[0]
    x2 = x.reshape(n, d)
    for i in range(depth):
        u, q, k, v, gs, ga = _proj(x2, norm_mix[i].reshape(1, d), w_in[i].astype(BF16))
        bblk, cblk, lam_k, dsk = _s5_params(ssm_lam_re[i], ssm_lam_im[i], ssm_log_dt[i],
                                            ssm_b_re[i], ssm_b_im[i], ssm_c_re[i], ssm_c_im[i],
                                            ssm_d[i])
        z = _s5(u.reshape(bsz, seq, SSM_WIDTH), bblk, cblk, lam_k, dsk)
        ya = _attn(q.reshape(bsz, seq, ATTN_WIDTH), k.reshape(bsz, seq, KV_WIDTH),
                   v.reshape(bsz, seq, KV_WIDTH), q_norm[i], k_norm[i], attn_sinks[i])
        wglu = jnp.concatenate([w_ssm_glu_val[i], w_ssm_glu_gate[i]], axis=1).astype(BF16)
        wr = jnp.concatenate(
            [w_coarse[i], w_fine[i],
             jnp.zeros((d, ROUTER_LANES - N_GROUPS - N_EXPERTS), w_fine.dtype)], axis=1).astype(F32)
        x1, h2, gates = _mix(z.reshape(n, SSM_WIDTH), ya.reshape(n, ATTN_WIDTH), gs, ga, x2,
                             wglu, w_attn_up[i].astype(BF16), w_out[i].astype(BF16),
                             norm_ffn[i].reshape(1, d), wr)
        x2 = _moe(h2, gates, x1, w_expert_up[i], w_expert_down[i])
    return x2.reshape(bsz, seq, d)
```

```python
import functools
import math

import jax
import jax.numpy as jnp
from jax import lax
from jax.experimental import pallas as pl
from jax.experimental.pallas import tpu as pltpu

F32 = jnp.float32
BF16 = jnp.bfloat16

D_MODEL = 1024
SSM_WIDTH = 512
SSM_GROUP = 16
SSM_GROUPS = 32
SSM_STATE = 64
N_HEADS = 8
N_KV_HEADS = 2
HEAD_DIM = 64
ATTN_WIDTH = 512
KV_WIDTH = 128
BLOCK = 128
N_GROUPS = 4
EXPERTS_PER_GROUP = 4
N_EXPERTS = 16
D_EXPERT = 256
EPS = 1e-6
NEG = -1e30

LANES = 128
SLABS = SSM_WIDTH // LANES
GROUPS_PER_SLAB = LANES // SSM_GROUP
SLAB_STATE = GROUPS_PER_SLAB * SSM_STATE
ROUTER_LANES = 128
ROUTER_ROWS = 32
FINE_OFF = N_GROUPS
ROUTE_GROUP_OFF = 8

VMEM_LIMIT = 56 * 1024 * 1024


def _cparams(sem):
    return pltpu.CompilerParams(dimension_semantics=sem, vmem_limit_bytes=VMEM_LIMIT)


PROJ_TM = 512


def _proj_body(x_ref, g_ref, w_ref, u_ref, q_ref, k_ref, v_ref, gs_ref, ga_ref):
    x = x_ref[...]
    ms = jnp.mean(x * x, axis=-1, keepdims=True)
    h = (x * lax.rsqrt(ms + EPS) * g_ref[...]).astype(BF16)

    def seg(a, b):
        return jnp.dot(h, w_ref[:, a:b], preferred_element_type=F32)

    o = 0
    u_ref[...] = seg(o, o + SSM_WIDTH)
    o += SSM_WIDTH
    q_ref[...] = seg(o, o + ATTN_WIDTH).astype(BF16)
    o += ATTN_WIDTH
    k_ref[...] = seg(o, o + KV_WIDTH).astype(BF16)
    o += KV_WIDTH
    v_ref[...] = seg(o, o + KV_WIDTH).astype(BF16)
    o += KV_WIDTH
    gs_ref[...] = jax.nn.sigmoid(seg(o, o + D_MODEL)).astype(BF16)
    o += D_MODEL
    ga_ref[...] = jax.nn.sigmoid(seg(o, o + D_MODEL)).astype(BF16)


def _proj(x2, g, w):
    n = x2.shape[0]
    cols = w.shape[1]
    row = lambda c: pl.BlockSpec((PROJ_TM, c), lambda i: (i, 0))
    full = lambda a, b: pl.BlockSpec((a, b), lambda i: (0, 0))
    return pl.pallas_call(
        _proj_body,
        grid=(n // PROJ_TM,),
        in_specs=[row(D_MODEL), full(1, D_MODEL), full(D_MODEL, cols)],
        out_specs=[row(SSM_WIDTH), row(ATTN_WIDTH), row(KV_WIDTH), row(KV_WIDTH),
                   row(D_MODEL), row(D_MODEL)],
        out_shape=[jax.ShapeDtypeStruct((n, SSM_WIDTH), F32),
                   jax.ShapeDtypeStruct((n, ATTN_WIDTH), BF16),
                   jax.ShapeDtypeStruct((n, KV_WIDTH), BF16),
                   jax.ShapeDtypeStruct((n, KV_WIDTH), BF16),
                   jax.ShapeDtypeStruct((n, D_MODEL), BF16),
                   jax.ShapeDtypeStruct((n, D_MODEL), BF16)],
        compiler_params=_cparams(("arbitrary",)),
        name="proj",
    )(x2, g, w)


S5_TT = 32


def _s5_body(u_ref, bblk_ref, cblk_ref, lam_ref, d_ref, z_ref,
             ut_ref, y_ref, st_ref, carry_ref, *, bsz):
    rows = S5_TT * bsz

    @pl.when(pl.program_id(0) == 0)
    def _():
        carry_ref[...] = jnp.zeros_like(carry_ref)

    for b in range(bsz):
        for j in range(SLABS):
            ut_ref[j, pl.ds(b, S5_TT, stride=bsz), :] = u_ref[b, :, j * LANES:(j + 1) * LANES]

    for j in range(SLABS):
        uj = ut_ref[j]
        st_ref[...] = jnp.dot(uj.astype(BF16), bblk_ref[j], preferred_element_type=F32)
        a = jnp.broadcast_to(lam_ref[j, 0:1, :], (bsz, SLAB_STATE))
        bb = jnp.broadcast_to(lam_ref[j, 1:2, :], (bsz, SLAB_STATE))

        def step(t, c):
            cre, cim = c
            r0 = pl.multiple_of(t * bsz, bsz)
            bre = st_ref[pl.ds(r0, bsz), 0:SLAB_STATE]
            bim = st_ref[pl.ds(r0, bsz), SLAB_STATE:2 * SLAB_STATE]
            nre = a * cre - bb * cim + bre
            nim = a * cim + bb * cre + bim
            st_ref[pl.ds(r0, bsz), 0:SLAB_STATE] = nre
            st_ref[pl.ds(r0, bsz), SLAB_STATE:2 * SLAB_STATE] = nim
            return nre, nim

        c0 = (carry_ref[j, :, 0:SLAB_STATE], carry_ref[j, :, SLAB_STATE:2 * SLAB_STATE])
        cre, cim = lax.fori_loop(0, S5_TT, step, c0, unroll=4)
        carry_ref[j, :, 0:SLAB_STATE] = cre
        carry_ref[j, :, SLAB_STATE:2 * SLAB_STATE] = cim

        y = jnp.dot(st_ref[...].astype(BF16), cblk_ref[j], preferred_element_type=F32)
        y = y + d_ref[j] * uj
        y_ref[j] = jax.nn.gelu(y)

    for b in range(bsz):
        for j in range(SLABS):
            z_ref[b, :, j * LANES:(j + 1) * LANES] = (
                y_ref[j, pl.ds(b, S5_TT, stride=bsz), :].astype(BF16))


def _s5(u3, bblk, cblk, lam, dskip):
    bsz, seq, _ = u3.shape
    rows = S5_TT * bsz
    full = lambda *s: pl.BlockSpec(s, lambda i: (0,) * len(s))
    return pl.pallas_call(
        functools.partial(_s5_body, bsz=bsz),
        grid=(seq // S5_TT,),
        in_specs=[pl.BlockSpec((bsz, S5_TT, SSM_WIDTH), lambda i: (0, i, 0)),
                  full(SLABS, LANES, 2 * SLAB_STATE),
                  full(SLABS, 2 * SLAB_STATE, LANES),
                  full(SLABS, 2, SLAB_STATE),
                  full(SLABS, 1, LANES)],
        out_specs=pl.BlockSpec((bsz, S5_TT, SSM_WIDTH), lambda i: (0, i, 0)),
        out_shape=jax.ShapeDtypeStruct((bsz, seq, SSM_WIDTH), BF16),
        scratch_shapes=[pltpu.VMEM((SLABS, rows, LANES), F32),
                        pltpu.VMEM((SLABS, rows, LANES), F32),
                        pltpu.VMEM((rows, 2 * SLAB_STATE), F32),
                        pltpu.VMEM((SLABS, bsz, 2 * SLAB_STATE), F32)],
        compiler_params=_cparams(("arbitrary",)),
        name="s5",
    )(u3, bblk, cblk, lam, dskip)


def _s5_params(lam_re, lam_im, log_dt, b_re, b_im, c_re, c_im, d_skip):
    lam = lax.complex(lam_re.astype(F32), lam_im.astype(F32))
    dt = jnp.exp(log_dt.astype(F32))[:, None]
    lam_bar = jnp.exp(lam * dt)
    b = lax.complex(b_re.astype(F32), b_im.astype(F32))
    b_bar = ((lam_bar - 1.0) / lam)[..., None] * b
    eye = jnp.eye(GROUPS_PER_SLAB, dtype=F32)

    def in_blk(part):
        p4 = part.reshape(SLABS, GROUPS_PER_SLAB, SSM_STATE, SSM_GROUP)
        m = jnp.einsum('jgph,gk->jghkp', p4, eye)
        return m.reshape(SLABS, LANES, SLAB_STATE)

    bblk = jnp.concatenate([in_blk(b_bar.real), in_blk(b_bar.imag)], axis=-1).astype(BF16)

    def out_blk(part):
        p4 = part.reshape(SLABS, GROUPS_PER_SLAB, SSM_GROUP, SSM_STATE)
        m = jnp.einsum('jghp,gk->jgpkh', p4, eye)
        return m.reshape(SLABS, SLAB_STATE, LANES)

    cblk = jnp.concatenate([out_blk(c_re.astype(F32)), -out_blk(c_im.astype(F32))],
                           axis=1).astype(BF16)
    lam_k = jnp.stack([lam_bar.real.reshape(SLABS, SLAB_STATE),
                       lam_bar.imag.reshape(SLABS, SLAB_STATE)], axis=1)
    dsk = d_skip.astype(F32).reshape(SLABS, 1, LANES)
    return bblk, cblk, lam_k, dsk


ATTN_QB = 4
HEADS_PER_TILE = LANES // HEAD_DIM


def _pair_norm(t, gain2, head_mean):
    sq = t * t
    hi = sq.astype(BF16)
    lo = (sq - hi.astype(F32)).astype(BF16)
    ms = (jnp.dot(hi, head_mean, preferred_element_type=F32)
          + jnp.dot(lo, head_mean, preferred_element_type=F32))
    return t * lax.rsqrt(ms + EPS) * gain2


def _attn_body(sink_ref, q_ref, kc_ref, kp_ref, vc_ref, vp_ref, qg_ref, kg_ref, o_ref):
    n = pl.program_id(1)
    lane = lax.broadcasted_iota(jnp.int32, (1, LANES), 1)
    lo = lane < HEAD_DIM
    li = lax.broadcasted_iota(jnp.int32, (LANES, LANES), 0)
    lj = lax.broadcasted_iota(jnp.int32, (LANES, LANES), 1)
    head_mean = jnp.where((li < HEAD_DIM) == (lj < HEAD_DIM), 1.0 / HEAD_DIM, 0.0).astype(BF16)
    ones = jnp.ones((LANES, LANES), BF16)

    kall = jnp.concatenate([kp_ref[0], kc_ref[0]], axis=0).astype(F32)
    kn = _pair_norm(kall, kg_ref[...], head_mean)
    kroll = pltpu.roll(kn, HEAD_DIM, axis=1)
    vall = jnp.concatenate([vp_ref[0], vc_ref[0]], axis=0).astype(F32)
    vroll = pltpu.roll(vall, HEAD_DIM, axis=1)
    zero = jnp.zeros_like(kn)
    k_var = [[jnp.where(lo, kn, zero).astype(BF16), jnp.where(lo, zero, kroll).astype(BF16)],
             [jnp.where(lo, kroll, zero).astype(BF16), jnp.where(lo, zero, kn).astype(BF16)]]
    v_var = [[jnp.where(lo, vall, zero).astype(BF16), jnp.where(lo, zero, vroll).astype(BF16)],
             [jnp.where(lo, vroll, zero).astype(BF16), jnp.where(lo, zero, vall).astype(BF16)]]

    qn = []
    for i in range(ATTN_WIDTH // LANES):
        t = _pair_norm(q_ref[0, :, i * LANES:(i + 1) * LANES].astype(F32), qg_ref[...], head_mean)
        qn.append((t * (HEAD_DIM ** -0.5)).astype(BF16))

    qi = lax.broadcasted_iota(jnp.int32, (BLOCK, BLOCK), 0)
    si = lax.broadcasted_iota(jnp.int32, (BLOCK, BLOCK), 1)
    is_cur = si <= qi
    dist = jnp.where(is_cur, qi - si, qi - si + BLOCK).astype(F32)
    alibi = [(2.0 ** (-8.0 * (h + 1) / N_HEADS)) * dist for h in range(N_HEADS)]
    sinkmat = jnp.concatenate([jnp.full((BLOCK, LANES), sink_ref[h], F32) for h in range(N_HEADS)],
                              axis=0)

    for b in range(ATTN_QB):
        rows = slice(b * BLOCK, (b + 1) * BLOCK)
        win = slice(b * BLOCK, (b + 2) * BLOCK)
        s_all = []
        for h in range(N_HEADS):
            i, slot = divmod(h, HEADS_PER_TILE)
            kvh = h // (N_HEADS // N_KV_HEADS)
            sc = lax.dot_general(qn[i][rows], k_var[kvh][slot][win], (((1,), (1,)), ((), ())),
                                 preferred_element_type=F32)
            prev = sc[:, :BLOCK]
            if b == 0:
                prev = jnp.where(n > 0, prev, NEG)
            s_all.append(jnp.where(is_cur, sc[:, BLOCK:], prev) - alibi[h])
        s = jnp.concatenate(s_all, axis=0)
        m = jnp.maximum(jnp.max(s, axis=-1, keepdims=True), sinkmat)
        p = jnp.exp(s - m)
        denom = jnp.dot(p.astype(BF16), ones, preferred_element_type=F32) + jnp.exp(sinkmat - m)
        pn = p / denom
        for i in range(ATTN_WIDTH // LANES):
            acc = jnp.zeros((BLOCK, LANES), F32)
            for slot in range(HEADS_PER_TILE):
                h = i * HEADS_PER_TILE + slot
                kvh = h // (N_HEADS // N_KV_HEADS)
                ph = pn[h * BLOCK:(h + 1) * BLOCK]
                pcat = jnp.concatenate([jnp.where(is_cur, 0.0, ph), jnp.where(is_cur, ph, 0.0)],
                                       axis=1).astype(BF16)
                acc = acc + jnp.dot(pcat, v_var[kvh][slot][win], preferred_element_type=F32)
            o_ref[0, rows, i * LANES:(i + 1) * LANES] = acc.astype(BF16)


def _attn(q3, k3, v3, q_gain, k_gain, sinks):
    bsz, seq, _ = q3.shape
    qrows = ATTN_QB * BLOCK
    cur = lambda b, n: (b, n, 0)
    prev = lambda b, n: (b, jnp.maximum(n * ATTN_QB - 1, 0), 0)
    qg2 = jnp.tile(q_gain.astype(F32), HEADS_PER_TILE).reshape(1, LANES)
    kg2 = jnp.tile(k_gain.astype(F32), HEADS_PER_TILE).reshape(1, LANES)
    gain_spec = pl.BlockSpec((1, LANES), lambda b, n: (0, 0))
    return pl.pallas_call(
        _attn_body,
        grid=(bsz, seq // qrows),
        in_specs=[pl.BlockSpec(memory_space=pltpu.SMEM),
                  pl.BlockSpec((1, qrows, ATTN_WIDTH), cur),
                  pl.BlockSpec((1, qrows, KV_WIDTH), cur),
                  pl.BlockSpec((1, BLOCK, KV_WIDTH), prev),
                  pl.BlockSpec((1, qrows, KV_WIDTH), cur),
                  pl.BlockSpec((1, BLOCK, KV_WIDTH), prev),
                  gain_spec, gain_spec],
        out_specs=pl.BlockSpec((1, qrows, ATTN_WIDTH), cur),
        out_shape=jax.ShapeDtypeStruct((bsz, seq, ATTN_WIDTH), BF16),
        compiler_params=_cparams(("arbitrary", "arbitrary")),
        name="attn",
    )(sinks.astype(F32), q3, k3, k3, v3, v3, qg2, kg2)


MIX_TM = 512


def _mix_body(z_ref, ya_ref, gs_ref, ga_ref, x_ref, wglu_ref, wup_ref, wout_ref, g2_ref, wr_ref,
              x1_ref, h2_ref, gate_ref):
    z = z_ref[...]
    bv = jnp.dot(z, wglu_ref[:, :D_MODEL], preferred_element_type=F32)
    bg = jnp.dot(z, wglu_ref[:, D_MODEL:], preferred_element_type=F32)
    bs = bv * jax.nn.sigmoid(bg)
    ba = jnp.dot(ya_ref[...], wup_ref[...], preferred_element_type=F32)
    merged = gs_ref[...].astype(F32) * bs + ga_ref[...].astype(F32) * ba
    x1 = x_ref[...] + jnp.dot(merged.astype(BF16), wout_ref[...], preferred_element_type=F32)
    x1_ref[...] = x1

    ms = jnp.mean(x1 * x1, axis=-1, keepdims=True)
    h2 = x1 * lax.rsqrt(ms + EPS) * g2_ref[...]
    h2b = h2.astype(BF16)
    h2_ref[...] = h2b

    lt = lax.dot_general(wr_ref[...], h2b, (((1,), (1,)), ((), ())),
                         preferred_element_type=F32)
    logits = lt[:ROUTER_ROWS] + lt[ROUTER_ROWS:]
    row = lax.broadcasted_iota(jnp.int32, logits.shape, 0)
    ninf = -jnp.inf
    cm = jnp.where(row < N_GROUPS, logits, ninf)
    cmax = jnp.max(cm, axis=0, keepdims=True)
    g_prob = 1.0 / jnp.sum(jnp.exp(cm - cmax), axis=0, keepdims=True)
    g_idx = jnp.min(jnp.where(cm == cmax, row, ROUTER_ROWS), axis=0, keepdims=True)
    f0 = FINE_OFF + EXPERTS_PER_GROUP * g_idx
    fm = jnp.where((row >= f0) & (row < f0 + EXPERTS_PER_GROUP), logits, ninf)
    v1 = jnp.max(fm, axis=0, keepdims=True)
    i1 = jnp.min(jnp.where(fm == v1, row, ROUTER_ROWS), axis=0, keepdims=True)
    fm2 = jnp.where(row == i1, ninf, fm)
    v2 = jnp.max(fm2, axis=0, keepdims=True)
    i2 = jnp.min(jnp.where(fm2 == v2, row, ROUTER_ROWS), axis=0, keepdims=True)
    e21 = jnp.exp(v2 - v1)
    w1 = g_prob / (1.0 + e21)
    w2 = w1 * e21
    rt = jnp.where(row == i1 - f0, w1,
                   jnp.where(row == i2 - f0, w2,
                             jnp.where(row == ROUTE_GROUP_OFF + g_idx, 1.0, 0.0)))
    rt = jnp.concatenate([rt, jnp.zeros((ROUTER_LANES - ROUTER_ROWS, rt.shape[1]), F32)], axis=0)
    gate_ref[...] = rt.T


def _mix(z2, ya2, gs2, ga2, x2, wglu, wup, wout, g2, wr):
    n = x2.shape[0]
    row = lambda c: pl.BlockSpec((MIX_TM, c), lambda i: (i, 0))
    full = lambda a, b: pl.BlockSpec((a, b), lambda i: (0, 0))
    return pl.pallas_call(
        _mix_body,
        grid=(n // MIX_TM,),
        in_specs=[row(SSM_WIDTH), row(ATTN_WIDTH), row(D_MODEL), row(D_MODEL), row(D_MODEL),
                  full(SSM_WIDTH, 2 * D_MODEL), full(ATTN_WIDTH, D_MODEL), full(D_MODEL, D_MODEL),
                  full(1, D_MODEL), full(2 * ROUTER_ROWS, D_MODEL)],
        out_specs=[row(D_MODEL), row(D_MODEL), row(ROUTER_LANES)],
        out_shape=[jax.ShapeDtypeStruct((n, D_MODEL), F32),
                   jax.ShapeDtypeStruct((n, D_MODEL), BF16),
                   jax.ShapeDtypeStruct((n, ROUTER_LANES), F32)],
        compiler_params=_cparams(("arbitrary",)),
        name="mix",
    )(z2, ya2, gs2, ga2, x2, wglu, wup, wout, g2, wr)


SORT_TM = 256
MOE_TM = 512
TOK_ROWS = 8
H_WORDS = D_MODEL // 2
H_CHUNKS = H_WORDS // LANES
ROUTE_ROW = H_CHUNKS
ZFILL_TOK = MOE_TM + SORT_TM
HI_MASK = 0xFFFF0000


def _region_cap(n):
    cap = n + ZFILL_TOK
    return -(-cap // MOE_TM) * MOE_TM


def _tile_positions(route):
    tm = route.shape[0]
    lane = lax.broadcasted_iota(jnp.int32, route.shape, 1)
    onehot = jnp.where((lane >= ROUTE_GROUP_OFF) & (lane < ROUTE_GROUP_OFF + N_GROUPS), route, 0.0)
    ci = lax.broadcasted_iota(jnp.int32, (tm, tm), 0)
    cj = lax.broadcasted_iota(jnp.int32, (tm, tm), 1)
    earlier = (cj < ci).astype(BF16)
    rank = jnp.dot(earlier, onehot.astype(BF16), preferred_element_type=F32)
    cnt = jnp.sum(onehot, axis=0, keepdims=True)
    li = lax.broadcasted_iota(jnp.int32, (LANES, LANES), 0)
    lj = lax.broadcasted_iota(jnp.int32, (LANES, LANES), 1)
    base = jnp.dot(jnp.broadcast_to(cnt, (8, LANES)).astype(BF16), (li < lj).astype(BF16),
                   preferred_element_type=F32)[0:1]
    return onehot, rank, base, cnt


def _sort_body(h_ref, route_ref, hs_hbm, starts_ref, counts_ref, stage_ref, run_ref, sem):
    t = pl.program_id(0)
    nt = pl.num_programs(0)
    tm = SORT_TM
    slot_rows = 2 * tm * TOK_ROWS
    cap = hs_hbm.shape[0] // (N_GROUPS * TOK_ROWS)

    @pl.when(t == 0)
    def _():
        stage_ref[...] = jnp.zeros_like(stage_ref)
        for g in range(N_GROUPS):
            run_ref[g] = 0

    route = route_ref[...]
    onehot, rank, base, cnt = _tile_positions(route)
    w = (onehot * (rank + base)).astype(BF16)
    pos_row = lax.dot_general(jnp.ones((8, LANES), BF16), w, (((1,), (1,)), ((), ())),
                              preferred_element_type=F32)[0:1]
    ri = lax.broadcasted_iota(jnp.int32, (tm, tm), 0).astype(F32)
    perm = ri == pos_row
    sh = jnp.dot(perm.astype(BF16), h_ref[...], preferred_element_type=F32)
    sr = jnp.dot(perm.astype(F32), route, preferred_element_type=F32,
                 precision=lax.Precision.HIGHEST)
    hb = lax.bitcast_convert_type(sh, jnp.uint32)
    words = (hb[:, :H_WORDS] & jnp.uint32(HI_MASK)) | (hb[:, H_WORDS:] >> 16)

    slot0 = pl.multiple_of((t % 2) * slot_rows, TOK_ROWS)
    for s in range(H_CHUNKS):
        stage_ref[pl.ds(slot0 + s, tm, stride=TOK_ROWS), :] = words[:, s * LANES:(s + 1) * LANES]
    stage_ref[pl.ds(slot0 + ROUTE_ROW, tm, stride=TOK_ROWS), :] = lax.bitcast_convert_type(
        sr, jnp.uint32)

    def run_copy(src_row, dst_row, g):
        return pltpu.make_async_copy(stage_ref.at[pl.ds(src_row, tm * TOK_ROWS), :],
                                     hs_hbm.at[pl.ds(dst_row, tm * TOK_ROWS), :], sem.at[g])

    @pl.when(t > 0)
    def _():
        for g in range(N_GROUPS):
            run_copy(0, 0, g).wait()

    for g in range(N_GROUPS):
        c_g = cnt[0, ROUTE_GROUP_OFF + g].astype(jnp.int32)
        b_g = base[0, ROUTE_GROUP_OFF + g].astype(jnp.int32)
        start = run_ref[g]
        starts_ref[t * N_GROUPS + g] = start
        run_ref[g] = start + c_g
        src = pl.multiple_of(slot0 + b_g * TOK_ROWS, TOK_ROWS)
        dst = pl.multiple_of((g * cap + start) * TOK_ROWS, TOK_ROWS)
        run_copy(src, dst, g).start()

    @pl.when(t == nt - 1)
    def _():
        for g in range(N_GROUPS):
            run_copy(0, 0, g).wait()
        stage_ref[...] = jnp.zeros_like(stage_ref)
        zrows = ZFILL_TOK * TOK_ROWS
        for g in range(N_GROUPS):
            total = run_ref[g]
            counts_ref[g] = total
            dst = pl.multiple_of((g * cap + total) * TOK_ROWS, TOK_ROWS)
            pltpu.make_async_copy(stage_ref.at[pl.ds(0, zrows), :],
                                  hs_hbm.at[pl.ds(dst, zrows), :], sem.at[g]).start()
        for g in range(N_GROUPS):
            pltpu.make_async_copy(stage_ref.at[pl.ds(0, zrows), :],
                                  hs_hbm.at[pl.ds(0, zrows), :], sem.at[g]).wait()


def _sort(h2, route):
    n = h2.shape[0]
    nt = n // SORT_TM
    cap = _region_cap(n)
    assert 4 * SORT_TM >= ZFILL_TOK
    smem = pl.BlockSpec(memory_space=pltpu.SMEM)
    return pl.pallas_call(
        _sort_body,
        grid=(nt,),
        in_specs=[pl.BlockSpec((SORT_TM, D_MODEL), lambda i: (i, 0)),
                  pl.BlockSpec((SORT_TM, ROUTER_LANES), lambda i: (i, 0))],
        out_specs=[pl.BlockSpec(memory_space=pl.ANY), smem, smem],
        out_shape=[jax.ShapeDtypeStruct((N_GROUPS * cap * TOK_ROWS, LANES), jnp.uint32),
                   jax.ShapeDtypeStruct((nt * N_GROUPS,), jnp.int32),
                   jax.ShapeDtypeStruct((N_GROUPS,), jnp.int32)],
        scratch_shapes=[pltpu.VMEM((4 * SORT_TM * TOK_ROWS, LANES), jnp.uint32),
                        pltpu.SMEM((N_GROUPS,), jnp.int32),
                        pltpu.SemaphoreType.DMA((N_GROUPS,))],
        compiler_params=_cparams(("arbitrary",)),
        name="moe_sort",
    )(h2, route)


def _experts_body(blk_ref, grp_ref, valid_ref, hs_ref, wu_ref, wd_ref, ys_ref):
    t = pl.program_id(0)
    tm = MOE_TM

    @pl.when(valid_ref[t] > 0)
    def _():
        chunks = [hs_ref[pl.ds(s, tm, stride=TOK_ROWS), :] for s in range(H_CHUNKS + 1)]
        hi = [lax.bitcast_convert_type(c & jnp.uint32(HI_MASK), F32) for c in chunks[:H_CHUNKS]]
        lo = [lax.bitcast_convert_type(c << 16, F32) for c in chunks[:H_CHUNKS]]
        h = jnp.concatenate(hi + lo, axis=1).astype(BF16)
        route = lax.bitcast_convert_type(chunks[ROUTE_ROW], F32)
        acts = []
        for e in range(EXPERTS_PER_GROUP):
            gu = jnp.dot(h, wu_ref[e], preferred_element_type=F32)
            acts.append((jax.nn.silu(gu[:, :D_EXPERT]) * gu[:, D_EXPERT:]
                         * route[:, e:e + 1]).astype(BF16))
        act = jnp.concatenate(acts, axis=1)
        wd = wd_ref[...].reshape(EXPERTS_PER_GROUP * D_EXPERT, D_MODEL)
        out = jnp.dot(act, wd, preferred_element_type=F32)
        for j in range(D_MODEL // LANES):
            ys_ref[pl.ds(j, tm, stride=TOK_ROWS), :] = out[:, j * LANES:(j + 1) * LANES]


def _experts(blk, grp, valid, hs, wu, wd):
    n_tiles = blk.shape[0]
    epg = EXPERTS_PER_GROUP
    return pl.pallas_call(
        _experts_body,
        grid_spec=pltpu.PrefetchScalarGridSpec(
            num_scalar_prefetch=3,
            grid=(n_tiles,),
            in_specs=[pl.BlockSpec((MOE_TM * TOK_ROWS, LANES), lambda t, b, g, v: (b[t], 0)),
                      pl.BlockSpec((epg, D_MODEL, 2 * D_EXPERT), lambda t, b, g, v: (g[t], 0, 0)),
                      pl.BlockSpec((epg, D_EXPERT, D_MODEL), lambda t, b, g, v: (g[t], 0, 0))],
            out_specs=pl.BlockSpec((MOE_TM * TOK_ROWS, LANES), lambda t, b, g, v: (b[t], 0)),
        ),
        out_shape=jax.ShapeDtypeStruct(hs.shape, F32),
        compiler_params=_cparams(("arbitrary",)),
        name="moe_experts",
    )(blk, grp, valid, hs, wu, wd)


def _combine_body(starts_ref, x1_ref, route_ref, ys_hbm, o_ref, buf_ref, sem):
    t = pl.program_id(0)
    nt = pl.num_programs(0)
    tm = SORT_TM
    blk_rows = tm * TOK_ROWS
    slot_rows = N_GROUPS * blk_rows
    cap = ys_hbm.shape[0] // (N_GROUPS * TOK_ROWS)

    def fetch(step, slot, g):
        src = pl.multiple_of((g * cap + starts_ref[step * N_GROUPS + g]) * TOK_ROWS, TOK_ROWS)
        dst = pl.multiple_of(slot * slot_rows + g * blk_rows, TOK_ROWS)
        return pltpu.make_async_copy(ys_hbm.at[pl.ds(src, blk_rows), :],
                                     buf_ref.at[pl.ds(dst, blk_rows), :], sem.at[slot, g])

    @pl.when(t == 0)
    def _():
        for g in range(N_GROUPS):
            fetch(0, 0, g).start()

    @pl.when(t + 1 < nt)
    def _():
        for g in range(N_GROUPS):
            fetch(t + 1, (t + 1) % 2, g).start()

    onehot, rank, _, _ = _tile_positions(route_ref[...])
    own_rank = jnp.sum(onehot * rank, axis=1, keepdims=True)
    lane = lax.broadcasted_iota(jnp.int32, onehot.shape, 1)
    ri = lax.broadcasted_iota(jnp.int32, (tm, tm), 1).astype(F32)
    slot = t % 2
    acc = x1_ref[...]
    for g in range(N_GROUPS):
        fetch(t, slot, g).wait()
        in_g = jnp.sum(jnp.where(lane == ROUTE_GROUP_OFF + g, onehot, 0.0), axis=1, keepdims=True)
        sel = ((ri == own_rank) & (in_g > 0.5)).astype(BF16)
        r0 = pl.multiple_of(slot * slot_rows + g * blk_rows, TOK_ROWS)
        y = jnp.concatenate(
            [buf_ref[pl.ds(r0 + j, tm, stride=TOK_ROWS), :] for j in range(D_MODEL // LANES)],
            axis=1).astype(BF16)
        acc = acc + jnp.dot(sel, y, preferred_element_type=F32)
    o_ref[...] = acc


def _combine(starts, x1, route, ys):
    n = x1.shape[0]
    nt = n // SORT_TM
    return pl.pallas_call(
        _combine_body,
        grid_spec=pltpu.PrefetchScalarGridSpec(
            num_scalar_prefetch=1,
            grid=(nt,),
            in_specs=[pl.BlockSpec((SORT_TM, D_MODEL), lambda i, s: (i, 0)),
                      pl.BlockSpec((SORT_TM, ROUTER_LANES), lambda i, s: (i, 0)),
                      pl.BlockSpec(memory_space=pl.ANY)],
            out_specs=pl.BlockSpec((SORT_TM, D_MODEL), lambda i, s: (i, 0)),
            scratch_shapes=[pltpu.VMEM((2 * N_GROUPS * SORT_TM * TOK_ROWS, LANES), F32),
                            pltpu.SemaphoreType.DMA((2, N_GROUPS))],
        ),
        out_shape=jax.ShapeDtypeStruct((n, D_MODEL), F32),
        compiler_params=_cparams(("arbitrary",)),
        name="moe_combine",
    )(starts, x1, route, ys)


def _expert_tiles(counts, n):
    cap_blocks = _region_cap(n) // MOE_TM
    n_tiles = (n + N_GROUPS * SORT_TM) // MOE_TM + N_GROUPS
    per_group = (counts + SORT_TM + MOE_TM - 1) // MOE_TM
    ends = jnp.cumsum(per_group)
    total = ends[-1]
    t = jnp.minimum(jnp.arange(n_tiles, dtype=jnp.int32), total - 1)
    grp = jnp.sum((t[:, None] >= ends[None, :]).astype(jnp.int32), axis=1)
    first = ends - per_group
    blk = grp * cap_blocks + (t - first[grp])
    valid = (jnp.arange(n_tiles, dtype=jnp.int32) < total).astype(jnp.int32)
    return blk.astype(jnp.int32), grp.astype(jnp.int32), valid


def _moe(h2, route, x1, w_up, w_down):
    n = h2.shape[0]
    hs, starts, counts = _sort(h2, route)
    blk, grp, valid = _expert_tiles(counts, n)
    ys = _experts(blk, grp, valid, hs, w_up.astype(BF16), w_down.astype(BF16))
    return _combine(starts, x1, route, ys)


def kernel(x, norm_mix, w_in, ssm_lam_re, ssm_lam_im, ssm_log_dt, ssm_b_re, ssm_b_im, ssm_c_re,
           ssm_c_im, ssm_d, w_ssm_glu_val, w_ssm_glu_gate, q_norm, k_norm, attn_sinks, w_attn_up,
           w_out, norm_ffn, w_coarse, w_fine, w_expert_up, w_expert_down):
    bsz, seq, d = x.shape
    n = bsz * seq
    depth = w_in.shape[0]
    x2 = x.reshape(n, d)
    for i in range(depth):
        u, q, k, v, gs, ga = _proj(x2, norm_mix[i].reshape(1, d), w_in[i].astype(BF16))
        bblk, cblk, lam_k, dsk = _s5_params(ssm_lam_re[i], ssm_lam_im[i], ssm_log_dt[i],
                                            ssm_b_re[i], ssm_b_im[i], ssm_c_re[i], ssm_c_im[i],
                                            ssm_d[i])
        z = _s5(u.reshape(bsz, seq, SSM_WIDTH), bblk, cblk, lam_k, dsk)
        ya = _attn(q.reshape(bsz, seq, ATTN_WIDTH), k.reshape(bsz, seq, KV_WIDTH),
                   v.reshape(bsz, seq, KV_WIDTH), q_norm[i], k_norm[i], attn_sinks[i])
        wglu = jnp.concatenate([w_ssm_glu_val[i], w_ssm_glu_gate[i]], axis=1).astype(BF16)
        wr_t = jnp.concatenate(
            [w_coarse[i].T, w_fine[i].T,
             jnp.zeros((ROUTER_ROWS - N_GROUPS - N_EXPERTS, d), w_fine.dtype)], axis=0).astype(F32)
        wr_hi = wr_t.astype(BF16)
        wr = jnp.concatenate([wr_hi, (wr_t - wr_hi.astype(F32)).astype(BF16)], axis=0)
        x1, h2, gates = _mix(z.reshape(n, SSM_WIDTH), ya.reshape(n, ATTN_WIDTH), gs, ga, x2,
                             wglu, w_attn_up[i].astype(BF16), w_out[i].astype(BF16),
                             norm_ffn[i].reshape(1, d), wr)
        x2 = _moe(h2, gates, x1, w_expert_up[i], w_expert_down[i])
    return x2.reshape(bsz, seq, d)
```

```python
import functools
import math

import jax
import jax.numpy as jnp
from jax import lax
from jax.experimental import pallas as pl
from jax.experimental.pallas import tpu as pltpu

F32 = jnp.float32
BF16 = jnp.bfloat16

D_MODEL = 1024
SSM_WIDTH = 512
SSM_GROUP = 16
SSM_GROUPS = 32
SSM_STATE = 64
N_HEADS = 8
N_KV_HEADS = 2
HEAD_DIM = 64
ATTN_WIDTH = 512
KV_WIDTH = 128
BLOCK = 128
N_GROUPS = 4
EXPERTS_PER_GROUP = 4
N_EXPERTS = 16
D_EXPERT = 256
EPS = 1e-6
NEG = -1e30

LANES = 128
SLABS = SSM_WIDTH // LANES
GROUPS_PER_SLAB = LANES // SSM_GROUP
SLAB_STATE = GROUPS_PER_SLAB * SSM_STATE
ROUTER_LANES = 128
ROUTER_ROWS = 32
FINE_OFF = N_GROUPS
ROUTE_GROUP_OFF = 8

VMEM_LIMIT = 56 * 1024 * 1024


def _cparams(sem):
    return pltpu.CompilerParams(dimension_semantics=sem, vmem_limit_bytes=VMEM_LIMIT)


PROJ_TM = 512


def _proj_body(x_ref, g_ref, w_ref, u_ref, q_ref, k_ref, v_ref, gs_ref, ga_ref):
    x = x_ref[...]
    ms = jnp.mean(x * x, axis=-1, keepdims=True)
    h = (x * lax.rsqrt(ms + EPS) * g_ref[...]).astype(BF16)

    def seg(a, b):
        return jnp.dot(h, w_ref[:, a:b], preferred_element_type=F32)

    o = 0
    u_ref[...] = seg(o, o + SSM_WIDTH)
    o += SSM_WIDTH
    q_ref[...] = seg(o, o + ATTN_WIDTH).astype(BF16)
    o += ATTN_WIDTH
    k_ref[...] = seg(o, o + KV_WIDTH).astype(BF16)
    o += KV_WIDTH
    v_ref[...] = seg(o, o + KV_WIDTH).astype(BF16)
    o += KV_WIDTH
    gs_ref[...] = jax.nn.sigmoid(seg(o, o + D_MODEL)).astype(BF16)
    o += D_MODEL
    ga_ref[...] = jax.nn.sigmoid(seg(o, o + D_MODEL)).astype(BF16)


def _proj(x2, g, w):
    n = x2.shape[0]
    cols = w.shape[1]
    row = lambda c: pl.BlockSpec((PROJ_TM, c), lambda i: (i, 0))
    full = lambda a, b: pl.BlockSpec((a, b), lambda i: (0, 0))
    return pl.pallas_call(
        _proj_body,
        grid=(n // PROJ_TM,),
        in_specs=[row(D_MODEL), full(1, D_MODEL), full(D_MODEL, cols)],
        out_specs=[row(SSM_WIDTH), row(ATTN_WIDTH), row(KV_WIDTH), row(KV_WIDTH),
                   row(D_MODEL), row(D_MODEL)],
        out_shape=[jax.ShapeDtypeStruct((n, SSM_WIDTH), F32),
                   jax.ShapeDtypeStruct((n, ATTN_WIDTH), BF16),
                   jax.ShapeDtypeStruct((n, KV_WIDTH), BF16),
                   jax.ShapeDtypeStruct((n, KV_WIDTH), BF16),
                   jax.ShapeDtypeStruct((n, D_MODEL), BF16),
                   jax.ShapeDtypeStruct((n, D_MODEL), BF16)],
        compiler_params=_cparams(("arbitrary",)),
        name="proj",
    )(x2, g, w)


S5_TT = 64
S5_PAIRS = S5_TT // 2


def _s5_body(u_ref, bblk_ref, cblk_ref, k0_ref, lam_ref, d_ref, z_ref,
             ut_ref, y_ref, st_ref, carry_ref, *, bsz):
    rows = S5_TT * bsz
    prow = S5_PAIRS * bsz

    @pl.when(pl.program_id(0) == 0)
    def _():
        carry_ref[...] = jnp.zeros_like(carry_ref)

    for b in range(bsz):
        for j in range(SLABS):
            ut_ref[j, pl.ds(b, S5_TT, stride=bsz), :] = u_ref[b, :, j * LANES:(j + 1) * LANES]

    for j in range(SLABS):
        u3 = ut_ref[j].reshape(S5_PAIRS, 2 * bsz, LANES)
        u0 = u3[:, :bsz, :].reshape(prow, LANES)
        u1 = u3[:, bsz:, :].reshape(prow, LANES)
        lhs = jnp.concatenate([u1, u0], axis=1).astype(BF16)
        st_ref[0:bsz, :] = carry_ref[j]
        st_ref[bsz:, :] = jnp.dot(lhs, bblk_ref[j], preferred_element_type=F32)
        a = jnp.broadcast_to(lam_ref[j, 0:1, :], (bsz, SLAB_STATE))
        bb = jnp.broadcast_to(lam_ref[j, 1:2, :], (bsz, SLAB_STATE))

        def step(k, c):
            cre, cim = c
            r0 = pl.multiple_of((k + 1) * bsz, bsz)
            bre = st_ref[pl.ds(r0, bsz), 0:SLAB_STATE]
            bim = st_ref[pl.ds(r0, bsz), SLAB_STATE:2 * SLAB_STATE]
            nre = a * cre - bb * cim + bre
            nim = a * cim + bb * cre + bim
            st_ref[pl.ds(r0, bsz), 0:SLAB_STATE] = nre
            st_ref[pl.ds(r0, bsz), SLAB_STATE:2 * SLAB_STATE] = nim
            return nre, nim

        c0 = (carry_ref[j, :, 0:SLAB_STATE], carry_ref[j, :, SLAB_STATE:2 * SLAB_STATE])
        cre, cim = lax.fori_loop(0, S5_PAIRS, step, c0, unroll=4)
        carry_ref[j, :, 0:SLAB_STATE] = cre
        carry_ref[j, :, SLAB_STATE:2 * SLAB_STATE] = cim

        cs = jnp.dot(st_ref[...].astype(BF16), cblk_ref[j], preferred_element_type=F32)
        y1 = cs[bsz:, :LANES] + d_ref[j] * u1
        y0 = (cs[:prow, LANES:] + jnp.dot(u0.astype(BF16), k0_ref[j], preferred_element_type=F32)
              + d_ref[j] * u0)
        y = jnp.concatenate([y0.reshape(S5_PAIRS, bsz, LANES), y1.reshape(S5_PAIRS, bsz, LANES)],
                            axis=1).reshape(rows, LANES)
        y_ref[j] = jax.nn.gelu(y)

    for b in range(bsz):
        for j in range(SLABS):
            z_ref[b, :, j * LANES:(j + 1) * LANES] = (
                y_ref[j, pl.ds(b, S5_TT, stride=bsz), :].astype(BF16))


def _s5(u3, bblk, cblk, k0, lam, dskip):
    bsz, seq, _ = u3.shape
    rows = S5_TT * bsz
    full = lambda *s: pl.BlockSpec(s, lambda i: (0,) * len(s))
    return pl.pallas_call(
        functools.partial(_s5_body, bsz=bsz),
        grid=(seq // S5_TT,),
        in_specs=[pl.BlockSpec((bsz, S5_TT, SSM_WIDTH), lambda i: (0, i, 0)),
                  full(SLABS, 2 * LANES, 2 * SLAB_STATE),
                  full(SLABS, 2 * SLAB_STATE, 2 * LANES),
                  full(SLABS, LANES, LANES),
                  full(SLABS, 2, SLAB_STATE),
                  full(SLABS, 1, LANES)],
        out_specs=pl.BlockSpec((bsz, S5_TT, SSM_WIDTH), lambda i: (0, i, 0)),
        out_shape=jax.ShapeDtypeStruct((bsz, seq, SSM_WIDTH), BF16),
        scratch_shapes=[pltpu.VMEM((SLABS, rows, LANES), F32),
                        pltpu.VMEM((SLABS, rows, LANES), F32),
                        pltpu.VMEM((bsz + S5_PAIRS * bsz, 2 * SLAB_STATE), F32),
                        pltpu.VMEM((SLABS, bsz, 2 * SLAB_STATE), F32)],
        compiler_params=_cparams(("arbitrary",)),
        name="s5",
    )(u3, bblk, cblk, k0, lam, dskip)


def _s5_params(lam_re, lam_im, log_dt, b_re, b_im, c_re, c_im, d_skip):
    lam = lax.complex(lam_re.astype(F32), lam_im.astype(F32))
    dt = jnp.exp(log_dt.astype(F32))[:, None]
    lam_bar = jnp.exp(lam * dt)
    b = lax.complex(b_re.astype(F32), b_im.astype(F32))
    b_bar = ((lam_bar - 1.0) / lam)[..., None] * b
    eye = jnp.eye(GROUPS_PER_SLAB, dtype=F32)

    def in_blk(part):
        p4 = part.reshape(SLABS, GROUPS_PER_SLAB, SSM_STATE, SSM_GROUP)
        m = jnp.einsum('jgph,gk->jghkp', p4, eye)
        return m.reshape(SLABS, LANES, SLAB_STATE)

    def in_cplx(z):
        return jnp.concatenate([in_blk(z.real), in_blk(z.imag)], axis=-1)

    bblk = jnp.concatenate([in_cplx(b_bar), in_cplx(lam_bar[..., None] * b_bar)],
                           axis=1).astype(BF16)

    def out_blk(part):
        p4 = part.reshape(SLABS, GROUPS_PER_SLAB, SSM_GROUP, SSM_STATE)
        m = jnp.einsum('jghp,gk->jgpkh', p4, eye)
        return m.reshape(SLABS, SLAB_STATE, LANES)

    def out_cplx(z):
        return jnp.concatenate([out_blk(z.real), -out_blk(z.imag)], axis=1)

    c = lax.complex(c_re.astype(F32), c_im.astype(F32))
    cblk = jnp.concatenate([out_cplx(c), out_cplx(c * lam_bar[:, None, :])], axis=-1).astype(BF16)
    k0 = jnp.einsum('ghp,gpk->gkh', c, b_bar).real.reshape(SLABS, GROUPS_PER_SLAB, SSM_GROUP,
                                                           SSM_GROUP)
    k0 = jnp.einsum('jgkh,gq->jgkqh', k0, eye).reshape(SLABS, LANES, LANES).astype(BF16)
    lam2 = lam_bar * lam_bar
    lam_k = jnp.stack([lam2.real.reshape(SLABS, SLAB_STATE),
                       lam2.imag.reshape(SLABS, SLAB_STATE)], axis=1)
    dsk = d_skip.astype(F32).reshape(SLABS, 1, LANES)
    return bblk, cblk, k0, lam_k, dsk


ATTN_QB = 4
HEADS_PER_TILE = LANES // HEAD_DIM


def _pair_norm(t, gain2, head_mean):
    sq = t * t
    hi = sq.astype(BF16)
    lo = (sq - hi.astype(F32)).astype(BF16)
    ms = (jnp.dot(hi, head_mean, preferred_element_type=F32)
          + jnp.dot(lo, head_mean, preferred_element_type=F32))
    return t * lax.rsqrt(ms + EPS) * gain2


def _attn_body(sink_ref, q_ref, kc_ref, kp_ref, vc_ref, vp_ref, qg_ref, kg_ref, o_ref):
    n = pl.program_id(1)
    lane = lax.broadcasted_iota(jnp.int32, (1, LANES), 1)
    lo = lane < HEAD_DIM
    li = lax.broadcasted_iota(jnp.int32, (LANES, LANES), 0)
    lj = lax.broadcasted_iota(jnp.int32, (LANES, LANES), 1)
    head_mean = jnp.where((li < HEAD_DIM) == (lj < HEAD_DIM), 1.0 / HEAD_DIM, 0.0).astype(BF16)
    ones = jnp.ones((LANES, LANES), BF16)

    kall = jnp.concatenate([kp_ref[0], kc_ref[0]], axis=0).astype(F32)
    kn = _pair_norm(kall, kg_ref[...], head_mean)
    kroll = pltpu.roll(kn, HEAD_DIM, axis=1)
    vall = jnp.concatenate([vp_ref[0], vc_ref[0]], axis=0).astype(F32)
    vroll = pltpu.roll(vall, HEAD_DIM, axis=1)
    zero = jnp.zeros_like(kn)
    k_var = [[jnp.where(lo, kn, zero).astype(BF16), jnp.where(lo, zero, kroll).astype(BF16)],
             [jnp.where(lo, kroll, zero).astype(BF16), jnp.where(lo, zero, kn).astype(BF16)]]
    v_var = [[jnp.where(lo, vall, zero).astype(BF16), jnp.where(lo, zero, vroll).astype(BF16)],
             [jnp.where(lo, vroll, zero).astype(BF16), jnp.where(lo, zero, vall).astype(BF16)]]

    qn = []
    for i in range(ATTN_WIDTH // LANES):
        t = _pair_norm(q_ref[0, :, i * LANES:(i + 1) * LANES].astype(F32), qg_ref[...], head_mean)
        qn.append((t * (HEAD_DIM ** -0.5)).astype(BF16))

    qi = lax.broadcasted_iota(jnp.int32, (BLOCK, BLOCK), 0)
    si = lax.broadcasted_iota(jnp.int32, (BLOCK, BLOCK), 1)
    is_cur = si <= qi
    dist = jnp.where(is_cur, qi - si, qi - si + BLOCK).astype(F32)
    alibi = [(2.0 ** (-8.0 * (h + 1) / N_HEADS)) * dist for h in range(N_HEADS)]
    sinkmat = jnp.concatenate([jnp.full((BLOCK, LANES), sink_ref[h], F32) for h in range(N_HEADS)],
                              axis=0)

    for b in range(ATTN_QB):
        rows = slice(b * BLOCK, (b + 1) * BLOCK)
        win = slice(b * BLOCK, (b + 2) * BLOCK)
        s_all = []
        for h in range(N_HEADS):
            i, slot = divmod(h, HEADS_PER_TILE)
            kvh = h // (N_HEADS // N_KV_HEADS)
            sc = lax.dot_general(qn[i][rows], k_var[kvh][slot][win], (((1,), (1,)), ((), ())),
                                 preferred_element_type=F32)
            prev = sc[:, :BLOCK]
            if b == 0:
                prev = jnp.where(n > 0, prev, NEG)
            s_all.append(jnp.where(is_cur, sc[:, BLOCK:], prev) - alibi[h])
        s = jnp.concatenate(s_all, axis=0)
        m = jnp.maximum(jnp.max(s, axis=-1, keepdims=True), sinkmat)
        p = jnp.exp(s - m)
        denom = jnp.dot(p.astype(BF16), ones, preferred_element_type=F32) + jnp.exp(sinkmat - m)
        pn = p / denom
        for i in range(ATTN_WIDTH // LANES):
            acc = jnp.zeros((BLOCK, LANES), F32)
            for slot in range(HEADS_PER_TILE):
                h = i * HEADS_PER_TILE + slot
                kvh = h // (N_HEADS // N_KV_HEADS)
                ph = pn[h * BLOCK:(h + 1) * BLOCK]
                pcat = jnp.concatenate([jnp.where(is_cur, 0.0, ph), jnp.where(is_cur, ph, 0.0)],
                                       axis=1).astype(BF16)
                acc = acc + jnp.dot(pcat, v_var[kvh][slot][win], preferred_element_type=F32)
            o_ref[0, rows, i * LANES:(i + 1) * LANES] = acc.astype(BF16)


def _attn(q3, k3, v3, q_gain, k_gain, sinks):
    bsz, seq, _ = q3.shape
    qrows = ATTN_QB * BLOCK
    cur = lambda b, n: (b, n, 0)
    prev = lambda b, n: (b, jnp.maximum(n * ATTN_QB - 1, 0), 0)
    qg2 = jnp.tile(q_gain.astype(F32), HEADS_PER_TILE).reshape(1, LANES)
    kg2 = jnp.tile(k_gain.astype(F32), HEADS_PER_TILE).reshape(1, LANES)
    gain_spec = pl.BlockSpec((1, LANES), lambda b, n: (0, 0))
    return pl.pallas_call(
        _attn_body,
        grid=(bsz, seq // qrows),
        in_specs=[pl.BlockSpec(memory_space=pltpu.SMEM),
                  pl.BlockSpec((1, qrows, ATTN_WIDTH), cur),
                  pl.BlockSpec((1, qrows, KV_WIDTH), cur),
                  pl.BlockSpec((1, BLOCK, KV_WIDTH), prev),
                  pl.BlockSpec((1, qrows, KV_WIDTH), cur),
                  pl.BlockSpec((1, BLOCK, KV_WIDTH), prev),
                  gain_spec, gain_spec],
        out_specs=pl.BlockSpec((1, qrows, ATTN_WIDTH), cur),
        out_shape=jax.ShapeDtypeStruct((bsz, seq, ATTN_WIDTH), BF16),
        compiler_params=_cparams(("arbitrary", "arbitrary")),
        name="attn",
    )(sinks.astype(F32), q3, k3, k3, v3, v3, qg2, kg2)


MIX_TM = 512


def _mix_body(z_ref, ya_ref, gs_ref, ga_ref, x_ref, wglu_ref, wup_ref, wout_ref, g2_ref, wr_ref,
              x1_ref, h2_ref, gate_ref):
    z = z_ref[...]
    bv = jnp.dot(z, wglu_ref[:, :D_MODEL], preferred_element_type=F32)
    bg = jnp.dot(z, wglu_ref[:, D_MODEL:], preferred_element_type=F32)
    bs = bv * jax.nn.sigmoid(bg)
    ba = jnp.dot(ya_ref[...], wup_ref[...], preferred_element_type=F32)
    merged = gs_ref[...].astype(F32) * bs + ga_ref[...].astype(F32) * ba
    x1 = x_ref[...] + jnp.dot(merged.astype(BF16), wout_ref[...], preferred_element_type=F32)
    x1_ref[...] = x1

    ms = jnp.mean(x1 * x1, axis=-1, keepdims=True)
    h2 = x1 * lax.rsqrt(ms + EPS) * g2_ref[...]
    h2b = h2.astype(BF16)
    h2_ref[...] = h2b

    lt = lax.dot_general(wr_ref[...], h2b, (((1,), (1,)), ((), ())),
                         preferred_element_type=F32)
    logits = lt[:ROUTER_ROWS] + lt[ROUTER_ROWS:]
    row = lax.broadcasted_iota(jnp.int32, logits.shape, 0)
    ninf = -jnp.inf
    cm = jnp.where(row < N_GROUPS, logits, ninf)
    cmax = jnp.max(cm, axis=0, keepdims=True)
    g_prob = 1.0 / jnp.sum(jnp.exp(cm - cmax), axis=0, keepdims=True)
    g_idx = jnp.min(jnp.where(cm == cmax, row, ROUTER_ROWS), axis=0, keepdims=True)
    f0 = FINE_OFF + EXPERTS_PER_GROUP * g_idx
    fm = jnp.where((row >= f0) & (row < f0 + EXPERTS_PER_GROUP), logits, ninf)
    v1 = jnp.max(fm, axis=0, keepdims=True)
    i1 = jnp.min(jnp.where(fm == v1, row, ROUTER_ROWS), axis=0, keepdims=True)
    fm2 = jnp.where(row == i1, ninf, fm)
    v2 = jnp.max(fm2, axis=0, keepdims=True)
    i2 = jnp.min(jnp.where(fm2 == v2, row, ROUTER_ROWS), axis=0, keepdims=True)
    e21 = jnp.exp(v2 - v1)
    w1 = g_prob / (1.0 + e21)
    w2 = w1 * e21
    rt = jnp.where(row == i1 - f0, w1,
                   jnp.where(row == i2 - f0, w2,
                             jnp.where(row == ROUTE_GROUP_OFF + g_idx, 1.0, 0.0)))
    rt = jnp.concatenate([rt, jnp.zeros((ROUTER_LANES - ROUTER_ROWS, rt.shape[1]), F32)], axis=0)
    gate_ref[...] = rt.T


def _mix(z2, ya2, gs2, ga2, x2, wglu, wup, wout, g2, wr):
    n = x2.shape[0]
    row = lambda c: pl.BlockSpec((MIX_TM, c), lambda i: (i, 0))
    full = lambda a, b: pl.BlockSpec((a, b), lambda i: (0, 0))
    return pl.pallas_call(
        _mix_body,
        grid=(n // MIX_TM,),
        in_specs=[row(SSM_WIDTH), row(ATTN_WIDTH), row(D_MODEL), row(D_MODEL), row(D_MODEL),
                  full(SSM_WIDTH, 2 * D_MODEL), full(ATTN_WIDTH, D_MODEL), full(D_MODEL, D_MODEL),
                  full(1, D_MODEL), full(2 * ROUTER_ROWS, D_MODEL)],
        out_specs=[row(D_MODEL), row(D_MODEL), row(ROUTER_LANES)],
        out_shape=[jax.ShapeDtypeStruct((n, D_MODEL), F32),
                   jax.ShapeDtypeStruct((n, D_MODEL), BF16),
                   jax.ShapeDtypeStruct((n, ROUTER_LANES), F32)],
        compiler_params=_cparams(("arbitrary",)),
        name="mix",
    )(z2, ya2, gs2, ga2, x2, wglu, wup, wout, g2, wr)


SORT_TM = 256
RUN_SHORT = 128
MOE_TM = 512
TOK_ROWS = 8
H_WORDS = D_MODEL // 2
H_CHUNKS = H_WORDS // LANES
ROUTE_ROW = H_CHUNKS
ZFILL_TOK = MOE_TM + SORT_TM
HI_MASK = 0xFFFF0000


def _region_cap(n):
    cap = n + ZFILL_TOK
    return -(-cap // MOE_TM) * MOE_TM


def _tile_positions(route):
    tm = route.shape[0]
    lane = lax.broadcasted_iota(jnp.int32, route.shape, 1)
    onehot = jnp.where((lane >= ROUTE_GROUP_OFF) & (lane < ROUTE_GROUP_OFF + N_GROUPS), route, 0.0)
    ci = lax.broadcasted_iota(jnp.int32, (tm, tm), 0)
    cj = lax.broadcasted_iota(jnp.int32, (tm, tm), 1)
    earlier = (cj < ci).astype(BF16)
    rank = jnp.dot(earlier, onehot.astype(BF16), preferred_element_type=F32)
    cnt = jnp.sum(onehot, axis=0, keepdims=True)
    li = lax.broadcasted_iota(jnp.int32, (LANES, LANES), 0)
    lj = lax.broadcasted_iota(jnp.int32, (LANES, LANES), 1)
    base = jnp.dot(jnp.broadcast_to(cnt, (8, LANES)).astype(BF16), (li < lj).astype(BF16),
                   preferred_element_type=F32)[0:1]
    return onehot, rank, base, cnt


def _by_run_size(is_short, fn):
    @pl.when(is_short)
    def _():
        fn(RUN_SHORT)

    @pl.when(jnp.logical_not(is_short))
    def _():
        fn(SORT_TM)


def _sort_body(h_ref, route_ref, hs_hbm, starts_ref, short_ref, counts_ref,
               stage_ref, run_ref, prev_short_ref, sem):
    t = pl.program_id(0)
    nt = pl.num_programs(0)
    tm = SORT_TM
    slot_rows = 2 * tm * TOK_ROWS
    cap = hs_hbm.shape[0] // (N_GROUPS * TOK_ROWS)

    @pl.when(t == 0)
    def _():
        stage_ref[...] = jnp.zeros_like(stage_ref)
        for g in range(N_GROUPS):
            run_ref[g] = 0

    route = route_ref[...]
    onehot, rank, base, cnt = _tile_positions(route)
    w = (onehot * (rank + base)).astype(BF16)
    pos_row = lax.dot_general(jnp.ones((8, LANES), BF16), w, (((1,), (1,)), ((), ())),
                              preferred_element_type=F32)[0:1]
    ri = lax.broadcasted_iota(jnp.int32, (tm, tm), 0).astype(F32)
    perm = (ri == pos_row).astype(BF16)
    r_hi = route.astype(BF16)
    r_mid = (route - r_hi.astype(F32)).astype(BF16)
    r_lo = (route - r_hi.astype(F32) - r_mid.astype(F32)).astype(BF16)
    sorted_all = jnp.dot(perm, jnp.concatenate([h_ref[...], r_hi, r_mid, r_lo], axis=1),
                         preferred_element_type=F32)
    sh = sorted_all[:, :D_MODEL]
    sr = (sorted_all[:, D_MODEL:D_MODEL + LANES] + sorted_all[:, D_MODEL + LANES:D_MODEL + 2 * LANES]
          + sorted_all[:, D_MODEL + 2 * LANES:])
    hb = lax.bitcast_convert_type(sh, jnp.uint32)
    words = (hb[:, :H_WORDS] & jnp.uint32(HI_MASK)) | (hb[:, H_WORDS:] >> 16)

    slot0 = pl.multiple_of((t % 2) * slot_rows, TOK_ROWS)
    for s in range(H_CHUNKS):
        stage_ref[pl.ds(slot0 + s, tm, stride=TOK_ROWS), :] = words[:, s * LANES:(s + 1) * LANES]
    stage_ref[pl.ds(slot0 + ROUTE_ROW, tm, stride=TOK_ROWS), :] = lax.bitcast_convert_type(
        sr, jnp.uint32)

    def run_copy(src_row, dst_row, g, tokens):
        return pltpu.make_async_copy(stage_ref.at[pl.ds(src_row, tokens * TOK_ROWS), :],
                                     hs_hbm.at[pl.ds(dst_row, tokens * TOK_ROWS), :], sem.at[g])

    def wait_previous():
        for g in range(N_GROUPS):
            _by_run_size(prev_short_ref[g] == 1, lambda tokens, g=g: run_copy(0, 0, g, tokens).wait())

    @pl.when(t > 0)
    def _():
        wait_previous()

    for g in range(N_GROUPS):
        c_g = cnt[0, ROUTE_GROUP_OFF + g].astype(jnp.int32)
        b_g = base[0, ROUTE_GROUP_OFF + g].astype(jnp.int32)
        start = run_ref[g]
        starts_ref[t * N_GROUPS + g] = start
        run_ref[g] = start + c_g
        is_short = c_g <= RUN_SHORT
        short_ref[t * N_GROUPS + g] = is_short.astype(jnp.int32)
        prev_short_ref[g] = is_short.astype(jnp.int32)
        src = pl.multiple_of(slot0 + b_g * TOK_ROWS, TOK_ROWS)
        dst = pl.multiple_of((g * cap + start) * TOK_ROWS, TOK_ROWS)
        _by_run_size(is_short, lambda tokens, g=g, src=src, dst=dst:
                     run_copy(src, dst, g, tokens).start())

    @pl.when(t == nt - 1)
    def _():
        wait_previous()
        stage_ref[...] = jnp.zeros_like(stage_ref)
        zrows = ZFILL_TOK * TOK_ROWS
        for g in range(N_GROUPS):
            total = run_ref[g]
            counts_ref[g] = total
            dst = pl.multiple_of((g * cap + total) * TOK_ROWS, TOK_ROWS)
            pltpu.make_async_copy(stage_ref.at[pl.ds(0, zrows), :],
                                  hs_hbm.at[pl.ds(dst, zrows), :], sem.at[g]).start()
        for g in range(N_GROUPS):
            pltpu.make_async_copy(stage_ref.at[pl.ds(0, zrows), :],
                                  hs_hbm.at[pl.ds(0, zrows), :], sem.at[g]).wait()


def _sort(h2, route):
    n = h2.shape[0]
    nt = n // SORT_TM
    cap = _region_cap(n)
    assert 4 * SORT_TM >= ZFILL_TOK
    smem = pl.BlockSpec(memory_space=pltpu.SMEM)
    return pl.pallas_call(
        _sort_body,
        grid=(nt,),
        in_specs=[pl.BlockSpec((SORT_TM, D_MODEL), lambda i: (i, 0)),
                  pl.BlockSpec((SORT_TM, ROUTER_LANES), lambda i: (i, 0))],
        out_specs=[pl.BlockSpec(memory_space=pl.ANY), smem, smem, smem],
        out_shape=[jax.ShapeDtypeStruct((N_GROUPS * cap * TOK_ROWS, LANES), jnp.uint32),
                   jax.ShapeDtypeStruct((nt * N_GROUPS,), jnp.int32),
                   jax.ShapeDtypeStruct((nt * N_GROUPS,), jnp.int32),
                   jax.ShapeDtypeStruct((N_GROUPS,), jnp.int32)],
        scratch_shapes=[pltpu.VMEM((4 * SORT_TM * TOK_ROWS, LANES), jnp.uint32),
                        pltpu.SMEM((N_GROUPS,), jnp.int32),
                        pltpu.SMEM((N_GROUPS,), jnp.int32),
                        pltpu.SemaphoreType.DMA((N_GROUPS,))],
        compiler_params=_cparams(("arbitrary",)),
        name="moe_sort",
    )(h2, route)


def _experts_body(blk_ref, grp_ref, valid_ref, hs_ref, wu_ref, wd_ref, ys_ref):
    t = pl.program_id(0)
    tm = MOE_TM

    @pl.when(valid_ref[t] > 0)
    def _():
        chunks = [hs_ref[pl.ds(s, tm, stride=TOK_ROWS), :] for s in range(H_CHUNKS + 1)]
        hi = [lax.bitcast_convert_type(c & jnp.uint32(HI_MASK), F32) for c in chunks[:H_CHUNKS]]
        lo = [lax.bitcast_convert_type(c << 16, F32) for c in chunks[:H_CHUNKS]]
        h = jnp.concatenate(hi + lo, axis=1).astype(BF16)
        route = lax.bitcast_convert_type(chunks[ROUTE_ROW], F32)
        acts = []
        for e in range(EXPERTS_PER_GROUP):
            gu = jnp.dot(h, wu_ref[e], preferred_element_type=F32)
            acts.append((jax.nn.silu(gu[:, :D_EXPERT]) * gu[:, D_EXPERT:]
                         * route[:, e:e + 1]).astype(BF16))
        act = jnp.concatenate(acts, axis=1)
        wd = wd_ref[...].reshape(EXPERTS_PER_GROUP * D_EXPERT, D_MODEL)
        out = jnp.dot(act, wd, preferred_element_type=F32)
        for j in range(D_MODEL // LANES):
            ys_ref[pl.ds(j, tm, stride=TOK_ROWS), :] = out[:, j * LANES:(j + 1) * LANES]


def _experts(blk, grp, valid, hs, wu, wd):
    n_tiles = blk.shape[0]
    epg = EXPERTS_PER_GROUP
    return pl.pallas_call(
        _experts_body,
        grid_spec=pltpu.PrefetchScalarGridSpec(
            num_scalar_prefetch=3,
            grid=(n_tiles,),
            in_specs=[pl.BlockSpec((MOE_TM * TOK_ROWS, LANES), lambda t, b, g, v: (b[t], 0)),
                      pl.BlockSpec((epg, D_MODEL, 2 * D_EXPERT), lambda t, b, g, v: (g[t], 0, 0)),
                      pl.BlockSpec((epg, D_EXPERT, D_MODEL), lambda t, b, g, v: (g[t], 0, 0))],
            out_specs=pl.BlockSpec((MOE_TM * TOK_ROWS, LANES), lambda t, b, g, v: (b[t], 0)),
        ),
        out_shape=jax.ShapeDtypeStruct(hs.shape, F32),
        compiler_params=_cparams(("arbitrary",)),
        name="moe_experts",
    )(blk, grp, valid, hs, wu, wd)


def _combine_body(starts_ref, short_ref, x1_ref, route_ref, ys_hbm, o_ref, buf_ref, sem):
    t = pl.program_id(0)
    nt = pl.num_programs(0)
    tm = SORT_TM
    blk_rows = tm * TOK_ROWS
    slot_rows = N_GROUPS * blk_rows
    cap = ys_hbm.shape[0] // (N_GROUPS * TOK_ROWS)

    def fetch(step, slot, g, tokens):
        src = pl.multiple_of((g * cap + starts_ref[step * N_GROUPS + g]) * TOK_ROWS, TOK_ROWS)
        dst = pl.multiple_of(slot * slot_rows + g * blk_rows, TOK_ROWS)
        return pltpu.make_async_copy(ys_hbm.at[pl.ds(src, tokens * TOK_ROWS), :],
                                     buf_ref.at[pl.ds(dst, tokens * TOK_ROWS), :], sem.at[slot, g])

    def start_fetches(step, slot):
        for g in range(N_GROUPS):
            _by_run_size(short_ref[step * N_GROUPS + g] == 1,
                         lambda tokens, g=g: fetch(step, slot, g, tokens).start())

    @pl.when(t == 0)
    def _():
        start_fetches(0, 0)

    @pl.when(t + 1 < nt)
    def _():
        start_fetches(t + 1, (t + 1) % 2)

    onehot, rank, _, _ = _tile_positions(route_ref[...])
    own_rank = jnp.sum(onehot * rank, axis=1, keepdims=True)
    lane = lax.broadcasted_iota(jnp.int32, onehot.shape, 1)
    slot = t % 2

    def unsort(g, first, tokens):
        in_g = jnp.sum(jnp.where(lane == ROUTE_GROUP_OFF + g, onehot, 0.0), axis=1, keepdims=True)
        ri = (lax.broadcasted_iota(jnp.int32, (tm, tokens), 1) + first).astype(F32)
        sel = ((ri == own_rank) & (in_g > 0.5)).astype(BF16)
        r0 = pl.multiple_of(slot * slot_rows + (g * tm + first) * TOK_ROWS, TOK_ROWS)
        y = jnp.concatenate(
            [buf_ref[pl.ds(r0 + j, tokens, stride=TOK_ROWS), :] for j in range(D_MODEL // LANES)],
            axis=1).astype(BF16)
        return jnp.dot(sel, y, preferred_element_type=F32)

    acc = x1_ref[...]
    for g in range(N_GROUPS):
        _by_run_size(short_ref[t * N_GROUPS + g] == 1,
                     lambda tokens, g=g: fetch(t, slot, g, tokens).wait())
        acc = acc + unsort(g, 0, RUN_SHORT)
    o_ref[...] = acc
    for g in range(N_GROUPS):
        @pl.when(short_ref[t * N_GROUPS + g] == 0)
        def _(g=g):
            o_ref[...] += unsort(g, RUN_SHORT, tm - RUN_SHORT)


def _combine(starts, short, x1, route, ys):
    n = x1.shape[0]
    nt = n // SORT_TM
    return pl.pallas_call(
        _combine_body,
        grid_spec=pltpu.PrefetchScalarGridSpec(
            num_scalar_prefetch=2,
            grid=(nt,),
            in_specs=[pl.BlockSpec((SORT_TM, D_MODEL), lambda i, s, c: (i, 0)),
                      pl.BlockSpec((SORT_TM, ROUTER_LANES), lambda i, s, c: (i, 0)),
                      pl.BlockSpec(memory_space=pl.ANY)],
            out_specs=pl.BlockSpec((SORT_TM, D_MODEL), lambda i, s, c: (i, 0)),
            scratch_shapes=[pltpu.VMEM((2 * N_GROUPS * SORT_TM * TOK_ROWS, LANES), F32),
                            pltpu.SemaphoreType.DMA((2, N_GROUPS))],
        ),
        out_shape=jax.ShapeDtypeStruct((n, D_MODEL), F32),
        compiler_params=_cparams(("arbitrary",)),
        name="moe_combine",
    )(starts, short, x1, route, ys)


def _expert_tiles(counts, n):
    cap_blocks = _region_cap(n) // MOE_TM
    n_tiles = (n + N_GROUPS * SORT_TM) // MOE_TM + N_GROUPS
    per_group = (counts + SORT_TM + MOE_TM - 1) // MOE_TM
    ends = jnp.cumsum(per_group)
    total = ends[-1]
    t = jnp.minimum(jnp.arange(n_tiles, dtype=jnp.int32), total - 1)
    grp = jnp.sum((t[:, None] >= ends[None, :]).astype(jnp.int32), axis=1)
    first = ends - per_group
    blk = grp * cap_blocks + (t - first[grp])
    valid = (jnp.arange(n_tiles, dtype=jnp.int32) < total).astype(jnp.int32)
    return blk.astype(jnp.int32), grp.astype(jnp.int32), valid


def _moe(h2, route, x1, w_up, w_down):
    n = h2.shape[0]
    hs, starts, short, counts = _sort(h2, route)
    blk, grp, valid = _expert_tiles(counts, n)
    ys = _experts(blk, grp, valid, hs, w_up.astype(BF16), w_down.astype(BF16))
    return _combine(starts, short, x1, route, ys)


def kernel(x, norm_mix, w_in, ssm_lam_re, ssm_lam_im, ssm_log_dt, ssm_b_re, ssm_b_im, ssm_c_re,
           ssm_c_im, ssm_d, w_ssm_glu_val, w_ssm_glu_gate, q_norm, k_norm, attn_sinks, w_attn_up,
           w_out, norm_ffn, w_coarse, w_fine, w_expert_up, w_expert_down):
    bsz, seq, d = x.shape
    n = bsz * seq
    depth = w_in.shape[0]
    x2 = x.reshape(n, d)
    for i in range(depth):
        u, q, k, v, gs, ga = _proj(x2, norm_mix[i].reshape(1, d), w_in[i].astype(BF16))
        bblk, cblk, k0, lam_k, dsk = _s5_params(ssm_lam_re[i], ssm_lam_im[i], ssm_log_dt[i],
                                            ssm_b_re[i], ssm_b_im[i], ssm_c_re[i], ssm_c_im[i],
                                            ssm_d[i])
        z = _s5(u.reshape(bsz, seq, SSM_WIDTH), bblk, cblk, k0, lam_k, dsk)
        ya = _attn(q.reshape(bsz, seq, ATTN_WIDTH), k.reshape(bsz, seq, KV_WIDTH),
                   v.reshape(bsz, seq, KV_WIDTH), q_norm[i], k_norm[i], attn_sinks[i])
        wglu = jnp.concatenate([w_ssm_glu_val[i], w_ssm_glu_gate[i]], axis=1).astype(BF16)
        wr_t = jnp.concatenate(
            [w_coarse[i].T, w_fine[i].T,
             jnp.zeros((ROUTER_ROWS - N_GROUPS - N_EXPERTS, d), w_fine.dtype)], axis=0).astype(F32)
        wr_hi = wr_t.astype(BF16)
        wr = jnp.concatenate([wr_hi, (wr_t - wr_hi.astype(F32)).astype(BF16)], axis=0)
        x1, h2, gates = _mix(z.reshape(n, SSM_WIDTH), ya.reshape(n, ATTN_WIDTH), gs, ga, x2,
                             wglu, w_attn_up[i].astype(BF16), w_out[i].astype(BF16),
                             norm_ffn[i].reshape(1, d), wr)
        x2 = _moe(h2, gates, x1, w_expert_up[i], w_expert_down[i])
    return x2.reshape(bsz, seq, d)
```

```python
import functools
import math

import jax
import jax.numpy as jnp
from jax import lax
from jax.experimental import pallas as pl
from jax.experimental.pallas import tpu as pltpu

F32 = jnp.float32
BF16 = jnp.bfloat16

D_MODEL = 1024
SSM_WIDTH = 512
SSM_GROUP = 16
SSM_GROUPS = 32
SSM_STATE = 64
N_HEADS = 8
N_KV_HEADS = 2
HEAD_DIM = 64
ATTN_WIDTH = 512
KV_WIDTH = 128
BLOCK = 128
N_GROUPS = 4
EXPERTS_PER_GROUP = 4
N_EXPERTS = 16
D_EXPERT = 256
EPS = 1e-6
NEG = -1e30

LANES = 128
SLABS = SSM_WIDTH // LANES
GROUPS_PER_SLAB = LANES // SSM_GROUP
SLAB_STATE = GROUPS_PER_SLAB * SSM_STATE
ROUTER_LANES = 128
ROUTER_ROWS = 32
FINE_OFF = N_GROUPS
ROUTE_GROUP_OFF = 8

VMEM_LIMIT = 56 * 1024 * 1024


def _cparams(sem):
    return pltpu.CompilerParams(dimension_semantics=sem, vmem_limit_bytes=VMEM_LIMIT)


PROJ_TM = 1024


def _proj_body(x_ref, g_ref, w_ref, u_ref, q_ref, k_ref, v_ref, gs_ref, ga_ref):
    x = x_ref[...]
    ms = jnp.mean(x * x, axis=-1, keepdims=True)
    h = (x * lax.rsqrt(ms + EPS) * g_ref[...]).astype(BF16)

    def seg(a, b):
        return jnp.dot(h, w_ref[:, a:b], preferred_element_type=F32)

    o = 0
    u_ref[...] = seg(o, o + SSM_WIDTH)
    o += SSM_WIDTH
    q_ref[...] = seg(o, o + ATTN_WIDTH).astype(BF16)
    o += ATTN_WIDTH
    kv = seg(o, o + 2 * KV_WIDTH).astype(BF16)
    k_ref[...] = kv[:, :KV_WIDTH]
    v_ref[...] = kv[:, KV_WIDTH:]
    o += 2 * KV_WIDTH
    gs_ref[...] = jax.nn.sigmoid(seg(o, o + D_MODEL)).astype(BF16)
    o += D_MODEL
    ga_ref[...] = jax.nn.sigmoid(seg(o, o + D_MODEL)).astype(BF16)


def _proj(x2, g, w):
    n = x2.shape[0]
    cols = w.shape[1]
    row = lambda c: pl.BlockSpec((PROJ_TM, c), lambda i: (i, 0))
    full = lambda a, b: pl.BlockSpec((a, b), lambda i: (0, 0))
    return pl.pallas_call(
        _proj_body,
        grid=(n // PROJ_TM,),
        in_specs=[row(D_MODEL), full(1, D_MODEL), full(D_MODEL, cols)],
        out_specs=[row(SSM_WIDTH), row(ATTN_WIDTH), row(KV_WIDTH), row(KV_WIDTH),
                   row(D_MODEL), row(D_MODEL)],
        out_shape=[jax.ShapeDtypeStruct((n, SSM_WIDTH), F32),
                   jax.ShapeDtypeStruct((n, ATTN_WIDTH), BF16),
                   jax.ShapeDtypeStruct((n, KV_WIDTH), BF16),
                   jax.ShapeDtypeStruct((n, KV_WIDTH), BF16),
                   jax.ShapeDtypeStruct((n, D_MODEL), BF16),
                   jax.ShapeDtypeStruct((n, D_MODEL), BF16)],
        compiler_params=_cparams(("arbitrary",)),
        name="proj",
    )(x2, g, w)


S5_TT = 64
S5_PAIRS = S5_TT // 2


def _s5_body(u_ref, bblk_ref, cblk_ref, k0_ref, lam_ref, d_ref, z_ref,
             ut_ref, y_ref, st_ref, carry_ref, *, bsz):
    rows = S5_TT * bsz
    prow = S5_PAIRS * bsz

    @pl.when(pl.program_id(0) == 0)
    def _():
        carry_ref[...] = jnp.zeros_like(carry_ref)

    for b in range(bsz):
        for j in range(SLABS):
            ut_ref[j, pl.ds(b, S5_TT, stride=bsz), :] = u_ref[b, :, j * LANES:(j + 1) * LANES]

    for j in range(SLABS):
        u3 = ut_ref[j].reshape(S5_PAIRS, 2 * bsz, LANES)
        u0 = u3[:, :bsz, :].reshape(prow, LANES)
        u1 = u3[:, bsz:, :].reshape(prow, LANES)
        lhs = jnp.concatenate([u1, u0], axis=1).astype(BF16)
        st_ref[0:bsz, :] = carry_ref[j]
        st_ref[bsz:, :] = jnp.dot(lhs, bblk_ref[j], preferred_element_type=F32)
        a = jnp.broadcast_to(lam_ref[j, 0:1, :], (bsz, SLAB_STATE))
        bb = jnp.broadcast_to(lam_ref[j, 1:2, :], (bsz, SLAB_STATE))

        def step(k, c):
            cre, cim = c
            r0 = pl.multiple_of((k + 1) * bsz, bsz)
            bre = st_ref[pl.ds(r0, bsz), 0:SLAB_STATE]
            bim = st_ref[pl.ds(r0, bsz), SLAB_STATE:2 * SLAB_STATE]
            nre = a * cre - bb * cim + bre
            nim = a * cim + bb * cre + bim
            st_ref[pl.ds(r0, bsz), 0:SLAB_STATE] = nre
            st_ref[pl.ds(r0, bsz), SLAB_STATE:2 * SLAB_STATE] = nim
            return nre, nim

        c0 = (carry_ref[j, :, 0:SLAB_STATE], carry_ref[j, :, SLAB_STATE:2 * SLAB_STATE])
        cre, cim = lax.fori_loop(0, S5_PAIRS, step, c0, unroll=4)
        carry_ref[j, :, 0:SLAB_STATE] = cre
        carry_ref[j, :, SLAB_STATE:2 * SLAB_STATE] = cim

        cs = jnp.dot(st_ref[...].astype(BF16), cblk_ref[j], preferred_element_type=F32)
        y1 = cs[bsz:, :LANES] + d_ref[j] * u1
        y0 = (cs[:prow, LANES:] + jnp.dot(u0.astype(BF16), k0_ref[j], preferred_element_type=F32)
              + d_ref[j] * u0)
        y = jnp.concatenate([y0.reshape(S5_PAIRS, bsz, LANES), y1.reshape(S5_PAIRS, bsz, LANES)],
                            axis=1).reshape(rows, LANES)
        y_ref[j] = jax.nn.gelu(y)

    for b in range(bsz):
        for j in range(SLABS):
            z_ref[b, :, j * LANES:(j + 1) * LANES] = (
                y_ref[j, pl.ds(b, S5_TT, stride=bsz), :].astype(BF16))


def _s5(u3, bblk, cblk, k0, lam, dskip):
    bsz, seq, _ = u3.shape
    rows = S5_TT * bsz
    full = lambda *s: pl.BlockSpec(s, lambda i: (0,) * len(s))
    return pl.pallas_call(
        functools.partial(_s5_body, bsz=bsz),
        grid=(seq // S5_TT,),
        in_specs=[pl.BlockSpec((bsz, S5_TT, SSM_WIDTH), lambda i: (0, i, 0)),
                  full(SLABS, 2 * LANES, 2 * SLAB_STATE),
                  full(SLABS, 2 * SLAB_STATE, 2 * LANES),
                  full(SLABS, LANES, LANES),
                  full(SLABS, 2, SLAB_STATE),
                  full(SLABS, 1, LANES)],
        out_specs=pl.BlockSpec((bsz, S5_TT, SSM_WIDTH), lambda i: (0, i, 0)),
        out_shape=jax.ShapeDtypeStruct((bsz, seq, SSM_WIDTH), BF16),
        scratch_shapes=[pltpu.VMEM((SLABS, rows, LANES), F32),
                        pltpu.VMEM((SLABS, rows, LANES), F32),
                        pltpu.VMEM((bsz + S5_PAIRS * bsz, 2 * SLAB_STATE), F32),
                        pltpu.VMEM((SLABS, bsz, 2 * SLAB_STATE), F32)],
        compiler_params=_cparams(("arbitrary",)),
        name="s5",
    )(u3, bblk, cblk, k0, lam, dskip)


def _s5_params(lam_re, lam_im, log_dt, b_re, b_im, c_re, c_im, d_skip):
    lam = lax.complex(lam_re.astype(F32), lam_im.astype(F32))
    dt = jnp.exp(log_dt.astype(F32))[:, None]
    lam_bar = jnp.exp(lam * dt)
    b = lax.complex(b_re.astype(F32), b_im.astype(F32))
    b_bar = ((lam_bar - 1.0) / lam)[..., None] * b
    eye = jnp.eye(GROUPS_PER_SLAB, dtype=F32)

    def in_blk(part):
        p4 = part.reshape(SLABS, GROUPS_PER_SLAB, SSM_STATE, SSM_GROUP)
        m = jnp.einsum('jgph,gk->jghkp', p4, eye)
        return m.reshape(SLABS, LANES, SLAB_STATE)

    def in_cplx(z):
        return jnp.concatenate([in_blk(z.real), in_blk(z.imag)], axis=-1)

    bblk = jnp.concatenate([in_cplx(b_bar), in_cplx(lam_bar[..., None] * b_bar)],
                           axis=1).astype(BF16)

    def out_blk(part):
        p4 = part.reshape(SLABS, GROUPS_PER_SLAB, SSM_GROUP, SSM_STATE)
        m = jnp.einsum('jghp,gk->jgpkh', p4, eye)
        return m.reshape(SLABS, SLAB_STATE, LANES)

    def out_cplx(z):
        return jnp.concatenate([out_blk(z.real), -out_blk(z.imag)], axis=1)

    c = lax.complex(c_re.astype(F32), c_im.astype(F32))
    cblk = jnp.concatenate([out_cplx(c), out_cplx(c * lam_bar[:, None, :])], axis=-1).astype(BF16)
    k0 = jnp.einsum('ghp,gpk->gkh', c, b_bar).real.reshape(SLABS, GROUPS_PER_SLAB, SSM_GROUP,
                                                           SSM_GROUP)
    k0 = jnp.einsum('jgkh,gq->jgkqh', k0, eye).reshape(SLABS, LANES, LANES).astype(BF16)
    lam2 = lam_bar * lam_bar
    lam_k = jnp.stack([lam2.real.reshape(SLABS, SLAB_STATE),
                       lam2.imag.reshape(SLABS, SLAB_STATE)], axis=1)
    dsk = d_skip.astype(F32).reshape(SLABS, 1, LANES)
    return bblk, cblk, k0, lam_k, dsk


ATTN_QB = 4
HEADS_PER_TILE = LANES // HEAD_DIM


def _pair_norm(t, gain2, head_mean):
    sq = t * t
    hi = sq.astype(BF16)
    lo = (sq - hi.astype(F32)).astype(BF16)
    ms = (jnp.dot(hi, head_mean, preferred_element_type=F32)
          + jnp.dot(lo, head_mean, preferred_element_type=F32))
    return t * lax.rsqrt(ms + EPS) * gain2


def _attn_body(sink_ref, q_ref, kc_ref, kp_ref, vc_ref, vp_ref, qg_ref, kg_ref, o_ref):
    n = pl.program_id(1)
    lane = lax.broadcasted_iota(jnp.int32, (1, LANES), 1)
    lo = lane < HEAD_DIM
    li = lax.broadcasted_iota(jnp.int32, (LANES, LANES), 0)
    lj = lax.broadcasted_iota(jnp.int32, (LANES, LANES), 1)
    head_mean = jnp.where((li < HEAD_DIM) == (lj < HEAD_DIM), 1.0 / HEAD_DIM, 0.0).astype(BF16)
    ones = jnp.ones((LANES, LANES), BF16)

    kall = jnp.concatenate([kp_ref[0], kc_ref[0]], axis=0).astype(F32)
    kn = _pair_norm(kall, kg_ref[...], head_mean)
    kroll = pltpu.roll(kn, HEAD_DIM, axis=1)
    vall = jnp.concatenate([vp_ref[0], vc_ref[0]], axis=0).astype(F32)
    vroll = pltpu.roll(vall, HEAD_DIM, axis=1)
    zero = jnp.zeros_like(kn)
    k_var = [[jnp.where(lo, kn, zero).astype(BF16), jnp.where(lo, zero, kroll).astype(BF16)],
             [jnp.where(lo, kroll, zero).astype(BF16), jnp.where(lo, zero, kn).astype(BF16)]]
    v_var = [[jnp.where(lo, vall, zero).astype(BF16), jnp.where(lo, zero, vroll).astype(BF16)],
             [jnp.where(lo, vroll, zero).astype(BF16), jnp.where(lo, zero, vall).astype(BF16)]]

    qn = []
    for i in range(ATTN_WIDTH // LANES):
        t = _pair_norm(q_ref[0, :, i * LANES:(i + 1) * LANES].astype(F32), qg_ref[...], head_mean)
        qn.append((t * (HEAD_DIM ** -0.5)).astype(BF16))

    qi = lax.broadcasted_iota(jnp.int32, (BLOCK, BLOCK), 0)
    si = lax.broadcasted_iota(jnp.int32, (BLOCK, BLOCK), 1)
    is_cur = si <= qi
    dist = jnp.where(is_cur, qi - si, qi - si + BLOCK).astype(F32)
    alibi = [(2.0 ** (-8.0 * (h + 1) / N_HEADS)) * dist for h in range(N_HEADS)]
    sinkmat = jnp.concatenate([jnp.full((BLOCK, LANES), sink_ref[h], F32) for h in range(N_HEADS)],
                              axis=0)

    for b in range(ATTN_QB):
        rows = slice(b * BLOCK, (b + 1) * BLOCK)
        win = slice(b * BLOCK, (b + 2) * BLOCK)
        s_all = []
        for h in range(N_HEADS):
            i, slot = divmod(h, HEADS_PER_TILE)
            kvh = h // (N_HEADS // N_KV_HEADS)
            sc = lax.dot_general(qn[i][rows], k_var[kvh][slot][win], (((1,), (1,)), ((), ())),
                                 preferred_element_type=F32)
            prev = sc[:, :BLOCK]
            if b == 0:
                prev = jnp.where(n > 0, prev, NEG)
            s_all.append(jnp.where(is_cur, sc[:, BLOCK:], prev) - alibi[h])
        s = jnp.concatenate(s_all, axis=0)
        m = jnp.maximum(jnp.max(s, axis=-1, keepdims=True), sinkmat)
        p = jnp.exp(s - m)
        denom = jnp.dot(p.astype(BF16), ones, preferred_element_type=F32) + jnp.exp(sinkmat - m)
        pn = p / denom
        for i in range(ATTN_WIDTH // LANES):
            acc = jnp.zeros((BLOCK, LANES), F32)
            for slot in range(HEADS_PER_TILE):
                h = i * HEADS_PER_TILE + slot
                kvh = h // (N_HEADS // N_KV_HEADS)
                ph = pn[h * BLOCK:(h + 1) * BLOCK]
                pcat = jnp.concatenate([jnp.where(is_cur, 0.0, ph), jnp.where(is_cur, ph, 0.0)],
                                       axis=1).astype(BF16)
                acc = acc + jnp.dot(pcat, v_var[kvh][slot][win], preferred_element_type=F32)
            o_ref[0, rows, i * LANES:(i + 1) * LANES] = acc.astype(BF16)


def _attn(q3, k3, v3, q_gain, k_gain, sinks):
    bsz, seq, _ = q3.shape
    qrows = ATTN_QB * BLOCK
    cur = lambda b, n: (b, n, 0)
    prev = lambda b, n: (b, jnp.maximum(n * ATTN_QB - 1, 0), 0)
    qg2 = jnp.tile(q_gain.astype(F32), HEADS_PER_TILE).reshape(1, LANES)
    kg2 = jnp.tile(k_gain.astype(F32), HEADS_PER_TILE).reshape(1, LANES)
    gain_spec = pl.BlockSpec((1, LANES), lambda b, n: (0, 0))
    return pl.pallas_call(
        _attn_body,
        grid=(bsz, seq // qrows),
        in_specs=[pl.BlockSpec(memory_space=pltpu.SMEM),
                  pl.BlockSpec((1, qrows, ATTN_WIDTH), cur),
                  pl.BlockSpec((1, qrows, KV_WIDTH), cur),
                  pl.BlockSpec((1, BLOCK, KV_WIDTH), prev),
                  pl.BlockSpec((1, qrows, KV_WIDTH), cur),
                  pl.BlockSpec((1, BLOCK, KV_WIDTH), prev),
                  gain_spec, gain_spec],
        out_specs=pl.BlockSpec((1, qrows, ATTN_WIDTH), cur),
        out_shape=jax.ShapeDtypeStruct((bsz, seq, ATTN_WIDTH), BF16),
        compiler_params=_cparams(("arbitrary", "arbitrary")),
        name="attn",
    )(sinks.astype(F32), q3, k3, k3, v3, v3, qg2, kg2)


MIX_TM = 1024


def _mix_body(z_ref, ya_ref, gs_ref, ga_ref, x_ref, wglu_ref, wup_ref, wout_ref, g2_ref, wr_ref,
              x1_ref, h2_ref, gate_ref):
    z = z_ref[...]
    bv = jnp.dot(z, wglu_ref[:, :D_MODEL], preferred_element_type=F32)
    bg = jnp.dot(z, wglu_ref[:, D_MODEL:], preferred_element_type=F32)
    bs = bv * jax.nn.sigmoid(bg)
    ba = jnp.dot(ya_ref[...], wup_ref[...], preferred_element_type=F32)
    merged = gs_ref[...].astype(F32) * bs + ga_ref[...].astype(F32) * ba
    x1 = x_ref[...] + jnp.dot(merged.astype(BF16), wout_ref[...], preferred_element_type=F32)
    x1_ref[...] = x1

    ms = jnp.mean(x1 * x1, axis=-1, keepdims=True)
    h2 = x1 * lax.rsqrt(ms + EPS) * g2_ref[...]
    h2b = h2.astype(BF16)
    h2_ref[...] = h2b

    lt = lax.dot_general(wr_ref[...], h2b, (((1,), (1,)), ((), ())),
                         preferred_element_type=F32)
    logits = lt[:ROUTER_ROWS] + lt[ROUTER_ROWS:]
    row = lax.broadcasted_iota(jnp.int32, logits.shape, 0)
    ninf = -jnp.inf
    cm = jnp.where(row < N_GROUPS, logits, ninf)
    cmax = jnp.max(cm, axis=0, keepdims=True)
    g_prob = 1.0 / jnp.sum(jnp.exp(cm - cmax), axis=0, keepdims=True)
    g_idx = jnp.min(jnp.where(cm == cmax, row, ROUTER_ROWS), axis=0, keepdims=True)
    f0 = FINE_OFF + EXPERTS_PER_GROUP * g_idx
    fm = jnp.where((row >= f0) & (row < f0 + EXPERTS_PER_GROUP), logits, ninf)
    v1 = jnp.max(fm, axis=0, keepdims=True)
    i1 = jnp.min(jnp.where(fm == v1, row, ROUTER_ROWS), axis=0, keepdims=True)
    fm2 = jnp.where(row == i1, ninf, fm)
    v2 = jnp.max(fm2, axis=0, keepdims=True)
    i2 = jnp.min(jnp.where(fm2 == v2, row, ROUTER_ROWS), axis=0, keepdims=True)
    e21 = jnp.exp(v2 - v1)
    w1 = g_prob / (1.0 + e21)
    w2 = w1 * e21
    rt = jnp.where(row == i1 - f0, w1,
                   jnp.where(row == i2 - f0, w2,
                             jnp.where(row == ROUTE_GROUP_OFF + g_idx, 1.0, 0.0)))
    rt = jnp.concatenate([rt, jnp.zeros((ROUTER_LANES - ROUTER_ROWS, rt.shape[1]), F32)], axis=0)
    gate_ref[...] = rt.T


def _mix(z2, ya2, gs2, ga2, x2, wglu, wup, wout, g2, wr):
    n = x2.shape[0]
    row = lambda c: pl.BlockSpec((MIX_TM, c), lambda i: (i, 0))
    full = lambda a, b: pl.BlockSpec((a, b), lambda i: (0, 0))
    return pl.pallas_call(
        _mix_body,
        grid=(n // MIX_TM,),
        in_specs=[row(SSM_WIDTH), row(ATTN_WIDTH), row(D_MODEL), row(D_MODEL), row(D_MODEL),
                  full(SSM_WIDTH, 2 * D_MODEL), full(ATTN_WIDTH, D_MODEL), full(D_MODEL, D_MODEL),
                  full(1, D_MODEL), full(2 * ROUTER_ROWS, D_MODEL)],
        out_specs=[row(D_MODEL), row(D_MODEL), row(ROUTER_LANES)],
        out_shape=[jax.ShapeDtypeStruct((n, D_MODEL), F32),
                   jax.ShapeDtypeStruct((n, D_MODEL), BF16),
                   jax.ShapeDtypeStruct((n, ROUTER_LANES), F32)],
        compiler_params=_cparams(("arbitrary",)),
        name="mix",
    )(z2, ya2, gs2, ga2, x2, wglu, wup, wout, g2, wr)


SORT_TM = 256
RUN_SHORT = 128
MOE_TM = 512
TOK_ROWS = 8
H_WORDS = D_MODEL // 2
H_CHUNKS = H_WORDS // LANES
ROUTE_ROW = H_CHUNKS
ZFILL_TOK = MOE_TM
HI_MASK = 0xFFFF0000


def _region_cap(n):
    cap = n + ZFILL_TOK
    return -(-cap // MOE_TM) * MOE_TM


def _tile_positions(route):
    tm = route.shape[0]
    lane = lax.broadcasted_iota(jnp.int32, route.shape, 1)
    onehot = jnp.where((lane >= ROUTE_GROUP_OFF) & (lane < ROUTE_GROUP_OFF + N_GROUPS), route, 0.0)
    ci = lax.broadcasted_iota(jnp.int32, (tm, tm), 0)
    cj = lax.broadcasted_iota(jnp.int32, (tm, tm), 1)
    earlier = (cj < ci).astype(BF16)
    rank = jnp.dot(earlier, onehot.astype(BF16), preferred_element_type=F32)
    cnt = jnp.sum(onehot, axis=0, keepdims=True)
    li = lax.broadcasted_iota(jnp.int32, (LANES, LANES), 0)
    lj = lax.broadcasted_iota(jnp.int32, (LANES, LANES), 1)
    base = jnp.dot(jnp.broadcast_to(cnt, (8, LANES)).astype(BF16), (li < lj).astype(BF16),
                   preferred_element_type=F32)[0:1]
    return onehot, rank, base, cnt


def _sort_body(h_ref, route_ref, hs_hbm, starts_ref, lens_ref, counts_ref,
               stage_ref, run_ref, pend_ref, sem):
    t = pl.program_id(0)
    nt = pl.num_programs(0)
    tm = SORT_TM
    slot = t % 2
    slot_rows = 2 * tm * TOK_ROWS
    cap = hs_hbm.shape[0] // (N_GROUPS * TOK_ROWS)

    def run_copy(src_row, dst_row, sl, g, tokens):
        return pltpu.make_async_copy(stage_ref.at[pl.ds(src_row, tokens * TOK_ROWS), :],
                                     hs_hbm.at[pl.ds(dst_row, tokens * TOK_ROWS), :],
                                     sem.at[sl, g])

    def wait_slot(sl):
        for g in range(N_GROUPS):
            n = pend_ref[sl * N_GROUPS + g]

            @pl.when(n > 0)
            def _(g=g, n=n):
                run_copy(0, 0, sl, g, n).wait()

    @pl.when(t == 0)
    def _():
        stage_ref[...] = jnp.zeros_like(stage_ref)
        for g in range(N_GROUPS):
            run_ref[g] = 0

    @pl.when(t >= 2)
    def _():
        wait_slot(slot)

    route = route_ref[...]
    onehot, rank, base, cnt = _tile_positions(route)
    w = (onehot * (rank + base)).astype(BF16)
    pos_row = lax.dot_general(jnp.ones((8, LANES), BF16), w, (((1,), (1,)), ((), ())),
                              preferred_element_type=F32)[0:1]
    ri = lax.broadcasted_iota(jnp.int32, (tm, tm), 0).astype(F32)
    perm = (ri == pos_row).astype(BF16)
    r_hi = route.astype(BF16)
    r_mid = (route - r_hi.astype(F32)).astype(BF16)
    r_lo = (route - r_hi.astype(F32) - r_mid.astype(F32)).astype(BF16)
    sorted_all = jnp.dot(perm, jnp.concatenate([h_ref[...], r_hi, r_mid, r_lo], axis=1),
                         preferred_element_type=F32)
    sh = sorted_all[:, :D_MODEL]
    sr = (sorted_all[:, D_MODEL:D_MODEL + LANES] + sorted_all[:, D_MODEL + LANES:D_MODEL + 2 * LANES]
          + sorted_all[:, D_MODEL + 2 * LANES:])
    hb = lax.bitcast_convert_type(sh, jnp.uint32)
    words = (hb[:, :H_WORDS] & jnp.uint32(HI_MASK)) | (hb[:, H_WORDS:] >> 16)

    slot0 = pl.multiple_of(slot * slot_rows, TOK_ROWS)
    for s in range(H_CHUNKS):
        stage_ref[pl.ds(slot0 + s, tm, stride=TOK_ROWS), :] = words[:, s * LANES:(s + 1) * LANES]
    stage_ref[pl.ds(slot0 + ROUTE_ROW, tm, stride=TOK_ROWS), :] = lax.bitcast_convert_type(
        sr, jnp.uint32)

    for g in range(N_GROUPS):
        c_g = cnt[0, ROUTE_GROUP_OFF + g].astype(jnp.int32)
        b_g = base[0, ROUTE_GROUP_OFF + g].astype(jnp.int32)
        start = run_ref[g]
        starts_ref[t * N_GROUPS + g] = start
        lens_ref[t * N_GROUPS + g] = c_g
        pend_ref[slot * N_GROUPS + g] = c_g
        run_ref[g] = start + c_g
        src = pl.multiple_of(slot0 + b_g * TOK_ROWS, TOK_ROWS)
        dst = pl.multiple_of((g * cap + start) * TOK_ROWS, TOK_ROWS)

        @pl.when(c_g > 0)
        def _(g=g, src=src, dst=dst, c_g=c_g):
            run_copy(src, dst, slot, g, c_g).start()

    @pl.when(t == nt - 1)
    def _():
        @pl.when(t >= 1)
        def _():
            wait_slot(1 - slot)
        wait_slot(slot)
        stage_ref[...] = jnp.zeros_like(stage_ref)
        zrows = ZFILL_TOK * TOK_ROWS
        for g in range(N_GROUPS):
            total = run_ref[g]
            counts_ref[g] = total
            dst = pl.multiple_of((g * cap + total) * TOK_ROWS, TOK_ROWS)
            pltpu.make_async_copy(stage_ref.at[pl.ds(0, zrows), :],
                                  hs_hbm.at[pl.ds(dst, zrows), :], sem.at[0, g]).start()
        for g in range(N_GROUPS):
            pltpu.make_async_copy(stage_ref.at[pl.ds(0, zrows), :],
                                  hs_hbm.at[pl.ds(0, zrows), :], sem.at[0, g]).wait()


def _sort(h2, route):
    n = h2.shape[0]
    nt = n // SORT_TM
    cap = _region_cap(n)
    assert 4 * SORT_TM >= ZFILL_TOK
    smem = pl.BlockSpec(memory_space=pltpu.SMEM)
    return pl.pallas_call(
        _sort_body,
        grid=(nt,),
        in_specs=[pl.BlockSpec((SORT_TM, D_MODEL), lambda i: (i, 0)),
                  pl.BlockSpec((SORT_TM, ROUTER_LANES), lambda i: (i, 0))],
        out_specs=[pl.BlockSpec(memory_space=pl.ANY), smem, smem, smem],
        out_shape=[jax.ShapeDtypeStruct((N_GROUPS * cap * TOK_ROWS, LANES), jnp.uint32),
                   jax.ShapeDtypeStruct((nt * N_GROUPS,), jnp.int32),
                   jax.ShapeDtypeStruct((nt * N_GROUPS,), jnp.int32),
                   jax.ShapeDtypeStruct((N_GROUPS,), jnp.int32)],
        scratch_shapes=[pltpu.VMEM((4 * SORT_TM * TOK_ROWS, LANES), jnp.uint32),
                        pltpu.SMEM((N_GROUPS,), jnp.int32),
                        pltpu.SMEM((2 * N_GROUPS,), jnp.int32),
                        pltpu.SemaphoreType.DMA((2, N_GROUPS))],
        compiler_params=_cparams(("arbitrary",)),
        name="moe_sort",
    )(h2, route)


def _experts_body(blk_ref, grp_ref, valid_ref, hs_ref, wu_ref, wd_ref, ys_ref):
    t = pl.program_id(0)
    tm = MOE_TM

    @pl.when(valid_ref[t] > 0)
    def _():
        chunks = [hs_ref[pl.ds(s, tm, stride=TOK_ROWS), :] for s in range(H_CHUNKS + 1)]
        hi = [lax.bitcast_convert_type(c & jnp.uint32(HI_MASK), F32) for c in chunks[:H_CHUNKS]]
        lo = [lax.bitcast_convert_type(c << 16, F32) for c in chunks[:H_CHUNKS]]
        h = jnp.concatenate(hi + lo, axis=1).astype(BF16)
        route = lax.bitcast_convert_type(chunks[ROUTE_ROW], F32)
        acts = []
        for e in range(EXPERTS_PER_GROUP):
            gu = jnp.dot(h, wu_ref[e], preferred_element_type=F32)
            acts.append((jax.nn.silu(gu[:, :D_EXPERT]) * gu[:, D_EXPERT:]
                         * route[:, e:e + 1]).astype(BF16))
        act = jnp.concatenate(acts, axis=1)
        wd = wd_ref[...].reshape(EXPERTS_PER_GROUP * D_EXPERT, D_MODEL)
        out = jnp.dot(act, wd, preferred_element_type=F32)
        for j in range(D_MODEL // LANES):
            ys_ref[pl.ds(j, tm, stride=TOK_ROWS), :] = out[:, j * LANES:(j + 1) * LANES]


def _experts(blk, grp, valid, hs, wu, wd):
    n_tiles = blk.shape[0]
    epg = EXPERTS_PER_GROUP
    return pl.pallas_call(
        _experts_body,
        grid_spec=pltpu.PrefetchScalarGridSpec(
            num_scalar_prefetch=3,
            grid=(n_tiles,),
            in_specs=[pl.BlockSpec((MOE_TM * TOK_ROWS, LANES), lambda t, b, g, v: (b[t], 0)),
                      pl.BlockSpec((epg, D_MODEL, 2 * D_EXPERT), lambda t, b, g, v: (g[t], 0, 0)),
                      pl.BlockSpec((epg, D_EXPERT, D_MODEL), lambda t, b, g, v: (g[t], 0, 0))],
            out_specs=pl.BlockSpec((MOE_TM * TOK_ROWS, LANES), lambda t, b, g, v: (b[t], 0)),
        ),
        out_shape=jax.ShapeDtypeStruct(hs.shape, F32),
        compiler_params=_cparams(("arbitrary",)),
        name="moe_experts",
    )(blk, grp, valid, hs, wu, wd)


def _combine_body(starts_ref, lens_ref, x1_ref, route_ref, ys_hbm, o_ref, buf_ref, sem):
    t = pl.program_id(0)
    nt = pl.num_programs(0)
    tm = SORT_TM
    blk_rows = tm * TOK_ROWS
    slot_rows = N_GROUPS * blk_rows
    cap = ys_hbm.shape[0] // (N_GROUPS * TOK_ROWS)

    def fetch(step, slot, g):
        tokens = lens_ref[step * N_GROUPS + g]
        src = pl.multiple_of((g * cap + starts_ref[step * N_GROUPS + g]) * TOK_ROWS, TOK_ROWS)
        dst = pl.multiple_of(slot * slot_rows + g * blk_rows, TOK_ROWS)
        return pltpu.make_async_copy(ys_hbm.at[pl.ds(src, tokens * TOK_ROWS), :],
                                     buf_ref.at[pl.ds(dst, tokens * TOK_ROWS), :], sem.at[slot, g])

    def start_fetches(step, slot):
        for g in range(N_GROUPS):
            @pl.when(lens_ref[step * N_GROUPS + g] > 0)
            def _(g=g):
                fetch(step, slot, g).start()

    @pl.when(t == 0)
    def _():
        buf_ref[...] = jnp.zeros_like(buf_ref)
        start_fetches(0, 0)

    @pl.when(t + 1 < nt)
    def _():
        start_fetches(t + 1, (t + 1) % 2)

    onehot, rank, _, _ = _tile_positions(route_ref[...])
    own_rank = jnp.sum(onehot * rank, axis=1, keepdims=True)
    lane = lax.broadcasted_iota(jnp.int32, onehot.shape, 1)
    slot = t % 2

    def unsort(g, first, tokens):
        in_g = jnp.sum(jnp.where(lane == ROUTE_GROUP_OFF + g, onehot, 0.0), axis=1, keepdims=True)
        ri = (lax.broadcasted_iota(jnp.int32, (tm, tokens), 1) + first).astype(F32)
        sel = ((ri == own_rank) & (in_g > 0.5)).astype(BF16)
        r0 = pl.multiple_of(slot * slot_rows + (g * tm + first) * TOK_ROWS, TOK_ROWS)
        y = jnp.concatenate(
            [buf_ref[pl.ds(r0 + j, tokens, stride=TOK_ROWS), :] for j in range(D_MODEL // LANES)],
            axis=1).astype(BF16)
        return jnp.dot(sel, y, preferred_element_type=F32)

    acc = x1_ref[...]
    for g in range(N_GROUPS):
        @pl.when(lens_ref[t * N_GROUPS + g] > 0)
        def _(g=g):
            fetch(t, slot, g).wait()
        acc = acc + unsort(g, 0, RUN_SHORT)
    o_ref[...] = acc
    for g in range(N_GROUPS):
        @pl.when(lens_ref[t * N_GROUPS + g] > RUN_SHORT)
        def _(g=g):
            o_ref[...] += unsort(g, RUN_SHORT, tm - RUN_SHORT)


def _combine(starts, lens, x1, route, ys):
    n = x1.shape[0]
    nt = n // SORT_TM
    return pl.pallas_call(
        _combine_body,
        grid_spec=pltpu.PrefetchScalarGridSpec(
            num_scalar_prefetch=2,
            grid=(nt,),
            in_specs=[pl.BlockSpec((SORT_TM, D_MODEL), lambda i, s, c: (i, 0)),
                      pl.BlockSpec((SORT_TM, ROUTER_LANES), lambda i, s, c: (i, 0)),
                      pl.BlockSpec(memory_space=pl.ANY)],
            out_specs=pl.BlockSpec((SORT_TM, D_MODEL), lambda i, s, c: (i, 0)),
            scratch_shapes=[pltpu.VMEM((2 * N_GROUPS * SORT_TM * TOK_ROWS, LANES), F32),
                            pltpu.SemaphoreType.DMA((2, N_GROUPS))],
        ),
        out_shape=jax.ShapeDtypeStruct((n, D_MODEL), F32),
        compiler_params=_cparams(("arbitrary",)),
        name="moe_combine",
    )(starts, lens, x1, route, ys)


def _expert_tiles(counts, n):
    cap_blocks = _region_cap(n) // MOE_TM
    n_tiles = n // MOE_TM + N_GROUPS
    per_group = (counts + MOE_TM - 1) // MOE_TM
    ends = jnp.cumsum(per_group)
    total = ends[-1]
    t = jnp.minimum(jnp.arange(n_tiles, dtype=jnp.int32), total - 1)
    grp = jnp.sum((t[:, None] >= ends[None, :]).astype(jnp.int32), axis=1)
    first = ends - per_group
    blk = grp * cap_blocks + (t - first[grp])
    valid = (jnp.arange(n_tiles, dtype=jnp.int32) < total).astype(jnp.int32)
    return blk.astype(jnp.int32), grp.astype(jnp.int32), valid


def _moe(h2, route, x1, w_up, w_down):
    n = h2.shape[0]
    hs, starts, lens, counts = _sort(h2, route)
    blk, grp, valid = _expert_tiles(counts, n)
    ys = _experts(blk, grp, valid, hs, w_up.astype(BF16), w_down.astype(BF16))
    return _combine(starts, lens, x1, route, ys)


def kernel(x, norm_mix, w_in, ssm_lam_re, ssm_lam_im, ssm_log_dt, ssm_b_re, ssm_b_im, ssm_c_re,
           ssm_c_im, ssm_d, w_ssm_glu_val, w_ssm_glu_gate, q_norm, k_norm, attn_sinks, w_attn_up,
           w_out, norm_ffn, w_coarse, w_fine, w_expert_up, w_expert_down):
    bsz, seq, d = x.shape
    n = bsz * seq
    depth = w_in.shape[0]
    x2 = x.reshape(n, d)
    for i in range(depth):
        u, q, k, v, gs, ga = _proj(x2, norm_mix[i].reshape(1, d), w_in[i].astype(BF16))
        bblk, cblk, k0, lam_k, dsk = _s5_params(ssm_lam_re[i], ssm_lam_im[i], ssm_log_dt[i],
                                            ssm_b_re[i], ssm_b_im[i], ssm_c_re[i], ssm_c_im[i],
                                            ssm_d[i])
        z = _s5(u.reshape(bsz, seq, SSM_WIDTH), bblk, cblk, k0, lam_k, dsk)
        ya = _attn(q.reshape(bsz, seq, ATTN_WIDTH), k.reshape(bsz, seq, KV_WIDTH),
                   v.reshape(bsz, seq, KV_WIDTH), q_norm[i], k_norm[i], attn_sinks[i])
        wglu = jnp.concatenate([w_ssm_glu_val[i], w_ssm_glu_gate[i]], axis=1).astype(BF16)
        wr_t = jnp.concatenate(
            [w_coarse[i].T, w_fine[i].T,
             jnp.zeros((ROUTER_ROWS - N_GROUPS - N_EXPERTS, d), w_fine.dtype)], axis=0).astype(F32)
        wr_hi = wr_t.astype(BF16)
        wr = jnp.concatenate([wr_hi, (wr_t - wr_hi.astype(F32)).astype(BF16)], axis=0)
        x1, h2, gates = _mix(z.reshape(n, SSM_WIDTH), ya.reshape(n, ATTN_WIDTH), gs, ga, x2,
                             wglu, w_attn_up[i].astype(BF16), w_out[i].astype(BF16),
                             norm_ffn[i].reshape(1, d), wr)
        x2 = _moe(h2, gates, x1, w_expert_up[i], w_expert_down[i])
    return x2.reshape(bsz, seq, d)
```

```python
import functools
import math

import jax
import jax.numpy as jnp
from jax import lax
from jax.experimental import pallas as pl
from jax.experimental.pallas import tpu as pltpu

F32 = jnp.float32
BF16 = jnp.bfloat16

D_MODEL = 1024
SSM_WIDTH = 512
SSM_GROUP = 16
SSM_GROUPS = 32
SSM_STATE = 64
N_HEADS = 8
N_KV_HEADS = 2
HEAD_DIM = 64
ATTN_WIDTH = 512
KV_WIDTH = 128
BLOCK = 128
N_GROUPS = 4
EXPERTS_PER_GROUP = 4
N_EXPERTS = 16
D_EXPERT = 256
EPS = 1e-6
NEG = -1e30

LANES = 128
SLABS = SSM_WIDTH // LANES
GROUPS_PER_SLAB = LANES // SSM_GROUP
SLAB_STATE = GROUPS_PER_SLAB * SSM_STATE
ROUTER_LANES = 128
ROUTER_ROWS = 32
FINE_OFF = N_GROUPS
ROUTE_GROUP_OFF = 8

VMEM_LIMIT = 56 * 1024 * 1024


def _cparams(sem):
    return pltpu.CompilerParams(dimension_semantics=sem, vmem_limit_bytes=VMEM_LIMIT)


PROJ_TM = 1024


def _proj_body(x_ref, g_ref, w_ref, u_ref, q_ref, k_ref, v_ref, gs_ref, ga_ref):
    x = x_ref[...]
    ms = jnp.mean(x * x, axis=-1, keepdims=True)
    h = (x * lax.rsqrt(ms + EPS) * g_ref[...]).astype(BF16)

    def seg(a, b):
        return jnp.dot(h, w_ref[:, a:b], preferred_element_type=F32)

    o = 0
    u_ref[...] = seg(o, o + SSM_WIDTH)
    o += SSM_WIDTH
    q_ref[...] = seg(o, o + ATTN_WIDTH).astype(BF16)
    o += ATTN_WIDTH
    kv = seg(o, o + 2 * KV_WIDTH).astype(BF16)
    k_ref[...] = kv[:, :KV_WIDTH]
    v_ref[...] = kv[:, KV_WIDTH:]
    o += 2 * KV_WIDTH
    gs_ref[...] = jax.nn.sigmoid(seg(o, o + D_MODEL)).astype(BF16)
    o += D_MODEL
    ga_ref[...] = jax.nn.sigmoid(seg(o, o + D_MODEL)).astype(BF16)


def _layer_spec(layer, *shape):
    return pl.BlockSpec((None,) + shape, lambda *_: (layer,) + (0,) * len(shape))


def _proj(x2, g, w, layer):
    n = x2.shape[0]
    cols = w.shape[2]
    row = lambda c: pl.BlockSpec((PROJ_TM, c), lambda i: (i, 0))
    full = lambda a, b: pl.BlockSpec((a, b), lambda i: (0, 0))
    return pl.pallas_call(
        _proj_body,
        grid=(n // PROJ_TM,),
        in_specs=[row(D_MODEL), full(1, D_MODEL), _layer_spec(layer, D_MODEL, cols)],
        out_specs=[row(SSM_WIDTH), row(ATTN_WIDTH), row(KV_WIDTH), row(KV_WIDTH),
                   row(D_MODEL), row(D_MODEL)],
        out_shape=[jax.ShapeDtypeStruct((n, SSM_WIDTH), F32),
                   jax.ShapeDtypeStruct((n, ATTN_WIDTH), BF16),
                   jax.ShapeDtypeStruct((n, KV_WIDTH), BF16),
                   jax.ShapeDtypeStruct((n, KV_WIDTH), BF16),
                   jax.ShapeDtypeStruct((n, D_MODEL), BF16),
                   jax.ShapeDtypeStruct((n, D_MODEL), BF16)],
        compiler_params=_cparams(("arbitrary",)),
        name="proj",
    )(x2, g, w)


S5_TT = 64
S5_PAIRS = S5_TT // 2


def _s5_body(u_ref, bblk_ref, cblk_ref, k0_ref, lam_ref, d_ref, z_ref,
             ut_ref, y_ref, st_ref, carry_ref, *, bsz):
    rows = S5_TT * bsz
    prow = S5_PAIRS * bsz

    @pl.when(pl.program_id(0) == 0)
    def _():
        carry_ref[...] = jnp.zeros_like(carry_ref)

    for b in range(bsz):
        for j in range(SLABS):
            ut_ref[j, pl.ds(b, S5_TT, stride=bsz), :] = u_ref[b, :, j * LANES:(j + 1) * LANES]

    for j in range(SLABS):
        u3 = ut_ref[j].reshape(S5_PAIRS, 2 * bsz, LANES)
        u0 = u3[:, :bsz, :].reshape(prow, LANES)
        u1 = u3[:, bsz:, :].reshape(prow, LANES)
        lhs = jnp.concatenate([u1, u0], axis=1).astype(BF16)
        st_ref[j, 0:bsz, :] = carry_ref[j]
        st_ref[j, bsz:, :] = jnp.dot(lhs, bblk_ref[j], preferred_element_type=F32)
        a = jnp.broadcast_to(lam_ref[j, 0:1, :], (bsz, SLAB_STATE))
        bb = jnp.broadcast_to(lam_ref[j, 1:2, :], (bsz, SLAB_STATE))

        cre = carry_ref[j, :, 0:SLAB_STATE]
        cim = carry_ref[j, :, SLAB_STATE:2 * SLAB_STATE]
        for k in range(S5_PAIRS):
            r0 = (k + 1) * bsz
            bre = st_ref[j, r0:r0 + bsz, 0:SLAB_STATE]
            bim = st_ref[j, r0:r0 + bsz, SLAB_STATE:2 * SLAB_STATE]
            cre, cim = a * cre - bb * cim + bre, a * cim + bb * cre + bim
            st_ref[j, r0:r0 + bsz, 0:SLAB_STATE] = cre
            st_ref[j, r0:r0 + bsz, SLAB_STATE:2 * SLAB_STATE] = cim
        carry_ref[j, :, 0:SLAB_STATE] = cre
        carry_ref[j, :, SLAB_STATE:2 * SLAB_STATE] = cim

        cs = jnp.dot(st_ref[j].astype(BF16), cblk_ref[j], preferred_element_type=F32)
        y1 = cs[bsz:, :LANES] + d_ref[j] * u1
        y0 = (cs[:prow, LANES:] + jnp.dot(u0.astype(BF16), k0_ref[j], preferred_element_type=F32)
              + d_ref[j] * u0)
        y = jnp.concatenate([y0.reshape(S5_PAIRS, bsz, LANES), y1.reshape(S5_PAIRS, bsz, LANES)],
                            axis=1).reshape(rows, LANES)
        y_ref[j] = jax.nn.gelu(y)

    for b in range(bsz):
        for j in range(SLABS):
            z_ref[b, :, j * LANES:(j + 1) * LANES] = (
                y_ref[j, pl.ds(b, S5_TT, stride=bsz), :].astype(BF16))


def _s5(u3, bblk, cblk, k0, lam, dskip):
    bsz, seq, _ = u3.shape
    rows = S5_TT * bsz
    full = lambda *s: pl.BlockSpec(s, lambda i: (0,) * len(s))
    return pl.pallas_call(
        functools.partial(_s5_body, bsz=bsz),
        grid=(seq // S5_TT,),
        in_specs=[pl.BlockSpec((bsz, S5_TT, SSM_WIDTH), lambda i: (0, i, 0)),
                  full(SLABS, 2 * LANES, 2 * SLAB_STATE),
                  full(SLABS, 2 * SLAB_STATE, 2 * LANES),
                  full(SLABS, LANES, LANES),
                  full(SLABS, 2, SLAB_STATE),
                  full(SLABS, 1, LANES)],
        out_specs=pl.BlockSpec((bsz, S5_TT, SSM_WIDTH), lambda i: (0, i, 0)),
        out_shape=jax.ShapeDtypeStruct((bsz, seq, SSM_WIDTH), BF16),
        scratch_shapes=[pltpu.VMEM((SLABS, rows, LANES), F32),
                        pltpu.VMEM((SLABS, rows, LANES), F32),
                        pltpu.VMEM((SLABS, bsz + S5_PAIRS * bsz, 2 * SLAB_STATE), F32),
                        pltpu.VMEM((SLABS, bsz, 2 * SLAB_STATE), F32)],
        compiler_params=_cparams(("arbitrary",)),
        name="s5",
    )(u3, bblk, cblk, k0, lam, dskip)


def _s5_params(lam_re, lam_im, log_dt, b_re, b_im, c_re, c_im, d_skip):
    lam = lax.complex(lam_re.astype(F32), lam_im.astype(F32))
    dt = jnp.exp(log_dt.astype(F32))[:, None]
    lam_bar = jnp.exp(lam * dt)
    b = lax.complex(b_re.astype(F32), b_im.astype(F32))
    b_bar = ((lam_bar - 1.0) / lam)[..., None] * b
    eye = jnp.eye(GROUPS_PER_SLAB, dtype=F32)

    def in_blk(part):
        p4 = part.reshape(SLABS, GROUPS_PER_SLAB, SSM_STATE, SSM_GROUP)
        m = jnp.einsum('jgph,gk->jghkp', p4, eye)
        return m.reshape(SLABS, LANES, SLAB_STATE)

    def in_cplx(z):
        return jnp.concatenate([in_blk(z.real), in_blk(z.imag)], axis=-1)

    bblk = jnp.concatenate([in_cplx(b_bar), in_cplx(lam_bar[..., None] * b_bar)],
                           axis=1).astype(BF16)

    def out_blk(part):
        p4 = part.reshape(SLABS, GROUPS_PER_SLAB, SSM_GROUP, SSM_STATE)
        m = jnp.einsum('jghp,gk->jgpkh', p4, eye)
        return m.reshape(SLABS, SLAB_STATE, LANES)

    def out_cplx(z):
        return jnp.concatenate([out_blk(z.real), -out_blk(z.imag)], axis=1)

    c = lax.complex(c_re.astype(F32), c_im.astype(F32))
    cblk = jnp.concatenate([out_cplx(c), out_cplx(c * lam_bar[:, None, :])], axis=-1).astype(BF16)
    k0 = jnp.einsum('ghp,gpk->gkh', c, b_bar).real.reshape(SLABS, GROUPS_PER_SLAB, SSM_GROUP,
                                                           SSM_GROUP)
    k0 = jnp.einsum('jgkh,gq->jgkqh', k0, eye).reshape(SLABS, LANES, LANES).astype(BF16)
    lam2 = lam_bar * lam_bar
    lam_k = jnp.stack([lam2.real.reshape(SLABS, SLAB_STATE),
                       lam2.imag.reshape(SLABS, SLAB_STATE)], axis=1)
    dsk = d_skip.astype(F32).reshape(SLABS, 1, LANES)
    return bblk, cblk, k0, lam_k, dsk


ATTN_QB = 4
HEADS_PER_TILE = LANES // HEAD_DIM


def _pair_norm(t, gain2, head_mean):
    sq = t * t
    hi = sq.astype(BF16)
    lo = (sq - hi.astype(F32)).astype(BF16)
    ms = (jnp.dot(hi, head_mean, preferred_element_type=F32)
          + jnp.dot(lo, head_mean, preferred_element_type=F32))
    return t * lax.rsqrt(ms + EPS) * gain2


def _attn_body(sink_ref, q_ref, kc_ref, kp_ref, vc_ref, vp_ref, qg_ref, kg_ref, o_ref):
    n = pl.program_id(1)
    lane = lax.broadcasted_iota(jnp.int32, (1, LANES), 1)
    lo = lane < HEAD_DIM
    li = lax.broadcasted_iota(jnp.int32, (LANES, LANES), 0)
    lj = lax.broadcasted_iota(jnp.int32, (LANES, LANES), 1)
    head_mean = jnp.where((li < HEAD_DIM) == (lj < HEAD_DIM), 1.0 / HEAD_DIM, 0.0).astype(BF16)
    ones = jnp.ones((LANES, LANES), BF16)

    kall = jnp.concatenate([kp_ref[0], kc_ref[0]], axis=0).astype(F32)
    kn = _pair_norm(kall, kg_ref[...], head_mean)
    kroll = pltpu.roll(kn, HEAD_DIM, axis=1)
    vall = jnp.concatenate([vp_ref[0], vc_ref[0]], axis=0).astype(F32)
    vroll = pltpu.roll(vall, HEAD_DIM, axis=1)
    zero = jnp.zeros_like(kn)
    k_var = [[jnp.where(lo, kn, zero).astype(BF16), jnp.where(lo, zero, kroll).astype(BF16)],
             [jnp.where(lo, kroll, zero).astype(BF16), jnp.where(lo, zero, kn).astype(BF16)]]
    v_var = [[jnp.where(lo, vall, zero).astype(BF16), jnp.where(lo, zero, vroll).astype(BF16)],
             [jnp.where(lo, vroll, zero).astype(BF16), jnp.where(lo, zero, vall).astype(BF16)]]

    qn = []
    for i in range(ATTN_WIDTH // LANES):
        t = _pair_norm(q_ref[0, :, i * LANES:(i + 1) * LANES].astype(F32), qg_ref[...], head_mean)
        qn.append((t * (HEAD_DIM ** -0.5)).astype(BF16))

    qi = lax.broadcasted_iota(jnp.int32, (BLOCK, BLOCK), 0)
    si = lax.broadcasted_iota(jnp.int32, (BLOCK, BLOCK), 1)
    is_cur = si <= qi
    dist = jnp.where(is_cur, qi - si, qi - si + BLOCK).astype(F32)
    alibi = [(2.0 ** (-8.0 * (h + 1) / N_HEADS)) * dist for h in range(N_HEADS)]
    sinkmat = jnp.concatenate([jnp.full((BLOCK, LANES), sink_ref[h], F32) for h in range(N_HEADS)],
                              axis=0)

    for b in range(ATTN_QB):
        rows = slice(b * BLOCK, (b + 1) * BLOCK)
        win = slice(b * BLOCK, (b + 2) * BLOCK)
        s_all = []
        for h in range(N_HEADS):
            i, slot = divmod(h, HEADS_PER_TILE)
            kvh = h // (N_HEADS // N_KV_HEADS)
            sc = lax.dot_general(qn[i][rows], k_var[kvh][slot][win], (((1,), (1,)), ((), ())),
                                 preferred_element_type=F32)
            prev = sc[:, :BLOCK]
            if b == 0:
                prev = jnp.where(n > 0, prev, NEG)
            s_all.append(jnp.where(is_cur, sc[:, BLOCK:], prev) - alibi[h])
        s = jnp.concatenate(s_all, axis=0)
        m = jnp.maximum(jnp.max(s, axis=-1, keepdims=True), sinkmat)
        p = jnp.exp(s - m)
        denom = jnp.dot(p.astype(BF16), ones, preferred_element_type=F32) + jnp.exp(sinkmat - m)
        pn = p / denom
        for i in range(ATTN_WIDTH // LANES):
            acc = jnp.zeros((BLOCK, LANES), F32)
            for slot in range(HEADS_PER_TILE):
                h = i * HEADS_PER_TILE + slot
                kvh = h // (N_HEADS // N_KV_HEADS)
                ph = pn[h * BLOCK:(h + 1) * BLOCK]
                pcat = jnp.concatenate([jnp.where(is_cur, 0.0, ph), jnp.where(is_cur, ph, 0.0)],
                                       axis=1).astype(BF16)
                acc = acc + jnp.dot(pcat, v_var[kvh][slot][win], preferred_element_type=F32)
            o_ref[0, rows, i * LANES:(i + 1) * LANES] = acc.astype(BF16)


def _attn(q3, k3, v3, q_gain, k_gain, sinks):
    bsz, seq, _ = q3.shape
    qrows = ATTN_QB * BLOCK
    cur = lambda b, n: (b, n, 0)
    prev = lambda b, n: (b, jnp.maximum(n * ATTN_QB - 1, 0), 0)
    qg2 = jnp.tile(q_gain.astype(F32), HEADS_PER_TILE).reshape(1, LANES)
    kg2 = jnp.tile(k_gain.astype(F32), HEADS_PER_TILE).reshape(1, LANES)
    gain_spec = pl.BlockSpec((1, LANES), lambda b, n: (0, 0))
    return pl.pallas_call(
        _attn_body,
        grid=(bsz, seq // qrows),
        in_specs=[pl.BlockSpec(memory_space=pltpu.SMEM),
                  pl.BlockSpec((1, qrows, ATTN_WIDTH), cur),
                  pl.BlockSpec((1, qrows, KV_WIDTH), cur),
                  pl.BlockSpec((1, BLOCK, KV_WIDTH), prev),
                  pl.BlockSpec((1, qrows, KV_WIDTH), cur),
                  pl.BlockSpec((1, BLOCK, KV_WIDTH), prev),
                  gain_spec, gain_spec],
        out_specs=pl.BlockSpec((1, qrows, ATTN_WIDTH), cur),
        out_shape=jax.ShapeDtypeStruct((bsz, seq, ATTN_WIDTH), BF16),
        compiler_params=_cparams(("arbitrary", "arbitrary")),
        name="attn",
    )(sinks.astype(F32), q3, k3, k3, v3, v3, qg2, kg2)


MIX_TM = 1024


def _mix_body(z_ref, ya_ref, gs_ref, ga_ref, x_ref, wglu_ref, wup_ref, wout_ref, g2_ref, wr_ref,
              x1_ref, h2_ref, gate_ref):
    z = z_ref[...]
    bv = jnp.dot(z, wglu_ref[:, :D_MODEL], preferred_element_type=F32)
    bg = jnp.dot(z, wglu_ref[:, D_MODEL:], preferred_element_type=F32)
    bs = bv * jax.nn.sigmoid(bg)
    ba = jnp.dot(ya_ref[...], wup_ref[...], preferred_element_type=F32)
    merged = gs_ref[...].astype(F32) * bs + ga_ref[...].astype(F32) * ba
    x1 = x_ref[...] + jnp.dot(merged.astype(BF16), wout_ref[...], preferred_element_type=F32)
    x1_ref[...] = x1

    ms = jnp.mean(x1 * x1, axis=-1, keepdims=True)
    h2 = x1 * lax.rsqrt(ms + EPS) * g2_ref[...]
    h2b = h2.astype(BF16)
    h2_ref[...] = h2b

    lt = lax.dot_general(wr_ref[...], h2b, (((1,), (1,)), ((), ())),
                         preferred_element_type=F32)
    logits = lt[:ROUTER_ROWS] + lt[ROUTER_ROWS:]
    row = lax.broadcasted_iota(jnp.int32, logits.shape, 0)
    ninf = -jnp.inf
    cm = jnp.where(row < N_GROUPS, logits, ninf)
    cmax = jnp.max(cm, axis=0, keepdims=True)
    g_prob = 1.0 / jnp.sum(jnp.exp(cm - cmax), axis=0, keepdims=True)
    g_idx = jnp.min(jnp.where(cm == cmax, row, ROUTER_ROWS), axis=0, keepdims=True)
    f0 = FINE_OFF + EXPERTS_PER_GROUP * g_idx
    fm = jnp.where((row >= f0) & (row < f0 + EXPERTS_PER_GROUP), logits, ninf)
    v1 = jnp.max(fm, axis=0, keepdims=True)
    i1 = jnp.min(jnp.where(fm == v1, row, ROUTER_ROWS), axis=0, keepdims=True)
    fm2 = jnp.where(row == i1, ninf, fm)
    v2 = jnp.max(fm2, axis=0, keepdims=True)
    i2 = jnp.min(jnp.where(fm2 == v2, row, ROUTER_ROWS), axis=0, keepdims=True)
    e21 = jnp.exp(v2 - v1)
    w1 = g_prob / (1.0 + e21)
    w2 = w1 * e21
    rt = jnp.where(row == i1 - f0, w1,
                   jnp.where(row == i2 - f0, w2,
                             jnp.where(row == ROUTE_GROUP_OFF + g_idx, 1.0, 0.0)))
    rt = jnp.concatenate([rt, jnp.zeros((ROUTER_LANES - ROUTER_ROWS, rt.shape[1]), F32)], axis=0)
    gate_ref[...] = rt.T


def _mix(z2, ya2, gs2, ga2, x2, wglu, wup, wout, g2, wr, layer):
    n = x2.shape[0]
    row = lambda c: pl.BlockSpec((MIX_TM, c), lambda i: (i, 0))
    full = lambda a, b: pl.BlockSpec((a, b), lambda i: (0, 0))
    return pl.pallas_call(
        _mix_body,
        grid=(n // MIX_TM,),
        in_specs=[row(SSM_WIDTH), row(ATTN_WIDTH), row(D_MODEL), row(D_MODEL), row(D_MODEL),
                  _layer_spec(layer, SSM_WIDTH, 2 * D_MODEL), _layer_spec(layer, ATTN_WIDTH, D_MODEL),
                  _layer_spec(layer, D_MODEL, D_MODEL), full(1, D_MODEL),
                  _layer_spec(layer, 2 * ROUTER_ROWS, D_MODEL)],
        out_specs=[row(D_MODEL), row(D_MODEL), row(ROUTER_LANES)],
        out_shape=[jax.ShapeDtypeStruct((n, D_MODEL), F32),
                   jax.ShapeDtypeStruct((n, D_MODEL), BF16),
                   jax.ShapeDtypeStruct((n, ROUTER_LANES), F32)],
        compiler_params=_cparams(("arbitrary",)),
        name="mix",
    )(z2, ya2, gs2, ga2, x2, wglu, wup, wout, g2, wr)


SORT_TM = 256
RUN_SHORT = 128
MOE_TM = 512
TOK_ROWS = 8
H_WORDS = D_MODEL // 2
H_CHUNKS = H_WORDS // LANES
ROUTE_ROW = H_CHUNKS
ZFILL_TOK = MOE_TM
HI_MASK = 0xFFFF0000


def _region_cap(n):
    cap = n + ZFILL_TOK
    return -(-cap // MOE_TM) * MOE_TM


def _tile_positions(route):
    tm = route.shape[0]
    lane = lax.broadcasted_iota(jnp.int32, route.shape, 1)
    onehot = jnp.where((lane >= ROUTE_GROUP_OFF) & (lane < ROUTE_GROUP_OFF + N_GROUPS), route, 0.0)
    ci = lax.broadcasted_iota(jnp.int32, (tm, tm), 0)
    cj = lax.broadcasted_iota(jnp.int32, (tm, tm), 1)
    earlier = (cj < ci).astype(BF16)
    rank = jnp.dot(earlier, onehot.astype(BF16), preferred_element_type=F32)
    cnt = jnp.sum(onehot, axis=0, keepdims=True)
    li = lax.broadcasted_iota(jnp.int32, (LANES, LANES), 0)
    lj = lax.broadcasted_iota(jnp.int32, (LANES, LANES), 1)
    base = jnp.dot(jnp.broadcast_to(cnt, (8, LANES)).astype(BF16), (li < lj).astype(BF16),
                   preferred_element_type=F32)[0:1]
    return onehot, rank, base, cnt


def _sort_body(h_ref, route_ref, hs_hbm, starts_ref, lens_ref, counts_ref,
               stage_ref, run_ref, pend_ref, sem):
    t = pl.program_id(0)
    nt = pl.num_programs(0)
    tm = SORT_TM
    slot = t % 2
    slot_rows = 2 * tm * TOK_ROWS
    cap = hs_hbm.shape[0] // (N_GROUPS * TOK_ROWS)

    def run_copy(src_row, dst_row, sl, g, tokens):
        return pltpu.make_async_copy(stage_ref.at[pl.ds(src_row, tokens * TOK_ROWS), :],
                                     hs_hbm.at[pl.ds(dst_row, tokens * TOK_ROWS), :],
                                     sem.at[sl, g])

    def wait_slot(sl):
        for g in range(N_GROUPS):
            n = pend_ref[sl * N_GROUPS + g]

            @pl.when(n > 0)
            def _(g=g, n=n):
                run_copy(0, 0, sl, g, n).wait()

    @pl.when(t == 0)
    def _():
        stage_ref[...] = jnp.zeros_like(stage_ref)
        for g in range(N_GROUPS):
            run_ref[g] = 0

    @pl.when(t >= 2)
    def _():
        wait_slot(slot)

    route = route_ref[...]
    onehot, rank, base, cnt = _tile_positions(route)
    w = (onehot * (rank + base)).astype(BF16)
    pos_row = lax.dot_general(jnp.ones((8, LANES), BF16), w, (((1,), (1,)), ((), ())),
                              preferred_element_type=F32)[0:1]
    ri = lax.broadcasted_iota(jnp.int32, (tm, tm), 0).astype(F32)
    perm = (ri == pos_row).astype(BF16)
    r_hi = route.astype(BF16)
    r_mid = (route - r_hi.astype(F32)).astype(BF16)
    r_lo = (route - r_hi.astype(F32) - r_mid.astype(F32)).astype(BF16)
    sorted_all = jnp.dot(perm, jnp.concatenate([h_ref[...], r_hi, r_mid, r_lo], axis=1),
                         preferred_element_type=F32)
    sh = sorted_all[:, :D_MODEL]
    sr = (sorted_all[:, D_MODEL:D_MODEL + LANES] + sorted_all[:, D_MODEL + LANES:D_MODEL + 2 * LANES]
          + sorted_all[:, D_MODEL + 2 * LANES:])
    hb = lax.bitcast_convert_type(sh, jnp.uint32)
    words = (hb[:, :H_WORDS] & jnp.uint32(HI_MASK)) | (hb[:, H_WORDS:] >> 16)

    slot0 = pl.multiple_of(slot * slot_rows, TOK_ROWS)
    for s in range(H_CHUNKS):
        stage_ref[pl.ds(slot0 + s, tm, stride=TOK_ROWS), :] = words[:, s * LANES:(s + 1) * LANES]
    stage_ref[pl.ds(slot0 + ROUTE_ROW, tm, stride=TOK_ROWS), :] = lax.bitcast_convert_type(
        sr, jnp.uint32)

    for g in range(N_GROUPS):
        c_g = cnt[0, ROUTE_GROUP_OFF + g].astype(jnp.int32)
        b_g = base[0, ROUTE_GROUP_OFF + g].astype(jnp.int32)
        start = run_ref[g]
        starts_ref[t * N_GROUPS + g] = start
        lens_ref[t * N_GROUPS + g] = c_g
        pend_ref[slot * N_GROUPS + g] = c_g
        run_ref[g] = start + c_g
        src = pl.multiple_of(slot0 + b_g * TOK_ROWS, TOK_ROWS)
        dst = pl.multiple_of((g * cap + start) * TOK_ROWS, TOK_ROWS)

        @pl.when(c_g > 0)
        def _(g=g, src=src, dst=dst, c_g=c_g):
            run_copy(src, dst, slot, g, c_g).start()

    @pl.when(t == nt - 1)
    def _():
        @pl.when(t >= 1)
        def _():
            wait_slot(1 - slot)
        wait_slot(slot)
        stage_ref[...] = jnp.zeros_like(stage_ref)
        zrows = ZFILL_TOK * TOK_ROWS
        for g in range(N_GROUPS):
            total = run_ref[g]
            counts_ref[g] = total
            dst = pl.multiple_of((g * cap + total) * TOK_ROWS, TOK_ROWS)
            pltpu.make_async_copy(stage_ref.at[pl.ds(0, zrows), :],
                                  hs_hbm.at[pl.ds(dst, zrows), :], sem.at[0, g]).start()
        for g in range(N_GROUPS):
            pltpu.make_async_copy(stage_ref.at[pl.ds(0, zrows), :],
                                  hs_hbm.at[pl.ds(0, zrows), :], sem.at[0, g]).wait()


def _sort(h2, route):
    n = h2.shape[0]
    nt = n // SORT_TM
    cap = _region_cap(n)
    assert 4 * SORT_TM >= ZFILL_TOK
    smem = pl.BlockSpec(memory_space=pltpu.SMEM)
    return pl.pallas_call(
        _sort_body,
        grid=(nt,),
        in_specs=[pl.BlockSpec((SORT_TM, D_MODEL), lambda i: (i, 0)),
                  pl.BlockSpec((SORT_TM, ROUTER_LANES), lambda i: (i, 0))],
        out_specs=[pl.BlockSpec(memory_space=pl.ANY), smem, smem, smem],
        out_shape=[jax.ShapeDtypeStruct((N_GROUPS * cap * TOK_ROWS, LANES), jnp.uint32),
                   jax.ShapeDtypeStruct((nt * N_GROUPS,), jnp.int32),
                   jax.ShapeDtypeStruct((nt * N_GROUPS,), jnp.int32),
                   jax.ShapeDtypeStruct((N_GROUPS,), jnp.int32)],
        scratch_shapes=[pltpu.VMEM((4 * SORT_TM * TOK_ROWS, LANES), jnp.uint32),
                        pltpu.SMEM((N_GROUPS,), jnp.int32),
                        pltpu.SMEM((2 * N_GROUPS,), jnp.int32),
                        pltpu.SemaphoreType.DMA((2, N_GROUPS))],
        compiler_params=_cparams(("arbitrary",)),
        name="moe_sort",
    )(h2, route)


def _experts_body(blk_ref, grp_ref, valid_ref, hs_ref, wu32_ref, wd32_ref, ys_ref, wu_ref, wd_ref):
    t = pl.program_id(0)
    tm = MOE_TM

    @pl.when((t == 0) | (grp_ref[t] != grp_ref[jnp.maximum(t - 1, 0)]))
    def _():
        wu_ref[...] = wu32_ref[...].astype(BF16)
        wd_ref[...] = wd32_ref[...].astype(BF16)

    @pl.when(valid_ref[t] > 0)
    def _():
        chunks = [hs_ref[pl.ds(s, tm, stride=TOK_ROWS), :] for s in range(H_CHUNKS + 1)]
        hi = [lax.bitcast_convert_type(c & jnp.uint32(HI_MASK), F32) for c in chunks[:H_CHUNKS]]
        lo = [lax.bitcast_convert_type(c << 16, F32) for c in chunks[:H_CHUNKS]]
        h = jnp.concatenate(hi + lo, axis=1).astype(BF16)
        route = lax.bitcast_convert_type(chunks[ROUTE_ROW], F32)
        acts = []
        for e in range(EXPERTS_PER_GROUP):
            gu = jnp.dot(h, wu_ref[e], preferred_element_type=F32)
            acts.append((jax.nn.silu(gu[:, :D_EXPERT]) * gu[:, D_EXPERT:]
                         * route[:, e:e + 1]).astype(BF16))
        act = jnp.concatenate(acts, axis=1)
        wd = wd_ref[...].reshape(EXPERTS_PER_GROUP * D_EXPERT, D_MODEL)
        out = jnp.dot(act, wd, preferred_element_type=F32)
        for j in range(D_MODEL // LANES):
            ys_ref[pl.ds(j, tm, stride=TOK_ROWS), :] = out[:, j * LANES:(j + 1) * LANES]


def _experts(blk, grp, valid, hs, wu, wd, layer):
    g0 = layer * N_GROUPS
    n_tiles = blk.shape[0]
    epg = EXPERTS_PER_GROUP
    return pl.pallas_call(
        _experts_body,
        grid_spec=pltpu.PrefetchScalarGridSpec(
            num_scalar_prefetch=3,
            grid=(n_tiles,),
            in_specs=[pl.BlockSpec((MOE_TM * TOK_ROWS, LANES), lambda t, b, g, v: (b[t], 0)),
                      pl.BlockSpec((epg, D_MODEL, 2 * D_EXPERT),
                                   lambda t, b, g, v: (g0 + g[t], 0, 0)),
                      pl.BlockSpec((epg, D_EXPERT, D_MODEL),
                                   lambda t, b, g, v: (g0 + g[t], 0, 0))],
            out_specs=pl.BlockSpec((MOE_TM * TOK_ROWS, LANES), lambda t, b, g, v: (b[t], 0)),
            scratch_shapes=[pltpu.VMEM((epg, D_MODEL, 2 * D_EXPERT), BF16),
                            pltpu.VMEM((epg, D_EXPERT, D_MODEL), BF16)],
        ),
        out_shape=jax.ShapeDtypeStruct(hs.shape, F32),
        compiler_params=_cparams(("arbitrary",)),
        name="moe_experts",
    )(blk, grp, valid, hs, wu, wd)


def _combine_body(starts_ref, lens_ref, x1_ref, route_ref, ys_hbm, o_ref, buf_ref, sem):
    t = pl.program_id(0)
    nt = pl.num_programs(0)
    tm = SORT_TM
    blk_rows = tm * TOK_ROWS
    slot_rows = N_GROUPS * blk_rows
    cap = ys_hbm.shape[0] // (N_GROUPS * TOK_ROWS)

    def fetch(step, slot, g):
        tokens = lens_ref[step * N_GROUPS + g]
        src = pl.multiple_of((g * cap + starts_ref[step * N_GROUPS + g]) * TOK_ROWS, TOK_ROWS)
        dst = pl.multiple_of(slot * slot_rows + g * blk_rows, TOK_ROWS)
        return pltpu.make_async_copy(ys_hbm.at[pl.ds(src, tokens * TOK_ROWS), :],
                                     buf_ref.at[pl.ds(dst, tokens * TOK_ROWS), :], sem.at[slot, g])

    def start_fetches(step, slot):
        for g in range(N_GROUPS):
            @pl.when(lens_ref[step * N_GROUPS + g] > 0)
            def _(g=g):
                fetch(step, slot, g).start()

    @pl.when(t == 0)
    def _():
        buf_ref[...] = jnp.zeros_like(buf_ref)
        start_fetches(0, 0)

    @pl.when(t + 1 < nt)
    def _():
        start_fetches(t + 1, (t + 1) % 2)

    onehot, rank, _, _ = _tile_positions(route_ref[...])
    own_rank = jnp.sum(onehot * rank, axis=1, keepdims=True)
    lane = lax.broadcasted_iota(jnp.int32, onehot.shape, 1)
    slot = t % 2

    def unsort(g, first, tokens):
        in_g = jnp.sum(jnp.where(lane == ROUTE_GROUP_OFF + g, onehot, 0.0), axis=1, keepdims=True)
        ri = (lax.broadcasted_iota(jnp.int32, (tm, tokens), 1) + first).astype(F32)
        sel = ((ri == own_rank) & (in_g > 0.5)).astype(BF16)
        r0 = pl.multiple_of(slot * slot_rows + (g * tm + first) * TOK_ROWS, TOK_ROWS)
        y = jnp.concatenate(
            [buf_ref[pl.ds(r0 + j, tokens, stride=TOK_ROWS), :] for j in range(D_MODEL // LANES)],
            axis=1).astype(BF16)
        return jnp.dot(sel, y, preferred_element_type=F32)

    acc = x1_ref[...]
    for g in range(N_GROUPS):
        @pl.when(lens_ref[t * N_GROUPS + g] > 0)
        def _(g=g):
            fetch(t, slot, g).wait()
        acc = acc + unsort(g, 0, RUN_SHORT)
    o_ref[...] = acc
    for g in range(N_GROUPS):
        @pl.when(lens_ref[t * N_GROUPS + g] > RUN_SHORT)
        def _(g=g):
            o_ref[...] += unsort(g, RUN_SHORT, tm - RUN_SHORT)


def _combine(starts, lens, x1, route, ys):
    n = x1.shape[0]
    nt = n // SORT_TM
    return pl.pallas_call(
        _combine_body,
        grid_spec=pltpu.PrefetchScalarGridSpec(
            num_scalar_prefetch=2,
            grid=(nt,),
            in_specs=[pl.BlockSpec((SORT_TM, D_MODEL), lambda i, s, c: (i, 0)),
                      pl.BlockSpec((SORT_TM, ROUTER_LANES), lambda i, s, c: (i, 0)),
                      pl.BlockSpec(memory_space=pl.ANY)],
            out_specs=pl.BlockSpec((SORT_TM, D_MODEL), lambda i, s, c: (i, 0)),
            scratch_shapes=[pltpu.VMEM((2 * N_GROUPS * SORT_TM * TOK_ROWS, LANES), F32),
                            pltpu.SemaphoreType.DMA((2, N_GROUPS))],
        ),
        out_shape=jax.ShapeDtypeStruct((n, D_MODEL), F32),
        compiler_params=_cparams(("arbitrary",)),
        name="moe_combine",
    )(starts, lens, x1, route, ys)


def _expert_tiles(counts, n):
    cap_blocks = _region_cap(n) // MOE_TM
    n_tiles = n // MOE_TM + N_GROUPS
    per_group = (counts + MOE_TM - 1) // MOE_TM
    ends = jnp.cumsum(per_group)
    total = ends[-1]
    t = jnp.minimum(jnp.arange(n_tiles, dtype=jnp.int32), total - 1)
    grp = jnp.sum((t[:, None] >= ends[None, :]).astype(jnp.int32), axis=1)
    first = ends - per_group
    blk = grp * cap_blocks + (t - first[grp])
    valid = (jnp.arange(n_tiles, dtype=jnp.int32) < total).astype(jnp.int32)
    return blk.astype(jnp.int32), grp.astype(jnp.int32), valid


def _moe(h2, route, x1, w_up, w_down, layer):
    n = h2.shape[0]
    hs, starts, lens, counts = _sort(h2, route)
    blk, grp, valid = _expert_tiles(counts, n)
    ys = _experts(blk, grp, valid, hs, w_up, w_down, layer)
    return _combine(starts, lens, x1, route, ys)


def kernel(x, norm_mix, w_in, ssm_lam_re, ssm_lam_im, ssm_log_dt, ssm_b_re, ssm_b_im, ssm_c_re,
           ssm_c_im, ssm_d, w_ssm_glu_val, w_ssm_glu_gate, q_norm, k_norm, attn_sinks, w_attn_up,
           w_out, norm_ffn, w_coarse, w_fine, w_expert_up, w_expert_down):
    bsz, seq, d = x.shape
    n = bsz * seq
    depth = w_in.shape[0]
    x2 = x.reshape(n, d)
    s5p = jax.vmap(_s5_params)(ssm_lam_re, ssm_lam_im, ssm_log_dt, ssm_b_re, ssm_b_im, ssm_c_re,
                               ssm_c_im, ssm_d)
    w_in_b = w_in.astype(BF16)
    wglu = jnp.concatenate([w_ssm_glu_val, w_ssm_glu_gate], axis=2).astype(BF16)
    wup_b = w_attn_up.astype(BF16)
    wout_b = w_out.astype(BF16)
    wr_t = jnp.concatenate(
        [jnp.swapaxes(w_coarse, 1, 2), jnp.swapaxes(w_fine, 1, 2),
         jnp.zeros((depth, ROUTER_ROWS - N_GROUPS - N_EXPERTS, d), w_fine.dtype)], axis=1).astype(F32)
    wr_hi = wr_t.astype(BF16)
    wr = jnp.concatenate([wr_hi, (wr_t - wr_hi.astype(F32)).astype(BF16)], axis=1)
    wu_all = w_expert_up.reshape((depth * N_EXPERTS,) + w_expert_up.shape[2:])
    wd_all = w_expert_down.reshape((depth * N_EXPERTS,) + w_expert_down.shape[2:])
    for i in range(depth):
        u, q, k, v, gs, ga = _proj(x2, norm_mix[i].reshape(1, d), w_in_b, i)
        z = _s5(u.reshape(bsz, seq, SSM_WIDTH), *(p[i] for p in s5p))
        ya = _attn(q.reshape(bsz, seq, ATTN_WIDTH), k.reshape(bsz, seq, KV_WIDTH),
                   v.reshape(bsz, seq, KV_WIDTH), q_norm[i], k_norm[i], attn_sinks[i])
        x1, h2, gates = _mix(z.reshape(n, SSM_WIDTH), ya.reshape(n, ATTN_WIDTH), gs, ga, x2,
                             wglu, wup_b, wout_b, norm_ffn[i].reshape(1, d), wr, i)
        x2 = _moe(h2, gates, x1, wu_all, wd_all, i)
    return x2.reshape(bsz, seq, d)
```

```python
import functools
import math

import jax
import jax.numpy as jnp
from jax import lax
from jax.experimental import pallas as pl
from jax.experimental.pallas import tpu as pltpu

F32 = jnp.float32
BF16 = jnp.bfloat16

D_MODEL = 1024
SSM_WIDTH = 512
SSM_GROUP = 16
SSM_GROUPS = 32
SSM_STATE = 64
N_HEADS = 8
N_KV_HEADS = 2
HEAD_DIM = 64
ATTN_WIDTH = 512
KV_WIDTH = 128
BLOCK = 128
N_GROUPS = 4
EXPERTS_PER_GROUP = 4
N_EXPERTS = 16
D_EXPERT = 256
EPS = 1e-6
NEG = -1e30

LANES = 128
SLABS = SSM_WIDTH // LANES
GROUPS_PER_SLAB = LANES // SSM_GROUP
SLAB_STATE = GROUPS_PER_SLAB * SSM_STATE
ROUTER_LANES = 128
ROUTER_ROWS = 32
FINE_OFF = N_GROUPS
ROUTE_GROUP_OFF = 8

VMEM_LIMIT = 56 * 1024 * 1024


def _cparams(sem):
    return pltpu.CompilerParams(dimension_semantics=sem, vmem_limit_bytes=VMEM_LIMIT)


PROJ_TM = 1024


def _proj_body(x_ref, g_ref, w_ref, u_ref, q_ref, k_ref, v_ref, gs_ref, ga_ref):
    x = x_ref[...]
    ms = jnp.mean(x * x, axis=-1, keepdims=True)
    h = (x * lax.rsqrt(ms + EPS) * g_ref[...]).astype(BF16)

    def seg(a, b):
        return jnp.dot(h, w_ref[:, a:b], preferred_element_type=F32)

    o = 0
    u_ref[...] = seg(o, o + SSM_WIDTH)
    o += SSM_WIDTH
    q_ref[...] = seg(o, o + ATTN_WIDTH).astype(BF16)
    o += ATTN_WIDTH
    kv = seg(o, o + 2 * KV_WIDTH).astype(BF16)
    k_ref[...] = kv[:, :KV_WIDTH]
    v_ref[...] = kv[:, KV_WIDTH:]
    o += 2 * KV_WIDTH
    gs_ref[...] = jax.nn.sigmoid(seg(o, o + D_MODEL)).astype(BF16)
    o += D_MODEL
    ga_ref[...] = jax.nn.sigmoid(seg(o, o + D_MODEL)).astype(BF16)


def _layer_spec(layer, *shape):
    return pl.BlockSpec((None,) + shape, lambda *_: (layer,) + (0,) * len(shape))


def _proj(x2, g, w, layer):
    n = x2.shape[0]
    cols = w.shape[2]
    row = lambda c: pl.BlockSpec((PROJ_TM, c), lambda i: (i, 0))
    full = lambda a, b: pl.BlockSpec((a, b), lambda i: (0, 0))
    return pl.pallas_call(
        _proj_body,
        grid=(n // PROJ_TM,),
        in_specs=[row(D_MODEL), full(1, D_MODEL), _layer_spec(layer, D_MODEL, cols)],
        out_specs=[row(SSM_WIDTH), row(ATTN_WIDTH), row(KV_WIDTH), row(KV_WIDTH),
                   row(D_MODEL), row(D_MODEL)],
        out_shape=[jax.ShapeDtypeStruct((n, SSM_WIDTH), F32),
                   jax.ShapeDtypeStruct((n, ATTN_WIDTH), BF16),
                   jax.ShapeDtypeStruct((n, KV_WIDTH), BF16),
                   jax.ShapeDtypeStruct((n, KV_WIDTH), BF16),
                   jax.ShapeDtypeStruct((n, D_MODEL), BF16),
                   jax.ShapeDtypeStruct((n, D_MODEL), BF16)],
        compiler_params=_cparams(("arbitrary",)),
        name="proj",
    )(x2, g, w)


S5_TT = 64
S5_PAIRS = S5_TT // 2


def _s5_body(u_ref, bblk_ref, cblk_ref, k0_ref, lam_ref, d_ref, z_ref,
             ut_ref, y_ref, st_ref, carry_ref, *, bsz):
    rows = S5_TT * bsz
    prow = S5_PAIRS * bsz

    @pl.when(pl.program_id(0) == 0)
    def _():
        carry_ref[...] = jnp.zeros_like(carry_ref)

    for b in range(bsz):
        for j in range(SLABS):
            ut_ref[j, pl.ds(b, S5_TT, stride=bsz), :] = u_ref[b, :, j * LANES:(j + 1) * LANES]

    for j in range(SLABS):
        u3 = ut_ref[j].reshape(S5_PAIRS, 2 * bsz, LANES)
        u0 = u3[:, :bsz, :].reshape(prow, LANES)
        u1 = u3[:, bsz:, :].reshape(prow, LANES)
        lhs = jnp.concatenate([u1, u0], axis=1).astype(BF16)
        st_ref[j, 0:bsz, :] = carry_ref[j]
        st_ref[j, bsz:, :] = jnp.dot(lhs, bblk_ref[j], preferred_element_type=F32)
        a = jnp.broadcast_to(lam_ref[j, 0:1, :], (bsz, SLAB_STATE))
        bb = jnp.broadcast_to(lam_ref[j, 1:2, :], (bsz, SLAB_STATE))

        cre = carry_ref[j, :, 0:SLAB_STATE]
        cim = carry_ref[j, :, SLAB_STATE:2 * SLAB_STATE]
        for k in range(S5_PAIRS):
            r0 = (k + 1) * bsz
            bre = st_ref[j, r0:r0 + bsz, 0:SLAB_STATE]
            bim = st_ref[j, r0:r0 + bsz, SLAB_STATE:2 * SLAB_STATE]
            cre, cim = a * cre - bb * cim + bre, a * cim + bb * cre + bim
            st_ref[j, r0:r0 + bsz, 0:SLAB_STATE] = cre
            st_ref[j, r0:r0 + bsz, SLAB_STATE:2 * SLAB_STATE] = cim
        carry_ref[j, :, 0:SLAB_STATE] = cre
        carry_ref[j, :, SLAB_STATE:2 * SLAB_STATE] = cim

        cs = jnp.dot(st_ref[j].astype(BF16), cblk_ref[j], preferred_element_type=F32)
        y1 = cs[bsz:, :LANES] + d_ref[j] * u1
        y0 = (cs[:prow, LANES:] + jnp.dot(u0.astype(BF16), k0_ref[j], preferred_element_type=F32)
              + d_ref[j] * u0)
        y = jnp.concatenate([y0.reshape(S5_PAIRS, bsz, LANES), y1.reshape(S5_PAIRS, bsz, LANES)],
                            axis=1).reshape(rows, LANES)
        y_ref[j] = jax.nn.gelu(y)

    for b in range(bsz):
        for j in range(SLABS):
            z_ref[b, :, j * LANES:(j + 1) * LANES] = (
                y_ref[j, pl.ds(b, S5_TT, stride=bsz), :].astype(BF16))


def _s5(u3, bblk, cblk, k0, lam, dskip):
    bsz, seq, _ = u3.shape
    rows = S5_TT * bsz
    full = lambda *s: pl.BlockSpec(s, lambda i: (0,) * len(s))
    return pl.pallas_call(
        functools.partial(_s5_body, bsz=bsz),
        grid=(seq // S5_TT,),
        in_specs=[pl.BlockSpec((bsz, S5_TT, SSM_WIDTH), lambda i: (0, i, 0)),
                  full(SLABS, 2 * LANES, 2 * SLAB_STATE),
                  full(SLABS, 2 * SLAB_STATE, 2 * LANES),
                  full(SLABS, LANES, LANES),
                  full(SLABS, 2, SLAB_STATE),
                  full(SLABS, 1, LANES)],
        out_specs=pl.BlockSpec((bsz, S5_TT, SSM_WIDTH), lambda i: (0, i, 0)),
        out_shape=jax.ShapeDtypeStruct((bsz, seq, SSM_WIDTH), BF16),
        scratch_shapes=[pltpu.VMEM((SLABS, rows, LANES), F32),
                        pltpu.VMEM((SLABS, rows, LANES), F32),
                        pltpu.VMEM((SLABS, bsz + S5_PAIRS * bsz, 2 * SLAB_STATE), F32),
                        pltpu.VMEM((SLABS, bsz, 2 * SLAB_STATE), F32)],
        compiler_params=_cparams(("arbitrary",)),
        name="s5",
    )(u3, bblk, cblk, k0, lam, dskip)


def _s5_params(lam_re, lam_im, log_dt, b_re, b_im, c_re, c_im, d_skip):
    lam = lax.complex(lam_re.astype(F32), lam_im.astype(F32))
    dt = jnp.exp(log_dt.astype(F32))[:, None]
    lam_bar = jnp.exp(lam * dt)
    b = lax.complex(b_re.astype(F32), b_im.astype(F32))
    b_bar = ((lam_bar - 1.0) / lam)[..., None] * b
    eye = jnp.eye(GROUPS_PER_SLAB, dtype=F32)

    def in_blk(part):
        p4 = part.reshape(SLABS, GROUPS_PER_SLAB, SSM_STATE, SSM_GROUP)
        m = jnp.einsum('jgph,gk->jghkp', p4, eye)
        return m.reshape(SLABS, LANES, SLAB_STATE)

    def in_cplx(z):
        return jnp.concatenate([in_blk(z.real), in_blk(z.imag)], axis=-1)

    bblk = jnp.concatenate([in_cplx(b_bar), in_cplx(lam_bar[..., None] * b_bar)],
                           axis=1).astype(BF16)

    def out_blk(part):
        p4 = part.reshape(SLABS, GROUPS_PER_SLAB, SSM_GROUP, SSM_STATE)
        m = jnp.einsum('jghp,gk->jgpkh', p4, eye)
        return m.reshape(SLABS, SLAB_STATE, LANES)

    def out_cplx(z):
        return jnp.concatenate([out_blk(z.real), -out_blk(z.imag)], axis=1)

    c = lax.complex(c_re.astype(F32), c_im.astype(F32))
    cblk = jnp.concatenate([out_cplx(c), out_cplx(c * lam_bar[:, None, :])], axis=-1).astype(BF16)
    k0 = jnp.einsum('ghp,gpk->gkh', c, b_bar).real.reshape(SLABS, GROUPS_PER_SLAB, SSM_GROUP,
                                                           SSM_GROUP)
    k0 = jnp.einsum('jgkh,gq->jgkqh', k0, eye).reshape(SLABS, LANES, LANES).astype(BF16)
    lam2 = lam_bar * lam_bar
    lam_k = jnp.stack([lam2.real.reshape(SLABS, SLAB_STATE),
                       lam2.imag.reshape(SLABS, SLAB_STATE)], axis=1)
    dsk = d_skip.astype(F32).reshape(SLABS, 1, LANES)
    return bblk, cblk, k0, lam_k, dsk


ATTN_QB = 4
HEADS_PER_TILE = LANES // HEAD_DIM


def _pair_norm(t, gain2, head_mean):
    sq = t * t
    hi = sq.astype(BF16)
    lo = (sq - hi.astype(F32)).astype(BF16)
    ms = (jnp.dot(hi, head_mean, preferred_element_type=F32)
          + jnp.dot(lo, head_mean, preferred_element_type=F32))
    return t * lax.rsqrt(ms + EPS) * gain2


def _attn_body(sink_ref, q_ref, kc_ref, kp_ref, vc_ref, vp_ref, qg_ref, kg_ref, o_ref):
    n = pl.program_id(1)
    lane = lax.broadcasted_iota(jnp.int32, (1, LANES), 1)
    lo = lane < HEAD_DIM
    li = lax.broadcasted_iota(jnp.int32, (LANES, LANES), 0)
    lj = lax.broadcasted_iota(jnp.int32, (LANES, LANES), 1)
    head_mean = jnp.where((li < HEAD_DIM) == (lj < HEAD_DIM), 1.0 / HEAD_DIM, 0.0).astype(BF16)
    ones = jnp.ones((LANES, LANES), BF16)

    kall = jnp.concatenate([kp_ref[0], kc_ref[0]], axis=0).astype(F32)
    kn = _pair_norm(kall, kg_ref[...], head_mean)
    kroll = pltpu.roll(kn, HEAD_DIM, axis=1)
    vall = jnp.concatenate([vp_ref[0], vc_ref[0]], axis=0).astype(F32)
    vroll = pltpu.roll(vall, HEAD_DIM, axis=1)
    zero = jnp.zeros_like(kn)
    k_var = [[jnp.where(lo, kn, zero).astype(BF16), jnp.where(lo, zero, kroll).astype(BF16)],
             [jnp.where(lo, kroll, zero).astype(BF16), jnp.where(lo, zero, kn).astype(BF16)]]
    v_var = [[jnp.where(lo, vall, zero).astype(BF16), jnp.where(lo, zero, vroll).astype(BF16)],
             [jnp.where(lo, vroll, zero).astype(BF16), jnp.where(lo, zero, vall).astype(BF16)]]

    qn = []
    for i in range(ATTN_WIDTH // LANES):
        t = _pair_norm(q_ref[0, :, i * LANES:(i + 1) * LANES].astype(F32), qg_ref[...], head_mean)
        qn.append((t * (HEAD_DIM ** -0.5)).astype(BF16))

    qi = lax.broadcasted_iota(jnp.int32, (BLOCK, BLOCK), 0)
    si = lax.broadcasted_iota(jnp.int32, (BLOCK, BLOCK), 1)
    is_cur = si <= qi
    dist = jnp.where(is_cur, qi - si, qi - si + BLOCK).astype(F32)
    alibi = [(2.0 ** (-8.0 * (h + 1) / N_HEADS)) * dist for h in range(N_HEADS)]
    sinkmat = jnp.concatenate([jnp.full((BLOCK, LANES), sink_ref[h], F32) for h in range(N_HEADS)],
                              axis=0)

    for b in range(ATTN_QB):
        rows = slice(b * BLOCK, (b + 1) * BLOCK)
        win = slice(b * BLOCK, (b + 2) * BLOCK)
        s_all = []
        for h in range(N_HEADS):
            i, slot = divmod(h, HEADS_PER_TILE)
            kvh = h // (N_HEADS // N_KV_HEADS)
            sc = lax.dot_general(qn[i][rows], k_var[kvh][slot][win], (((1,), (1,)), ((), ())),
                                 preferred_element_type=F32)
            prev = sc[:, :BLOCK]
            if b == 0:
                prev = jnp.where(n > 0, prev, NEG)
            s_all.append(jnp.where(is_cur, sc[:, BLOCK:], prev) - alibi[h])
        s = jnp.concatenate(s_all, axis=0)
        m = jnp.maximum(jnp.max(s, axis=-1, keepdims=True), sinkmat)
        p = jnp.exp(s - m)
        denom = jnp.dot(p.astype(BF16), ones, preferred_element_type=F32) + jnp.exp(sinkmat - m)
        pn = p / denom
        for i in range(ATTN_WIDTH // LANES):
            acc = jnp.zeros((BLOCK, LANES), F32)
            for slot in range(HEADS_PER_TILE):
                h = i * HEADS_PER_TILE + slot
                kvh = h // (N_HEADS // N_KV_HEADS)
                ph = pn[h * BLOCK:(h + 1) * BLOCK]
                pcat = jnp.concatenate([jnp.where(is_cur, 0.0, ph), jnp.where(is_cur, ph, 0.0)],
                                       axis=1).astype(BF16)
                acc = acc + jnp.dot(pcat, v_var[kvh][slot][win], preferred_element_type=F32)
            o_ref[0, rows, i * LANES:(i + 1) * LANES] = acc.astype(BF16)


def _attn(q3, k3, v3, q_gain, k_gain, sinks):
    bsz, seq, _ = q3.shape
    qrows = ATTN_QB * BLOCK
    cur = lambda b, n: (b, n, 0)
    prev = lambda b, n: (b, jnp.maximum(n * ATTN_QB - 1, 0), 0)
    qg2 = jnp.tile(q_gain.astype(F32), HEADS_PER_TILE).reshape(1, LANES)
    kg2 = jnp.tile(k_gain.astype(F32), HEADS_PER_TILE).reshape(1, LANES)
    gain_spec = pl.BlockSpec((1, LANES), lambda b, n: (0, 0))
    return pl.pallas_call(
        _attn_body,
        grid=(bsz, seq // qrows),
        in_specs=[pl.BlockSpec(memory_space=pltpu.SMEM),
                  pl.BlockSpec((1, qrows, ATTN_WIDTH), cur),
                  pl.BlockSpec((1, qrows, KV_WIDTH), cur),
                  pl.BlockSpec((1, BLOCK, KV_WIDTH), prev),
                  pl.BlockSpec((1, qrows, KV_WIDTH), cur),
                  pl.BlockSpec((1, BLOCK, KV_WIDTH), prev),
                  gain_spec, gain_spec],
        out_specs=pl.BlockSpec((1, qrows, ATTN_WIDTH), cur),
        out_shape=jax.ShapeDtypeStruct((bsz, seq, ATTN_WIDTH), BF16),
        compiler_params=_cparams(("arbitrary", "arbitrary")),
        name="attn",
    )(sinks.astype(F32), q3, k3, k3, v3, v3, qg2, kg2)


MIX_TM = 1024


def _mix_body(z_ref, ya_ref, gs_ref, ga_ref, x_ref, wglu_ref, wup_ref, wout_ref, g2_ref, wr_ref,
              x1_ref, gate_ref, hs_hbm, starts_ref, lens_ref, counts_ref,
              stage_ref, run_ref, pend_ref, sem):
    z = z_ref[...]
    bv = jnp.dot(z, wglu_ref[:, :D_MODEL], preferred_element_type=F32)
    bg = jnp.dot(z, wglu_ref[:, D_MODEL:], preferred_element_type=F32)
    bs = bv * jax.nn.sigmoid(bg)
    ba = jnp.dot(ya_ref[...], wup_ref[...], preferred_element_type=F32)
    merged = gs_ref[...].astype(F32) * bs + ga_ref[...].astype(F32) * ba
    x1 = x_ref[...] + jnp.dot(merged.astype(BF16), wout_ref[...], preferred_element_type=F32)
    x1_ref[...] = x1

    ms = jnp.mean(x1 * x1, axis=-1, keepdims=True)
    h2 = x1 * lax.rsqrt(ms + EPS) * g2_ref[...]
    h2b = h2.astype(BF16)

    lt = lax.dot_general(wr_ref[...], h2b, (((1,), (1,)), ((), ())),
                         preferred_element_type=F32)
    logits = lt[:ROUTER_ROWS] + lt[ROUTER_ROWS:]
    row = lax.broadcasted_iota(jnp.int32, logits.shape, 0)
    ninf = -jnp.inf
    cm = jnp.where(row < N_GROUPS, logits, ninf)
    cmax = jnp.max(cm, axis=0, keepdims=True)
    g_prob = 1.0 / jnp.sum(jnp.exp(cm - cmax), axis=0, keepdims=True)
    g_idx = jnp.min(jnp.where(cm == cmax, row, ROUTER_ROWS), axis=0, keepdims=True)
    f0 = FINE_OFF + EXPERTS_PER_GROUP * g_idx
    fm = jnp.where((row >= f0) & (row < f0 + EXPERTS_PER_GROUP), logits, ninf)
    v1 = jnp.max(fm, axis=0, keepdims=True)
    i1 = jnp.min(jnp.where(fm == v1, row, ROUTER_ROWS), axis=0, keepdims=True)
    fm2 = jnp.where(row == i1, ninf, fm)
    v2 = jnp.max(fm2, axis=0, keepdims=True)
    i2 = jnp.min(jnp.where(fm2 == v2, row, ROUTER_ROWS), axis=0, keepdims=True)
    e21 = jnp.exp(v2 - v1)
    w1 = g_prob / (1.0 + e21)
    w2 = w1 * e21
    rt = jnp.where(row == i1 - f0, w1,
                   jnp.where(row == i2 - f0, w2,
                             jnp.where(row == ROUTE_GROUP_OFF + g_idx, 1.0, 0.0)))
    rt = jnp.concatenate([rt, jnp.zeros((ROUTER_LANES - ROUTER_ROWS, rt.shape[1]), F32)], axis=0)
    route = rt.T
    gate_ref[...] = route
    _sort_tiles(h2b, route, hs_hbm, starts_ref, lens_ref, counts_ref, stage_ref, run_ref, pend_ref,
                sem)


def _mix(z2, ya2, gs2, ga2, x2, wglu, wup, wout, g2, wr, layer):
    n = x2.shape[0]
    n_sort = n // SORT_TM
    subs = MIX_TM // SORT_TM
    cap = _region_cap(n)
    assert subs * SORT_TM >= ZFILL_TOK
    row = lambda c: pl.BlockSpec((MIX_TM, c), lambda i: (i, 0))
    full = lambda a, b: pl.BlockSpec((a, b), lambda i: (0, 0))
    smem = pl.BlockSpec(memory_space=pltpu.SMEM)
    return pl.pallas_call(
        _mix_body,
        grid=(n // MIX_TM,),
        in_specs=[row(SSM_WIDTH), row(ATTN_WIDTH), row(D_MODEL), row(D_MODEL), row(D_MODEL),
                  _layer_spec(layer, SSM_WIDTH, 2 * D_MODEL), _layer_spec(layer, ATTN_WIDTH, D_MODEL),
                  _layer_spec(layer, D_MODEL, D_MODEL), full(1, D_MODEL),
                  _layer_spec(layer, 2 * ROUTER_ROWS, D_MODEL)],
        out_specs=[row(D_MODEL), row(ROUTER_LANES), pl.BlockSpec(memory_space=pl.ANY),
                   smem, smem, smem],
        out_shape=[jax.ShapeDtypeStruct((n, D_MODEL), F32),
                   jax.ShapeDtypeStruct((n, ROUTER_LANES), F32),
                   jax.ShapeDtypeStruct((N_GROUPS * cap * TOK_ROWS, LANES), jnp.uint32),
                   jax.ShapeDtypeStruct((n_sort * N_GROUPS,), jnp.int32),
                   jax.ShapeDtypeStruct((n_sort * N_GROUPS,), jnp.int32),
                   jax.ShapeDtypeStruct((N_GROUPS,), jnp.int32)],
        scratch_shapes=[pltpu.VMEM((subs * SORT_TM * TOK_ROWS, LANES), jnp.uint32),
                        pltpu.SMEM((N_GROUPS,), jnp.int32),
                        pltpu.SMEM((subs * N_GROUPS,), jnp.int32),
                        pltpu.SemaphoreType.DMA((subs, N_GROUPS))],
        compiler_params=_cparams(("arbitrary",)),
        name="mix",
    )(z2, ya2, gs2, ga2, x2, wglu, wup, wout, g2, wr)


SORT_TM = 256
RUN_SHORT = 128
MOE_TM = 512
TOK_ROWS = 8
H_WORDS = D_MODEL // 2
H_CHUNKS = H_WORDS // LANES
ROUTE_ROW = H_CHUNKS
ZFILL_TOK = MOE_TM
HI_MASK = 0xFFFF0000


def _region_cap(n):
    cap = n + ZFILL_TOK
    return -(-cap // MOE_TM) * MOE_TM


def _tile_positions(route):
    tm = route.shape[0]
    lane = lax.broadcasted_iota(jnp.int32, route.shape, 1)
    onehot = jnp.where((lane >= ROUTE_GROUP_OFF) & (lane < ROUTE_GROUP_OFF + N_GROUPS), route, 0.0)
    ci = lax.broadcasted_iota(jnp.int32, (tm, tm), 0)
    cj = lax.broadcasted_iota(jnp.int32, (tm, tm), 1)
    earlier = (cj < ci).astype(BF16)
    rank = jnp.dot(earlier, onehot.astype(BF16), preferred_element_type=F32)
    cnt = jnp.sum(onehot, axis=0, keepdims=True)
    li = lax.broadcasted_iota(jnp.int32, (LANES, LANES), 0)
    lj = lax.broadcasted_iota(jnp.int32, (LANES, LANES), 1)
    base = jnp.dot(jnp.broadcast_to(cnt, (8, LANES)).astype(BF16), (li < lj).astype(BF16),
                   preferred_element_type=F32)[0:1]
    return onehot, rank, base, cnt


def _sort_tiles(h, route, hs_hbm, starts_ref, lens_ref, counts_ref, stage_ref, run_ref, pend_ref,
                sem):
    t = pl.program_id(0)
    nt = pl.num_programs(0)
    tm = SORT_TM
    subs = h.shape[0] // tm
    slot_rows = tm * TOK_ROWS
    cap = hs_hbm.shape[0] // (N_GROUPS * TOK_ROWS)

    def run_copy(src_row, dst_row, sl, g, tokens):
        return pltpu.make_async_copy(stage_ref.at[pl.ds(src_row, tokens * TOK_ROWS), :],
                                     hs_hbm.at[pl.ds(dst_row, tokens * TOK_ROWS), :],
                                     sem.at[sl, g])

    def wait_slot(sl):
        for g in range(N_GROUPS):
            n = pend_ref[sl * N_GROUPS + g]

            @pl.when(n > 0)
            def _(g=g, n=n):
                run_copy(0, 0, sl, g, n).wait()

    @pl.when(t == 0)
    def _():
        stage_ref[...] = jnp.zeros_like(stage_ref)
        for g in range(N_GROUPS):
            run_ref[g] = 0

    @pl.when(t > 0)
    def _():
        for sl in range(subs):
            wait_slot(sl)

    for sl in range(subs):
        tile = t * subs + sl
        r = route[sl * tm:(sl + 1) * tm]
        onehot, rank, base, cnt = _tile_positions(r)
        w = (onehot * (rank + base)).astype(BF16)
        pos_row = lax.dot_general(jnp.ones((8, LANES), BF16), w, (((1,), (1,)), ((), ())),
                                  preferred_element_type=F32)[0:1]
        ri = lax.broadcasted_iota(jnp.int32, (tm, tm), 0).astype(F32)
        perm = (ri == pos_row).astype(BF16)
        r_hi = r.astype(BF16)
        r_mid = (r - r_hi.astype(F32)).astype(BF16)
        r_lo = (r - r_hi.astype(F32) - r_mid.astype(F32)).astype(BF16)
        sorted_all = jnp.dot(
            perm, jnp.concatenate([h[sl * tm:(sl + 1) * tm], r_hi, r_mid, r_lo], axis=1),
            preferred_element_type=F32)
        sh = sorted_all[:, :D_MODEL]
        sr = (sorted_all[:, D_MODEL:D_MODEL + LANES]
              + sorted_all[:, D_MODEL + LANES:D_MODEL + 2 * LANES]
              + sorted_all[:, D_MODEL + 2 * LANES:])
        hb = lax.bitcast_convert_type(sh, jnp.uint32)
        words = (hb[:, :H_WORDS] & jnp.uint32(HI_MASK)) | (hb[:, H_WORDS:] >> 16)

        slot0 = sl * slot_rows
        for s in range(H_CHUNKS):
            stage_ref[pl.ds(slot0 + s, tm, stride=TOK_ROWS), :] = words[:, s * LANES:(s + 1) * LANES]
        stage_ref[pl.ds(slot0 + ROUTE_ROW, tm, stride=TOK_ROWS), :] = lax.bitcast_convert_type(
            sr, jnp.uint32)

        for g in range(N_GROUPS):
            c_g = cnt[0, ROUTE_GROUP_OFF + g].astype(jnp.int32)
            b_g = base[0, ROUTE_GROUP_OFF + g].astype(jnp.int32)
            start = run_ref[g]
            starts_ref[tile * N_GROUPS + g] = start
            lens_ref[tile * N_GROUPS + g] = c_g
            pend_ref[sl * N_GROUPS + g] = c_g
            run_ref[g] = start + c_g
            src = pl.multiple_of(slot0 + b_g * TOK_ROWS, TOK_ROWS)
            dst = pl.multiple_of((g * cap + start) * TOK_ROWS, TOK_ROWS)

            @pl.when(c_g > 0)
            def _(sl=sl, g=g, src=src, dst=dst, c_g=c_g):
                run_copy(src, dst, sl, g, c_g).start()

    @pl.when(t == nt - 1)
    def _():
        for sl in range(subs):
            wait_slot(sl)
        stage_ref[...] = jnp.zeros_like(stage_ref)
        zrows = ZFILL_TOK * TOK_ROWS
        for g in range(N_GROUPS):
            total = run_ref[g]
            counts_ref[g] = total
            dst = pl.multiple_of((g * cap + total) * TOK_ROWS, TOK_ROWS)
            pltpu.make_async_copy(stage_ref.at[pl.ds(0, zrows), :],
                                  hs_hbm.at[pl.ds(dst, zrows), :], sem.at[0, g]).start()
        for g in range(N_GROUPS):
            pltpu.make_async_copy(stage_ref.at[pl.ds(0, zrows), :],
                                  hs_hbm.at[pl.ds(0, zrows), :], sem.at[0, g]).wait()


def _experts_body(blk_ref, grp_ref, valid_ref, hs_ref, wu32_ref, wd32_ref, ys_ref, wu_ref, wd_ref):
    t = pl.program_id(0)
    tm = MOE_TM

    @pl.when((t == 0) | (grp_ref[t] != grp_ref[jnp.maximum(t - 1, 0)]))
    def _():
        wu_ref[...] = wu32_ref[...].astype(BF16)
        wd_ref[...] = wd32_ref[...].astype(BF16)

    @pl.when(valid_ref[t] > 0)
    def _():
        chunks = [hs_ref[pl.ds(s, tm, stride=TOK_ROWS), :] for s in range(H_CHUNKS + 1)]
        hi = [lax.bitcast_convert_type(c & jnp.uint32(HI_MASK), F32) for c in chunks[:H_CHUNKS]]
        lo = [lax.bitcast_convert_type(c << 16, F32) for c in chunks[:H_CHUNKS]]
        h = jnp.concatenate(hi + lo, axis=1).astype(BF16)
        route = lax.bitcast_convert_type(chunks[ROUTE_ROW], F32)
        acts = []
        for e in range(EXPERTS_PER_GROUP):
            gu = jnp.dot(h, wu_ref[e], preferred_element_type=F32)
            acts.append((jax.nn.silu(gu[:, :D_EXPERT]) * gu[:, D_EXPERT:]
                         * route[:, e:e + 1]).astype(BF16))
        act = jnp.concatenate(acts, axis=1)
        wd = wd_ref[...].reshape(EXPERTS_PER_GROUP * D_EXPERT, D_MODEL)
        out = jnp.dot(act, wd, preferred_element_type=F32)
        for j in range(D_MODEL // LANES):
            ys_ref[pl.ds(j, tm, stride=TOK_ROWS), :] = out[:, j * LANES:(j + 1) * LANES]


def _experts(blk, grp, valid, hs, wu, wd, layer):
    g0 = layer * N_GROUPS
    n_tiles = blk.shape[0]
    epg = EXPERTS_PER_GROUP
    return pl.pallas_call(
        _experts_body,
        grid_spec=pltpu.PrefetchScalarGridSpec(
            num_scalar_prefetch=3,
            grid=(n_tiles,),
            in_specs=[pl.BlockSpec((MOE_TM * TOK_ROWS, LANES), lambda t, b, g, v: (b[t], 0)),
                      pl.BlockSpec((epg, D_MODEL, 2 * D_EXPERT),
                                   lambda t, b, g, v: (g0 + g[t], 0, 0)),
                      pl.BlockSpec((epg, D_EXPERT, D_MODEL),
                                   lambda t, b, g, v: (g0 + g[t], 0, 0))],
            out_specs=pl.BlockSpec((MOE_TM * TOK_ROWS, LANES), lambda t, b, g, v: (b[t], 0)),
            scratch_shapes=[pltpu.VMEM((epg, D_MODEL, 2 * D_EXPERT), BF16),
                            pltpu.VMEM((epg, D_EXPERT, D_MODEL), BF16)],
        ),
        out_shape=jax.ShapeDtypeStruct(hs.shape, F32),
        compiler_params=_cparams(("arbitrary",)),
        name="moe_experts",
    )(blk, grp, valid, hs, wu, wd)


def _combine_body(starts_ref, lens_ref, x1_ref, route_ref, ys_hbm, o_ref, buf_ref, sem):
    t = pl.program_id(0)
    nt = pl.num_programs(0)
    tm = SORT_TM
    slot_rows = tm * TOK_ROWS
    cap = ys_hbm.shape[0] // (N_GROUPS * TOK_ROWS)

    def fetches(step, slot):
        copies = []
        first = 0
        for g in range(N_GROUPS):
            tokens = lens_ref[step * N_GROUPS + g]
            src = pl.multiple_of((g * cap + starts_ref[step * N_GROUPS + g]) * TOK_ROWS, TOK_ROWS)
            dst = pl.multiple_of(slot * slot_rows + first * TOK_ROWS, TOK_ROWS)
            copies.append((tokens, pltpu.make_async_copy(
                ys_hbm.at[pl.ds(src, tokens * TOK_ROWS), :],
                buf_ref.at[pl.ds(dst, tokens * TOK_ROWS), :], sem.at[slot, g])))
            first = first + tokens
        return copies

    def start_fetches(step, slot):
        for tokens, copy in fetches(step, slot):
            @pl.when(tokens > 0)
            def _(copy=copy):
                copy.start()

    @pl.when(t == 0)
    def _():
        start_fetches(0, 0)

    @pl.when(t + 1 < nt)
    def _():
        start_fetches(t + 1, (t + 1) % 2)

    onehot, rank, base, _ = _tile_positions(route_ref[...])
    pos = jnp.sum(onehot * (rank + base), axis=1, keepdims=True)
    ri = lax.broadcasted_iota(jnp.int32, (tm, tm), 1).astype(F32)
    sel = (ri == pos).astype(BF16)
    slot = t % 2
    for tokens, copy in fetches(t, slot):
        @pl.when(tokens > 0)
        def _(copy=copy):
            copy.wait()
    r0 = pl.multiple_of(slot * slot_rows, TOK_ROWS)
    y = jnp.concatenate(
        [buf_ref[pl.ds(r0 + j, tm, stride=TOK_ROWS), :] for j in range(D_MODEL // LANES)],
        axis=1).astype(BF16)
    o_ref[...] = x1_ref[...] + jnp.dot(sel, y, preferred_element_type=F32)


def _combine(starts, lens, x1, route, ys):
    n = x1.shape[0]
    nt = n // SORT_TM
    return pl.pallas_call(
        _combine_body,
        grid_spec=pltpu.PrefetchScalarGridSpec(
            num_scalar_prefetch=2,
            grid=(nt,),
            in_specs=[pl.BlockSpec((SORT_TM, D_MODEL), lambda i, s, c: (i, 0)),
                      pl.BlockSpec((SORT_TM, ROUTER_LANES), lambda i, s, c: (i, 0)),
                      pl.BlockSpec(memory_space=pl.ANY)],
            out_specs=pl.BlockSpec((SORT_TM, D_MODEL), lambda i, s, c: (i, 0)),
            scratch_shapes=[pltpu.VMEM((2 * SORT_TM * TOK_ROWS, LANES), F32),
                            pltpu.SemaphoreType.DMA((2, N_GROUPS))],
        ),
        out_shape=jax.ShapeDtypeStruct((n, D_MODEL), F32),
        compiler_params=_cparams(("arbitrary",)),
        name="moe_combine",
    )(starts, lens, x1, route, ys)


def _expert_tiles(counts, n):
    cap_blocks = _region_cap(n) // MOE_TM
    n_tiles = n // MOE_TM + N_GROUPS
    per_group = (counts + MOE_TM - 1) // MOE_TM
    ends = jnp.cumsum(per_group)
    total = ends[-1]
    t = jnp.minimum(jnp.arange(n_tiles, dtype=jnp.int32), total - 1)
    grp = jnp.sum((t[:, None] >= ends[None, :]).astype(jnp.int32), axis=1)
    first = ends - per_group
    blk = grp * cap_blocks + (t - first[grp])
    valid = (jnp.arange(n_tiles, dtype=jnp.int32) < total).astype(jnp.int32)
    return blk.astype(jnp.int32), grp.astype(jnp.int32), valid


def _moe(x1, route, hs, starts, lens, counts, w_up, w_down, layer):
    blk, grp, valid = _expert_tiles(counts, x1.shape[0])
    ys = _experts(blk, grp, valid, hs, w_up, w_down, layer)
    return _combine(starts, lens, x1, route, ys)


def kernel(x, norm_mix, w_in, ssm_lam_re, ssm_lam_im, ssm_log_dt, ssm_b_re, ssm_b_im, ssm_c_re,
           ssm_c_im, ssm_d, w_ssm_glu_val, w_ssm_glu_gate, q_norm, k_norm, attn_sinks, w_attn_up,
           w_out, norm_ffn, w_coarse, w_fine, w_expert_up, w_expert_down):
    bsz, seq, d = x.shape
    n = bsz * seq
    depth = w_in.shape[0]
    x2 = x.reshape(n, d)
    s5p = jax.vmap(_s5_params)(ssm_lam_re, ssm_lam_im, ssm_log_dt, ssm_b_re, ssm_b_im, ssm_c_re,
                               ssm_c_im, ssm_d)
    w_in_b = w_in.astype(BF16)
    wglu = jnp.concatenate([w_ssm_glu_val, w_ssm_glu_gate], axis=2).astype(BF16)
    wup_b = w_attn_up.astype(BF16)
    wout_b = w_out.astype(BF16)
    wr_t = jnp.concatenate(
        [jnp.swapaxes(w_coarse, 1, 2), jnp.swapaxes(w_fine, 1, 2),
         jnp.zeros((depth, ROUTER_ROWS - N_GROUPS - N_EXPERTS, d), w_fine.dtype)], axis=1).astype(F32)
    wr_hi = wr_t.astype(BF16)
    wr = jnp.concatenate([wr_hi, (wr_t - wr_hi.astype(F32)).astype(BF16)], axis=1)
    wu_all = w_expert_up.reshape((depth * N_EXPERTS,) + w_expert_up.shape[2:])
    wd_all = w_expert_down.reshape((depth * N_EXPERTS,) + w_expert_down.shape[2:])
    for i in range(depth):
        u, q, k, v, gs, ga = _proj(x2, norm_mix[i].reshape(1, d), w_in_b, i)
        z = _s5(u.reshape(bsz, seq, SSM_WIDTH), *(p[i] for p in s5p))
        ya = _attn(q.reshape(bsz, seq, ATTN_WIDTH), k.reshape(bsz, seq, KV_WIDTH),
                   v.reshape(bsz, seq, KV_WIDTH), q_norm[i], k_norm[i], attn_sinks[i])
        x1, route, hs, starts, lens, counts = _mix(
            z.reshape(n, SSM_WIDTH), ya.reshape(n, ATTN_WIDTH), gs, ga, x2,
            wglu, wup_b, wout_b, norm_ffn[i].reshape(1, d), wr, i)
        x2 = _moe(x1, route, hs, starts, lens, counts, wu_all, wd_all, i)
    return x2.reshape(bsz, seq, d)
```

```python
import functools
import math

import jax
import jax.numpy as jnp
from jax import lax
from jax.experimental import pallas as pl
from jax.experimental.pallas import tpu as pltpu

F32 = jnp.float32
BF16 = jnp.bfloat16

D_MODEL = 1024
SSM_WIDTH = 512
SSM_GROUP = 16
SSM_GROUPS = 32
SSM_STATE = 64
N_HEADS = 8
N_KV_HEADS = 2
HEAD_DIM = 64
ATTN_WIDTH = 512
KV_WIDTH = 128
BLOCK = 128
N_GROUPS = 4
EXPERTS_PER_GROUP = 4
N_EXPERTS = 16
D_EXPERT = 256
EPS = 1e-6
NEG = -1e30

LANES = 128
SLABS = SSM_WIDTH // LANES
GROUPS_PER_SLAB = LANES // SSM_GROUP
SLAB_STATE = GROUPS_PER_SLAB * SSM_STATE
ROUTER_LANES = 128
ROUTER_ROWS = 32
FINE_OFF = N_GROUPS
ROUTE_GROUP_OFF = 8

VMEM_LIMIT = 56 * 1024 * 1024


def _cparams(sem):
    return pltpu.CompilerParams(dimension_semantics=sem, vmem_limit_bytes=VMEM_LIMIT)


PROJ_TM = 1024


def _proj_body(x_ref, g_ref, w_ref, u_ref, q_ref, k_ref, v_ref, gs_ref, ga_ref):
    x = x_ref[...]
    ms = jnp.mean(x * x, axis=-1, keepdims=True)
    h = (x * lax.rsqrt(ms + EPS) * g_ref[...]).astype(BF16)

    def seg(a, b):
        return jnp.dot(h, w_ref[:, a:b], preferred_element_type=F32)

    o = 0
    u_ref[...] = seg(o, o + SSM_WIDTH)
    o += SSM_WIDTH
    q_ref[...] = seg(o, o + ATTN_WIDTH).astype(BF16)
    o += ATTN_WIDTH
    kv = seg(o, o + 2 * KV_WIDTH).astype(BF16)
    k_ref[...] = kv[:, :KV_WIDTH]
    v_ref[...] = kv[:, KV_WIDTH:]
    o += 2 * KV_WIDTH
    gs_ref[...] = jax.nn.sigmoid(seg(o, o + D_MODEL)).astype(BF16)
    o += D_MODEL
    ga_ref[...] = jax.nn.sigmoid(seg(o, o + D_MODEL)).astype(BF16)


def _layer_spec(layer, *shape):
    return pl.BlockSpec((None,) + shape, lambda *_: (layer,) + (0,) * len(shape))


def _proj(x2, g, w, layer):
    n = x2.shape[0]
    cols = w.shape[2]
    row = lambda c: pl.BlockSpec((PROJ_TM, c), lambda i: (i, 0))
    full = lambda a, b: pl.BlockSpec((a, b), lambda i: (0, 0))
    return pl.pallas_call(
        _proj_body,
        grid=(n // PROJ_TM,),
        in_specs=[row(D_MODEL), full(1, D_MODEL), _layer_spec(layer, D_MODEL, cols)],
        out_specs=[row(SSM_WIDTH), row(ATTN_WIDTH), row(KV_WIDTH), row(KV_WIDTH),
                   row(D_MODEL), row(D_MODEL)],
        out_shape=[jax.ShapeDtypeStruct((n, SSM_WIDTH), F32),
                   jax.ShapeDtypeStruct((n, ATTN_WIDTH), BF16),
                   jax.ShapeDtypeStruct((n, KV_WIDTH), BF16),
                   jax.ShapeDtypeStruct((n, KV_WIDTH), BF16),
                   jax.ShapeDtypeStruct((n, D_MODEL), BF16),
                   jax.ShapeDtypeStruct((n, D_MODEL), BF16)],
        compiler_params=_cparams(("arbitrary",)),
        name="proj",
    )(x2, g, w)


S5_TT = 64
S5_PAIRS = S5_TT // 2


def _s5_body(u_ref, bblk_ref, cblk_ref, k0_ref, lam_ref, d_ref, z_ref,
             ut_ref, y_ref, st_ref, carry_ref, *, bsz):
    rows = S5_TT * bsz
    prow = S5_PAIRS * bsz

    @pl.when(pl.program_id(0) == 0)
    def _():
        carry_ref[...] = jnp.zeros_like(carry_ref)

    for b in range(bsz):
        for j in range(SLABS):
            ut_ref[j, pl.ds(b, S5_TT, stride=bsz), :] = u_ref[b, :, j * LANES:(j + 1) * LANES]

    for j in range(SLABS):
        u3 = ut_ref[j].reshape(S5_PAIRS, 2 * bsz, LANES)
        u0 = u3[:, :bsz, :].reshape(prow, LANES)
        u1 = u3[:, bsz:, :].reshape(prow, LANES)
        lhs = jnp.concatenate([u1, u0], axis=1).astype(BF16)
        st_ref[j, 0:bsz, :] = carry_ref[j]
        st_ref[j, bsz:, :] = jnp.dot(lhs, bblk_ref[j], preferred_element_type=F32)
        a = jnp.broadcast_to(lam_ref[j, 0:1, :], (bsz, SLAB_STATE))
        bb = jnp.broadcast_to(lam_ref[j, 1:2, :], (bsz, SLAB_STATE))

        cre = carry_ref[j, :, 0:SLAB_STATE]
        cim = carry_ref[j, :, SLAB_STATE:2 * SLAB_STATE]
        for k in range(S5_PAIRS):
            r0 = (k + 1) * bsz
            bre = st_ref[j, r0:r0 + bsz, 0:SLAB_STATE]
            bim = st_ref[j, r0:r0 + bsz, SLAB_STATE:2 * SLAB_STATE]
            cre, cim = a * cre - bb * cim + bre, a * cim + bb * cre + bim
            st_ref[j, r0:r0 + bsz, 0:SLAB_STATE] = cre
            st_ref[j, r0:r0 + bsz, SLAB_STATE:2 * SLAB_STATE] = cim
        carry_ref[j, :, 0:SLAB_STATE] = cre
        carry_ref[j, :, SLAB_STATE:2 * SLAB_STATE] = cim

        cs = jnp.dot(st_ref[j].astype(BF16), cblk_ref[j], preferred_element_type=F32)
        y1 = cs[bsz:, :LANES] + d_ref[j] * u1
        y0 = (cs[:prow, LANES:] + jnp.dot(u0.astype(BF16), k0_ref[j], preferred_element_type=F32)
              + d_ref[j] * u0)
        y = jnp.concatenate([y0.reshape(S5_PAIRS, bsz, LANES), y1.reshape(S5_PAIRS, bsz, LANES)],
                            axis=1).reshape(rows, LANES)
        y_ref[j] = jax.nn.gelu(y)

    for b in range(bsz):
        for j in range(SLABS):
            z_ref[b, :, j * LANES:(j + 1) * LANES] = (
                y_ref[j, pl.ds(b, S5_TT, stride=bsz), :].astype(BF16))


def _s5(u3, bblk, cblk, k0, lam, dskip):
    bsz, seq, _ = u3.shape
    rows = S5_TT * bsz
    full = lambda *s: pl.BlockSpec(s, lambda i: (0,) * len(s))
    return pl.pallas_call(
        functools.partial(_s5_body, bsz=bsz),
        grid=(seq // S5_TT,),
        in_specs=[pl.BlockSpec((bsz, S5_TT, SSM_WIDTH), lambda i: (0, i, 0)),
                  full(SLABS, 2 * LANES, 2 * SLAB_STATE),
                  full(SLABS, 2 * SLAB_STATE, 2 * LANES),
                  full(SLABS, LANES, LANES),
                  full(SLABS, 2, SLAB_STATE),
                  full(SLABS, 1, LANES)],
        out_specs=pl.BlockSpec((bsz, S5_TT, SSM_WIDTH), lambda i: (0, i, 0)),
        out_shape=jax.ShapeDtypeStruct((bsz, seq, SSM_WIDTH), BF16),
        scratch_shapes=[pltpu.VMEM((SLABS, rows, LANES), F32),
                        pltpu.VMEM((SLABS, rows, LANES), F32),
                        pltpu.VMEM((SLABS, bsz + S5_PAIRS * bsz, 2 * SLAB_STATE), F32),
                        pltpu.VMEM((SLABS, bsz, 2 * SLAB_STATE), F32)],
        compiler_params=_cparams(("arbitrary",)),
        name="s5",
    )(u3, bblk, cblk, k0, lam, dskip)


def _s5_params(lam_re, lam_im, log_dt, b_re, b_im, c_re, c_im, d_skip):
    lam = lax.complex(lam_re.astype(F32), lam_im.astype(F32))
    dt = jnp.exp(log_dt.astype(F32))[:, None]
    lam_bar = jnp.exp(lam * dt)
    b = lax.complex(b_re.astype(F32), b_im.astype(F32))
    b_bar = ((lam_bar - 1.0) / lam)[..., None] * b
    eye = jnp.eye(GROUPS_PER_SLAB, dtype=F32)

    def in_blk(part):
        p4 = part.reshape(SLABS, GROUPS_PER_SLAB, SSM_STATE, SSM_GROUP)
        m = jnp.einsum('jgph,gk->jghkp', p4, eye)
        return m.reshape(SLABS, LANES, SLAB_STATE)

    def in_cplx(z):
        return jnp.concatenate([in_blk(z.real), in_blk(z.imag)], axis=-1)

    bblk = jnp.concatenate([in_cplx(b_bar), in_cplx(lam_bar[..., None] * b_bar)],
                           axis=1).astype(BF16)

    def out_blk(part):
        p4 = part.reshape(SLABS, GROUPS_PER_SLAB, SSM_GROUP, SSM_STATE)
        m = jnp.einsum('jghp,gk->jgpkh', p4, eye)
        return m.reshape(SLABS, SLAB_STATE, LANES)

    def out_cplx(z):
        return jnp.concatenate([out_blk(z.real), -out_blk(z.imag)], axis=1)

    c = lax.complex(c_re.astype(F32), c_im.astype(F32))
    cblk = jnp.concatenate([out_cplx(c), out_cplx(c * lam_bar[:, None, :])], axis=-1).astype(BF16)
    k0 = jnp.einsum('ghp,gpk->gkh', c, b_bar).real.reshape(SLABS, GROUPS_PER_SLAB, SSM_GROUP,
                                                           SSM_GROUP)
    k0 = jnp.einsum('jgkh,gq->jgkqh', k0, eye).reshape(SLABS, LANES, LANES).astype(BF16)
    lam2 = lam_bar * lam_bar
    lam_k = jnp.stack([lam2.real.reshape(SLABS, SLAB_STATE),
                       lam2.imag.reshape(SLABS, SLAB_STATE)], axis=1)
    dsk = d_skip.astype(F32).reshape(SLABS, 1, LANES)
    return bblk, cblk, k0, lam_k, dsk


ATTN_QB = 8
HEADS_PER_TILE = LANES // HEAD_DIM


def _pair_norm(t, gain2, head_mean):
    sq = t * t
    hi = sq.astype(BF16)
    lo = (sq - hi.astype(F32)).astype(BF16)
    ms = (jnp.dot(hi, head_mean, preferred_element_type=F32)
          + jnp.dot(lo, head_mean, preferred_element_type=F32))
    return t * lax.rsqrt(ms + EPS) * gain2


def _attn_body(sink_ref, q_ref, kc_ref, kp_ref, vc_ref, vp_ref, qg_ref, kg_ref, o_ref):
    n = pl.program_id(1)
    lane = lax.broadcasted_iota(jnp.int32, (1, LANES), 1)
    lo = lane < HEAD_DIM
    li = lax.broadcasted_iota(jnp.int32, (LANES, LANES), 0)
    lj = lax.broadcasted_iota(jnp.int32, (LANES, LANES), 1)
    head_mean = jnp.where((li < HEAD_DIM) == (lj < HEAD_DIM), 1.0 / HEAD_DIM, 0.0).astype(BF16)
    ones = jnp.ones((LANES, LANES), BF16)

    kall = jnp.concatenate([kp_ref[0], kc_ref[0]], axis=0).astype(F32)
    kn = _pair_norm(kall, kg_ref[...], head_mean)
    kroll = pltpu.roll(kn, HEAD_DIM, axis=1)
    vall = jnp.concatenate([vp_ref[0], vc_ref[0]], axis=0).astype(F32)
    vroll = pltpu.roll(vall, HEAD_DIM, axis=1)
    zero = jnp.zeros_like(kn)
    k_var = [[jnp.where(lo, kn, zero).astype(BF16), jnp.where(lo, zero, kroll).astype(BF16)],
             [jnp.where(lo, kroll, zero).astype(BF16), jnp.where(lo, zero, kn).astype(BF16)]]
    v_var = [[jnp.where(lo, vall, zero).astype(BF16), jnp.where(lo, zero, vroll).astype(BF16)],
             [jnp.where(lo, vroll, zero).astype(BF16), jnp.where(lo, zero, vall).astype(BF16)]]

    qn = []
    for i in range(ATTN_WIDTH // LANES):
        t = _pair_norm(q_ref[0, :, i * LANES:(i + 1) * LANES].astype(F32), qg_ref[...], head_mean)
        qn.append((t * (HEAD_DIM ** -0.5)).astype(BF16))

    qi = lax.broadcasted_iota(jnp.int32, (BLOCK, BLOCK), 0)
    si = lax.broadcasted_iota(jnp.int32, (BLOCK, BLOCK), 1)
    is_cur = si <= qi
    dist = jnp.where(is_cur, qi - si, qi - si + BLOCK).astype(F32)
    alibi = [(2.0 ** (-8.0 * (h + 1) / N_HEADS)) * dist for h in range(N_HEADS)]
    sinkmat = jnp.concatenate([jnp.full((BLOCK, LANES), sink_ref[h], F32) for h in range(N_HEADS)],
                              axis=0)

    for b in range(ATTN_QB):
        rows = slice(b * BLOCK, (b + 1) * BLOCK)
        win = slice(b * BLOCK, (b + 2) * BLOCK)
        s_all = []
        for h in range(N_HEADS):
            i, slot = divmod(h, HEADS_PER_TILE)
            kvh = h // (N_HEADS // N_KV_HEADS)
            sc = lax.dot_general(qn[i][rows], k_var[kvh][slot][win], (((1,), (1,)), ((), ())),
                                 preferred_element_type=F32)
            prev = sc[:, :BLOCK]
            if b == 0:
                prev = jnp.where(n > 0, prev, NEG)
            s_all.append(jnp.where(is_cur, sc[:, BLOCK:], prev) - alibi[h])
        s = jnp.concatenate(s_all, axis=0)
        m = jnp.maximum(jnp.max(s, axis=-1, keepdims=True), sinkmat)
        p = jnp.exp(s - m)
        denom = jnp.dot(p.astype(BF16), ones, preferred_element_type=F32) + jnp.exp(sinkmat - m)
        pn = p / denom
        for i in range(ATTN_WIDTH // LANES):
            acc = jnp.zeros((BLOCK, LANES), F32)
            for slot in range(HEADS_PER_TILE):
                h = i * HEADS_PER_TILE + slot
                kvh = h // (N_HEADS // N_KV_HEADS)
                ph = pn[h * BLOCK:(h + 1) * BLOCK]
                pcat = jnp.concatenate([jnp.where(is_cur, 0.0, ph), jnp.where(is_cur, ph, 0.0)],
                                       axis=1).astype(BF16)
                acc = acc + jnp.dot(pcat, v_var[kvh][slot][win], preferred_element_type=F32)
            o_ref[0, rows, i * LANES:(i + 1) * LANES] = acc.astype(BF16)


def _attn(q3, k3, v3, q_gain, k_gain, sinks):
    bsz, seq, _ = q3.shape
    qrows = ATTN_QB * BLOCK
    cur = lambda b, n: (b, n, 0)
    prev = lambda b, n: (b, jnp.maximum(n * ATTN_QB - 1, 0), 0)
    qg2 = jnp.tile(q_gain.astype(F32), HEADS_PER_TILE).reshape(1, LANES)
    kg2 = jnp.tile(k_gain.astype(F32), HEADS_PER_TILE).reshape(1, LANES)
    gain_spec = pl.BlockSpec((1, LANES), lambda b, n: (0, 0))
    return pl.pallas_call(
        _attn_body,
        grid=(bsz, seq // qrows),
        in_specs=[pl.BlockSpec(memory_space=pltpu.SMEM),
                  pl.BlockSpec((1, qrows, ATTN_WIDTH), cur),
                  pl.BlockSpec((1, qrows, KV_WIDTH), cur),
                  pl.BlockSpec((1, BLOCK, KV_WIDTH), prev),
                  pl.BlockSpec((1, qrows, KV_WIDTH), cur),
                  pl.BlockSpec((1, BLOCK, KV_WIDTH), prev),
                  gain_spec, gain_spec],
        out_specs=pl.BlockSpec((1, qrows, ATTN_WIDTH), cur),
        out_shape=jax.ShapeDtypeStruct((bsz, seq, ATTN_WIDTH), BF16),
        compiler_params=_cparams(("arbitrary", "arbitrary")),
        name="attn",
    )(sinks.astype(F32), q3, k3, k3, v3, v3, qg2, kg2)


MIX_TM = 1024


def _mix_body(z_ref, ya_ref, gs_ref, ga_ref, x_ref, wglu_ref, wup_ref, wout_ref, g2_ref, wr_ref,
              x1_ref, gate_ref, hs_hbm, starts_ref, lens_ref, counts_ref,
              stage_ref, run_ref, pend_ref, sem):
    z = z_ref[...]
    bv = jnp.dot(z, wglu_ref[:, :D_MODEL], preferred_element_type=F32)
    bg = jnp.dot(z, wglu_ref[:, D_MODEL:], preferred_element_type=F32)
    bs = bv * jax.nn.sigmoid(bg)
    ba = jnp.dot(ya_ref[...], wup_ref[...], preferred_element_type=F32)
    merged = gs_ref[...].astype(F32) * bs + ga_ref[...].astype(F32) * ba
    x1 = x_ref[...] + jnp.dot(merged.astype(BF16), wout_ref[...], preferred_element_type=F32)
    x1_ref[...] = x1

    ms = jnp.mean(x1 * x1, axis=-1, keepdims=True)
    h2 = x1 * lax.rsqrt(ms + EPS) * g2_ref[...]
    h2b = h2.astype(BF16)

    lt = lax.dot_general(wr_ref[...], h2b, (((1,), (1,)), ((), ())),
                         preferred_element_type=F32)
    logits = lt[:ROUTER_ROWS] + lt[ROUTER_ROWS:]
    row = lax.broadcasted_iota(jnp.int32, logits.shape, 0)
    ninf = -jnp.inf
    cm = jnp.where(row < N_GROUPS, logits, ninf)
    cmax = jnp.max(cm, axis=0, keepdims=True)
    g_prob = 1.0 / jnp.sum(jnp.exp(cm - cmax), axis=0, keepdims=True)
    g_idx = jnp.min(jnp.where(cm == cmax, row, ROUTER_ROWS), axis=0, keepdims=True)
    f0 = FINE_OFF + EXPERTS_PER_GROUP * g_idx
    fm = jnp.where((row >= f0) & (row < f0 + EXPERTS_PER_GROUP), logits, ninf)
    v1 = jnp.max(fm, axis=0, keepdims=True)
    i1 = jnp.min(jnp.where(fm == v1, row, ROUTER_ROWS), axis=0, keepdims=True)
    fm2 = jnp.where(row == i1, ninf, fm)
    v2 = jnp.max(fm2, axis=0, keepdims=True)
    i2 = jnp.min(jnp.where(fm2 == v2, row, ROUTER_ROWS), axis=0, keepdims=True)
    e21 = jnp.exp(v2 - v1)
    w1 = g_prob / (1.0 + e21)
    w2 = w1 * e21
    rt = jnp.where(row == i1 - f0, w1,
                   jnp.where(row == i2 - f0, w2,
                             jnp.where(row == ROUTE_GROUP_OFF + g_idx, 1.0, 0.0)))
    rt = jnp.concatenate([rt, jnp.zeros((ROUTER_LANES - ROUTER_ROWS, rt.shape[1]), F32)], axis=0)
    route = rt.T
    gate_ref[...] = route
    _sort_tiles(h2b, route, hs_hbm, starts_ref, lens_ref, counts_ref, stage_ref, run_ref, pend_ref,
                sem)


def _mix(z2, ya2, gs2, ga2, x2, wglu, wup, wout, g2, wr, layer):
    n = x2.shape[0]
    n_sort = n // SORT_TM
    subs = MIX_TM // SORT_TM
    cap = _region_cap(n)
    assert subs * SORT_TM >= ZFILL_TOK
    row = lambda c: pl.BlockSpec((MIX_TM, c), lambda i: (i, 0))
    full = lambda a, b: pl.BlockSpec((a, b), lambda i: (0, 0))
    smem = pl.BlockSpec(memory_space=pltpu.SMEM)
    return pl.pallas_call(
        _mix_body,
        grid=(n // MIX_TM,),
        in_specs=[row(SSM_WIDTH), row(ATTN_WIDTH), row(D_MODEL), row(D_MODEL), row(D_MODEL),
                  _layer_spec(layer, SSM_WIDTH, 2 * D_MODEL), _layer_spec(layer, ATTN_WIDTH, D_MODEL),
                  _layer_spec(layer, D_MODEL, D_MODEL), full(1, D_MODEL),
                  _layer_spec(layer, 2 * ROUTER_ROWS, D_MODEL)],
        out_specs=[row(D_MODEL), row(ROUTER_LANES), pl.BlockSpec(memory_space=pl.ANY),
                   smem, smem, smem],
        out_shape=[jax.ShapeDtypeStruct((n, D_MODEL), F32),
                   jax.ShapeDtypeStruct((n, ROUTER_LANES), F32),
                   jax.ShapeDtypeStruct((N_GROUPS * cap * TOK_ROWS, LANES), jnp.uint32),
                   jax.ShapeDtypeStruct((n_sort * N_GROUPS,), jnp.int32),
                   jax.ShapeDtypeStruct((n_sort * N_GROUPS,), jnp.int32),
                   jax.ShapeDtypeStruct((N_GROUPS,), jnp.int32)],
        scratch_shapes=[pltpu.VMEM((subs * SORT_TM * TOK_ROWS, LANES), jnp.uint32),
                        pltpu.SMEM((N_GROUPS,), jnp.int32),
                        pltpu.SMEM((subs * N_GROUPS,), jnp.int32),
                        pltpu.SemaphoreType.DMA((subs, N_GROUPS))],
        compiler_params=_cparams(("arbitrary",)),
        name="mix",
    )(z2, ya2, gs2, ga2, x2, wglu, wup, wout, g2, wr)


SORT_TM = 256
COMBINE_TM = 1024
MOE_TM = 512
TOK_ROWS = 8
H_WORDS = D_MODEL // 2
H_CHUNKS = H_WORDS // LANES
ROUTE_ROW = H_CHUNKS
ZFILL_TOK = MOE_TM
HI_MASK = 0xFFFF0000


def _region_cap(n):
    cap = n + ZFILL_TOK
    return -(-cap // MOE_TM) * MOE_TM


def _tile_positions(route):
    tm = route.shape[0]
    lane = lax.broadcasted_iota(jnp.int32, route.shape, 1)
    onehot = jnp.where((lane >= ROUTE_GROUP_OFF) & (lane < ROUTE_GROUP_OFF + N_GROUPS), route, 0.0)
    ci = lax.broadcasted_iota(jnp.int32, (tm, tm), 0)
    cj = lax.broadcasted_iota(jnp.int32, (tm, tm), 1)
    earlier = (cj < ci).astype(BF16)
    rank = jnp.dot(earlier, onehot.astype(BF16), preferred_element_type=F32)
    cnt = jnp.sum(onehot, axis=0, keepdims=True)
    li = lax.broadcasted_iota(jnp.int32, (LANES, LANES), 0)
    lj = lax.broadcasted_iota(jnp.int32, (LANES, LANES), 1)
    base = jnp.dot(jnp.broadcast_to(cnt, (8, LANES)).astype(BF16), (li < lj).astype(BF16),
                   preferred_element_type=F32)[0:1]
    return onehot, rank, base, cnt


def _sort_tiles(h, route, hs_hbm, starts_ref, lens_ref, counts_ref, stage_ref, run_ref, pend_ref,
                sem):
    t = pl.program_id(0)
    nt = pl.num_programs(0)
    tm = SORT_TM
    subs = h.shape[0] // tm
    slot_rows = tm * TOK_ROWS
    cap = hs_hbm.shape[0] // (N_GROUPS * TOK_ROWS)

    def run_copy(src_row, dst_row, sl, g, tokens):
        return pltpu.make_async_copy(stage_ref.at[pl.ds(src_row, tokens * TOK_ROWS), :],
                                     hs_hbm.at[pl.ds(dst_row, tokens * TOK_ROWS), :],
                                     sem.at[sl, g])

    def wait_slot(sl):
        for g in range(N_GROUPS):
            n = pend_ref[sl * N_GROUPS + g]

            @pl.when(n > 0)
            def _(g=g, n=n):
                run_copy(0, 0, sl, g, n).wait()

    @pl.when(t == 0)
    def _():
        stage_ref[...] = jnp.zeros_like(stage_ref)
        for g in range(N_GROUPS):
            run_ref[g] = 0

    @pl.when(t > 0)
    def _():
        for sl in range(subs):
            wait_slot(sl)

    for sl in range(subs):
        tile = t * subs + sl
        r = route[sl * tm:(sl + 1) * tm]
        onehot, rank, base, cnt = _tile_positions(r)
        w = (onehot * (rank + base)).astype(BF16)
        pos_row = lax.dot_general(jnp.ones((8, LANES), BF16), w, (((1,), (1,)), ((), ())),
                                  preferred_element_type=F32)[0:1]
        ri = lax.broadcasted_iota(jnp.int32, (tm, tm), 0).astype(F32)
        perm = (ri == pos_row).astype(BF16)
        r_hi = r.astype(BF16)
        r_mid = (r - r_hi.astype(F32)).astype(BF16)
        r_lo = (r - r_hi.astype(F32) - r_mid.astype(F32)).astype(BF16)
        sorted_all = jnp.dot(
            perm, jnp.concatenate([h[sl * tm:(sl + 1) * tm], r_hi, r_mid, r_lo], axis=1),
            preferred_element_type=F32)
        sh = sorted_all[:, :D_MODEL]
        sr = (sorted_all[:, D_MODEL:D_MODEL + LANES]
              + sorted_all[:, D_MODEL + LANES:D_MODEL + 2 * LANES]
              + sorted_all[:, D_MODEL + 2 * LANES:])
        hb = lax.bitcast_convert_type(sh, jnp.uint32)
        words = (hb[:, :H_WORDS] & jnp.uint32(HI_MASK)) | (hb[:, H_WORDS:] >> 16)

        slot0 = sl * slot_rows
        for s in range(H_CHUNKS):
            stage_ref[pl.ds(slot0 + s, tm, stride=TOK_ROWS), :] = words[:, s * LANES:(s + 1) * LANES]
        stage_ref[pl.ds(slot0 + ROUTE_ROW, tm, stride=TOK_ROWS), :] = lax.bitcast_convert_type(
            sr, jnp.uint32)

        for g in range(N_GROUPS):
            c_g = cnt[0, ROUTE_GROUP_OFF + g].astype(jnp.int32)
            b_g = base[0, ROUTE_GROUP_OFF + g].astype(jnp.int32)
            start = run_ref[g]
            starts_ref[tile * N_GROUPS + g] = start
            lens_ref[tile * N_GROUPS + g] = c_g
            pend_ref[sl * N_GROUPS + g] = c_g
            run_ref[g] = start + c_g
            src = pl.multiple_of(slot0 + b_g * TOK_ROWS, TOK_ROWS)
            dst = pl.multiple_of((g * cap + start) * TOK_ROWS, TOK_ROWS)

            @pl.when(c_g > 0)
            def _(sl=sl, g=g, src=src, dst=dst, c_g=c_g):
                run_copy(src, dst, sl, g, c_g).start()

    @pl.when(t == nt - 1)
    def _():
        for sl in range(subs):
            wait_slot(sl)
        stage_ref[...] = jnp.zeros_like(stage_ref)
        zrows = ZFILL_TOK * TOK_ROWS
        for g in range(N_GROUPS):
            total = run_ref[g]
            counts_ref[g] = total
            dst = pl.multiple_of((g * cap + total) * TOK_ROWS, TOK_ROWS)
            pltpu.make_async_copy(stage_ref.at[pl.ds(0, zrows), :],
                                  hs_hbm.at[pl.ds(dst, zrows), :], sem.at[0, g]).start()
        for g in range(N_GROUPS):
            pltpu.make_async_copy(stage_ref.at[pl.ds(0, zrows), :],
                                  hs_hbm.at[pl.ds(0, zrows), :], sem.at[0, g]).wait()


def _experts_body(blk_ref, grp_ref, valid_ref, hs_ref, wu32_ref, wd32_ref, ys_ref, wu_ref, wd_ref):
    t = pl.program_id(0)
    tm = MOE_TM

    @pl.when((t == 0) | (grp_ref[t] != grp_ref[jnp.maximum(t - 1, 0)]))
    def _():
        wu_ref[...] = wu32_ref[...].astype(BF16)
        wd_ref[...] = wd32_ref[...].astype(BF16)

    @pl.when(valid_ref[t] > 0)
    def _():
        chunks = [hs_ref[pl.ds(s, tm, stride=TOK_ROWS), :] for s in range(H_CHUNKS + 1)]
        hi = [lax.bitcast_convert_type(c & jnp.uint32(HI_MASK), F32) for c in chunks[:H_CHUNKS]]
        lo = [lax.bitcast_convert_type(c << 16, F32) for c in chunks[:H_CHUNKS]]
        h = jnp.concatenate(hi + lo, axis=1).astype(BF16)
        route = lax.bitcast_convert_type(chunks[ROUTE_ROW], F32)
        acts = []
        for e in range(EXPERTS_PER_GROUP):
            gu = jnp.dot(h, wu_ref[e], preferred_element_type=F32)
            acts.append((jax.nn.silu(gu[:, :D_EXPERT]) * gu[:, D_EXPERT:]
                         * route[:, e:e + 1]).astype(BF16))
        act = jnp.concatenate(acts, axis=1)
        wd = wd_ref[...].reshape(EXPERTS_PER_GROUP * D_EXPERT, D_MODEL)
        out = jnp.dot(act, wd, preferred_element_type=F32)
        for j in range(D_MODEL // LANES):
            ys_ref[pl.ds(j, tm, stride=TOK_ROWS), :] = out[:, j * LANES:(j + 1) * LANES]


def _experts(blk, grp, valid, hs, wu, wd, layer):
    g0 = layer * N_GROUPS
    n_tiles = blk.shape[0]
    epg = EXPERTS_PER_GROUP
    return pl.pallas_call(
        _experts_body,
        grid_spec=pltpu.PrefetchScalarGridSpec(
            num_scalar_prefetch=3,
            grid=(n_tiles,),
            in_specs=[pl.BlockSpec((MOE_TM * TOK_ROWS, LANES), lambda t, b, g, v: (b[t], 0)),
                      pl.BlockSpec((epg, D_MODEL, 2 * D_EXPERT),
                                   lambda t, b, g, v: (g0 + g[t], 0, 0)),
                      pl.BlockSpec((epg, D_EXPERT, D_MODEL),
                                   lambda t, b, g, v: (g0 + g[t], 0, 0))],
            out_specs=pl.BlockSpec((MOE_TM * TOK_ROWS, LANES), lambda t, b, g, v: (b[t], 0)),
            scratch_shapes=[pltpu.VMEM((epg, D_MODEL, 2 * D_EXPERT), BF16),
                            pltpu.VMEM((epg, D_EXPERT, D_MODEL), BF16)],
        ),
        out_shape=jax.ShapeDtypeStruct(hs.shape, F32),
        compiler_params=_cparams(("arbitrary",)),
        name="moe_experts",
    )(blk, grp, valid, hs, wu, wd)


def _combine_body(starts_ref, lens_ref, x1_ref, route_ref, ys_hbm, o_ref, buf_ref, sem):
    t = pl.program_id(0)
    nt = pl.num_programs(0)
    tm = SORT_TM
    subs = COMBINE_TM // tm
    slot_rows = tm * TOK_ROWS
    cap = ys_hbm.shape[0] // (N_GROUPS * TOK_ROWS)

    def fetches(step, sl):
        tile = step * subs + sl
        slot = (step % 2) * subs + sl
        copies = []
        first = 0
        for g in range(N_GROUPS):
            tokens = lens_ref[tile * N_GROUPS + g]
            src = pl.multiple_of((g * cap + starts_ref[tile * N_GROUPS + g]) * TOK_ROWS, TOK_ROWS)
            dst = pl.multiple_of(slot * slot_rows + first * TOK_ROWS, TOK_ROWS)
            copies.append((tokens, pltpu.make_async_copy(
                ys_hbm.at[pl.ds(src, tokens * TOK_ROWS), :],
                buf_ref.at[pl.ds(dst, tokens * TOK_ROWS), :], sem.at[slot, g])))
            first = first + tokens
        return copies

    def start_fetches(step):
        for sl in range(subs):
            for tokens, copy in fetches(step, sl):
                @pl.when(tokens > 0)
                def _(copy=copy):
                    copy.start()

    @pl.when(t == 0)
    def _():
        start_fetches(0)

    @pl.when(t + 1 < nt)
    def _():
        start_fetches(t + 1)

    for sl in range(subs):
        rows = slice(sl * tm, (sl + 1) * tm)
        onehot, rank, base, _ = _tile_positions(route_ref[rows, :])
        pos = jnp.sum(onehot * (rank + base), axis=1, keepdims=True)
        ri = lax.broadcasted_iota(jnp.int32, (tm, tm), 1).astype(F32)
        sel = (ri == pos).astype(BF16)
        for tokens, copy in fetches(t, sl):
            @pl.when(tokens > 0)
            def _(copy=copy):
                copy.wait()
        r0 = pl.multiple_of(((t % 2) * subs + sl) * slot_rows, TOK_ROWS)
        y = jnp.concatenate(
            [buf_ref[pl.ds(r0 + j, tm, stride=TOK_ROWS), :] for j in range(D_MODEL // LANES)],
            axis=1).astype(BF16)
        o_ref[rows, :] = x1_ref[rows, :] + jnp.dot(sel, y, preferred_element_type=F32)


def _combine(starts, lens, x1, route, ys):
    n = x1.shape[0]
    subs = COMBINE_TM // SORT_TM
    return pl.pallas_call(
        _combine_body,
        grid_spec=pltpu.PrefetchScalarGridSpec(
            num_scalar_prefetch=2,
            grid=(n // COMBINE_TM,),
            in_specs=[pl.BlockSpec((COMBINE_TM, D_MODEL), lambda i, s, c: (i, 0)),
                      pl.BlockSpec((COMBINE_TM, ROUTER_LANES), lambda i, s, c: (i, 0)),
                      pl.BlockSpec(memory_space=pl.ANY)],
            out_specs=pl.BlockSpec((COMBINE_TM, D_MODEL), lambda i, s, c: (i, 0)),
            scratch_shapes=[pltpu.VMEM((2 * COMBINE_TM * TOK_ROWS, LANES), F32),
                            pltpu.SemaphoreType.DMA((2 * subs, N_GROUPS))],
        ),
        out_shape=jax.ShapeDtypeStruct((n, D_MODEL), F32),
        compiler_params=_cparams(("arbitrary",)),
        name="moe_combine",
    )(starts, lens, x1, route, ys)


def _expert_tiles(counts, n):
    cap_blocks = _region_cap(n) // MOE_TM
    n_tiles = n // MOE_TM + N_GROUPS
    per_group = (counts + MOE_TM - 1) // MOE_TM
    ends = jnp.cumsum(per_group)
    total = ends[-1]
    t = jnp.minimum(jnp.arange(n_tiles, dtype=jnp.int32), total - 1)
    grp = jnp.sum((t[:, None] >= ends[None, :]).astype(jnp.int32), axis=1)
    first = ends - per_group
    blk = grp * cap_blocks + (t - first[grp])
    valid = (jnp.arange(n_tiles, dtype=jnp.int32) < total).astype(jnp.int32)
    return blk.astype(jnp.int32), grp.astype(jnp.int32), valid


def _moe(x1, route, hs, starts, lens, counts, w_up, w_down, layer):
    blk, grp, valid = _expert_tiles(counts, x1.shape[0])
    ys = _experts(blk, grp, valid, hs, w_up, w_down, layer)
    return _combine(starts, lens, x1, route, ys)


def kernel(x, norm_mix, w_in, ssm_lam_re, ssm_lam_im, ssm_log_dt, ssm_b_re, ssm_b_im, ssm_c_re,
           ssm_c_im, ssm_d, w_ssm_glu_val, w_ssm_glu_gate, q_norm, k_norm, attn_sinks, w_attn_up,
           w_out, norm_ffn, w_coarse, w_fine, w_expert_up, w_expert_down):
    bsz, seq, d = x.shape
    n = bsz * seq
    depth = w_in.shape[0]
    x2 = x.reshape(n, d)
    s5p = jax.vmap(_s5_params)(ssm_lam_re, ssm_lam_im, ssm_log_dt, ssm_b_re, ssm_b_im, ssm_c_re,
                               ssm_c_im, ssm_d)
    w_in_b = w_in.astype(BF16)
    wglu = jnp.concatenate([w_ssm_glu_val, w_ssm_glu_gate], axis=2).astype(BF16)
    wup_b = w_attn_up.astype(BF16)
    wout_b = w_out.astype(BF16)
    wr_t = jnp.concatenate(
        [jnp.swapaxes(w_coarse, 1, 2), jnp.swapaxes(w_fine, 1, 2),
         jnp.zeros((depth, ROUTER_ROWS - N_GROUPS - N_EXPERTS, d), w_fine.dtype)], axis=1).astype(F32)
    wr_hi = wr_t.astype(BF16)
    wr = jnp.concatenate([wr_hi, (wr_t - wr_hi.astype(F32)).astype(BF16)], axis=1)
    wu_all = w_expert_up.reshape((depth * N_EXPERTS,) + w_expert_up.shape[2:])
    wd_all = w_expert_down.reshape((depth * N_EXPERTS,) + w_expert_down.shape[2:])
    for i in range(depth):
        u, q, k, v, gs, ga = _proj(x2, norm_mix[i].reshape(1, d), w_in_b, i)
        z = _s5(u.reshape(bsz, seq, SSM_WIDTH), *(p[i] for p in s5p))
        ya = _attn(q.reshape(bsz, seq, ATTN_WIDTH), k.reshape(bsz, seq, KV_WIDTH),
                   v.reshape(bsz, seq, KV_WIDTH), q_norm[i], k_norm[i], attn_sinks[i])
        x1, route, hs, starts, lens, counts = _mix(
            z.reshape(n, SSM_WIDTH), ya.reshape(n, ATTN_WIDTH), gs, ga, x2,
            wglu, wup_b, wout_b, norm_ffn[i].reshape(1, d), wr, i)
        x2 = _moe(x1, route, hs, starts, lens, counts, wu_all, wd_all, i)
    return x2.reshape(bsz, seq, d)
```

```python
import functools
import math

import jax
import jax.numpy as jnp
from jax import lax
from jax.experimental import pallas as pl
from jax.experimental.pallas import tpu as pltpu

F32 = jnp.float32
BF16 = jnp.bfloat16

D_MODEL = 1024
SSM_WIDTH = 512
SSM_GROUP = 16
SSM_GROUPS = 32
SSM_STATE = 64
N_HEADS = 8
N_KV_HEADS = 2
HEAD_DIM = 64
ATTN_WIDTH = 512
KV_WIDTH = 128
BLOCK = 128
N_GROUPS = 4
EXPERTS_PER_GROUP = 4
N_EXPERTS = 16
D_EXPERT = 256
EPS = 1e-6
NEG = -1e30

LANES = 128
SLABS = SSM_WIDTH // LANES
GROUPS_PER_SLAB = LANES // SSM_GROUP
SLAB_STATE = GROUPS_PER_SLAB * SSM_STATE
ROUTER_LANES = 128
ROUTER_ROWS = 32
FINE_OFF = N_GROUPS
ROUTE_GROUP_OFF = 8

VMEM_LIMIT = 56 * 1024 * 1024


def _cparams(sem):
    return pltpu.CompilerParams(dimension_semantics=sem, vmem_limit_bytes=VMEM_LIMIT)


PROJ_TM = 1024


def _proj_body(x_ref, g_ref, w_ref, *out_refs):
    _project(x_ref[...], g_ref, w_ref, *out_refs)


def _proj_moe_body(starts_ref, lens_ref, x1_ref, route_ref, ys_hbm, g_ref, w_ref,
                   x_ref, *rest):
    out_refs, (buf_ref, sem) = rest[:-2], rest[-2:]
    moe = _moe_outputs(starts_ref, lens_ref, route_ref, ys_hbm, buf_ref, sem)
    x = x1_ref[...] + jnp.concatenate(moe, axis=0)
    x_ref[...] = x
    _project(x, g_ref, w_ref, *out_refs)


def _project(x, g_ref, w_ref, u_ref, q_ref, k_ref, v_ref, gs_ref, ga_ref):
    ms = jnp.mean(x * x, axis=-1, keepdims=True)
    h = (x * lax.rsqrt(ms + EPS) * g_ref[...]).astype(BF16)

    def seg(a, b):
        return jnp.dot(h, w_ref[:, a:b], preferred_element_type=F32)

    o = 0
    u_ref[...] = seg(o, o + SSM_WIDTH)
    o += SSM_WIDTH
    q_ref[...] = seg(o, o + ATTN_WIDTH).astype(BF16)
    o += ATTN_WIDTH
    kv = seg(o, o + 2 * KV_WIDTH).astype(BF16)
    k_ref[...] = kv[:, :KV_WIDTH]
    v_ref[...] = kv[:, KV_WIDTH:]
    o += 2 * KV_WIDTH
    gs_ref[...] = jax.nn.sigmoid(seg(o, o + D_MODEL)).astype(BF16)
    o += D_MODEL
    ga_ref[...] = jax.nn.sigmoid(seg(o, o + D_MODEL)).astype(BF16)


def _layer_spec(layer, *shape):
    return pl.BlockSpec((None,) + shape, lambda *_: (layer,) + (0,) * len(shape))


def _proj_out(n):
    row = lambda c: pl.BlockSpec((PROJ_TM, c), lambda i, *_: (i, 0))
    specs = [row(SSM_WIDTH), row(ATTN_WIDTH), row(KV_WIDTH), row(KV_WIDTH), row(D_MODEL),
             row(D_MODEL)]
    shapes = [jax.ShapeDtypeStruct((n, SSM_WIDTH), F32),
              jax.ShapeDtypeStruct((n, ATTN_WIDTH), BF16),
              jax.ShapeDtypeStruct((n, KV_WIDTH), BF16),
              jax.ShapeDtypeStruct((n, KV_WIDTH), BF16),
              jax.ShapeDtypeStruct((n, D_MODEL), BF16),
              jax.ShapeDtypeStruct((n, D_MODEL), BF16)]
    return specs, shapes


def _proj(x2, g, w, layer):
    n = x2.shape[0]
    cols = w.shape[2]
    row = lambda c: pl.BlockSpec((PROJ_TM, c), lambda i: (i, 0))
    full = lambda a, b: pl.BlockSpec((a, b), lambda i: (0, 0))
    out_specs, out_shape = _proj_out(n)
    return pl.pallas_call(
        _proj_body,
        grid=(n // PROJ_TM,),
        in_specs=[row(D_MODEL), full(1, D_MODEL), _layer_spec(layer, D_MODEL, cols)],
        out_specs=out_specs,
        out_shape=out_shape,
        compiler_params=_cparams(("arbitrary",)),
        name="proj",
    )(x2, g, w)


def _proj_moe(starts, lens, x1, route, ys, g, w, layer):
    n = x1.shape[0]
    cols = w.shape[2]
    subs = PROJ_TM // SORT_TM
    row = lambda c: pl.BlockSpec((PROJ_TM, c), lambda i, *_: (i, 0))
    out_specs, out_shape = _proj_out(n)
    return pl.pallas_call(
        _proj_moe_body,
        grid_spec=pltpu.PrefetchScalarGridSpec(
            num_scalar_prefetch=2,
            grid=(n // PROJ_TM,),
            in_specs=[row(D_MODEL), row(ROUTER_LANES), pl.BlockSpec(memory_space=pl.ANY),
                      pl.BlockSpec((1, D_MODEL), lambda i, *_: (0, 0)),
                      _layer_spec(layer, D_MODEL, cols)],
            out_specs=[row(D_MODEL)] + out_specs,
            scratch_shapes=[pltpu.VMEM((2 * PROJ_TM * TOK_ROWS, LANES), F32),
                            pltpu.SemaphoreType.DMA((2 * subs, N_GROUPS))],
        ),
        out_shape=[jax.ShapeDtypeStruct((n, D_MODEL), F32)] + out_shape,
        compiler_params=_cparams(("arbitrary",)),
        name="proj_moe",
    )(starts, lens, x1, route, ys, g, w)


S5_TT = 64
S5_PAIRS = S5_TT // 2


def _s5_body(u_ref, bblk_ref, cblk_ref, k0_ref, lam_ref, d_ref, z_ref,
             ut_ref, y_ref, st_ref, carry_ref, *, bsz):
    rows = S5_TT * bsz
    prow = S5_PAIRS * bsz

    @pl.when(pl.program_id(0) == 0)
    def _():
        carry_ref[...] = jnp.zeros_like(carry_ref)

    for b in range(bsz):
        for j in range(SLABS):
            ut_ref[j, pl.ds(b, S5_TT, stride=bsz), :] = u_ref[b, :, j * LANES:(j + 1) * LANES]

    for j in range(SLABS):
        u3 = ut_ref[j].reshape(S5_PAIRS, 2 * bsz, LANES)
        u0 = u3[:, :bsz, :].reshape(prow, LANES)
        u1 = u3[:, bsz:, :].reshape(prow, LANES)
        lhs = jnp.concatenate([u1, u0], axis=1).astype(BF16)
        st_ref[j, 0:bsz, :] = carry_ref[j]
        st_ref[j, bsz:, :] = jnp.dot(lhs, bblk_ref[j], preferred_element_type=F32)
        a = jnp.broadcast_to(lam_ref[j, 0:1, :], (bsz, SLAB_STATE))
        bb = jnp.broadcast_to(lam_ref[j, 1:2, :], (bsz, SLAB_STATE))

        cre = carry_ref[j, :, 0:SLAB_STATE]
        cim = carry_ref[j, :, SLAB_STATE:2 * SLAB_STATE]
        for k in range(S5_PAIRS):
            r0 = (k + 1) * bsz
            bre = st_ref[j, r0:r0 + bsz, 0:SLAB_STATE]
            bim = st_ref[j, r0:r0 + bsz, SLAB_STATE:2 * SLAB_STATE]
            cre, cim = a * cre - bb * cim + bre, a * cim + bb * cre + bim
            st_ref[j, r0:r0 + bsz, 0:SLAB_STATE] = cre
            st_ref[j, r0:r0 + bsz, SLAB_STATE:2 * SLAB_STATE] = cim
        carry_ref[j, :, 0:SLAB_STATE] = cre
        carry_ref[j, :, SLAB_STATE:2 * SLAB_STATE] = cim

        cs = jnp.dot(st_ref[j].astype(BF16), cblk_ref[j], preferred_element_type=F32)
        y1 = cs[bsz:, :LANES] + d_ref[j] * u1
        y0 = (cs[:prow, LANES:] + jnp.dot(u0.astype(BF16), k0_ref[j], preferred_element_type=F32)
              + d_ref[j] * u0)
        y = jnp.concatenate([y0.reshape(S5_PAIRS, bsz, LANES), y1.reshape(S5_PAIRS, bsz, LANES)],
                            axis=1).reshape(rows, LANES)
        y_ref[j] = jax.nn.gelu(y)

    for b in range(bsz):
        for j in range(SLABS):
            z_ref[b, :, j * LANES:(j + 1) * LANES] = (
                y_ref[j, pl.ds(b, S5_TT, stride=bsz), :].astype(BF16))


def _s5(u3, bblk, cblk, k0, lam, dskip):
    bsz, seq, _ = u3.shape
    rows = S5_TT * bsz
    full = lambda *s: pl.BlockSpec(s, lambda i: (0,) * len(s))
    return pl.pallas_call(
        functools.partial(_s5_body, bsz=bsz),
        grid=(seq // S5_TT,),
        in_specs=[pl.BlockSpec((bsz, S5_TT, SSM_WIDTH), lambda i: (0, i, 0)),
                  full(SLABS, 2 * LANES, 2 * SLAB_STATE),
                  full(SLABS, 2 * SLAB_STATE, 2 * LANES),
                  full(SLABS, LANES, LANES),
                  full(SLABS, 2, SLAB_STATE),
                  full(SLABS, 1, LANES)],
        out_specs=pl.BlockSpec((bsz, S5_TT, SSM_WIDTH), lambda i: (0, i, 0)),
        out_shape=jax.ShapeDtypeStruct((bsz, seq, SSM_WIDTH), BF16),
        scratch_shapes=[pltpu.VMEM((SLABS, rows, LANES), F32),
                        pltpu.VMEM((SLABS, rows, LANES), F32),
                        pltpu.VMEM((SLABS, bsz + S5_PAIRS * bsz, 2 * SLAB_STATE), F32),
                        pltpu.VMEM((SLABS, bsz, 2 * SLAB_STATE), F32)],
        compiler_params=_cparams(("arbitrary",)),
        name="s5",
    )(u3, bblk, cblk, k0, lam, dskip)


def _s5_params(lam_re, lam_im, log_dt, b_re, b_im, c_re, c_im, d_skip):
    lam = lax.complex(lam_re.astype(F32), lam_im.astype(F32))
    dt = jnp.exp(log_dt.astype(F32))[:, None]
    lam_bar = jnp.exp(lam * dt)
    b = lax.complex(b_re.astype(F32), b_im.astype(F32))
    b_bar = ((lam_bar - 1.0) / lam)[..., None] * b
    eye = jnp.eye(GROUPS_PER_SLAB, dtype=F32)

    def in_blk(part):
        p4 = part.reshape(SLABS, GROUPS_PER_SLAB, SSM_STATE, SSM_GROUP)
        m = jnp.einsum('jgph,gk->jghkp', p4, eye)
        return m.reshape(SLABS, LANES, SLAB_STATE)

    def in_cplx(z):
        return jnp.concatenate([in_blk(z.real), in_blk(z.imag)], axis=-1)

    bblk = jnp.concatenate([in_cplx(b_bar), in_cplx(lam_bar[..., None] * b_bar)],
                           axis=1).astype(BF16)

    def out_blk(part):
        p4 = part.reshape(SLABS, GROUPS_PER_SLAB, SSM_GROUP, SSM_STATE)
        m = jnp.einsum('jghp,gk->jgpkh', p4, eye)
        return m.reshape(SLABS, SLAB_STATE, LANES)

    def out_cplx(z):
        return jnp.concatenate([out_blk(z.real), -out_blk(z.imag)], axis=1)

    c = lax.complex(c_re.astype(F32), c_im.astype(F32))
    cblk = jnp.concatenate([out_cplx(c), out_cplx(c * lam_bar[:, None, :])], axis=-1).astype(BF16)
    k0 = jnp.einsum('ghp,gpk->gkh', c, b_bar).real.reshape(SLABS, GROUPS_PER_SLAB, SSM_GROUP,
                                                           SSM_GROUP)
    k0 = jnp.einsum('jgkh,gq->jgkqh', k0, eye).reshape(SLABS, LANES, LANES).astype(BF16)
    lam2 = lam_bar * lam_bar
    lam_k = jnp.stack([lam2.real.reshape(SLABS, SLAB_STATE),
                       lam2.imag.reshape(SLABS, SLAB_STATE)], axis=1)
    dsk = d_skip.astype(F32).reshape(SLABS, 1, LANES)
    return bblk, cblk, k0, lam_k, dsk


ATTN_QB = 8
HEADS_PER_TILE = LANES // HEAD_DIM


def _pair_norm(t, gain2, head_mean):
    sq = t * t
    hi = sq.astype(BF16)
    lo = (sq - hi.astype(F32)).astype(BF16)
    ms = (jnp.dot(hi, head_mean, preferred_element_type=F32)
          + jnp.dot(lo, head_mean, preferred_element_type=F32))
    return t * lax.rsqrt(ms + EPS) * gain2


def _attn_body(sink_ref, q_ref, kc_ref, kp_ref, vc_ref, vp_ref, qg_ref, kg_ref, o_ref):
    n = pl.program_id(1)
    lane = lax.broadcasted_iota(jnp.int32, (1, LANES), 1)
    lo = lane < HEAD_DIM
    li = lax.broadcasted_iota(jnp.int32, (LANES, LANES), 0)
    lj = lax.broadcasted_iota(jnp.int32, (LANES, LANES), 1)
    head_mean = jnp.where((li < HEAD_DIM) == (lj < HEAD_DIM), 1.0 / HEAD_DIM, 0.0).astype(BF16)
    ones = jnp.ones((LANES, LANES), BF16)

    kall = jnp.concatenate([kp_ref[0], kc_ref[0]], axis=0).astype(F32)
    kn = _pair_norm(kall, kg_ref[...], head_mean)
    kroll = pltpu.roll(kn, HEAD_DIM, axis=1)
    vall = jnp.concatenate([vp_ref[0], vc_ref[0]], axis=0).astype(F32)
    vroll = pltpu.roll(vall, HEAD_DIM, axis=1)
    zero = jnp.zeros_like(kn)
    k_var = [[jnp.where(lo, kn, zero).astype(BF16), jnp.where(lo, zero, kroll).astype(BF16)],
             [jnp.where(lo, kroll, zero).astype(BF16), jnp.where(lo, zero, kn).astype(BF16)]]
    v_var = [[jnp.where(lo, vall, zero).astype(BF16), jnp.where(lo, zero, vroll).astype(BF16)],
             [jnp.where(lo, vroll, zero).astype(BF16), jnp.where(lo, zero, vall).astype(BF16)]]

    qn = []
    for i in range(ATTN_WIDTH // LANES):
        t = _pair_norm(q_ref[0, :, i * LANES:(i + 1) * LANES].astype(F32), qg_ref[...], head_mean)
        qn.append((t * (HEAD_DIM ** -0.5)).astype(BF16))

    qi = lax.broadcasted_iota(jnp.int32, (BLOCK, BLOCK), 0)
    si = lax.broadcasted_iota(jnp.int32, (BLOCK, BLOCK), 1)
    is_cur = si <= qi
    dist = jnp.where(is_cur, qi - si, qi - si + BLOCK).astype(F32)
    alibi = [(2.0 ** (-8.0 * (h + 1) / N_HEADS)) * dist for h in range(N_HEADS)]
    sinkmat = jnp.concatenate([jnp.full((BLOCK, LANES), sink_ref[h], F32) for h in range(N_HEADS)],
                              axis=0)

    for b in range(ATTN_QB):
        rows = slice(b * BLOCK, (b + 1) * BLOCK)
        win = slice(b * BLOCK, (b + 2) * BLOCK)
        s_all = []
        for h in range(N_HEADS):
            i, slot = divmod(h, HEADS_PER_TILE)
            kvh = h // (N_HEADS // N_KV_HEADS)
            sc = lax.dot_general(qn[i][rows], k_var[kvh][slot][win], (((1,), (1,)), ((), ())),
                                 preferred_element_type=F32)
            prev = sc[:, :BLOCK]
            if b == 0:
                prev = jnp.where(n > 0, prev, NEG)
            s_all.append(jnp.where(is_cur, sc[:, BLOCK:], prev) - alibi[h])
        s = jnp.concatenate(s_all, axis=0)
        m = jnp.maximum(jnp.max(s, axis=-1, keepdims=True), sinkmat)
        p = jnp.exp(s - m)
        denom = jnp.dot(p.astype(BF16), ones, preferred_element_type=F32) + jnp.exp(sinkmat - m)
        pn = p / denom
        for i in range(ATTN_WIDTH // LANES):
            acc = jnp.zeros((BLOCK, LANES), F32)
            for slot in range(HEADS_PER_TILE):
                h = i * HEADS_PER_TILE + slot
                kvh = h // (N_HEADS // N_KV_HEADS)
                ph = pn[h * BLOCK:(h + 1) * BLOCK]
                pcat = jnp.concatenate([jnp.where(is_cur, 0.0, ph), jnp.where(is_cur, ph, 0.0)],
                                       axis=1).astype(BF16)
                acc = acc + jnp.dot(pcat, v_var[kvh][slot][win], preferred_element_type=F32)
            o_ref[0, rows, i * LANES:(i + 1) * LANES] = acc.astype(BF16)


def _attn(q3, k3, v3, q_gain, k_gain, sinks):
    bsz, seq, _ = q3.shape
    qrows = ATTN_QB * BLOCK
    cur = lambda b, n: (b, n, 0)
    prev = lambda b, n: (b, jnp.maximum(n * ATTN_QB - 1, 0), 0)
    qg2 = jnp.tile(q_gain.astype(F32), HEADS_PER_TILE).reshape(1, LANES)
    kg2 = jnp.tile(k_gain.astype(F32), HEADS_PER_TILE).reshape(1, LANES)
    gain_spec = pl.BlockSpec((1, LANES), lambda b, n: (0, 0))
    return pl.pallas_call(
        _attn_body,
        grid=(bsz, seq // qrows),
        in_specs=[pl.BlockSpec(memory_space=pltpu.SMEM),
                  pl.BlockSpec((1, qrows, ATTN_WIDTH), cur),
                  pl.BlockSpec((1, qrows, KV_WIDTH), cur),
                  pl.BlockSpec((1, BLOCK, KV_WIDTH), prev),
                  pl.BlockSpec((1, qrows, KV_WIDTH), cur),
                  pl.BlockSpec((1, BLOCK, KV_WIDTH), prev),
                  gain_spec, gain_spec],
        out_specs=pl.BlockSpec((1, qrows, ATTN_WIDTH), cur),
        out_shape=jax.ShapeDtypeStruct((bsz, seq, ATTN_WIDTH), BF16),
        compiler_params=_cparams(("arbitrary", "arbitrary")),
        name="attn",
    )(sinks.astype(F32), q3, k3, k3, v3, v3, qg2, kg2)


MIX_TM = 1024


def _mix_body(z_ref, ya_ref, gs_ref, ga_ref, x_ref, wglu_ref, wup_ref, wout_ref, g2_ref, wr_ref,
              x1_ref, gate_ref, hs_hbm, starts_ref, lens_ref, counts_ref,
              stage_ref, run_ref, pend_ref, sem):
    t = pl.program_id(0)
    sorter = _GroupSorter(t, MIX_TM // SORT_TM, hs_hbm, starts_ref, lens_ref, counts_ref,
                          stage_ref, run_ref, pend_ref, sem)
    sorter.begin()

    z = z_ref[...]
    bv = jnp.dot(z, wglu_ref[:, :D_MODEL], preferred_element_type=F32)
    bg = jnp.dot(z, wglu_ref[:, D_MODEL:], preferred_element_type=F32)
    bs = bv * jax.nn.sigmoid(bg)
    ba = jnp.dot(ya_ref[...], wup_ref[...], preferred_element_type=F32)
    merged = gs_ref[...].astype(F32) * bs + ga_ref[...].astype(F32) * ba
    x1 = x_ref[...] + jnp.dot(merged.astype(BF16), wout_ref[...], preferred_element_type=F32)
    x1_ref[...] = x1

    ms = jnp.mean(x1 * x1, axis=-1, keepdims=True)
    h2 = x1 * lax.rsqrt(ms + EPS) * g2_ref[...]
    h2b = h2.astype(BF16)

    lt = lax.dot_general(wr_ref[...], h2b, (((1,), (1,)), ((), ())),
                         preferred_element_type=F32)
    logits = lt[:ROUTER_ROWS] + lt[ROUTER_ROWS:]
    row = lax.broadcasted_iota(jnp.int32, logits.shape, 0)
    ninf = -jnp.inf
    cm = jnp.where(row < N_GROUPS, logits, ninf)
    cmax = jnp.max(cm, axis=0, keepdims=True)
    g_prob = 1.0 / jnp.sum(jnp.exp(cm - cmax), axis=0, keepdims=True)
    g_idx = jnp.min(jnp.where(cm == cmax, row, ROUTER_ROWS), axis=0, keepdims=True)
    f0 = FINE_OFF + EXPERTS_PER_GROUP * g_idx
    fm = jnp.where((row >= f0) & (row < f0 + EXPERTS_PER_GROUP), logits, ninf)
    v1 = jnp.max(fm, axis=0, keepdims=True)
    i1 = jnp.min(jnp.where(fm == v1, row, ROUTER_ROWS), axis=0, keepdims=True)
    fm2 = jnp.where(row == i1, ninf, fm)
    v2 = jnp.max(fm2, axis=0, keepdims=True)
    i2 = jnp.min(jnp.where(fm2 == v2, row, ROUTER_ROWS), axis=0, keepdims=True)
    e21 = jnp.exp(v2 - v1)
    w1 = g_prob / (1.0 + e21)
    w2 = w1 * e21
    rt = jnp.where(row == i1 - f0, w1,
                   jnp.where(row == i2 - f0, w2,
                             jnp.where(row == ROUTE_GROUP_OFF + g_idx, 1.0, 0.0)))
    rt = jnp.concatenate([rt, jnp.zeros((ROUTER_LANES - ROUTER_ROWS, rt.shape[1]), F32)], axis=0)
    route = rt.T
    gate_ref[...] = route
    sorter.issue(sorter.stage(h2b, route))

    @pl.when(t == pl.num_programs(0) - 1)
    def _():
        sorter.finish()


def _mix(z2, ya2, gs2, ga2, x2, wglu, wup, wout, g2, wr, layer):
    n = x2.shape[0]
    n_sort = n // SORT_TM
    subs = MIX_TM // SORT_TM
    cap = _region_cap(n)
    assert subs * SORT_TM >= ZFILL_TOK
    row = lambda c: pl.BlockSpec((MIX_TM, c), lambda i: (i, 0))
    full = lambda a, b: pl.BlockSpec((a, b), lambda i: (0, 0))
    smem = pl.BlockSpec(memory_space=pltpu.SMEM)
    return pl.pallas_call(
        _mix_body,
        grid=(n // MIX_TM,),
        in_specs=[row(SSM_WIDTH), row(ATTN_WIDTH), row(D_MODEL), row(D_MODEL), row(D_MODEL),
                  _layer_spec(layer, SSM_WIDTH, 2 * D_MODEL), _layer_spec(layer, ATTN_WIDTH, D_MODEL),
                  _layer_spec(layer, D_MODEL, D_MODEL), full(1, D_MODEL),
                  _layer_spec(layer, 2 * ROUTER_ROWS, D_MODEL)],
        out_specs=[row(D_MODEL), row(ROUTER_LANES), pl.BlockSpec(memory_space=pl.ANY),
                   smem, smem, smem],
        out_shape=[jax.ShapeDtypeStruct((n, D_MODEL), F32),
                   jax.ShapeDtypeStruct((n, ROUTER_LANES), F32),
                   jax.ShapeDtypeStruct((N_GROUPS * cap * TOK_ROWS, LANES), jnp.uint32),
                   jax.ShapeDtypeStruct((n_sort * N_GROUPS,), jnp.int32),
                   jax.ShapeDtypeStruct((n_sort * N_GROUPS,), jnp.int32),
                   jax.ShapeDtypeStruct((N_GROUPS,), jnp.int32)],
        scratch_shapes=[pltpu.VMEM((subs * SORT_TM * TOK_ROWS, LANES), jnp.uint32),
                        pltpu.SMEM((N_GROUPS,), jnp.int32),
                        pltpu.SMEM((subs * N_GROUPS,), jnp.int32),
                        pltpu.SemaphoreType.DMA((subs, N_GROUPS))],
        compiler_params=_cparams(("arbitrary",)),
        name="mix",
    )(z2, ya2, gs2, ga2, x2, wglu, wup, wout, g2, wr)


SORT_TM = 256
COMBINE_TM = 1024
MOE_TM = 512
TOK_ROWS = 8
H_WORDS = D_MODEL // 2
H_CHUNKS = H_WORDS // LANES
ROUTE_ROW = H_CHUNKS
ZFILL_TOK = MOE_TM
HI_MASK = 0xFFFF0000


def _region_cap(n):
    cap = n + ZFILL_TOK
    return -(-cap // MOE_TM) * MOE_TM


def _tile_positions(route):
    tm = route.shape[0]
    lane = lax.broadcasted_iota(jnp.int32, route.shape, 1)
    onehot = jnp.where((lane >= ROUTE_GROUP_OFF) & (lane < ROUTE_GROUP_OFF + N_GROUPS), route, 0.0)
    ci = lax.broadcasted_iota(jnp.int32, (tm, tm), 0)
    cj = lax.broadcasted_iota(jnp.int32, (tm, tm), 1)
    earlier = (cj < ci).astype(BF16)
    rank = jnp.dot(earlier, onehot.astype(BF16), preferred_element_type=F32)
    cnt = jnp.sum(onehot, axis=0, keepdims=True)
    li = lax.broadcasted_iota(jnp.int32, (LANES, LANES), 0)
    lj = lax.broadcasted_iota(jnp.int32, (LANES, LANES), 1)
    base = jnp.dot(jnp.broadcast_to(cnt, (8, LANES)).astype(BF16), (li < lj).astype(BF16),
                   preferred_element_type=F32)[0:1]
    return onehot, rank, base, cnt


class _GroupSorter:
    def __init__(self, t, subs, hs_hbm, starts_ref, lens_ref, counts_ref, stage_ref, run_ref,
                 pend_ref, sem):
        self.t, self.subs = t, subs
        self.hs_hbm, self.starts_ref, self.lens_ref, self.counts_ref = (
            hs_hbm, starts_ref, lens_ref, counts_ref)
        self.stage_ref, self.run_ref, self.pend_ref, self.sem = stage_ref, run_ref, pend_ref, sem
        self.slot_rows = SORT_TM * TOK_ROWS
        self.cap = hs_hbm.shape[0] // (N_GROUPS * TOK_ROWS)
        self.set0 = 0

    def _copy(self, src_row, dst_row, slot, g, tokens):
        return pltpu.make_async_copy(
            self.stage_ref.at[pl.ds(src_row, tokens * TOK_ROWS), :],
            self.hs_hbm.at[pl.ds(dst_row, tokens * TOK_ROWS), :], self.sem.at[slot, g])

    def _wait_set(self, set0):
        for sl in range(self.subs):
            for g in range(N_GROUPS):
                n = self.pend_ref[(set0 + sl) * N_GROUPS + g]

                @pl.when(n > 0)
                def _(sl=sl, g=g, n=n):
                    self._copy(0, 0, set0 + sl, g, n).wait()

    def begin(self):
        @pl.when(self.t == 0)
        def _():
            self.stage_ref[...] = jnp.zeros_like(self.stage_ref)
            for g in range(N_GROUPS):
                self.run_ref[g] = 0

        @pl.when(self.t > 0)
        def _():
            self._wait_set(self.set0)

    def stage(self, h, route):
        tm = SORT_TM
        runs = []
        for sl in range(self.subs):
            r = route[sl * tm:(sl + 1) * tm]
            onehot, rank, base, cnt = _tile_positions(r)
            w = (onehot * (rank + base)).astype(BF16)
            pos_row = lax.dot_general(jnp.ones((8, LANES), BF16), w, (((1,), (1,)), ((), ())),
                                      preferred_element_type=F32)[0:1]
            ri = lax.broadcasted_iota(jnp.int32, (tm, tm), 0).astype(F32)
            perm = (ri == pos_row).astype(BF16)
            r_hi = r.astype(BF16)
            r_mid = (r - r_hi.astype(F32)).astype(BF16)
            r_lo = (r - r_hi.astype(F32) - r_mid.astype(F32)).astype(BF16)
            sorted_all = jnp.dot(
                perm,
                jnp.concatenate([h[sl * tm:(sl + 1) * tm], r_hi, r_mid, r_lo], axis=1),
                preferred_element_type=F32)
            sh = sorted_all[:, :D_MODEL]
            sr = (sorted_all[:, D_MODEL:D_MODEL + LANES]
                  + sorted_all[:, D_MODEL + LANES:D_MODEL + 2 * LANES]
                  + sorted_all[:, D_MODEL + 2 * LANES:])
            hb = lax.bitcast_convert_type(sh, jnp.uint32)
            words = (hb[:, :H_WORDS] & jnp.uint32(HI_MASK)) | (hb[:, H_WORDS:] >> 16)

            slot0 = (self.set0 + sl) * self.slot_rows
            for s in range(H_CHUNKS):
                self.stage_ref[pl.ds(slot0 + s, tm, stride=TOK_ROWS), :] = (
                    words[:, s * LANES:(s + 1) * LANES])
            self.stage_ref[pl.ds(slot0 + ROUTE_ROW, tm, stride=TOK_ROWS), :] = (
                lax.bitcast_convert_type(sr, jnp.uint32))
            for g in range(N_GROUPS):
                runs.append((sl, g, slot0,
                             cnt[0, ROUTE_GROUP_OFF + g].astype(jnp.int32),
                             base[0, ROUTE_GROUP_OFF + g].astype(jnp.int32)))
        return runs

    def issue(self, runs):
        first_tile = self.t * self.subs
        for sl, g, slot0, c_g, b_g in runs:
            start = self.run_ref[g]
            self.starts_ref[(first_tile + sl) * N_GROUPS + g] = start
            self.lens_ref[(first_tile + sl) * N_GROUPS + g] = c_g
            self.pend_ref[(self.set0 + sl) * N_GROUPS + g] = c_g
            self.run_ref[g] = start + c_g
            src = pl.multiple_of(slot0 + b_g * TOK_ROWS, TOK_ROWS)
            dst = pl.multiple_of((g * self.cap + start) * TOK_ROWS, TOK_ROWS)

            @pl.when(c_g > 0)
            def _(sl=sl, g=g, src=src, dst=dst, c_g=c_g):
                self._copy(src, dst, self.set0 + sl, g, c_g).start()

    def finish(self):
        self._wait_set(self.set0)
        self.stage_ref[...] = jnp.zeros_like(self.stage_ref)
        zrows = ZFILL_TOK * TOK_ROWS
        for g in range(N_GROUPS):
            total = self.run_ref[g]
            self.counts_ref[g] = total
            dst = pl.multiple_of((g * self.cap + total) * TOK_ROWS, TOK_ROWS)
            pltpu.make_async_copy(self.stage_ref.at[pl.ds(0, zrows), :],
                                  self.hs_hbm.at[pl.ds(dst, zrows), :], self.sem.at[0, g]).start()
        for g in range(N_GROUPS):
            pltpu.make_async_copy(self.stage_ref.at[pl.ds(0, zrows), :],
                                  self.hs_hbm.at[pl.ds(0, zrows), :], self.sem.at[0, g]).wait()


def _experts_body(blk_ref, grp_ref, valid_ref, hs_ref, wu32_ref, wd32_ref, ys_ref, wu_ref, wd_ref):
    t = pl.program_id(0)
    tm = MOE_TM

    @pl.when((t == 0) | (grp_ref[t] != grp_ref[jnp.maximum(t - 1, 0)]))
    def _():
        wu_ref[...] = wu32_ref[...].astype(BF16)
        wd_ref[...] = wd32_ref[...].astype(BF16)

    @pl.when(valid_ref[t] > 0)
    def _():
        chunks = [hs_ref[pl.ds(s, tm, stride=TOK_ROWS), :] for s in range(H_CHUNKS + 1)]
        hi = [lax.bitcast_convert_type(c & jnp.uint32(HI_MASK), F32) for c in chunks[:H_CHUNKS]]
        lo = [lax.bitcast_convert_type(c << 16, F32) for c in chunks[:H_CHUNKS]]
        h = jnp.concatenate(hi + lo, axis=1).astype(BF16)
        route = lax.bitcast_convert_type(chunks[ROUTE_ROW], F32)
        acts = []
        for e in range(EXPERTS_PER_GROUP):
            gu = jnp.dot(h, wu_ref[e], preferred_element_type=F32)
            acts.append((jax.nn.silu(gu[:, :D_EXPERT]) * gu[:, D_EXPERT:]
                         * route[:, e:e + 1]).astype(BF16))
        act = jnp.concatenate(acts, axis=1)
        wd = wd_ref[...].reshape(EXPERTS_PER_GROUP * D_EXPERT, D_MODEL)
        out = jnp.dot(act, wd, preferred_element_type=F32)
        for j in range(D_MODEL // LANES):
            ys_ref[pl.ds(j, tm, stride=TOK_ROWS), :] = out[:, j * LANES:(j + 1) * LANES]


def _experts(blk, grp, valid, hs, wu, wd, layer):
    g0 = layer * N_GROUPS
    n_tiles = blk.shape[0]
    epg = EXPERTS_PER_GROUP
    return pl.pallas_call(
        _experts_body,
        grid_spec=pltpu.PrefetchScalarGridSpec(
            num_scalar_prefetch=3,
            grid=(n_tiles,),
            in_specs=[pl.BlockSpec((MOE_TM * TOK_ROWS, LANES), lambda t, b, g, v: (b[t], 0)),
                      pl.BlockSpec((epg, D_MODEL, 2 * D_EXPERT),
                                   lambda t, b, g, v: (g0 + g[t], 0, 0)),
                      pl.BlockSpec((epg, D_EXPERT, D_MODEL),
                                   lambda t, b, g, v: (g0 + g[t], 0, 0))],
            out_specs=pl.BlockSpec((MOE_TM * TOK_ROWS, LANES), lambda t, b, g, v: (b[t], 0)),
            scratch_shapes=[pltpu.VMEM((epg, D_MODEL, 2 * D_EXPERT), BF16),
                            pltpu.VMEM((epg, D_EXPERT, D_MODEL), BF16)],
        ),
        out_shape=jax.ShapeDtypeStruct(hs.shape, F32),
        compiler_params=_cparams(("arbitrary",)),
        name="moe_experts",
    )(blk, grp, valid, hs, wu, wd)


def _moe_outputs(starts_ref, lens_ref, route_ref, ys_hbm, buf_ref, sem):
    t = pl.program_id(0)
    nt = pl.num_programs(0)
    tm = SORT_TM
    subs = route_ref.shape[0] // tm
    slot_rows = tm * TOK_ROWS
    cap = ys_hbm.shape[0] // (N_GROUPS * TOK_ROWS)

    def fetches(step, sl):
        tile = step * subs + sl
        slot = (step % 2) * subs + sl
        copies = []
        first = 0
        for g in range(N_GROUPS):
            tokens = lens_ref[tile * N_GROUPS + g]
            src = pl.multiple_of((g * cap + starts_ref[tile * N_GROUPS + g]) * TOK_ROWS, TOK_ROWS)
            dst = pl.multiple_of(slot * slot_rows + first * TOK_ROWS, TOK_ROWS)
            copies.append((tokens, pltpu.make_async_copy(
                ys_hbm.at[pl.ds(src, tokens * TOK_ROWS), :],
                buf_ref.at[pl.ds(dst, tokens * TOK_ROWS), :], sem.at[slot, g])))
            first = first + tokens
        return copies

    def start_fetches(step):
        for sl in range(subs):
            for tokens, copy in fetches(step, sl):
                @pl.when(tokens > 0)
                def _(copy=copy):
                    copy.start()

    @pl.when(t == 0)
    def _():
        start_fetches(0)

    @pl.when(t + 1 < nt)
    def _():
        start_fetches(t + 1)

    for sl in range(subs):
        for tokens, copy in fetches(t, sl):
            @pl.when(tokens > 0)
            def _(copy=copy):
                copy.wait()

    outs = []
    for sl in range(subs):
        onehot, rank, base, _ = _tile_positions(route_ref[sl * tm:(sl + 1) * tm, :])
        pos = jnp.sum(onehot * (rank + base), axis=1, keepdims=True)
        ri = lax.broadcasted_iota(jnp.int32, (tm, tm), 1).astype(F32)
        sel = (ri == pos).astype(BF16)
        r0 = pl.multiple_of(((t % 2) * subs + sl) * slot_rows, TOK_ROWS)
        y = jnp.concatenate(
            [buf_ref[pl.ds(r0 + j, tm, stride=TOK_ROWS), :] for j in range(D_MODEL // LANES)],
            axis=1).astype(BF16)
        outs.append(jnp.dot(sel, y, preferred_element_type=F32))
    return outs


def _combine_body(starts_ref, lens_ref, x1_ref, route_ref, ys_hbm, o_ref, buf_ref, sem):
    outs = _moe_outputs(starts_ref, lens_ref, route_ref, ys_hbm, buf_ref, sem)
    for sl, y in enumerate(outs):
        rows = slice(sl * SORT_TM, (sl + 1) * SORT_TM)
        o_ref[rows, :] = x1_ref[rows, :] + y


def _combine(starts, lens, x1, route, ys):
    n = x1.shape[0]
    subs = COMBINE_TM // SORT_TM
    return pl.pallas_call(
        _combine_body,
        grid_spec=pltpu.PrefetchScalarGridSpec(
            num_scalar_prefetch=2,
            grid=(n // COMBINE_TM,),
            in_specs=[pl.BlockSpec((COMBINE_TM, D_MODEL), lambda i, s, c: (i, 0)),
                      pl.BlockSpec((COMBINE_TM, ROUTER_LANES), lambda i, s, c: (i, 0)),
                      pl.BlockSpec(memory_space=pl.ANY)],
            out_specs=pl.BlockSpec((COMBINE_TM, D_MODEL), lambda i, s, c: (i, 0)),
            scratch_shapes=[pltpu.VMEM((2 * COMBINE_TM * TOK_ROWS, LANES), F32),
                            pltpu.SemaphoreType.DMA((2 * subs, N_GROUPS))],
        ),
        out_shape=jax.ShapeDtypeStruct((n, D_MODEL), F32),
        compiler_params=_cparams(("arbitrary",)),
        name="moe_combine",
    )(starts, lens, x1, route, ys)


def _expert_tiles(counts, n):
    cap_blocks = _region_cap(n) // MOE_TM
    n_tiles = n // MOE_TM + N_GROUPS
    per_group = (counts + MOE_TM - 1) // MOE_TM
    ends = jnp.cumsum(per_group)
    total = ends[-1]
    t = jnp.minimum(jnp.arange(n_tiles, dtype=jnp.int32), total - 1)
    grp = jnp.sum((t[:, None] >= ends[None, :]).astype(jnp.int32), axis=1)
    first = ends - per_group
    blk = grp * cap_blocks + (t - first[grp])
    valid = (jnp.arange(n_tiles, dtype=jnp.int32) < total).astype(jnp.int32)
    return blk.astype(jnp.int32), grp.astype(jnp.int32), valid


def _moe_sorted(n, hs, counts, w_up, w_down, layer):
    blk, grp, valid = _expert_tiles(counts, n)
    return _experts(blk, grp, valid, hs, w_up, w_down, layer)


def kernel(x, norm_mix, w_in, ssm_lam_re, ssm_lam_im, ssm_log_dt, ssm_b_re, ssm_b_im, ssm_c_re,
           ssm_c_im, ssm_d, w_ssm_glu_val, w_ssm_glu_gate, q_norm, k_norm, attn_sinks, w_attn_up,
           w_out, norm_ffn, w_coarse, w_fine, w_expert_up, w_expert_down):
    bsz, seq, d = x.shape
    n = bsz * seq
    depth = w_in.shape[0]
    x2 = x.reshape(n, d)
    s5p = jax.vmap(_s5_params)(ssm_lam_re, ssm_lam_im, ssm_log_dt, ssm_b_re, ssm_b_im, ssm_c_re,
                               ssm_c_im, ssm_d)
    w_in_b = w_in.astype(BF16)
    wglu = jnp.concatenate([w_ssm_glu_val, w_ssm_glu_gate], axis=2).astype(BF16)
    wup_b = w_attn_up.astype(BF16)
    wout_b = w_out.astype(BF16)
    wr_t = jnp.concatenate(
        [jnp.swapaxes(w_coarse, 1, 2), jnp.swapaxes(w_fine, 1, 2),
         jnp.zeros((depth, ROUTER_ROWS - N_GROUPS - N_EXPERTS, d), w_fine.dtype)], axis=1).astype(F32)
    wr_hi = wr_t.astype(BF16)
    wr = jnp.concatenate([wr_hi, (wr_t - wr_hi.astype(F32)).astype(BF16)], axis=1)
    wu_all = w_expert_up.reshape((depth * N_EXPERTS,) + w_expert_up.shape[2:])
    wd_all = w_expert_down.reshape((depth * N_EXPERTS,) + w_expert_down.shape[2:])
    pending = None
    for i in range(depth):
        if pending is None:
            u, q, k, v, gs, ga = _proj(x2, norm_mix[i].reshape(1, d), w_in_b, i)
        else:
            x2, u, q, k, v, gs, ga = _proj_moe(*pending, norm_mix[i].reshape(1, d), w_in_b, i)
        z = _s5(u.reshape(bsz, seq, SSM_WIDTH), *(p[i] for p in s5p))
        ya = _attn(q.reshape(bsz, seq, ATTN_WIDTH), k.reshape(bsz, seq, KV_WIDTH),
                   v.reshape(bsz, seq, KV_WIDTH), q_norm[i], k_norm[i], attn_sinks[i])
        x1, route, hs, starts, lens, counts = _mix(
            z.reshape(n, SSM_WIDTH), ya.reshape(n, ATTN_WIDTH), gs, ga, x2,
            wglu, wup_b, wout_b, norm_ffn[i].reshape(1, d), wr, i)
        ys = _moe_sorted(n, hs, counts, wu_all, wd_all, i)
        pending = (starts, lens, x1, route, ys)
    return _combine(*pending).reshape(bsz, seq, d)
```

```python
import functools
import math

import jax
import jax.numpy as jnp
from jax import lax
from jax.experimental import pallas as pl
from jax.experimental.pallas import tpu as pltpu

F32 = jnp.float32
BF16 = jnp.bfloat16

D_MODEL = 1024
SSM_WIDTH = 512
SSM_GROUP = 16
SSM_GROUPS = 32
SSM_STATE = 64
N_HEADS = 8
N_KV_HEADS = 2
HEAD_DIM = 64
ATTN_WIDTH = 512
KV_WIDTH = 128
BLOCK = 128
N_GROUPS = 4
EXPERTS_PER_GROUP = 4
N_EXPERTS = 16
D_EXPERT = 256
EPS = 1e-6
NEG = -1e30

LANES = 128
SLABS = SSM_WIDTH // LANES
GROUPS_PER_SLAB = LANES // SSM_GROUP
SLAB_STATE = GROUPS_PER_SLAB * SSM_STATE
ROUTER_LANES = 128
ROUTER_ROWS = 32
FINE_OFF = N_GROUPS
ROUTE_GROUP_OFF = 8

VMEM_LIMIT = 56 * 1024 * 1024


def _cparams(sem):
    return pltpu.CompilerParams(dimension_semantics=sem, vmem_limit_bytes=VMEM_LIMIT)


PROJ_TM = 1024


def _proj_body(x_ref, g_ref, w_ref, *out_refs):
    _project(x_ref[...], g_ref, w_ref, *out_refs)


def _proj_moe_body(starts_ref, lens_ref, x1_ref, route_ref, ys_hbm, g_ref, w_ref,
                   x_ref, *rest):
    out_refs, (buf_ref, sem) = rest[:-2], rest[-2:]
    moe = _moe_outputs(starts_ref, lens_ref, route_ref, ys_hbm, buf_ref, sem)
    x = x1_ref[...] + jnp.concatenate(moe, axis=0)
    x_ref[...] = x
    _project(x, g_ref, w_ref, *out_refs)


def _project(x, g_ref, w_ref, u_ref, q_ref, k_ref, v_ref, gs_ref, ga_ref):
    ms = jnp.mean(x * x, axis=-1, keepdims=True)
    h = (x * lax.rsqrt(ms + EPS) * g_ref[...]).astype(BF16)

    def seg(a, b):
        return jnp.dot(h, w_ref[:, a:b], preferred_element_type=F32)

    o = 0
    u_ref[...] = seg(o, o + SSM_WIDTH)
    o += SSM_WIDTH
    q_ref[...] = seg(o, o + ATTN_WIDTH).astype(BF16)
    o += ATTN_WIDTH
    kv = seg(o, o + 2 * KV_WIDTH).astype(BF16)
    k_ref[...] = kv[:, :KV_WIDTH]
    v_ref[...] = kv[:, KV_WIDTH:]
    o += 2 * KV_WIDTH
    gs_ref[...] = jax.nn.sigmoid(seg(o, o + D_MODEL)).astype(BF16)
    o += D_MODEL
    ga_ref[...] = jax.nn.sigmoid(seg(o, o + D_MODEL)).astype(BF16)


def _layer_spec(layer, *shape):
    return pl.BlockSpec((None,) + shape, lambda *_: (layer,) + (0,) * len(shape))


def _proj_out(n):
    row = lambda c: pl.BlockSpec((PROJ_TM, c), lambda i, *_: (i, 0))
    specs = [row(SSM_WIDTH), row(ATTN_WIDTH), row(KV_WIDTH), row(KV_WIDTH), row(D_MODEL),
             row(D_MODEL)]
    shapes = [jax.ShapeDtypeStruct((n, SSM_WIDTH), F32),
              jax.ShapeDtypeStruct((n, ATTN_WIDTH), BF16),
              jax.ShapeDtypeStruct((n, KV_WIDTH), BF16),
              jax.ShapeDtypeStruct((n, KV_WIDTH), BF16),
              jax.ShapeDtypeStruct((n, D_MODEL), BF16),
              jax.ShapeDtypeStruct((n, D_MODEL), BF16)]
    return specs, shapes


def _proj(x2, g, w, layer):
    n = x2.shape[0]
    cols = w.shape[2]
    row = lambda c: pl.BlockSpec((PROJ_TM, c), lambda i: (i, 0))
    full = lambda a, b: pl.BlockSpec((a, b), lambda i: (0, 0))
    out_specs, out_shape = _proj_out(n)
    return pl.pallas_call(
        _proj_body,
        grid=(n // PROJ_TM,),
        in_specs=[row(D_MODEL), full(1, D_MODEL), _layer_spec(layer, D_MODEL, cols)],
        out_specs=out_specs,
        out_shape=out_shape,
        compiler_params=_cparams(("arbitrary",)),
        name="proj",
    )(x2, g, w)


def _proj_moe(starts, lens, x1, route, ys, g, w, layer):
    n = x1.shape[0]
    cols = w.shape[2]
    subs = PROJ_TM // SORT_TM
    row = lambda c: pl.BlockSpec((PROJ_TM, c), lambda i, *_: (i, 0))
    out_specs, out_shape = _proj_out(n)
    return pl.pallas_call(
        _proj_moe_body,
        grid_spec=pltpu.PrefetchScalarGridSpec(
            num_scalar_prefetch=2,
            grid=(n // PROJ_TM,),
            in_specs=[row(D_MODEL), row(ROUTER_LANES), pl.BlockSpec(memory_space=pl.ANY),
                      pl.BlockSpec((1, D_MODEL), lambda i, *_: (0, 0)),
                      _layer_spec(layer, D_MODEL, cols)],
            out_specs=[row(D_MODEL)] + out_specs,
            scratch_shapes=[pltpu.VMEM((2 * PROJ_TM * TOK_ROWS, LANES), F32),
                            pltpu.SemaphoreType.DMA((2 * subs, N_GROUPS))],
        ),
        out_shape=[jax.ShapeDtypeStruct((n, D_MODEL), F32)] + out_shape,
        compiler_params=_cparams(("arbitrary",)),
        name="proj_moe",
    )(starts, lens, x1, route, ys, g, w)


S5_TT = 64
S5_PAIRS = S5_TT // 2


def _s5_body(u_ref, bblk_ref, cblk_ref, k0_ref, lam_ref, d_ref, z_ref,
             ut_ref, y_ref, st_ref, carry_ref, *, bsz):
    rows = S5_TT * bsz
    prow = S5_PAIRS * bsz

    @pl.when(pl.program_id(0) == 0)
    def _():
        carry_ref[...] = jnp.zeros_like(carry_ref)

    for b in range(bsz):
        for j in range(SLABS):
            ut_ref[j, pl.ds(b, S5_TT, stride=bsz), :] = u_ref[b, :, j * LANES:(j + 1) * LANES]

    for j in range(SLABS):
        u3 = ut_ref[j].reshape(S5_PAIRS, 2 * bsz, LANES)
        u0 = u3[:, :bsz, :].reshape(prow, LANES)
        u1 = u3[:, bsz:, :].reshape(prow, LANES)
        lhs = jnp.concatenate([u1, u0], axis=1).astype(BF16)
        st_ref[j, 0:bsz, :] = carry_ref[j]
        st_ref[j, bsz:, :] = jnp.dot(lhs, bblk_ref[j], preferred_element_type=F32)
        a = jnp.broadcast_to(lam_ref[j, 0:1, :], (bsz, SLAB_STATE))
        bb = jnp.broadcast_to(lam_ref[j, 1:2, :], (bsz, SLAB_STATE))

        cre = carry_ref[j, :, 0:SLAB_STATE]
        cim = carry_ref[j, :, SLAB_STATE:2 * SLAB_STATE]
        for k in range(S5_PAIRS):
            r0 = (k + 1) * bsz
            bre = st_ref[j, r0:r0 + bsz, 0:SLAB_STATE]
            bim = st_ref[j, r0:r0 + bsz, SLAB_STATE:2 * SLAB_STATE]
            cre, cim = a * cre - bb * cim + bre, a * cim + bb * cre + bim
            st_ref[j, r0:r0 + bsz, 0:SLAB_STATE] = cre
            st_ref[j, r0:r0 + bsz, SLAB_STATE:2 * SLAB_STATE] = cim
        carry_ref[j, :, 0:SLAB_STATE] = cre
        carry_ref[j, :, SLAB_STATE:2 * SLAB_STATE] = cim

        cs = jnp.dot(st_ref[j].astype(BF16), cblk_ref[j], preferred_element_type=F32)
        y1 = cs[bsz:, :LANES] + d_ref[j] * u1
        y0 = (cs[:prow, LANES:] + jnp.dot(u0.astype(BF16), k0_ref[j], preferred_element_type=F32)
              + d_ref[j] * u0)
        y = jnp.concatenate([y0.reshape(S5_PAIRS, bsz, LANES), y1.reshape(S5_PAIRS, bsz, LANES)],
                            axis=1).reshape(rows, LANES)
        y_ref[j] = jax.nn.gelu(y)

    for b in range(bsz):
        for j in range(SLABS):
            z_ref[b, :, j * LANES:(j + 1) * LANES] = (
                y_ref[j, pl.ds(b, S5_TT, stride=bsz), :].astype(BF16))


def _s5(u3, bblk, cblk, k0, lam, dskip):
    bsz, seq, _ = u3.shape
    rows = S5_TT * bsz
    full = lambda *s: pl.BlockSpec(s, lambda i: (0,) * len(s))
    return pl.pallas_call(
        functools.partial(_s5_body, bsz=bsz),
        grid=(seq // S5_TT,),
        in_specs=[pl.BlockSpec((bsz, S5_TT, SSM_WIDTH), lambda i: (0, i, 0)),
                  full(SLABS, 2 * LANES, 2 * SLAB_STATE),
                  full(SLABS, 2 * SLAB_STATE, 2 * LANES),
                  full(SLABS, LANES, LANES),
                  full(SLABS, 2, SLAB_STATE),
                  full(SLABS, 1, LANES)],
        out_specs=pl.BlockSpec((bsz, S5_TT, SSM_WIDTH), lambda i: (0, i, 0)),
        out_shape=jax.ShapeDtypeStruct((bsz, seq, SSM_WIDTH), BF16),
        scratch_shapes=[pltpu.VMEM((SLABS, rows, LANES), F32),
                        pltpu.VMEM((SLABS, rows, LANES), F32),
                        pltpu.VMEM((SLABS, bsz + S5_PAIRS * bsz, 2 * SLAB_STATE), F32),
                        pltpu.VMEM((SLABS, bsz, 2 * SLAB_STATE), F32)],
        compiler_params=_cparams(("arbitrary",)),
        name="s5",
    )(u3, bblk, cblk, k0, lam, dskip)


def _s5_params(lam_re, lam_im, log_dt, b_re, b_im, c_re, c_im, d_skip):
    lam = lax.complex(lam_re.astype(F32), lam_im.astype(F32))
    dt = jnp.exp(log_dt.astype(F32))[:, None]
    lam_bar = jnp.exp(lam * dt)
    b = lax.complex(b_re.astype(F32), b_im.astype(F32))
    b_bar = ((lam_bar - 1.0) / lam)[..., None] * b
    eye = jnp.eye(GROUPS_PER_SLAB, dtype=F32)

    def in_blk(part):
        p4 = part.reshape(SLABS, GROUPS_PER_SLAB, SSM_STATE, SSM_GROUP)
        m = jnp.einsum('jgph,gk->jghkp', p4, eye)
        return m.reshape(SLABS, LANES, SLAB_STATE)

    def in_cplx(z):
        return jnp.concatenate([in_blk(z.real), in_blk(z.imag)], axis=-1)

    bblk = jnp.concatenate([in_cplx(b_bar), in_cplx(lam_bar[..., None] * b_bar)],
                           axis=1).astype(BF16)

    def out_blk(part):
        p4 = part.reshape(SLABS, GROUPS_PER_SLAB, SSM_GROUP, SSM_STATE)
        m = jnp.einsum('jghp,gk->jgpkh', p4, eye)
        return m.reshape(SLABS, SLAB_STATE, LANES)

    def out_cplx(z):
        return jnp.concatenate([out_blk(z.real), -out_blk(z.imag)], axis=1)

    c = lax.complex(c_re.astype(F32), c_im.astype(F32))
    cblk = jnp.concatenate([out_cplx(c), out_cplx(c * lam_bar[:, None, :])], axis=-1).astype(BF16)
    k0 = jnp.einsum('ghp,gpk->gkh', c, b_bar).real.reshape(SLABS, GROUPS_PER_SLAB, SSM_GROUP,
                                                           SSM_GROUP)
    k0 = jnp.einsum('jgkh,gq->jgkqh', k0, eye).reshape(SLABS, LANES, LANES).astype(BF16)
    lam2 = lam_bar * lam_bar
    lam_k = jnp.stack([lam2.real.reshape(SLABS, SLAB_STATE),
                       lam2.imag.reshape(SLABS, SLAB_STATE)], axis=1)
    dsk = d_skip.astype(F32).reshape(SLABS, 1, LANES)
    return bblk, cblk, k0, lam_k, dsk


ATTN_QB = 8
HEADS_PER_TILE = LANES // HEAD_DIM


def _pair_norm(t, gain2, head_mean):
    sq = t * t
    hi = sq.astype(BF16)
    lo = (sq - hi.astype(F32)).astype(BF16)
    ms = (jnp.dot(hi, head_mean, preferred_element_type=F32)
          + jnp.dot(lo, head_mean, preferred_element_type=F32))
    return t * lax.rsqrt(ms + EPS) * gain2


def _attn_body(sink_ref, q_ref, kc_ref, kp_ref, vc_ref, vp_ref, qg_ref, kg_ref, o_ref):
    n = pl.program_id(1)
    lane = lax.broadcasted_iota(jnp.int32, (1, LANES), 1)
    lo = lane < HEAD_DIM
    li = lax.broadcasted_iota(jnp.int32, (LANES, LANES), 0)
    lj = lax.broadcasted_iota(jnp.int32, (LANES, LANES), 1)
    head_mean = jnp.where((li < HEAD_DIM) == (lj < HEAD_DIM), 1.0 / HEAD_DIM, 0.0).astype(BF16)
    ones = jnp.ones((LANES, LANES), BF16)

    kall = jnp.concatenate([kp_ref[0], kc_ref[0]], axis=0).astype(F32)
    kn = _pair_norm(kall, kg_ref[...], head_mean)
    kroll = pltpu.roll(kn, HEAD_DIM, axis=1)
    vall = jnp.concatenate([vp_ref[0], vc_ref[0]], axis=0).astype(F32)
    vroll = pltpu.roll(vall, HEAD_DIM, axis=1)
    zero = jnp.zeros_like(kn)
    k_var = [[jnp.where(lo, kn, zero).astype(BF16), jnp.where(lo, zero, kroll).astype(BF16)],
             [jnp.where(lo, kroll, zero).astype(BF16), jnp.where(lo, zero, kn).astype(BF16)]]
    v_var = [[jnp.where(lo, vall, zero).astype(BF16), jnp.where(lo, zero, vroll).astype(BF16)],
             [jnp.where(lo, vroll, zero).astype(BF16), jnp.where(lo, zero, vall).astype(BF16)]]

    qn = []
    for i in range(ATTN_WIDTH // LANES):
        t = _pair_norm(q_ref[0, :, i * LANES:(i + 1) * LANES].astype(F32), qg_ref[...], head_mean)
        qn.append((t * (HEAD_DIM ** -0.5)).astype(BF16))

    qi = lax.broadcasted_iota(jnp.int32, (BLOCK, BLOCK), 0)
    si = lax.broadcasted_iota(jnp.int32, (BLOCK, BLOCK), 1)
    is_cur = si <= qi
    dist = jnp.where(is_cur, qi - si, qi - si + BLOCK).astype(F32)
    alibi = [(2.0 ** (-8.0 * (h + 1) / N_HEADS)) * dist for h in range(N_HEADS)]
    sinkmat = jnp.concatenate([jnp.full((BLOCK, LANES), sink_ref[h], F32) for h in range(N_HEADS)],
                              axis=0)

    for b in range(ATTN_QB):
        rows = slice(b * BLOCK, (b + 1) * BLOCK)
        win = slice(b * BLOCK, (b + 2) * BLOCK)
        s_all = []
        for h in range(N_HEADS):
            i, slot = divmod(h, HEADS_PER_TILE)
            kvh = h // (N_HEADS // N_KV_HEADS)
            sc = lax.dot_general(qn[i][rows], k_var[kvh][slot][win], (((1,), (1,)), ((), ())),
                                 preferred_element_type=F32)
            prev = sc[:, :BLOCK]
            if b == 0:
                prev = jnp.where(n > 0, prev, NEG)
            s_all.append(jnp.where(is_cur, sc[:, BLOCK:], prev) - alibi[h])
        s = jnp.concatenate(s_all, axis=0)
        m = jnp.maximum(jnp.max(s, axis=-1, keepdims=True), sinkmat)
        p = jnp.exp(s - m)
        denom = jnp.dot(p.astype(BF16), ones, preferred_element_type=F32) + jnp.exp(sinkmat - m)
        pn = p / denom
        for i in range(ATTN_WIDTH // LANES):
            acc = jnp.zeros((BLOCK, LANES), F32)
            for slot in range(HEADS_PER_TILE):
                h = i * HEADS_PER_TILE + slot
                kvh = h // (N_HEADS // N_KV_HEADS)
                ph = pn[h * BLOCK:(h + 1) * BLOCK]
                pcat = jnp.concatenate([jnp.where(is_cur, 0.0, ph), jnp.where(is_cur, ph, 0.0)],
                                       axis=1).astype(BF16)
                acc = acc + jnp.dot(pcat, v_var[kvh][slot][win], preferred_element_type=F32)
            o_ref[0, rows, i * LANES:(i + 1) * LANES] = acc.astype(BF16)


def _attn(q3, k3, v3, q_gain, k_gain, sinks):
    bsz, seq, _ = q3.shape
    qrows = ATTN_QB * BLOCK
    cur = lambda b, n: (b, n, 0)
    prev = lambda b, n: (b, jnp.maximum(n * ATTN_QB - 1, 0), 0)
    qg2 = jnp.tile(q_gain.astype(F32), HEADS_PER_TILE).reshape(1, LANES)
    kg2 = jnp.tile(k_gain.astype(F32), HEADS_PER_TILE).reshape(1, LANES)
    gain_spec = pl.BlockSpec((1, LANES), lambda b, n: (0, 0))
    return pl.pallas_call(
        _attn_body,
        grid=(bsz, seq // qrows),
        in_specs=[pl.BlockSpec(memory_space=pltpu.SMEM),
                  pl.BlockSpec((1, qrows, ATTN_WIDTH), cur),
                  pl.BlockSpec((1, qrows, KV_WIDTH), cur),
                  pl.BlockSpec((1, BLOCK, KV_WIDTH), prev),
                  pl.BlockSpec((1, qrows, KV_WIDTH), cur),
                  pl.BlockSpec((1, BLOCK, KV_WIDTH), prev),
                  gain_spec, gain_spec],
        out_specs=pl.BlockSpec((1, qrows, ATTN_WIDTH), cur),
        out_shape=jax.ShapeDtypeStruct((bsz, seq, ATTN_WIDTH), BF16),
        compiler_params=_cparams(("arbitrary", "arbitrary")),
        name="attn",
    )(sinks.astype(F32), q3, k3, k3, v3, v3, qg2, kg2)


MIX_TM = 1024


def _mix_body(z_ref, ya_ref, gs_ref, ga_ref, x_ref, wglu_ref, wup_ref, wout_ref, g2_ref, wr_ref,
              x1_ref, gate_ref, hs_hbm, starts_ref, lens_ref, counts_ref,
              stage_ref, run_ref, pend_ref, sem):
    t = pl.program_id(0)
    sorter = _GroupSorter(t, MIX_TM // SORT_TM, hs_hbm, starts_ref, lens_ref, counts_ref,
                          stage_ref, run_ref, pend_ref, sem)

    z = z_ref[...]
    bv = jnp.dot(z, wglu_ref[:, :D_MODEL], preferred_element_type=F32)
    bg = jnp.dot(z, wglu_ref[:, D_MODEL:], preferred_element_type=F32)
    bs = bv * jax.nn.sigmoid(bg)
    ba = jnp.dot(ya_ref[...], wup_ref[...], preferred_element_type=F32)
    merged = gs_ref[...].astype(F32) * bs + ga_ref[...].astype(F32) * ba
    x1 = x_ref[...] + jnp.dot(merged.astype(BF16), wout_ref[...], preferred_element_type=F32)
    x1_ref[...] = x1

    ms = jnp.mean(x1 * x1, axis=-1, keepdims=True)
    h2 = x1 * lax.rsqrt(ms + EPS) * g2_ref[...]
    h2b = h2.astype(BF16)

    lt = lax.dot_general(wr_ref[...], h2b, (((1,), (1,)), ((), ())),
                         preferred_element_type=F32)
    logits = lt[:ROUTER_ROWS] + lt[ROUTER_ROWS:]
    row = lax.broadcasted_iota(jnp.int32, logits.shape, 0)
    ninf = -jnp.inf
    cm = jnp.where(row < N_GROUPS, logits, ninf)
    cmax = jnp.max(cm, axis=0, keepdims=True)
    g_prob = 1.0 / jnp.sum(jnp.exp(cm - cmax), axis=0, keepdims=True)
    g_idx = jnp.min(jnp.where(cm == cmax, row, ROUTER_ROWS), axis=0, keepdims=True)
    f0 = FINE_OFF + EXPERTS_PER_GROUP * g_idx
    fm = jnp.where((row >= f0) & (row < f0 + EXPERTS_PER_GROUP), logits, ninf)
    v1 = jnp.max(fm, axis=0, keepdims=True)
    i1 = jnp.min(jnp.where(fm == v1, row, ROUTER_ROWS), axis=0, keepdims=True)
    fm2 = jnp.where(row == i1, ninf, fm)
    v2 = jnp.max(fm2, axis=0, keepdims=True)
    i2 = jnp.min(jnp.where(fm2 == v2, row, ROUTER_ROWS), axis=0, keepdims=True)
    e21 = jnp.exp(v2 - v1)
    w1 = g_prob / (1.0 + e21)
    w2 = w1 * e21
    rt = jnp.where(row == i1 - f0, w1,
                   jnp.where(row == i2 - f0, w2,
                             jnp.where(row == ROUTE_GROUP_OFF + g_idx, 1.0, 0.0)))
    rt = jnp.concatenate([rt, jnp.zeros((ROUTER_LANES - ROUTER_ROWS, rt.shape[1]), F32)], axis=0)
    route = rt.T
    gate_ref[...] = route
    sorter.begin()
    sorter.issue(sorter.stage(h2b, route))

    @pl.when(t == pl.num_programs(0) - 1)
    def _():
        sorter.finish()


def _mix(z2, ya2, gs2, ga2, x2, wglu, wup, wout, g2, wr, layer):
    n = x2.shape[0]
    n_sort = n // SORT_TM
    subs = MIX_TM // SORT_TM
    cap = _region_cap(n)
    assert subs * SORT_TM >= ZFILL_TOK
    row = lambda c: pl.BlockSpec((MIX_TM, c), lambda i: (i, 0))
    full = lambda a, b: pl.BlockSpec((a, b), lambda i: (0, 0))
    smem = pl.BlockSpec(memory_space=pltpu.SMEM)
    return pl.pallas_call(
        _mix_body,
        grid=(n // MIX_TM,),
        in_specs=[row(SSM_WIDTH), row(ATTN_WIDTH), row(D_MODEL), row(D_MODEL), row(D_MODEL),
                  _layer_spec(layer, SSM_WIDTH, 2 * D_MODEL), _layer_spec(layer, ATTN_WIDTH, D_MODEL),
                  _layer_spec(layer, D_MODEL, D_MODEL), full(1, D_MODEL),
                  _layer_spec(layer, 2 * ROUTER_ROWS, D_MODEL)],
        out_specs=[row(D_MODEL), row(ROUTER_LANES), pl.BlockSpec(memory_space=pl.ANY),
                   smem, smem, smem],
        out_shape=[jax.ShapeDtypeStruct((n, D_MODEL), F32),
                   jax.ShapeDtypeStruct((n, ROUTER_LANES), F32),
                   jax.ShapeDtypeStruct((N_GROUPS * cap * TOK_ROWS, LANES), jnp.uint32),
                   jax.ShapeDtypeStruct((n_sort * N_GROUPS,), jnp.int32),
                   jax.ShapeDtypeStruct((n_sort * N_GROUPS,), jnp.int32),
                   jax.ShapeDtypeStruct((N_GROUPS,), jnp.int32)],
        scratch_shapes=[pltpu.VMEM((subs * SORT_TM * TOK_ROWS, LANES), jnp.uint32),
                        pltpu.SMEM((N_GROUPS,), jnp.int32),
                        pltpu.SMEM((subs * N_GROUPS,), jnp.int32),
                        pltpu.SemaphoreType.DMA((subs, N_GROUPS))],
        compiler_params=_cparams(("arbitrary",)),
        name="mix",
    )(z2, ya2, gs2, ga2, x2, wglu, wup, wout, g2, wr)


SORT_TM = 256
COMBINE_TM = 1024
MOE_TM = 512
TOK_ROWS = 8
H_WORDS = D_MODEL // 2
H_CHUNKS = H_WORDS // LANES
ROUTE_ROW = H_CHUNKS
ZFILL_TOK = MOE_TM
HI_MASK = 0xFFFF0000


def _region_cap(n):
    cap = n + ZFILL_TOK
    return -(-cap // MOE_TM) * MOE_TM


def _tile_positions(route):
    tm = route.shape[0]
    lane = lax.broadcasted_iota(jnp.int32, route.shape, 1)
    onehot = jnp.where((lane >= ROUTE_GROUP_OFF) & (lane < ROUTE_GROUP_OFF + N_GROUPS), route, 0.0)
    ci = lax.broadcasted_iota(jnp.int32, (tm, tm), 0)
    cj = lax.broadcasted_iota(jnp.int32, (tm, tm), 1)
    earlier = (cj < ci).astype(BF16)
    rank = jnp.dot(earlier, onehot.astype(BF16), preferred_element_type=F32)
    cnt = jnp.sum(onehot, axis=0, keepdims=True)
    li = lax.broadcasted_iota(jnp.int32, (LANES, LANES), 0)
    lj = lax.broadcasted_iota(jnp.int32, (LANES, LANES), 1)
    base = jnp.dot(jnp.broadcast_to(cnt, (8, LANES)).astype(BF16), (li < lj).astype(BF16),
                   preferred_element_type=F32)[0:1]
    return onehot, rank, base, cnt


class _GroupSorter:
    def __init__(self, t, subs, hs_hbm, starts_ref, lens_ref, counts_ref, stage_ref, run_ref,
                 pend_ref, sem):
        self.t, self.subs = t, subs
        self.hs_hbm, self.starts_ref, self.lens_ref, self.counts_ref = (
            hs_hbm, starts_ref, lens_ref, counts_ref)
        self.stage_ref, self.run_ref, self.pend_ref, self.sem = stage_ref, run_ref, pend_ref, sem
        self.slot_rows = SORT_TM * TOK_ROWS
        self.cap = hs_hbm.shape[0] // (N_GROUPS * TOK_ROWS)
        self.set0 = 0

    def _copy(self, src_row, dst_row, slot, g, tokens):
        return pltpu.make_async_copy(
            self.stage_ref.at[pl.ds(src_row, tokens * TOK_ROWS), :],
            self.hs_hbm.at[pl.ds(dst_row, tokens * TOK_ROWS), :], self.sem.at[slot, g])

    def _wait_set(self, set0):
        for sl in range(self.subs):
            for g in range(N_GROUPS):
                n = self.pend_ref[(set0 + sl) * N_GROUPS + g]

                @pl.when(n > 0)
                def _(sl=sl, g=g, n=n):
                    self._copy(0, 0, set0 + sl, g, n).wait()

    def begin(self):
        @pl.when(self.t == 0)
        def _():
            self.stage_ref[...] = jnp.zeros_like(self.stage_ref)
            for g in range(N_GROUPS):
                self.run_ref[g] = 0

        @pl.when(self.t > 0)
        def _():
            self._wait_set(self.set0)

    def stage(self, h, route):
        tm = SORT_TM
        runs = []
        for sl in range(self.subs):
            r = route[sl * tm:(sl + 1) * tm]
            onehot, rank, base, cnt = _tile_positions(r)
            w = (onehot * (rank + base)).astype(BF16)
            pos_row = lax.dot_general(jnp.ones((8, LANES), BF16), w, (((1,), (1,)), ((), ())),
                                      preferred_element_type=F32)[0:1]
            ri = lax.broadcasted_iota(jnp.int32, (tm, tm), 0).astype(F32)
            perm = (ri == pos_row).astype(BF16)
            r_hi = r.astype(BF16)
            r_mid = (r - r_hi.astype(F32)).astype(BF16)
            r_lo = (r - r_hi.astype(F32) - r_mid.astype(F32)).astype(BF16)
            sorted_all = jnp.dot(
                perm,
                jnp.concatenate([h[sl * tm:(sl + 1) * tm], r_hi, r_mid, r_lo], axis=1),
                preferred_element_type=F32)
            sh = sorted_all[:, :D_MODEL]
            sr = (sorted_all[:, D_MODEL:D_MODEL + LANES]
                  + sorted_all[:, D_MODEL + LANES:D_MODEL + 2 * LANES]
                  + sorted_all[:, D_MODEL + 2 * LANES:])
            hb = lax.bitcast_convert_type(sh, jnp.uint32)
            words = (hb[:, :H_WORDS] & jnp.uint32(HI_MASK)) | (hb[:, H_WORDS:] >> 16)

            slot0 = (self.set0 + sl) * self.slot_rows
            for s in range(H_CHUNKS):
                self.stage_ref[pl.ds(slot0 + s, tm, stride=TOK_ROWS), :] = (
                    words[:, s * LANES:(s + 1) * LANES])
            self.stage_ref[pl.ds(slot0 + ROUTE_ROW, tm, stride=TOK_ROWS), :] = (
                lax.bitcast_convert_type(sr, jnp.uint32))
            for g in range(N_GROUPS):
                runs.append((sl, g, slot0,
                             cnt[0, ROUTE_GROUP_OFF + g].astype(jnp.int32),
                             base[0, ROUTE_GROUP_OFF + g].astype(jnp.int32)))
        return runs

    def issue(self, runs):
        first_tile = self.t * self.subs
        for sl, g, slot0, c_g, b_g in runs:
            start = self.run_ref[g]
            self.starts_ref[(first_tile + sl) * N_GROUPS + g] = start
            self.lens_ref[(first_tile + sl) * N_GROUPS + g] = c_g
            self.pend_ref[(self.set0 + sl) * N_GROUPS + g] = c_g
            self.run_ref[g] = start + c_g
            src = pl.multiple_of(slot0 + b_g * TOK_ROWS, TOK_ROWS)
            dst = pl.multiple_of((g * self.cap + start) * TOK_ROWS, TOK_ROWS)

            @pl.when(c_g > 0)
            def _(sl=sl, g=g, src=src, dst=dst, c_g=c_g):
                self._copy(src, dst, self.set0 + sl, g, c_g).start()

    def finish(self):
        self._wait_set(self.set0)
        self.stage_ref[...] = jnp.zeros_like(self.stage_ref)
        zrows = ZFILL_TOK * TOK_ROWS
        for g in range(N_GROUPS):
            total = self.run_ref[g]
            self.counts_ref[g] = total
            dst = pl.multiple_of((g * self.cap + total) * TOK_ROWS, TOK_ROWS)
            pltpu.make_async_copy(self.stage_ref.at[pl.ds(0, zrows), :],
                                  self.hs_hbm.at[pl.ds(dst, zrows), :], self.sem.at[0, g]).start()
        for g in range(N_GROUPS):
            pltpu.make_async_copy(self.stage_ref.at[pl.ds(0, zrows), :],
                                  self.hs_hbm.at[pl.ds(0, zrows), :], self.sem.at[0, g]).wait()


def _experts_body(blk_ref, grp_ref, valid_ref, hs_ref, wu32_ref, wd32_ref, ys_ref, wu_ref, wd_ref):
    t = pl.program_id(0)
    tm = MOE_TM

    @pl.when((t == 0) | (grp_ref[t] != grp_ref[jnp.maximum(t - 1, 0)]))
    def _():
        wu_ref[...] = wu32_ref[...].astype(BF16)
        wd_ref[...] = wd32_ref[...].astype(BF16)

    @pl.when(valid_ref[t] > 0)
    def _():
        chunks = [hs_ref[pl.ds(s, tm, stride=TOK_ROWS), :] for s in range(H_CHUNKS + 1)]
        hi = [lax.bitcast_convert_type(c & jnp.uint32(HI_MASK), F32) for c in chunks[:H_CHUNKS]]
        lo = [lax.bitcast_convert_type(c << 16, F32) for c in chunks[:H_CHUNKS]]
        h = jnp.concatenate(hi + lo, axis=1).astype(BF16)
        route = lax.bitcast_convert_type(chunks[ROUTE_ROW], F32)
        acts = []
        for e in range(EXPERTS_PER_GROUP):
            gu = jnp.dot(h, wu_ref[e], preferred_element_type=F32)
            acts.append((jax.nn.silu(gu[:, :D_EXPERT]) * gu[:, D_EXPERT:]
                         * route[:, e:e + 1]).astype(BF16))
        act = jnp.concatenate(acts, axis=1)
        wd = wd_ref[...].reshape(EXPERTS_PER_GROUP * D_EXPERT, D_MODEL)
        out = jnp.dot(act, wd, preferred_element_type=F32)
        for j in range(D_MODEL // LANES):
            ys_ref[pl.ds(j, tm, stride=TOK_ROWS), :] = out[:, j * LANES:(j + 1) * LANES]


def _experts(blk, grp, valid, hs, wu, wd, layer):
    g0 = layer * N_GROUPS
    n_tiles = blk.shape[0]
    epg = EXPERTS_PER_GROUP
    return pl.pallas_call(
        _experts_body,
        grid_spec=pltpu.PrefetchScalarGridSpec(
            num_scalar_prefetch=3,
            grid=(n_tiles,),
            in_specs=[pl.BlockSpec((MOE_TM * TOK_ROWS, LANES), lambda t, b, g, v: (b[t], 0)),
                      pl.BlockSpec((epg, D_MODEL, 2 * D_EXPERT),
                                   lambda t, b, g, v: (g0 + g[t], 0, 0)),
                      pl.BlockSpec((epg, D_EXPERT, D_MODEL),
                                   lambda t, b, g, v: (g0 + g[t], 0, 0))],
            out_specs=pl.BlockSpec((MOE_TM * TOK_ROWS, LANES), lambda t, b, g, v: (b[t], 0)),
            scratch_shapes=[pltpu.VMEM((epg, D_MODEL, 2 * D_EXPERT), BF16),
                            pltpu.VMEM((epg, D_EXPERT, D_MODEL), BF16)],
        ),
        out_shape=jax.ShapeDtypeStruct(hs.shape, F32),
        compiler_params=_cparams(("arbitrary",)),
        name="moe_experts",
    )(blk, grp, valid, hs, wu, wd)


def _moe_outputs(starts_ref, lens_ref, route_ref, ys_hbm, buf_ref, sem):
    t = pl.program_id(0)
    nt = pl.num_programs(0)
    tm = SORT_TM
    subs = route_ref.shape[0] // tm
    slot_rows = tm * TOK_ROWS
    cap = ys_hbm.shape[0] // (N_GROUPS * TOK_ROWS)

    def fetches(step, sl):
        tile = step * subs + sl
        slot = (step % 2) * subs + sl
        copies = []
        first = 0
        for g in range(N_GROUPS):
            tokens = lens_ref[tile * N_GROUPS + g]
            src = pl.multiple_of((g * cap + starts_ref[tile * N_GROUPS + g]) * TOK_ROWS, TOK_ROWS)
            dst = pl.multiple_of(slot * slot_rows + first * TOK_ROWS, TOK_ROWS)
            copies.append((tokens, pltpu.make_async_copy(
                ys_hbm.at[pl.ds(src, tokens * TOK_ROWS), :],
                buf_ref.at[pl.ds(dst, tokens * TOK_ROWS), :], sem.at[slot, g])))
            first = first + tokens
        return copies

    def start_fetches(step):
        for sl in range(subs):
            for tokens, copy in fetches(step, sl):
                @pl.when(tokens > 0)
                def _(copy=copy):
                    copy.start()

    @pl.when(t == 0)
    def _():
        start_fetches(0)

    @pl.when(t + 1 < nt)
    def _():
        start_fetches(t + 1)

    for sl in range(subs):
        for tokens, copy in fetches(t, sl):
            @pl.when(tokens > 0)
            def _(copy=copy):
                copy.wait()

    outs = []
    for sl in range(subs):
        onehot, rank, base, _ = _tile_positions(route_ref[sl * tm:(sl + 1) * tm, :])
        pos = jnp.sum(onehot * (rank + base), axis=1, keepdims=True)
        ri = lax.broadcasted_iota(jnp.int32, (tm, tm), 1).astype(F32)
        sel = (ri == pos).astype(BF16)
        r0 = pl.multiple_of(((t % 2) * subs + sl) * slot_rows, TOK_ROWS)
        y = jnp.concatenate(
            [buf_ref[pl.ds(r0 + j, tm, stride=TOK_ROWS), :] for j in range(D_MODEL // LANES)],
            axis=1).astype(BF16)
        outs.append(jnp.dot(sel, y, preferred_element_type=F32))
    return outs


def _combine_body(starts_ref, lens_ref, x1_ref, route_ref, ys_hbm, o_ref, buf_ref, sem):
    outs = _moe_outputs(starts_ref, lens_ref, route_ref, ys_hbm, buf_ref, sem)
    for sl, y in enumerate(outs):
        rows = slice(sl * SORT_TM, (sl + 1) * SORT_TM)
        o_ref[rows, :] = x1_ref[rows, :] + y


def _combine(starts, lens, x1, route, ys):
    n = x1.shape[0]
    subs = COMBINE_TM // SORT_TM
    return pl.pallas_call(
        _combine_body,
        grid_spec=pltpu.PrefetchScalarGridSpec(
            num_scalar_prefetch=2,
            grid=(n // COMBINE_TM,),
            in_specs=[pl.BlockSpec((COMBINE_TM, D_MODEL), lambda i, s, c: (i, 0)),
                      pl.BlockSpec((COMBINE_TM, ROUTER_LANES), lambda i, s, c: (i, 0)),
                      pl.BlockSpec(memory_space=pl.ANY)],
            out_specs=pl.BlockSpec((COMBINE_TM, D_MODEL), lambda i, s, c: (i, 0)),
            scratch_shapes=[pltpu.VMEM((2 * COMBINE_TM * TOK_ROWS, LANES), F32),
                            pltpu.SemaphoreType.DMA((2 * subs, N_GROUPS))],
        ),
        out_shape=jax.ShapeDtypeStruct((n, D_MODEL), F32),
        compiler_params=_cparams(("arbitrary",)),
        name="moe_combine",
    )(starts, lens, x1, route, ys)


def _expert_tiles(counts, n):
    cap_blocks = _region_cap(n) // MOE_TM
    n_tiles = n // MOE_TM + N_GROUPS
    per_group = (counts + MOE_TM - 1) // MOE_TM
    ends = jnp.cumsum(per_group)
    total = ends[-1]
    t = jnp.minimum(jnp.arange(n_tiles, dtype=jnp.int32), total - 1)
    grp = jnp.sum((t[:, None] >= ends[None, :]).astype(jnp.int32), axis=1)
    first = ends - per_group
    blk = grp * cap_blocks + (t - first[grp])
    valid = (jnp.arange(n_tiles, dtype=jnp.int32) < total).astype(jnp.int32)
    return blk.astype(jnp.int32), grp.astype(jnp.int32), valid


def _moe_sorted(n, hs, counts, w_up, w_down, layer):
    blk, grp, valid = _expert_tiles(counts, n)
    return _experts(blk, grp, valid, hs, w_up, w_down, layer)


def kernel(x, norm_mix, w_in, ssm_lam_re, ssm_lam_im, ssm_log_dt, ssm_b_re, ssm_b_im, ssm_c_re,
           ssm_c_im, ssm_d, w_ssm_glu_val, w_ssm_glu_gate, q_norm, k_norm, attn_sinks, w_attn_up,
           w_out, norm_ffn, w_coarse, w_fine, w_expert_up, w_expert_down):
    bsz, seq, d = x.shape
    n = bsz * seq
    depth = w_in.shape[0]
    x2 = x.reshape(n, d)
    s5p = jax.vmap(_s5_params)(ssm_lam_re, ssm_lam_im, ssm_log_dt, ssm_b_re, ssm_b_im, ssm_c_re,
                               ssm_c_im, ssm_d)
    w_in_b = w_in.astype(BF16)
    wglu = jnp.concatenate([w_ssm_glu_val, w_ssm_glu_gate], axis=2).astype(BF16)
    wup_b = w_attn_up.astype(BF16)
    wout_b = w_out.astype(BF16)
    wr_t = jnp.concatenate(
        [jnp.swapaxes(w_coarse, 1, 2), jnp.swapaxes(w_fine, 1, 2),
         jnp.zeros((depth, ROUTER_ROWS - N_GROUPS - N_EXPERTS, d), w_fine.dtype)], axis=1).astype(F32)
    wr_hi = wr_t.astype(BF16)
    wr = jnp.concatenate([wr_hi, (wr_t - wr_hi.astype(F32)).astype(BF16)], axis=1)
    wu_all = w_expert_up.reshape((depth * N_EXPERTS,) + w_expert_up.shape[2:])
    wd_all = w_expert_down.reshape((depth * N_EXPERTS,) + w_expert_down.shape[2:])
    pending = None
    for i in range(depth):
        if pending is None:
            u, q, k, v, gs, ga = _proj(x2, norm_mix[i].reshape(1, d), w_in_b, i)
        else:
            x2, u, q, k, v, gs, ga = _proj_moe(*pending, norm_mix[i].reshape(1, d), w_in_b, i)
        z = _s5(u.reshape(bsz, seq, SSM_WIDTH), *(p[i] for p in s5p))
        ya = _attn(q.reshape(bsz, seq, ATTN_WIDTH), k.reshape(bsz, seq, KV_WIDTH),
                   v.reshape(bsz, seq, KV_WIDTH), q_norm[i], k_norm[i], attn_sinks[i])
        x1, route, hs, starts, lens, counts = _mix(
            z.reshape(n, SSM_WIDTH), ya.reshape(n, ATTN_WIDTH), gs, ga, x2,
            wglu, wup_b, wout_b, norm_ffn[i].reshape(1, d), wr, i)
        ys = _moe_sorted(n, hs, counts, wu_all, wd_all, i)
        pending = (starts, lens, x1, route, ys)
    return _combine(*pending).reshape(bsz, seq, d)
```

```python
import functools
import math

import jax
import jax.numpy as jnp
from jax import lax
from jax.experimental import pallas as pl
from jax.experimental.pallas import tpu as pltpu

F32 = jnp.float32
BF16 = jnp.bfloat16

D_MODEL = 1024
SSM_WIDTH = 512
SSM_GROUP = 16
SSM_GROUPS = 32
SSM_STATE = 64
N_HEADS = 8
N_KV_HEADS = 2
HEAD_DIM = 64
ATTN_WIDTH = 512
KV_WIDTH = 128
BLOCK = 128
N_GROUPS = 4
EXPERTS_PER_GROUP = 4
N_EXPERTS = 16
D_EXPERT = 256
EPS = 1e-6
NEG = -1e30

LANES = 128
SLABS = SSM_WIDTH // LANES
GROUPS_PER_SLAB = LANES // SSM_GROUP
SLAB_STATE = GROUPS_PER_SLAB * SSM_STATE
ROUTER_LANES = 128
ROUTER_ROWS = 32
FINE_OFF = N_GROUPS
ROUTE_GROUP_OFF = 8

VMEM_LIMIT = 56 * 1024 * 1024


def _cparams(sem):
    return pltpu.CompilerParams(dimension_semantics=sem, vmem_limit_bytes=VMEM_LIMIT)


PROJ_TM = 1024


def _proj_body(x_ref, g_ref, w_ref, *out_refs):
    _project(x_ref[...], g_ref, w_ref, *out_refs)


def _proj_moe_body(starts_ref, lens_ref, x1_ref, route_ref, ys_hbm, g_ref, w_ref,
                   x_ref, *rest):
    out_refs, (buf_ref, sem) = rest[:-2], rest[-2:]
    moe = _moe_outputs(starts_ref, lens_ref, route_ref, ys_hbm, buf_ref, sem)
    x = x1_ref[...] + jnp.concatenate(moe, axis=0)
    x_ref[...] = x
    _project(x, g_ref, w_ref, *out_refs)


def _project(x, g_ref, w_ref, u_ref, q_ref, k_ref, v_ref, gs_ref, ga_ref):
    ms = jnp.mean(x * x, axis=-1, keepdims=True)
    h = (x * lax.rsqrt(ms + EPS) * g_ref[...]).astype(BF16)

    def seg(a, b):
        return jnp.dot(h, w_ref[:, a:b], preferred_element_type=F32)

    o = 0
    u_ref[...] = seg(o, o + SSM_WIDTH)
    o += SSM_WIDTH
    q_ref[...] = seg(o, o + ATTN_WIDTH).astype(BF16)
    o += ATTN_WIDTH
    kv = seg(o, o + 2 * KV_WIDTH).astype(BF16)
    k_ref[...] = kv[:, :KV_WIDTH]
    v_ref[...] = kv[:, KV_WIDTH:]
    o += 2 * KV_WIDTH
    gs_ref[...] = jax.nn.sigmoid(seg(o, o + D_MODEL)).astype(BF16)
    o += D_MODEL
    ga_ref[...] = jax.nn.sigmoid(seg(o, o + D_MODEL)).astype(BF16)


def _layer_spec(layer, *shape):
    return pl.BlockSpec((None,) + shape, lambda *_: (layer,) + (0,) * len(shape))


def _proj_out(n):
    row = lambda c: pl.BlockSpec((PROJ_TM, c), lambda i, *_: (i, 0))
    specs = [row(SSM_WIDTH), row(ATTN_WIDTH), row(KV_WIDTH), row(KV_WIDTH), row(D_MODEL),
             row(D_MODEL)]
    shapes = [jax.ShapeDtypeStruct((n, SSM_WIDTH), F32),
              jax.ShapeDtypeStruct((n, ATTN_WIDTH), BF16),
              jax.ShapeDtypeStruct((n, KV_WIDTH), BF16),
              jax.ShapeDtypeStruct((n, KV_WIDTH), BF16),
              jax.ShapeDtypeStruct((n, D_MODEL), BF16),
              jax.ShapeDtypeStruct((n, D_MODEL), BF16)]
    return specs, shapes


def _proj(x2, g, w, layer):
    n = x2.shape[0]
    cols = w.shape[2]
    row = lambda c: pl.BlockSpec((PROJ_TM, c), lambda i: (i, 0))
    full = lambda a, b: pl.BlockSpec((a, b), lambda i: (0, 0))
    out_specs, out_shape = _proj_out(n)
    return pl.pallas_call(
        _proj_body,
        grid=(n // PROJ_TM,),
        in_specs=[row(D_MODEL), full(1, D_MODEL), _layer_spec(layer, D_MODEL, cols)],
        out_specs=out_specs,
        out_shape=out_shape,
        compiler_params=_cparams(("arbitrary",)),
        name="proj",
    )(x2, g, w)


def _proj_moe(starts, lens, x1, route, ys, g, w, layer):
    n = x1.shape[0]
    cols = w.shape[2]
    subs = PROJ_TM // SORT_TM
    row = lambda c: pl.BlockSpec((PROJ_TM, c), lambda i, *_: (i, 0))
    out_specs, out_shape = _proj_out(n)
    return pl.pallas_call(
        _proj_moe_body,
        grid_spec=pltpu.PrefetchScalarGridSpec(
            num_scalar_prefetch=2,
            grid=(n // PROJ_TM,),
            in_specs=[row(D_MODEL), row(ROUTER_LANES), pl.BlockSpec(memory_space=pl.ANY),
                      pl.BlockSpec((1, D_MODEL), lambda i, *_: (0, 0)),
                      _layer_spec(layer, D_MODEL, cols)],
            out_specs=[row(D_MODEL)] + out_specs,
            scratch_shapes=[pltpu.VMEM((2 * PROJ_TM * TOK_ROWS, LANES), F32),
                            pltpu.SemaphoreType.DMA((2 * subs, N_GROUPS))],
        ),
        out_shape=[jax.ShapeDtypeStruct((n, D_MODEL), F32)] + out_shape,
        compiler_params=_cparams(("arbitrary",)),
        name="proj_moe",
    )(starts, lens, x1, route, ys, g, w)


S5_TT = 64
S5_PAIRS = S5_TT // 2


def _s5_body(u_ref, bblk_ref, cblk_ref, k0_ref, lam_ref, d_ref, z_ref,
             ut_ref, y_ref, st_ref, carry_ref, *, bsz):
    rows = S5_TT * bsz
    prow = S5_PAIRS * bsz

    @pl.when(pl.program_id(0) == 0)
    def _():
        carry_ref[...] = jnp.zeros_like(carry_ref)

    for b in range(bsz):
        for j in range(SLABS):
            ut_ref[j, pl.ds(b, S5_TT, stride=bsz), :] = u_ref[b, :, j * LANES:(j + 1) * LANES]

    for j in range(SLABS):
        u3 = ut_ref[j].reshape(S5_PAIRS, 2 * bsz, LANES)
        u0 = u3[:, :bsz, :].reshape(prow, LANES)
        u1 = u3[:, bsz:, :].reshape(prow, LANES)
        lhs = jnp.concatenate([u1, u0], axis=1).astype(BF16)
        st_ref[j, 0:bsz, :] = carry_ref[j]
        st_ref[j, bsz:, :] = jnp.dot(lhs, bblk_ref[j], preferred_element_type=F32)
        a = jnp.broadcast_to(lam_ref[j, 0:1, :], (bsz, SLAB_STATE))
        bb = jnp.broadcast_to(lam_ref[j, 1:2, :], (bsz, SLAB_STATE))

        cre = carry_ref[j, :, 0:SLAB_STATE]
        cim = carry_ref[j, :, SLAB_STATE:2 * SLAB_STATE]
        for k in range(S5_PAIRS):
            r0 = (k + 1) * bsz
            bre = st_ref[j, r0:r0 + bsz, 0:SLAB_STATE]
            bim = st_ref[j, r0:r0 + bsz, SLAB_STATE:2 * SLAB_STATE]
            cre, cim = a * cre - bb * cim + bre, a * cim + bb * cre + bim
            st_ref[j, r0:r0 + bsz, 0:SLAB_STATE] = cre
            st_ref[j, r0:r0 + bsz, SLAB_STATE:2 * SLAB_STATE] = cim
        carry_ref[j, :, 0:SLAB_STATE] = cre
        carry_ref[j, :, SLAB_STATE:2 * SLAB_STATE] = cim

        cs = jnp.dot(st_ref[j].astype(BF16), cblk_ref[j], preferred_element_type=F32)
        y1 = cs[bsz:, :LANES] + d_ref[j] * u1
        y0 = (cs[:prow, LANES:] + jnp.dot(u0.astype(BF16), k0_ref[j], preferred_element_type=F32)
              + d_ref[j] * u0)
        y = jnp.concatenate([y0.reshape(S5_PAIRS, bsz, LANES), y1.reshape(S5_PAIRS, bsz, LANES)],
                            axis=1).reshape(rows, LANES)
        y_ref[j] = jax.nn.gelu(y)

    for b in range(bsz):
        for j in range(SLABS):
            z_ref[b, :, j * LANES:(j + 1) * LANES] = (
                y_ref[j, pl.ds(b, S5_TT, stride=bsz), :].astype(BF16))


def _s5(u3, bblk, cblk, k0, lam, dskip):
    bsz, seq, _ = u3.shape
    rows = S5_TT * bsz
    full = lambda *s: pl.BlockSpec(s, lambda i: (0,) * len(s))
    return pl.pallas_call(
        functools.partial(_s5_body, bsz=bsz),
        grid=(seq // S5_TT,),
        in_specs=[pl.BlockSpec((bsz, S5_TT, SSM_WIDTH), lambda i: (0, i, 0)),
                  full(SLABS, 2 * LANES, 2 * SLAB_STATE),
                  full(SLABS, 2 * SLAB_STATE, 2 * LANES),
                  full(SLABS, LANES, LANES),
                  full(SLABS, 2, SLAB_STATE),
                  full(SLABS, 1, LANES)],
        out_specs=pl.BlockSpec((bsz, S5_TT, SSM_WIDTH), lambda i: (0, i, 0)),
        out_shape=jax.ShapeDtypeStruct((bsz, seq, SSM_WIDTH), BF16),
        scratch_shapes=[pltpu.VMEM((SLABS, rows, LANES), F32),
                        pltpu.VMEM((SLABS, rows, LANES), F32),
                        pltpu.VMEM((SLABS, bsz + S5_PAIRS * bsz, 2 * SLAB_STATE), F32),
                        pltpu.VMEM((SLABS, bsz, 2 * SLAB_STATE), F32)],
        compiler_params=_cparams(("arbitrary",)),
        name="s5",
    )(u3, bblk, cblk, k0, lam, dskip)


def _s5_params(lam_re, lam_im, log_dt, b_re, b_im, c_re, c_im, d_skip):
    lam = lax.complex(lam_re.astype(F32), lam_im.astype(F32))
    dt = jnp.exp(log_dt.astype(F32))[:, None]
    lam_bar = jnp.exp(lam * dt)
    b = lax.complex(b_re.astype(F32), b_im.astype(F32))
    b_bar = ((lam_bar - 1.0) / lam)[..., None] * b
    eye = jnp.eye(GROUPS_PER_SLAB, dtype=F32)

    def in_blk(part):
        p4 = part.reshape(SLABS, GROUPS_PER_SLAB, SSM_STATE, SSM_GROUP)
        m = jnp.einsum('jgph,gk->jghkp', p4, eye)
        return m.reshape(SLABS, LANES, SLAB_STATE)

    def in_cplx(z):
        return jnp.concatenate([in_blk(z.real), in_blk(z.imag)], axis=-1)

    bblk = jnp.concatenate([in_cplx(b_bar), in_cplx(lam_bar[..., None] * b_bar)],
                           axis=1).astype(BF16)

    def out_blk(part):
        p4 = part.reshape(SLABS, GROUPS_PER_SLAB, SSM_GROUP, SSM_STATE)
        m = jnp.einsum('jghp,gk->jgpkh', p4, eye)
        return m.reshape(SLABS, SLAB_STATE, LANES)

    def out_cplx(z):
        return jnp.concatenate([out_blk(z.real), -out_blk(z.imag)], axis=1)

    c = lax.complex(c_re.astype(F32), c_im.astype(F32))
    cblk = jnp.concatenate([out_cplx(c), out_cplx(c * lam_bar[:, None, :])], axis=-1).astype(BF16)
    k0 = jnp.einsum('ghp,gpk->gkh', c, b_bar).real.reshape(SLABS, GROUPS_PER_SLAB, SSM_GROUP,
                                                           SSM_GROUP)
    k0 = jnp.einsum('jgkh,gq->jgkqh', k0, eye).reshape(SLABS, LANES, LANES).astype(BF16)
    lam2 = lam_bar * lam_bar
    lam_k = jnp.stack([lam2.real.reshape(SLABS, SLAB_STATE),
                       lam2.imag.reshape(SLABS, SLAB_STATE)], axis=1)
    dsk = d_skip.astype(F32).reshape(SLABS, 1, LANES)
    return bblk, cblk, k0, lam_k, dsk


ATTN_QB = 8
HEADS_PER_TILE = LANES // HEAD_DIM


def _pair_norm(t, gain2, head_mean):
    sq = t * t
    hi = sq.astype(BF16)
    lo = (sq - hi.astype(F32)).astype(BF16)
    ms = (jnp.dot(hi, head_mean, preferred_element_type=F32)
          + jnp.dot(lo, head_mean, preferred_element_type=F32))
    return t * lax.rsqrt(ms + EPS) * gain2


def _attn_body(sink_ref, q_ref, kc_ref, kp_ref, vc_ref, vp_ref, qg_ref, kg_ref, o_ref):
    n = pl.program_id(1)
    lane = lax.broadcasted_iota(jnp.int32, (1, LANES), 1)
    lo = lane < HEAD_DIM
    li = lax.broadcasted_iota(jnp.int32, (LANES, LANES), 0)
    lj = lax.broadcasted_iota(jnp.int32, (LANES, LANES), 1)
    head_mean = jnp.where((li < HEAD_DIM) == (lj < HEAD_DIM), 1.0 / HEAD_DIM, 0.0).astype(BF16)
    ones = jnp.ones((LANES, LANES), BF16)

    kall = jnp.concatenate([kp_ref[0], kc_ref[0]], axis=0).astype(F32)
    kn = _pair_norm(kall, kg_ref[...], head_mean)
    kroll = pltpu.roll(kn, HEAD_DIM, axis=1)
    vall = jnp.concatenate([vp_ref[0], vc_ref[0]], axis=0).astype(F32)
    vroll = pltpu.roll(vall, HEAD_DIM, axis=1)
    zero = jnp.zeros_like(kn)
    k_var = [[jnp.where(lo, kn, zero).astype(BF16), jnp.where(lo, zero, kroll).astype(BF16)],
             [jnp.where(lo, kroll, zero).astype(BF16), jnp.where(lo, zero, kn).astype(BF16)]]
    v_var = [[jnp.where(lo, vall, zero).astype(BF16), jnp.where(lo, zero, vroll).astype(BF16)],
             [jnp.where(lo, vroll, zero).astype(BF16), jnp.where(lo, zero, vall).astype(BF16)]]

    qn = []
    for i in range(ATTN_WIDTH // LANES):
        t = _pair_norm(q_ref[0, :, i * LANES:(i + 1) * LANES].astype(F32), qg_ref[...], head_mean)
        qn.append((t * (HEAD_DIM ** -0.5)).astype(BF16))

    qi = lax.broadcasted_iota(jnp.int32, (BLOCK, BLOCK), 0)
    si = lax.broadcasted_iota(jnp.int32, (BLOCK, BLOCK), 1)
    is_cur = si <= qi
    dist = jnp.where(is_cur, qi - si, qi - si + BLOCK).astype(F32)
    alibi = [(2.0 ** (-8.0 * (h + 1) / N_HEADS)) * dist for h in range(N_HEADS)]
    sinkmat = jnp.concatenate([jnp.full((BLOCK, LANES), sink_ref[h], F32) for h in range(N_HEADS)],
                              axis=0)

    for b in range(ATTN_QB):
        rows = slice(b * BLOCK, (b + 1) * BLOCK)
        win = slice(b * BLOCK, (b + 2) * BLOCK)
        s_all = []
        for h in range(N_HEADS):
            i, slot = divmod(h, HEADS_PER_TILE)
            kvh = h // (N_HEADS // N_KV_HEADS)
            sc = lax.dot_general(qn[i][rows], k_var[kvh][slot][win], (((1,), (1,)), ((), ())),
                                 preferred_element_type=F32)
            prev = sc[:, :BLOCK]
            if b == 0:
                prev = jnp.where(n > 0, prev, NEG)
            s_all.append(jnp.where(is_cur, sc[:, BLOCK:], prev) - alibi[h])
        s = jnp.concatenate(s_all, axis=0)
        m = jnp.maximum(jnp.max(s, axis=-1, keepdims=True), sinkmat)
        p = jnp.exp(s - m)
        denom = jnp.dot(p.astype(BF16), ones, preferred_element_type=F32) + jnp.exp(sinkmat - m)
        pn = p / denom
        for i in range(ATTN_WIDTH // LANES):
            acc = jnp.zeros((BLOCK, LANES), F32)
            for slot in range(HEADS_PER_TILE):
                h = i * HEADS_PER_TILE + slot
                kvh = h // (N_HEADS // N_KV_HEADS)
                ph = pn[h * BLOCK:(h + 1) * BLOCK]
                pcat = jnp.concatenate([jnp.where(is_cur, 0.0, ph), jnp.where(is_cur, ph, 0.0)],
                                       axis=1).astype(BF16)
                acc = acc + jnp.dot(pcat, v_var[kvh][slot][win], preferred_element_type=F32)
            o_ref[0, rows, i * LANES:(i + 1) * LANES] = acc.astype(BF16)


def _attn(q3, k3, v3, q_gain, k_gain, sinks):
    bsz, seq, _ = q3.shape
    qrows = ATTN_QB * BLOCK
    cur = lambda b, n: (b, n, 0)
    prev = lambda b, n: (b, jnp.maximum(n * ATTN_QB - 1, 0), 0)
    qg2 = jnp.tile(q_gain.astype(F32), HEADS_PER_TILE).reshape(1, LANES)
    kg2 = jnp.tile(k_gain.astype(F32), HEADS_PER_TILE).reshape(1, LANES)
    gain_spec = pl.BlockSpec((1, LANES), lambda b, n: (0, 0))
    return pl.pallas_call(
        _attn_body,
        grid=(bsz, seq // qrows),
        in_specs=[pl.BlockSpec(memory_space=pltpu.SMEM),
                  pl.BlockSpec((1, qrows, ATTN_WIDTH), cur),
                  pl.BlockSpec((1, qrows, KV_WIDTH), cur),
                  pl.BlockSpec((1, BLOCK, KV_WIDTH), prev),
                  pl.BlockSpec((1, qrows, KV_WIDTH), cur),
                  pl.BlockSpec((1, BLOCK, KV_WIDTH), prev),
                  gain_spec, gain_spec],
        out_specs=pl.BlockSpec((1, qrows, ATTN_WIDTH), cur),
        out_shape=jax.ShapeDtypeStruct((bsz, seq, ATTN_WIDTH), BF16),
        compiler_params=_cparams(("arbitrary", "arbitrary")),
        name="attn",
    )(sinks.astype(F32), q3, k3, k3, v3, v3, qg2, kg2)


MIX_TM = 1024


def _mix_body(z_ref, ya_ref, gs_ref, ga_ref, x_ref, wglu_ref, wup_ref, wout_ref, g2_ref, wr_ref,
              x1_ref, gate_ref, hs_hbm, starts_ref, lens_ref, counts_ref,
              stage_ref, run_ref, pend_ref, sem):
    t = pl.program_id(0)
    sorter = _GroupSorter(t, MIX_TM // SORT_TM, hs_hbm, starts_ref, lens_ref, counts_ref,
                          stage_ref, run_ref, pend_ref, sem)

    z = z_ref[...]
    bv = jnp.dot(z, wglu_ref[:, :D_MODEL], preferred_element_type=F32)
    bg = jnp.dot(z, wglu_ref[:, D_MODEL:], preferred_element_type=F32)
    bs = bv * jax.nn.sigmoid(bg)
    ba = jnp.dot(ya_ref[...], wup_ref[...], preferred_element_type=F32)
    merged = gs_ref[...].astype(F32) * bs + ga_ref[...].astype(F32) * ba
    x1 = x_ref[...] + jnp.dot(merged.astype(BF16), wout_ref[...], preferred_element_type=F32)
    x1_ref[...] = x1

    ms = jnp.mean(x1 * x1, axis=-1, keepdims=True)
    h2 = x1 * lax.rsqrt(ms + EPS) * g2_ref[...]
    h2b = h2.astype(BF16)

    lt = lax.dot_general(wr_ref[...], h2b, (((1,), (1,)), ((), ())),
                         preferred_element_type=F32)
    logits = lt[:ROUTER_ROWS] + lt[ROUTER_ROWS:]
    row = lax.broadcasted_iota(jnp.int32, logits.shape, 0)
    ninf = -jnp.inf
    cm = jnp.where(row < N_GROUPS, logits, ninf)
    cmax = jnp.max(cm, axis=0, keepdims=True)
    g_prob = 1.0 / jnp.sum(jnp.exp(cm - cmax), axis=0, keepdims=True)
    g_idx = jnp.min(jnp.where(cm == cmax, row, ROUTER_ROWS), axis=0, keepdims=True)
    f0 = FINE_OFF + EXPERTS_PER_GROUP * g_idx
    fm = jnp.where((row >= f0) & (row < f0 + EXPERTS_PER_GROUP), logits, ninf)
    v1 = jnp.max(fm, axis=0, keepdims=True)
    i1 = jnp.min(jnp.where(fm == v1, row, ROUTER_ROWS), axis=0, keepdims=True)
    fm2 = jnp.where(row == i1, ninf, fm)
    v2 = jnp.max(fm2, axis=0, keepdims=True)
    i2 = jnp.min(jnp.where(fm2 == v2, row, ROUTER_ROWS), axis=0, keepdims=True)
    e21 = jnp.exp(v2 - v1)
    w1 = g_prob / (1.0 + e21)
    w2 = w1 * e21
    rt = jnp.where(row == i1 - f0, w1,
                   jnp.where(row == i2 - f0, w2,
                             jnp.where(row == ROUTE_GROUP_OFF + g_idx, 1.0, 0.0)))
    rt = jnp.concatenate([rt, jnp.zeros((ROUTER_LANES - ROUTER_ROWS, rt.shape[1]), F32)], axis=0)
    route = rt.T
    gate_ref[...] = route
    sorter.begin()
    sorter.issue(sorter.stage(h2b, route))

    @pl.when(t == pl.num_programs(0) - 1)
    def _():
        sorter.finish()


def _mix(z2, ya2, gs2, ga2, x2, wglu, wup, wout, g2, wr, layer):
    n = x2.shape[0]
    n_sort = n // SORT_TM
    subs = MIX_TM // SORT_TM
    cap = _region_cap(n)
    assert subs * SORT_TM >= ZFILL_TOK
    row = lambda c: pl.BlockSpec((MIX_TM, c), lambda i: (i, 0))
    full = lambda a, b: pl.BlockSpec((a, b), lambda i: (0, 0))
    smem = pl.BlockSpec(memory_space=pltpu.SMEM)
    return pl.pallas_call(
        _mix_body,
        grid=(n // MIX_TM,),
        in_specs=[row(SSM_WIDTH), row(ATTN_WIDTH), row(D_MODEL), row(D_MODEL), row(D_MODEL),
                  _layer_spec(layer, SSM_WIDTH, 2 * D_MODEL), _layer_spec(layer, ATTN_WIDTH, D_MODEL),
                  _layer_spec(layer, D_MODEL, D_MODEL), full(1, D_MODEL),
                  _layer_spec(layer, 2 * ROUTER_ROWS, D_MODEL)],
        out_specs=[row(D_MODEL), row(ROUTER_LANES), pl.BlockSpec(memory_space=pl.ANY),
                   smem, smem, smem],
        out_shape=[jax.ShapeDtypeStruct((n, D_MODEL), F32),
                   jax.ShapeDtypeStruct((n, ROUTER_LANES), F32),
                   jax.ShapeDtypeStruct((N_GROUPS * cap * TOK_ROWS, LANES), jnp.uint32),
                   jax.ShapeDtypeStruct((n_sort * N_GROUPS,), jnp.int32),
                   jax.ShapeDtypeStruct((n_sort * N_GROUPS,), jnp.int32),
                   jax.ShapeDtypeStruct((N_GROUPS,), jnp.int32)],
        scratch_shapes=[pltpu.VMEM((subs * SORT_TM * TOK_ROWS, LANES), jnp.uint32),
                        pltpu.SMEM((N_GROUPS,), jnp.int32),
                        pltpu.SMEM((subs * N_GROUPS,), jnp.int32),
                        pltpu.SemaphoreType.DMA((subs, N_GROUPS))],
        compiler_params=_cparams(("arbitrary",)),
        name="mix",
    )(z2, ya2, gs2, ga2, x2, wglu, wup, wout, g2, wr)


SORT_TM = 256
COMBINE_TM = 1024
MOE_TM = 512
TOK_ROWS = 8
H_WORDS = D_MODEL // 2
H_CHUNKS = H_WORDS // LANES
ROUTE_ROW = H_CHUNKS
ZFILL_TOK = MOE_TM
HI_MASK = 0xFFFF0000


def _region_cap(n):
    cap = n + ZFILL_TOK
    return -(-cap // MOE_TM) * MOE_TM


def _for_nonempty(items, fn):
    all_pos = functools.reduce(jnp.logical_and, [c > 0 for c, _ in items])

    @pl.when(all_pos)
    def _():
        for _, payload in items:
            fn(payload)

    @pl.when(jnp.logical_not(all_pos))
    def _():
        for c, payload in items:
            @pl.when(c > 0)
            def _(payload=payload):
                fn(payload)


def _tile_positions(route):
    tm = route.shape[0]
    lane = lax.broadcasted_iota(jnp.int32, route.shape, 1)
    onehot = jnp.where((lane >= ROUTE_GROUP_OFF) & (lane < ROUTE_GROUP_OFF + N_GROUPS), route, 0.0)
    ci = lax.broadcasted_iota(jnp.int32, (tm, tm), 0)
    cj = lax.broadcasted_iota(jnp.int32, (tm, tm), 1)
    earlier = (cj < ci).astype(BF16)
    rank = jnp.dot(earlier, onehot.astype(BF16), preferred_element_type=F32)
    cnt = jnp.sum(onehot, axis=0, keepdims=True)
    li = lax.broadcasted_iota(jnp.int32, (LANES, LANES), 0)
    lj = lax.broadcasted_iota(jnp.int32, (LANES, LANES), 1)
    base = jnp.dot(jnp.broadcast_to(cnt, (8, LANES)).astype(BF16), (li < lj).astype(BF16),
                   preferred_element_type=F32)[0:1]
    return onehot, rank, base, cnt


class _GroupSorter:
    def __init__(self, t, subs, hs_hbm, starts_ref, lens_ref, counts_ref, stage_ref, run_ref,
                 pend_ref, sem):
        self.t, self.subs = t, subs
        self.hs_hbm, self.starts_ref, self.lens_ref, self.counts_ref = (
            hs_hbm, starts_ref, lens_ref, counts_ref)
        self.stage_ref, self.run_ref, self.pend_ref, self.sem = stage_ref, run_ref, pend_ref, sem
        self.slot_rows = SORT_TM * TOK_ROWS
        self.cap = hs_hbm.shape[0] // (N_GROUPS * TOK_ROWS)
        self.set0 = 0

    def _copy(self, src_row, dst_row, slot, g, tokens):
        return pltpu.make_async_copy(
            self.stage_ref.at[pl.ds(src_row, tokens * TOK_ROWS), :],
            self.hs_hbm.at[pl.ds(dst_row, tokens * TOK_ROWS), :], self.sem.at[slot, g])

    def _wait_set(self, set0):
        items = []
        for sl in range(self.subs):
            for g in range(N_GROUPS):
                n = self.pend_ref[(set0 + sl) * N_GROUPS + g]
                items.append((n, self._copy(0, 0, set0 + sl, g, n)))
        _for_nonempty(items, lambda copy: copy.wait())

    def begin(self):
        @pl.when(self.t == 0)
        def _():
            self.stage_ref[...] = jnp.zeros_like(self.stage_ref)
            for g in range(N_GROUPS):
                self.run_ref[g] = 0

        @pl.when(self.t > 0)
        def _():
            self._wait_set(self.set0)

    def stage(self, h, route):
        tm = SORT_TM
        runs = []
        for sl in range(self.subs):
            r = route[sl * tm:(sl + 1) * tm]
            onehot, rank, base, cnt = _tile_positions(r)
            w = (onehot * (rank + base)).astype(BF16)
            pos_row = lax.dot_general(jnp.ones((8, LANES), BF16), w, (((1,), (1,)), ((), ())),
                                      preferred_element_type=F32)[0:1]
            ri = lax.broadcasted_iota(jnp.int32, (tm, tm), 0).astype(F32)
            perm = (ri == pos_row).astype(BF16)
            r_hi = r.astype(BF16)
            r_mid = (r - r_hi.astype(F32)).astype(BF16)
            r_lo = (r - r_hi.astype(F32) - r_mid.astype(F32)).astype(BF16)
            sorted_all = jnp.dot(
                perm,
                jnp.concatenate([h[sl * tm:(sl + 1) * tm], r_hi, r_mid, r_lo], axis=1),
                preferred_element_type=F32)
            sh = sorted_all[:, :D_MODEL]
            sr = (sorted_all[:, D_MODEL:D_MODEL + LANES]
                  + sorted_all[:, D_MODEL + LANES:D_MODEL + 2 * LANES]
                  + sorted_all[:, D_MODEL + 2 * LANES:])
            hb = lax.bitcast_convert_type(sh, jnp.uint32)
            words = (hb[:, :H_WORDS] & jnp.uint32(HI_MASK)) | (hb[:, H_WORDS:] >> 16)

            slot0 = (self.set0 + sl) * self.slot_rows
            for s in range(H_CHUNKS):
                self.stage_ref[pl.ds(slot0 + s, tm, stride=TOK_ROWS), :] = (
                    words[:, s * LANES:(s + 1) * LANES])
            self.stage_ref[pl.ds(slot0 + ROUTE_ROW, tm, stride=TOK_ROWS), :] = (
                lax.bitcast_convert_type(sr, jnp.uint32))
            for g in range(N_GROUPS):
                runs.append((sl, g, slot0,
                             cnt[0, ROUTE_GROUP_OFF + g].astype(jnp.int32),
                             base[0, ROUTE_GROUP_OFF + g].astype(jnp.int32)))
        return runs

    def issue(self, runs):
        first_tile = self.t * self.subs
        items = []
        for sl, g, slot0, c_g, b_g in runs:
            start = self.run_ref[g]
            self.starts_ref[(first_tile + sl) * N_GROUPS + g] = start
            self.lens_ref[(first_tile + sl) * N_GROUPS + g] = c_g
            self.pend_ref[(self.set0 + sl) * N_GROUPS + g] = c_g
            self.run_ref[g] = start + c_g
            src = pl.multiple_of(slot0 + b_g * TOK_ROWS, TOK_ROWS)
            dst = pl.multiple_of((g * self.cap + start) * TOK_ROWS, TOK_ROWS)
            items.append((c_g, self._copy(src, dst, self.set0 + sl, g, c_g)))
        _for_nonempty(items, lambda copy: copy.start())

    def finish(self):
        self._wait_set(self.set0)
        self.stage_ref[...] = jnp.zeros_like(self.stage_ref)
        zrows = ZFILL_TOK * TOK_ROWS
        for g in range(N_GROUPS):
            total = self.run_ref[g]
            self.counts_ref[g] = total
            dst = pl.multiple_of((g * self.cap + total) * TOK_ROWS, TOK_ROWS)
            pltpu.make_async_copy(self.stage_ref.at[pl.ds(0, zrows), :],
                                  self.hs_hbm.at[pl.ds(dst, zrows), :], self.sem.at[0, g]).start()
        for g in range(N_GROUPS):
            pltpu.make_async_copy(self.stage_ref.at[pl.ds(0, zrows), :],
                                  self.hs_hbm.at[pl.ds(0, zrows), :], self.sem.at[0, g]).wait()


def _experts_body(blk_ref, grp_ref, valid_ref, hs_ref, wu32_ref, wd32_ref, ys_ref, wu_ref, wd_ref):
    t = pl.program_id(0)
    tm = MOE_TM

    @pl.when((t == 0) | (grp_ref[t] != grp_ref[jnp.maximum(t - 1, 0)]))
    def _():
        wu_ref[...] = wu32_ref[...].astype(BF16)
        wd_ref[...] = wd32_ref[...].astype(BF16)

    @pl.when(valid_ref[t] > 0)
    def _():
        chunks = [hs_ref[pl.ds(s, tm, stride=TOK_ROWS), :] for s in range(H_CHUNKS + 1)]
        hi = [lax.bitcast_convert_type(c & jnp.uint32(HI_MASK), F32) for c in chunks[:H_CHUNKS]]
        lo = [lax.bitcast_convert_type(c << 16, F32) for c in chunks[:H_CHUNKS]]
        h = jnp.concatenate(hi + lo, axis=1).astype(BF16)
        route = lax.bitcast_convert_type(chunks[ROUTE_ROW], F32)
        acts = []
        for e in range(EXPERTS_PER_GROUP):
            gu = jnp.dot(h, wu_ref[e], preferred_element_type=F32)
            acts.append((jax.nn.silu(gu[:, :D_EXPERT]) * gu[:, D_EXPERT:]
                         * route[:, e:e + 1]).astype(BF16))
        act = jnp.concatenate(acts, axis=1)
        wd = wd_ref[...].reshape(EXPERTS_PER_GROUP * D_EXPERT, D_MODEL)
        out = jnp.dot(act, wd, preferred_element_type=F32)
        for j in range(D_MODEL // LANES):
            ys_ref[pl.ds(j, tm, stride=TOK_ROWS), :] = out[:, j * LANES:(j + 1) * LANES]


def _experts(blk, grp, valid, hs, wu, wd, layer):
    g0 = layer * N_GROUPS
    n_tiles = blk.shape[0]
    epg = EXPERTS_PER_GROUP
    return pl.pallas_call(
        _experts_body,
        grid_spec=pltpu.PrefetchScalarGridSpec(
            num_scalar_prefetch=3,
            grid=(n_tiles,),
            in_specs=[pl.BlockSpec((MOE_TM * TOK_ROWS, LANES), lambda t, b, g, v: (b[t], 0)),
                      pl.BlockSpec((epg, D_MODEL, 2 * D_EXPERT),
                                   lambda t, b, g, v: (g0 + g[t], 0, 0)),
                      pl.BlockSpec((epg, D_EXPERT, D_MODEL),
                                   lambda t, b, g, v: (g0 + g[t], 0, 0))],
            out_specs=pl.BlockSpec((MOE_TM * TOK_ROWS, LANES), lambda t, b, g, v: (b[t], 0)),
            scratch_shapes=[pltpu.VMEM((epg, D_MODEL, 2 * D_EXPERT), BF16),
                            pltpu.VMEM((epg, D_EXPERT, D_MODEL), BF16)],
        ),
        out_shape=jax.ShapeDtypeStruct(hs.shape, F32),
        compiler_params=_cparams(("arbitrary",)),
        name="moe_experts",
    )(blk, grp, valid, hs, wu, wd)


def _moe_outputs(starts_ref, lens_ref, route_ref, ys_hbm, buf_ref, sem):
    t = pl.program_id(0)
    nt = pl.num_programs(0)
    tm = SORT_TM
    subs = route_ref.shape[0] // tm
    slot_rows = tm * TOK_ROWS
    cap = ys_hbm.shape[0] // (N_GROUPS * TOK_ROWS)

    def fetches(step, sl):
        tile = step * subs + sl
        slot = (step % 2) * subs + sl
        copies = []
        first = 0
        for g in range(N_GROUPS):
            tokens = lens_ref[tile * N_GROUPS + g]
            src = pl.multiple_of((g * cap + starts_ref[tile * N_GROUPS + g]) * TOK_ROWS, TOK_ROWS)
            dst = pl.multiple_of(slot * slot_rows + first * TOK_ROWS, TOK_ROWS)
            copies.append((tokens, pltpu.make_async_copy(
                ys_hbm.at[pl.ds(src, tokens * TOK_ROWS), :],
                buf_ref.at[pl.ds(dst, tokens * TOK_ROWS), :], sem.at[slot, g])))
            first = first + tokens
        return copies

    def start_fetches(step):
        items = [item for sl in range(subs) for item in fetches(step, sl)]
        _for_nonempty(items, lambda copy: copy.start())

    @pl.when(t == 0)
    def _():
        start_fetches(0)

    @pl.when(t + 1 < nt)
    def _():
        start_fetches(t + 1)

    _for_nonempty([item for sl in range(subs) for item in fetches(t, sl)],
                  lambda copy: copy.wait())

    outs = []
    for sl in range(subs):
        onehot, rank, base, _ = _tile_positions(route_ref[sl * tm:(sl + 1) * tm, :])
        pos = jnp.sum(onehot * (rank + base), axis=1, keepdims=True)
        ri = lax.broadcasted_iota(jnp.int32, (tm, tm), 1).astype(F32)
        sel = (ri == pos).astype(BF16)
        r0 = pl.multiple_of(((t % 2) * subs + sl) * slot_rows, TOK_ROWS)
        y = jnp.concatenate(
            [buf_ref[pl.ds(r0 + j, tm, stride=TOK_ROWS), :] for j in range(D_MODEL // LANES)],
            axis=1).astype(BF16)
        outs.append(jnp.dot(sel, y, preferred_element_type=F32))
    return outs


def _combine_body(starts_ref, lens_ref, x1_ref, route_ref, ys_hbm, o_ref, buf_ref, sem):
    outs = _moe_outputs(starts_ref, lens_ref, route_ref, ys_hbm, buf_ref, sem)
    for sl, y in enumerate(outs):
        rows = slice(sl * SORT_TM, (sl + 1) * SORT_TM)
        o_ref[rows, :] = x1_ref[rows, :] + y


def _combine(starts, lens, x1, route, ys):
    n = x1.shape[0]
    subs = COMBINE_TM // SORT_TM
    return pl.pallas_call(
        _combine_body,
        grid_spec=pltpu.PrefetchScalarGridSpec(
            num_scalar_prefetch=2,
            grid=(n // COMBINE_TM,),
            in_specs=[pl.BlockSpec((COMBINE_TM, D_MODEL), lambda i, s, c: (i, 0)),
                      pl.BlockSpec((COMBINE_TM, ROUTER_LANES), lambda i, s, c: (i, 0)),
                      pl.BlockSpec(memory_space=pl.ANY)],
            out_specs=pl.BlockSpec((COMBINE_TM, D_MODEL), lambda i, s, c: (i, 0)),
            scratch_shapes=[pltpu.VMEM((2 * COMBINE_TM * TOK_ROWS, LANES), F32),
                            pltpu.SemaphoreType.DMA((2 * subs, N_GROUPS))],
        ),
        out_shape=jax.ShapeDtypeStruct((n, D_MODEL), F32),
        compiler_params=_cparams(("arbitrary",)),
        name="moe_combine",
    )(starts, lens, x1, route, ys)


def _expert_tiles(counts, n):
    cap_blocks = _region_cap(n) // MOE_TM
    n_tiles = n // MOE_TM + N_GROUPS
    per_group = (counts + MOE_TM - 1) // MOE_TM
    ends = jnp.cumsum(per_group)
    total = ends[-1]
    t = jnp.minimum(jnp.arange(n_tiles, dtype=jnp.int32), total - 1)
    grp = jnp.sum((t[:, None] >= ends[None, :]).astype(jnp.int32), axis=1)
    first = ends - per_group
    blk = grp * cap_blocks + (t - first[grp])
    valid = (jnp.arange(n_tiles, dtype=jnp.int32) < total).astype(jnp.int32)
    return blk.astype(jnp.int32), grp.astype(jnp.int32), valid


def _moe_sorted(n, hs, counts, w_up, w_down, layer):
    blk, grp, valid = _expert_tiles(counts, n)
    return _experts(blk, grp, valid, hs, w_up, w_down, layer)


def kernel(x, norm_mix, w_in, ssm_lam_re, ssm_lam_im, ssm_log_dt, ssm_b_re, ssm_b_im, ssm_c_re,
           ssm_c_im, ssm_d, w_ssm_glu_val, w_ssm_glu_gate, q_norm, k_norm, attn_sinks, w_attn_up,
           w_out, norm_ffn, w_coarse, w_fine, w_expert_up, w_expert_down):
    bsz, seq, d = x.shape
    n = bsz * seq
    depth = w_in.shape[0]
    x2 = x.reshape(n, d)
    s5p = jax.vmap(_s5_params)(ssm_lam_re, ssm_lam_im, ssm_log_dt, ssm_b_re, ssm_b_im, ssm_c_re,
                               ssm_c_im, ssm_d)
    w_in_b = w_in.astype(BF16)
    wglu = jnp.concatenate([w_ssm_glu_val, w_ssm_glu_gate], axis=2).astype(BF16)
    wup_b = w_attn_up.astype(BF16)
    wout_b = w_out.astype(BF16)
    wr_t = jnp.concatenate(
        [jnp.swapaxes(w_coarse, 1, 2), jnp.swapaxes(w_fine, 1, 2),
         jnp.zeros((depth, ROUTER_ROWS - N_GROUPS - N_EXPERTS, d), w_fine.dtype)], axis=1).astype(F32)
    wr_hi = wr_t.astype(BF16)
    wr = jnp.concatenate([wr_hi, (wr_t - wr_hi.astype(F32)).astype(BF16)], axis=1)
    wu_all = w_expert_up.reshape((depth * N_EXPERTS,) + w_expert_up.shape[2:])
    wd_all = w_expert_down.reshape((depth * N_EXPERTS,) + w_expert_down.shape[2:])
    pending = None
    for i in range(depth):
        if pending is None:
            u, q, k, v, gs, ga = _proj(x2, norm_mix[i].reshape(1, d), w_in_b, i)
        else:
            x2, u, q, k, v, gs, ga = _proj_moe(*pending, norm_mix[i].reshape(1, d), w_in_b, i)
        z = _s5(u.reshape(bsz, seq, SSM_WIDTH), *(p[i] for p in s5p))
        ya = _attn(q.reshape(bsz, seq, ATTN_WIDTH), k.reshape(bsz, seq, KV_WIDTH),
                   v.reshape(bsz, seq, KV_WIDTH), q_norm[i], k_norm[i], attn_sinks[i])
        x1, route, hs, starts, lens, counts = _mix(
            z.reshape(n, SSM_WIDTH), ya.reshape(n, ATTN_WIDTH), gs, ga, x2,
            wglu, wup_b, wout_b, norm_ffn[i].reshape(1, d), wr, i)
        ys = _moe_sorted(n, hs, counts, wu_all, wd_all, i)
        pending = (starts, lens, x1, route, ys)
    return _combine(*pending).reshape(bsz, seq, d)
```

```python
import functools
import math

import jax
import jax.numpy as jnp
from jax import lax
from jax.experimental import pallas as pl
from jax.experimental.pallas import tpu as pltpu

F32 = jnp.float32
BF16 = jnp.bfloat16

D_MODEL = 1024
SSM_WIDTH = 512
SSM_GROUP = 16
SSM_GROUPS = 32
SSM_STATE = 64
N_HEADS = 8
N_KV_HEADS = 2
HEAD_DIM = 64
ATTN_WIDTH = 512
KV_WIDTH = 128
BLOCK = 128
N_GROUPS = 4
EXPERTS_PER_GROUP = 4
N_EXPERTS = 16
D_EXPERT = 256
EPS = 1e-6
NEG = -1e30

LANES = 128
SLABS = SSM_WIDTH // LANES
GROUPS_PER_SLAB = LANES // SSM_GROUP
SLAB_STATE = GROUPS_PER_SLAB * SSM_STATE
ROUTER_LANES = 128
ROUTER_ROWS = 32
FINE_OFF = N_GROUPS
ROUTE_GROUP_OFF = 8

VMEM_LIMIT = 56 * 1024 * 1024


def _cparams(sem):
    return pltpu.CompilerParams(dimension_semantics=sem, vmem_limit_bytes=VMEM_LIMIT)


PROJ_TM = 1024


def _proj_body(x_ref, g_ref, w_ref, *out_refs):
    _project(x_ref[...], g_ref, w_ref, *out_refs)


def _proj_moe_body(starts_ref, lens_ref, x1_ref, route_ref, ys_hbm, g_ref, w_ref,
                   x_ref, *rest):
    out_refs, (buf_ref, sem) = rest[:-2], rest[-2:]
    moe = _moe_outputs(starts_ref, lens_ref, route_ref, ys_hbm, buf_ref, sem)
    x = x1_ref[...] + jnp.concatenate(moe, axis=0)
    x_ref[...] = x
    _project(x, g_ref, w_ref, *out_refs)


def _project(x, g_ref, w_ref, u_ref, q_ref, k_ref, v_ref, gs_ref, ga_ref):
    ms = jnp.mean(x * x, axis=-1, keepdims=True)
    h = (x * lax.rsqrt(ms + EPS) * g_ref[...]).astype(BF16)

    def seg(a, b):
        return jnp.dot(h, w_ref[:, a:b], preferred_element_type=F32)

    o = 0
    u_ref[...] = seg(o, o + SSM_WIDTH)
    o += SSM_WIDTH
    q_ref[...] = seg(o, o + ATTN_WIDTH).astype(BF16)
    o += ATTN_WIDTH
    kv = seg(o, o + 2 * KV_WIDTH).astype(BF16)
    k_ref[...] = kv[:, :KV_WIDTH]
    v_ref[...] = kv[:, KV_WIDTH:]
    o += 2 * KV_WIDTH
    gs_ref[...] = jax.nn.sigmoid(seg(o, o + D_MODEL)).astype(BF16)
    o += D_MODEL
    ga_ref[...] = jax.nn.sigmoid(seg(o, o + D_MODEL)).astype(BF16)


def _layer_spec(layer, *shape):
    return pl.BlockSpec((None,) + shape, lambda *_: (layer,) + (0,) * len(shape))


def _proj_out(n):
    row = lambda c: pl.BlockSpec((PROJ_TM, c), lambda i, *_: (i, 0))
    specs = [row(SSM_WIDTH), row(ATTN_WIDTH), row(KV_WIDTH), row(KV_WIDTH), row(D_MODEL),
             row(D_MODEL)]
    shapes = [jax.ShapeDtypeStruct((n, SSM_WIDTH), F32),
              jax.ShapeDtypeStruct((n, ATTN_WIDTH), BF16),
              jax.ShapeDtypeStruct((n, KV_WIDTH), BF16),
              jax.ShapeDtypeStruct((n, KV_WIDTH), BF16),
              jax.ShapeDtypeStruct((n, D_MODEL), BF16),
              jax.ShapeDtypeStruct((n, D_MODEL), BF16)]
    return specs, shapes


def _proj(x2, g, w, layer):
    n = x2.shape[0]
    cols = w.shape[2]
    row = lambda c: pl.BlockSpec((PROJ_TM, c), lambda i: (i, 0))
    full = lambda a, b: pl.BlockSpec((a, b), lambda i: (0, 0))
    out_specs, out_shape = _proj_out(n)
    return pl.pallas_call(
        _proj_body,
        grid=(n // PROJ_TM,),
        in_specs=[row(D_MODEL), full(1, D_MODEL), _layer_spec(layer, D_MODEL, cols)],
        out_specs=out_specs,
        out_shape=out_shape,
        compiler_params=_cparams(("arbitrary",)),
        name="proj",
    )(x2, g, w)


def _proj_moe(starts, lens, x1, route, ys, g, w, layer):
    n = x1.shape[0]
    cols = w.shape[2]
    subs = PROJ_TM // SORT_TM
    row = lambda c: pl.BlockSpec((PROJ_TM, c), lambda i, *_: (i, 0))
    out_specs, out_shape = _proj_out(n)
    return pl.pallas_call(
        _proj_moe_body,
        grid_spec=pltpu.PrefetchScalarGridSpec(
            num_scalar_prefetch=2,
            grid=(n // PROJ_TM,),
            in_specs=[row(D_MODEL), row(ROUTER_LANES), pl.BlockSpec(memory_space=pl.ANY),
                      pl.BlockSpec((1, D_MODEL), lambda i, *_: (0, 0)),
                      _layer_spec(layer, D_MODEL, cols)],
            out_specs=[row(D_MODEL)] + out_specs,
            scratch_shapes=[pltpu.VMEM((2 * PROJ_TM * TOK_ROWS, LANES), F32),
                            pltpu.SemaphoreType.DMA((2 * subs, N_GROUPS))],
        ),
        out_shape=[jax.ShapeDtypeStruct((n, D_MODEL), F32)] + out_shape,
        compiler_params=_cparams(("arbitrary",)),
        name="proj_moe",
    )(starts, lens, x1, route, ys, g, w)


S5_TT = 64
S5_PAIRS = S5_TT // 2


def _s5_body(u_ref, bblk_ref, cblk_ref, k0_ref, lam_ref, d_ref, z_ref,
             ut_ref, y_ref, st_ref, carry_ref, *, bsz):
    rows = S5_TT * bsz
    prow = S5_PAIRS * bsz

    @pl.when(pl.program_id(0) == 0)
    def _():
        carry_ref[...] = jnp.zeros_like(carry_ref)

    for b in range(bsz):
        for j in range(SLABS):
            ut_ref[j, pl.ds(b, S5_TT, stride=bsz), :] = u_ref[b, :, j * LANES:(j + 1) * LANES]

    for j in range(SLABS):
        u3 = ut_ref[j].reshape(S5_PAIRS, 2 * bsz, LANES)
        u0 = u3[:, :bsz, :].reshape(prow, LANES)
        u1 = u3[:, bsz:, :].reshape(prow, LANES)
        lhs = jnp.concatenate([u1, u0], axis=1).astype(BF16)
        st_ref[j, 0:bsz, :] = carry_ref[j]
        st_ref[j, bsz:, :] = jnp.dot(lhs, bblk_ref[j], preferred_element_type=F32)
        a = jnp.broadcast_to(lam_ref[j, 0:1, :], (bsz, SLAB_STATE))
        bb = jnp.broadcast_to(lam_ref[j, 1:2, :], (bsz, SLAB_STATE))

        cre = carry_ref[j, :, 0:SLAB_STATE]
        cim = carry_ref[j, :, SLAB_STATE:2 * SLAB_STATE]
        for k in range(S5_PAIRS):
            r0 = (k + 1) * bsz
            bre = st_ref[j, r0:r0 + bsz, 0:SLAB_STATE]
            bim = st_ref[j, r0:r0 + bsz, SLAB_STATE:2 * SLAB_STATE]
            cre, cim = a * cre - bb * cim + bre, a * cim + bb * cre + bim
            st_ref[j, r0:r0 + bsz, 0:SLAB_STATE] = cre
            st_ref[j, r0:r0 + bsz, SLAB_STATE:2 * SLAB_STATE] = cim
        carry_ref[j, :, 0:SLAB_STATE] = cre
        carry_ref[j, :, SLAB_STATE:2 * SLAB_STATE] = cim

        cs = jnp.dot(st_ref[j].astype(BF16), cblk_ref[j], preferred_element_type=F32)
        y1 = cs[bsz:, :LANES] + d_ref[j] * u1
        y0 = (cs[:prow, LANES:] + jnp.dot(u0.astype(BF16), k0_ref[j], preferred_element_type=F32)
              + d_ref[j] * u0)
        y = jnp.concatenate([y0.reshape(S5_PAIRS, bsz, LANES), y1.reshape(S5_PAIRS, bsz, LANES)],
                            axis=1).reshape(rows, LANES)
        y_ref[j] = jax.nn.gelu(y)

    for b in range(bsz):
        for j in range(SLABS):
            z_ref[b, :, j * LANES:(j + 1) * LANES] = (
                y_ref[j, pl.ds(b, S5_TT, stride=bsz), :].astype(BF16))


def _s5(u3, bblk, cblk, k0, lam, dskip):
    bsz, seq, _ = u3.shape
    rows = S5_TT * bsz
    full = lambda *s: pl.BlockSpec(s, lambda i: (0,) * len(s))
    return pl.pallas_call(
        functools.partial(_s5_body, bsz=bsz),
        grid=(seq // S5_TT,),
        in_specs=[pl.BlockSpec((bsz, S5_TT, SSM_WIDTH), lambda i: (0, i, 0)),
                  full(SLABS, 2 * LANES, 2 * SLAB_STATE),
                  full(SLABS, 2 * SLAB_STATE, 2 * LANES),
                  full(SLABS, LANES, LANES),
                  full(SLABS, 2, SLAB_STATE),
                  full(SLABS, 1, LANES)],
        out_specs=pl.BlockSpec((bsz, S5_TT, SSM_WIDTH), lambda i: (0, i, 0)),
        out_shape=jax.ShapeDtypeStruct((bsz, seq, SSM_WIDTH), BF16),
        scratch_shapes=[pltpu.VMEM((SLABS, rows, LANES), F32),
                        pltpu.VMEM((SLABS, rows, LANES), F32),
                        pltpu.VMEM((SLABS, bsz + S5_PAIRS * bsz, 2 * SLAB_STATE), F32),
                        pltpu.VMEM((SLABS, bsz, 2 * SLAB_STATE), F32)],
        compiler_params=_cparams(("arbitrary",)),
        name="s5",
    )(u3, bblk, cblk, k0, lam, dskip)


def _s5_params(lam_re, lam_im, log_dt, b_re, b_im, c_re, c_im, d_skip):
    lam = lax.complex(lam_re.astype(F32), lam_im.astype(F32))
    dt = jnp.exp(log_dt.astype(F32))[:, None]
    lam_bar = jnp.exp(lam * dt)
    b = lax.complex(b_re.astype(F32), b_im.astype(F32))
    b_bar = ((lam_bar - 1.0) / lam)[..., None] * b
    eye = jnp.eye(GROUPS_PER_SLAB, dtype=F32)

    def in_blk(part):
        p4 = part.reshape(SLABS, GROUPS_PER_SLAB, SSM_STATE, SSM_GROUP)
        m = jnp.einsum('jgph,gk->jghkp', p4, eye)
        return m.reshape(SLABS, LANES, SLAB_STATE)

    def in_cplx(z):
        return jnp.concatenate([in_blk(z.real), in_blk(z.imag)], axis=-1)

    bblk = jnp.concatenate([in_cplx(b_bar), in_cplx(lam_bar[..., None] * b_bar)],
                           axis=1).astype(BF16)

    def out_blk(part):
        p4 = part.reshape(SLABS, GROUPS_PER_SLAB, SSM_GROUP, SSM_STATE)
        m = jnp.einsum('jghp,gk->jgpkh', p4, eye)
        return m.reshape(SLABS, SLAB_STATE, LANES)

    def out_cplx(z):
        return jnp.concatenate([out_blk(z.real), -out_blk(z.imag)], axis=1)

    c = lax.complex(c_re.astype(F32), c_im.astype(F32))
    cblk = jnp.concatenate([out_cplx(c), out_cplx(c * lam_bar[:, None, :])], axis=-1).astype(BF16)
    k0 = jnp.einsum('ghp,gpk->gkh', c, b_bar).real.reshape(SLABS, GROUPS_PER_SLAB, SSM_GROUP,
                                                           SSM_GROUP)
    k0 = jnp.einsum('jgkh,gq->jgkqh', k0, eye).reshape(SLABS, LANES, LANES).astype(BF16)
    lam2 = lam_bar * lam_bar
    lam_k = jnp.stack([lam2.real.reshape(SLABS, SLAB_STATE),
                       lam2.imag.reshape(SLABS, SLAB_STATE)], axis=1)
    dsk = d_skip.astype(F32).reshape(SLABS, 1, LANES)
    return bblk, cblk, k0, lam_k, dsk


ATTN_QB = 8
HEADS_PER_TILE = LANES // HEAD_DIM


def _pair_norm(t, gain2, head_mean):
    ms = jnp.dot((t * t).astype(BF16), head_mean, preferred_element_type=F32)
    return t * lax.rsqrt(ms + EPS) * gain2


def _attn_body(sink_ref, q_ref, kc_ref, kp_ref, vc_ref, vp_ref, qg_ref, kg_ref, o_ref):
    n = pl.program_id(1)
    lane = lax.broadcasted_iota(jnp.int32, (1, LANES), 1)
    lo = lane < HEAD_DIM
    li = lax.broadcasted_iota(jnp.int32, (LANES, LANES), 0)
    lj = lax.broadcasted_iota(jnp.int32, (LANES, LANES), 1)
    head_mean = jnp.where((li < HEAD_DIM) == (lj < HEAD_DIM), 1.0 / HEAD_DIM, 0.0).astype(BF16)

    kall = jnp.concatenate([kp_ref[0], kc_ref[0]], axis=0).astype(F32)
    kn = _pair_norm(kall, kg_ref[...], head_mean)
    kroll = pltpu.roll(kn, HEAD_DIM, axis=1)
    vall = jnp.concatenate([vp_ref[0], vc_ref[0]], axis=0).astype(F32)
    vroll = pltpu.roll(vall, HEAD_DIM, axis=1)
    zero = jnp.zeros_like(kn)
    one = jnp.ones_like(vall)
    k_var = [[jnp.where(lo, kn, zero).astype(BF16), jnp.where(lo, zero, kroll).astype(BF16)],
             [jnp.where(lo, kroll, zero).astype(BF16), jnp.where(lo, zero, kn).astype(BF16)]]
    v_var = [[jnp.where(lo, vall, one).astype(BF16), jnp.where(lo, one, vroll).astype(BF16)],
             [jnp.where(lo, vroll, one).astype(BF16), jnp.where(lo, one, vall).astype(BF16)]]

    qn = []
    for i in range(ATTN_WIDTH // LANES):
        t = _pair_norm(q_ref[0, :, i * LANES:(i + 1) * LANES].astype(F32), qg_ref[...], head_mean)
        qn.append((t * (HEAD_DIM ** -0.5)).astype(BF16))

    qi = lax.broadcasted_iota(jnp.int32, (BLOCK, BLOCK), 0)
    si = lax.broadcasted_iota(jnp.int32, (BLOCK, BLOCK), 1)
    is_cur = si <= qi
    dist = jnp.where(is_cur, qi - si, qi - si + BLOCK).astype(F32)
    alibi = [(2.0 ** (-8.0 * (h + 1) / N_HEADS)) * dist for h in range(N_HEADS)]
    sinkmat = jnp.concatenate([jnp.full((BLOCK, LANES), sink_ref[h], F32) for h in range(N_HEADS)],
                              axis=0)

    for b in range(ATTN_QB):
        rows = slice(b * BLOCK, (b + 1) * BLOCK)
        win = slice(b * BLOCK, (b + 2) * BLOCK)
        s_all = []
        for h in range(N_HEADS):
            i, slot = divmod(h, HEADS_PER_TILE)
            kvh = h // (N_HEADS // N_KV_HEADS)
            sc = lax.dot_general(qn[i][rows], k_var[kvh][slot][win], (((1,), (1,)), ((), ())),
                                 preferred_element_type=F32)
            prev = sc[:, :BLOCK]
            if b == 0:
                prev = jnp.where(n > 0, prev, NEG)
            s_all.append(jnp.where(is_cur, sc[:, BLOCK:], prev) - alibi[h])
        s = jnp.concatenate(s_all, axis=0)
        m = jnp.maximum(jnp.max(s, axis=-1, keepdims=True), sinkmat)
        p = jnp.exp(s - m).astype(BF16)
        esink = jnp.exp(sinkmat - m)
        for i in range(ATTN_WIDTH // LANES):
            halves = []
            for slot in range(HEADS_PER_TILE):
                h = i * HEADS_PER_TILE + slot
                kvh = h // (N_HEADS // N_KV_HEADS)
                ph = p[h * BLOCK:(h + 1) * BLOCK]
                pcat = jnp.concatenate([jnp.where(is_cur, 0.0, ph).astype(BF16),
                                        jnp.where(is_cur, ph, 0.0).astype(BF16)], axis=1)
                pv = jnp.dot(pcat, v_var[kvh][slot][win], preferred_element_type=F32)
                denom = pltpu.roll(pv, HEAD_DIM, axis=1) + esink[h * BLOCK:(h + 1) * BLOCK]
                halves.append(pv / denom)
            o_ref[0, rows, i * LANES:(i + 1) * LANES] = jnp.where(lo, halves[0],
                                                                  halves[1]).astype(BF16)


def _attn(q3, k3, v3, q_gain, k_gain, sinks):
    bsz, seq, _ = q3.shape
    qrows = ATTN_QB * BLOCK
    cur = lambda b, n: (b, n, 0)
    prev = lambda b, n: (b, jnp.maximum(n * ATTN_QB - 1, 0), 0)
    qg2 = jnp.tile(q_gain.astype(F32), HEADS_PER_TILE).reshape(1, LANES)
    kg2 = jnp.tile(k_gain.astype(F32), HEADS_PER_TILE).reshape(1, LANES)
    gain_spec = pl.BlockSpec((1, LANES), lambda b, n: (0, 0))
    return pl.pallas_call(
        _attn_body,
        grid=(bsz, seq // qrows),
        in_specs=[pl.BlockSpec(memory_space=pltpu.SMEM),
                  pl.BlockSpec((1, qrows, ATTN_WIDTH), cur),
                  pl.BlockSpec((1, qrows, KV_WIDTH), cur),
                  pl.BlockSpec((1, BLOCK, KV_WIDTH), prev),
                  pl.BlockSpec((1, qrows, KV_WIDTH), cur),
                  pl.BlockSpec((1, BLOCK, KV_WIDTH), prev),
                  gain_spec, gain_spec],
        out_specs=pl.BlockSpec((1, qrows, ATTN_WIDTH), cur),
        out_shape=jax.ShapeDtypeStruct((bsz, seq, ATTN_WIDTH), BF16),
        compiler_params=_cparams(("arbitrary", "arbitrary")),
        name="attn",
    )(sinks.astype(F32), q3, k3, k3, v3, v3, qg2, kg2)


MIX_TM = 1024


def _mix_body(z_ref, ya_ref, gs_ref, ga_ref, x_ref, wglu_ref, wup_ref, wout_ref, g2_ref, wr_ref,
              x1_ref, gate_ref, hs_hbm, starts_ref, lens_ref, counts_ref,
              stage_ref, run_ref, pend_ref, sem):
    t = pl.program_id(0)
    sorter = _GroupSorter(t, MIX_TM // SORT_TM, hs_hbm, starts_ref, lens_ref, counts_ref,
                          stage_ref, run_ref, pend_ref, sem)

    z = z_ref[...]
    bv = jnp.dot(z, wglu_ref[:, :D_MODEL], preferred_element_type=F32)
    bg = jnp.dot(z, wglu_ref[:, D_MODEL:], preferred_element_type=F32)
    bs = bv * jax.nn.sigmoid(bg)
    ba = jnp.dot(ya_ref[...], wup_ref[...], preferred_element_type=F32)
    merged = gs_ref[...].astype(F32) * bs + ga_ref[...].astype(F32) * ba
    x1 = x_ref[...] + jnp.dot(merged.astype(BF16), wout_ref[...], preferred_element_type=F32)
    x1_ref[...] = x1

    ms = jnp.mean(x1 * x1, axis=-1, keepdims=True)
    h2 = x1 * lax.rsqrt(ms + EPS) * g2_ref[...]
    h2b = h2.astype(BF16)

    lt = lax.dot_general(wr_ref[...], h2b, (((1,), (1,)), ((), ())),
                         preferred_element_type=F32)
    logits = lt[:ROUTER_ROWS] + lt[ROUTER_ROWS:]
    row = lax.broadcasted_iota(jnp.int32, logits.shape, 0)
    ninf = -jnp.inf
    cm = jnp.where(row < N_GROUPS, logits, ninf)
    cmax = jnp.max(cm, axis=0, keepdims=True)
    g_prob = 1.0 / jnp.sum(jnp.exp(cm - cmax), axis=0, keepdims=True)
    g_idx = jnp.min(jnp.where(cm == cmax, row, ROUTER_ROWS), axis=0, keepdims=True)
    f0 = FINE_OFF + EXPERTS_PER_GROUP * g_idx
    fm = jnp.where((row >= f0) & (row < f0 + EXPERTS_PER_GROUP), logits, ninf)
    v1 = jnp.max(fm, axis=0, keepdims=True)
    i1 = jnp.min(jnp.where(fm == v1, row, ROUTER_ROWS), axis=0, keepdims=True)
    fm2 = jnp.where(row == i1, ninf, fm)
    v2 = jnp.max(fm2, axis=0, keepdims=True)
    i2 = jnp.min(jnp.where(fm2 == v2, row, ROUTER_ROWS), axis=0, keepdims=True)
    e21 = jnp.exp(v2 - v1)
    w1 = g_prob / (1.0 + e21)
    w2 = w1 * e21
    rt = jnp.where(row == i1 - f0, w1,
                   jnp.where(row == i2 - f0, w2,
                             jnp.where(row == ROUTE_GROUP_OFF + g_idx, 1.0, 0.0)))
    rt = jnp.concatenate([rt, jnp.zeros((ROUTER_LANES - ROUTER_ROWS, rt.shape[1]), F32)], axis=0)
    route = rt.T
    gate_ref[...] = route
    sorter.begin()
    sorter.issue(sorter.stage(h2b, route))

    @pl.when(t == pl.num_programs(0) - 1)
    def _():
        sorter.finish()


def _mix(z2, ya2, gs2, ga2, x2, wglu, wup, wout, g2, wr, layer):
    n = x2.shape[0]
    n_sort = n // SORT_TM
    subs = MIX_TM // SORT_TM
    cap = _region_cap(n)
    assert subs * SORT_TM >= ZFILL_TOK
    row = lambda c: pl.BlockSpec((MIX_TM, c), lambda i: (i, 0))
    full = lambda a, b: pl.BlockSpec((a, b), lambda i: (0, 0))
    smem = pl.BlockSpec(memory_space=pltpu.SMEM)
    return pl.pallas_call(
        _mix_body,
        grid=(n // MIX_TM,),
        in_specs=[row(SSM_WIDTH), row(ATTN_WIDTH), row(D_MODEL), row(D_MODEL), row(D_MODEL),
                  _layer_spec(layer, SSM_WIDTH, 2 * D_MODEL), _layer_spec(layer, ATTN_WIDTH, D_MODEL),
                  _layer_spec(layer, D_MODEL, D_MODEL), full(1, D_MODEL),
                  _layer_spec(layer, 2 * ROUTER_ROWS, D_MODEL)],
        out_specs=[row(D_MODEL), row(ROUTER_LANES), pl.BlockSpec(memory_space=pl.ANY),
                   smem, smem, smem],
        out_shape=[jax.ShapeDtypeStruct((n, D_MODEL), F32),
                   jax.ShapeDtypeStruct((n, ROUTER_LANES), F32),
                   jax.ShapeDtypeStruct((N_GROUPS * cap * TOK_ROWS, LANES), jnp.uint32),
                   jax.ShapeDtypeStruct((n_sort * N_GROUPS,), jnp.int32),
                   jax.ShapeDtypeStruct((n_sort * N_GROUPS,), jnp.int32),
                   jax.ShapeDtypeStruct((N_GROUPS,), jnp.int32)],
        scratch_shapes=[pltpu.VMEM((subs * SORT_TM * TOK_ROWS, LANES), jnp.uint32),
                        pltpu.SMEM((N_GROUPS,), jnp.int32),
                        pltpu.SMEM((subs * N_GROUPS,), jnp.int32),
                        pltpu.SemaphoreType.DMA((subs, N_GROUPS))],
        compiler_params=_cparams(("arbitrary",)),
        name="mix",
    )(z2, ya2, gs2, ga2, x2, wglu, wup, wout, g2, wr)


SORT_TM = 256
COMBINE_TM = 1024
MOE_TM = 512
TOK_ROWS = 8
H_WORDS = D_MODEL // 2
H_CHUNKS = H_WORDS // LANES
ROUTE_ROW = H_CHUNKS
ZFILL_TOK = MOE_TM
HI_MASK = 0xFFFF0000


def _region_cap(n):
    cap = n + ZFILL_TOK
    return -(-cap // MOE_TM) * MOE_TM


def _tile_positions(route):
    tm = route.shape[0]
    lane = lax.broadcasted_iota(jnp.int32, route.shape, 1)
    onehot = jnp.where((lane >= ROUTE_GROUP_OFF) & (lane < ROUTE_GROUP_OFF + N_GROUPS), route, 0.0)
    ci = lax.broadcasted_iota(jnp.int32, (tm, tm), 0)
    cj = lax.broadcasted_iota(jnp.int32, (tm, tm), 1)
    earlier = (cj < ci).astype(BF16)
    rank = jnp.dot(earlier, onehot.astype(BF16), preferred_element_type=F32)
    cnt = jnp.sum(onehot, axis=0, keepdims=True)
    li = lax.broadcasted_iota(jnp.int32, (LANES, LANES), 0)
    lj = lax.broadcasted_iota(jnp.int32, (LANES, LANES), 1)
    base = jnp.dot(jnp.broadcast_to(cnt, (8, LANES)).astype(BF16), (li < lj).astype(BF16),
                   preferred_element_type=F32)[0:1]
    return onehot, rank, base, cnt


class _GroupSorter:
    def __init__(self, t, subs, hs_hbm, starts_ref, lens_ref, counts_ref, stage_ref, run_ref,
                 pend_ref, sem):
        self.t, self.subs = t, subs
        self.hs_hbm, self.starts_ref, self.lens_ref, self.counts_ref = (
            hs_hbm, starts_ref, lens_ref, counts_ref)
        self.stage_ref, self.run_ref, self.pend_ref, self.sem = stage_ref, run_ref, pend_ref, sem
        self.slot_rows = SORT_TM * TOK_ROWS
        self.cap = hs_hbm.shape[0] // (N_GROUPS * TOK_ROWS)
        self.set0 = 0

    def _copy(self, src_row, dst_row, slot, g, tokens):
        return pltpu.make_async_copy(
            self.stage_ref.at[pl.ds(src_row, tokens * TOK_ROWS), :],
            self.hs_hbm.at[pl.ds(dst_row, tokens * TOK_ROWS), :], self.sem.at[slot, g])

    def _wait_set(self, set0):
        for sl in range(self.subs):
            for g in range(N_GROUPS):
                n = self.pend_ref[(set0 + sl) * N_GROUPS + g]

                @pl.when(n > 0)
                def _(sl=sl, g=g, n=n):
                    self._copy(0, 0, set0 + sl, g, n).wait()

    def begin(self):
        @pl.when(self.t == 0)
        def _():
            self.stage_ref[...] = jnp.zeros_like(self.stage_ref)
            for g in range(N_GROUPS):
                self.run_ref[g] = 0

        @pl.when(self.t > 0)
        def _():
            self._wait_set(self.set0)

    def stage(self, h, route):
        tm = SORT_TM
        runs = []
        for sl in range(self.subs):
            r = route[sl * tm:(sl + 1) * tm]
            onehot, rank, base, cnt = _tile_positions(r)
            w = (onehot * (rank + base)).astype(BF16)
            pos_row = lax.dot_general(jnp.ones((8, LANES), BF16), w, (((1,), (1,)), ((), ())),
                                      preferred_element_type=F32)[0:1]
            ri = lax.broadcasted_iota(jnp.int32, (tm, tm), 0).astype(F32)
            perm = (ri == pos_row).astype(BF16)
            r_hi = r.astype(BF16)
            r_mid = (r - r_hi.astype(F32)).astype(BF16)
            r_lo = (r - r_hi.astype(F32) - r_mid.astype(F32)).astype(BF16)
            sorted_all = jnp.dot(
                perm,
                jnp.concatenate([h[sl * tm:(sl + 1) * tm], r_hi, r_mid, r_lo], axis=1),
                preferred_element_type=F32)
            sh = sorted_all[:, :D_MODEL]
            sr = (sorted_all[:, D_MODEL:D_MODEL + LANES]
                  + sorted_all[:, D_MODEL + LANES:D_MODEL + 2 * LANES]
                  + sorted_all[:, D_MODEL + 2 * LANES:])
            hb = lax.bitcast_convert_type(sh, jnp.uint32)
            words = (hb[:, :H_WORDS] & jnp.uint32(HI_MASK)) | (hb[:, H_WORDS:] >> 16)

            slot0 = (self.set0 + sl) * self.slot_rows
            for s in range(H_CHUNKS):
                self.stage_ref[pl.ds(slot0 + s, tm, stride=TOK_ROWS), :] = (
                    words[:, s * LANES:(s + 1) * LANES])
            self.stage_ref[pl.ds(slot0 + ROUTE_ROW, tm, stride=TOK_ROWS), :] = (
                lax.bitcast_convert_type(sr, jnp.uint32))
            for g in range(N_GROUPS):
                runs.append((sl, g, slot0,
                             cnt[0, ROUTE_GROUP_OFF + g].astype(jnp.int32),
                             base[0, ROUTE_GROUP_OFF + g].astype(jnp.int32)))
        return runs

    def issue(self, runs):
        first_tile = self.t * self.subs
        for sl, g, slot0, c_g, b_g in runs:
            start = self.run_ref[g]
            self.starts_ref[(first_tile + sl) * N_GROUPS + g] = start
            self.lens_ref[(first_tile + sl) * N_GROUPS + g] = c_g
            self.pend_ref[(self.set0 + sl) * N_GROUPS + g] = c_g
            self.run_ref[g] = start + c_g
            src = pl.multiple_of(slot0 + b_g * TOK_ROWS, TOK_ROWS)
            dst = pl.multiple_of((g * self.cap + start) * TOK_ROWS, TOK_ROWS)

            @pl.when(c_g > 0)
            def _(sl=sl, g=g, src=src, dst=dst, c_g=c_g):
                self._copy(src, dst, self.set0 + sl, g, c_g).start()

    def finish(self):
        self._wait_set(self.set0)
        self.stage_ref[...] = jnp.zeros_like(self.stage_ref)
        zrows = ZFILL_TOK * TOK_ROWS
        for g in range(N_GROUPS):
            total = self.run_ref[g]
            self.counts_ref[g] = total
            dst = pl.multiple_of((g * self.cap + total) * TOK_ROWS, TOK_ROWS)
            pltpu.make_async_copy(self.stage_ref.at[pl.ds(0, zrows), :],
                                  self.hs_hbm.at[pl.ds(dst, zrows), :], self.sem.at[0, g]).start()
        for g in range(N_GROUPS):
            pltpu.make_async_copy(self.stage_ref.at[pl.ds(0, zrows), :],
                                  self.hs_hbm.at[pl.ds(0, zrows), :], self.sem.at[0, g]).wait()


def _experts_body(blk_ref, grp_ref, valid_ref, hs_ref, wu32_ref, wd32_ref, ys_ref, wu_ref, wd_ref):
    t = pl.program_id(0)
    tm = MOE_TM

    @pl.when((t == 0) | (grp_ref[t] != grp_ref[jnp.maximum(t - 1, 0)]))
    def _():
        wu_ref[...] = wu32_ref[...].astype(BF16)
        wd_ref[...] = wd32_ref[...].astype(BF16)

    @pl.when(valid_ref[t] > 0)
    def _():
        chunks = [hs_ref[pl.ds(s, tm, stride=TOK_ROWS), :] for s in range(H_CHUNKS + 1)]
        hi = [lax.bitcast_convert_type(c & jnp.uint32(HI_MASK), F32) for c in chunks[:H_CHUNKS]]
        lo = [lax.bitcast_convert_type(c << 16, F32) for c in chunks[:H_CHUNKS]]
        h = jnp.concatenate(hi + lo, axis=1).astype(BF16)
        route = lax.bitcast_convert_type(chunks[ROUTE_ROW], F32)
        acts = []
        for e in range(EXPERTS_PER_GROUP):
            gu = jnp.dot(h, wu_ref[e], preferred_element_type=F32)
            acts.append((jax.nn.silu(gu[:, :D_EXPERT]) * gu[:, D_EXPERT:]
                         * route[:, e:e + 1]).astype(BF16))
        act = jnp.concatenate(acts, axis=1)
        wd = wd_ref[...].reshape(EXPERTS_PER_GROUP * D_EXPERT, D_MODEL)
        out = jnp.dot(act, wd, preferred_element_type=F32)
        for j in range(D_MODEL // LANES):
            ys_ref[pl.ds(j, tm, stride=TOK_ROWS), :] = out[:, j * LANES:(j + 1) * LANES]


def _experts(blk, grp, valid, hs, wu, wd, layer):
    g0 = layer * N_GROUPS
    n_tiles = blk.shape[0]
    epg = EXPERTS_PER_GROUP
    return pl.pallas_call(
        _experts_body,
        grid_spec=pltpu.PrefetchScalarGridSpec(
            num_scalar_prefetch=3,
            grid=(n_tiles,),
            in_specs=[pl.BlockSpec((MOE_TM * TOK_ROWS, LANES), lambda t, b, g, v: (b[t], 0)),
                      pl.BlockSpec((epg, D_MODEL, 2 * D_EXPERT),
                                   lambda t, b, g, v: (g0 + g[t], 0, 0)),
                      pl.BlockSpec((epg, D_EXPERT, D_MODEL),
                                   lambda t, b, g, v: (g0 + g[t], 0, 0))],
            out_specs=pl.BlockSpec((MOE_TM * TOK_ROWS, LANES), lambda t, b, g, v: (b[t], 0)),
            scratch_shapes=[pltpu.VMEM((epg, D_MODEL, 2 * D_EXPERT), BF16),
                            pltpu.VMEM((epg, D_EXPERT, D_MODEL), BF16)],
        ),
        out_shape=jax.ShapeDtypeStruct(hs.shape, F32),
        compiler_params=_cparams(("arbitrary",)),
        name="moe_experts",
    )(blk, grp, valid, hs, wu, wd)


def _moe_outputs(starts_ref, lens_ref, route_ref, ys_hbm, buf_ref, sem):
    t = pl.program_id(0)
    nt = pl.num_programs(0)
    tm = SORT_TM
    subs = route_ref.shape[0] // tm
    slot_rows = tm * TOK_ROWS
    cap = ys_hbm.shape[0] // (N_GROUPS * TOK_ROWS)

    def fetches(step, sl):
        tile = step * subs + sl
        slot = (step % 2) * subs + sl
        copies = []
        first = 0
        for g in range(N_GROUPS):
            tokens = lens_ref[tile * N_GROUPS + g]
            src = pl.multiple_of((g * cap + starts_ref[tile * N_GROUPS + g]) * TOK_ROWS, TOK_ROWS)
            dst = pl.multiple_of(slot * slot_rows + first * TOK_ROWS, TOK_ROWS)
            copies.append((tokens, pltpu.make_async_copy(
                ys_hbm.at[pl.ds(src, tokens * TOK_ROWS), :],
                buf_ref.at[pl.ds(dst, tokens * TOK_ROWS), :], sem.at[slot, g])))
            first = first + tokens
        return copies

    def start_fetches(step):
        for sl in range(subs):
            for tokens, copy in fetches(step, sl):
                @pl.when(tokens > 0)
                def _(copy=copy):
                    copy.start()

    @pl.when(t == 0)
    def _():
        start_fetches(0)

    @pl.when(t + 1 < nt)
    def _():
        start_fetches(t + 1)

    for sl in range(subs):
        for tokens, copy in fetches(t, sl):
            @pl.when(tokens > 0)
            def _(copy=copy):
                copy.wait()

    outs = []
    for sl in range(subs):
        onehot, rank, base, _ = _tile_positions(route_ref[sl * tm:(sl + 1) * tm, :])
        pos = jnp.sum(onehot * (rank + base), axis=1, keepdims=True)
        ri = lax.broadcasted_iota(jnp.int32, (tm, tm), 1).astype(F32)
        sel = (ri == pos).astype(BF16)
        r0 = pl.multiple_of(((t % 2) * subs + sl) * slot_rows, TOK_ROWS)
        y = jnp.concatenate(
            [buf_ref[pl.ds(r0 + j, tm, stride=TOK_ROWS), :] for j in range(D_MODEL // LANES)],
            axis=1).astype(BF16)
        outs.append(jnp.dot(sel, y, preferred_element_type=F32))
    return outs


def _combine_body(starts_ref, lens_ref, x1_ref, route_ref, ys_hbm, o_ref, buf_ref, sem):
    outs = _moe_outputs(starts_ref, lens_ref, route_ref, ys_hbm, buf_ref, sem)
    for sl, y in enumerate(outs):
        rows = slice(sl * SORT_TM, (sl + 1) * SORT_TM)
        o_ref[rows, :] = x1_ref[rows, :] + y


def _combine(starts, lens, x1, route, ys):
    n = x1.shape[0]
    subs = COMBINE_TM // SORT_TM
    return pl.pallas_call(
        _combine_body,
        grid_spec=pltpu.PrefetchScalarGridSpec(
            num_scalar_prefetch=2,
            grid=(n // COMBINE_TM,),
            in_specs=[pl.BlockSpec((COMBINE_TM, D_MODEL), lambda i, s, c: (i, 0)),
                      pl.BlockSpec((COMBINE_TM, ROUTER_LANES), lambda i, s, c: (i, 0)),
                      pl.BlockSpec(memory_space=pl.ANY)],
            out_specs=pl.BlockSpec((COMBINE_TM, D_MODEL), lambda i, s, c: (i, 0)),
            scratch_shapes=[pltpu.VMEM((2 * COMBINE_TM * TOK_ROWS, LANES), F32),
                            pltpu.SemaphoreType.DMA((2 * subs, N_GROUPS))],
        ),
        out_shape=jax.ShapeDtypeStruct((n, D_MODEL), F32),
        compiler_params=_cparams(("arbitrary",)),
        name="moe_combine",
    )(starts, lens, x1, route, ys)


def _expert_tiles(counts, n):
    cap_blocks = _region_cap(n) // MOE_TM
    n_tiles = n // MOE_TM + N_GROUPS
    per_group = (counts + MOE_TM - 1) // MOE_TM
    ends = jnp.cumsum(per_group)
    total = ends[-1]
    t = jnp.minimum(jnp.arange(n_tiles, dtype=jnp.int32), total - 1)
    grp = jnp.sum((t[:, None] >= ends[None, :]).astype(jnp.int32), axis=1)
    first = ends - per_group
    blk = grp * cap_blocks + (t - first[grp])
    valid = (jnp.arange(n_tiles, dtype=jnp.int32) < total).astype(jnp.int32)
    return blk.astype(jnp.int32), grp.astype(jnp.int32), valid


def _moe_sorted(n, hs, counts, w_up, w_down, layer):
    blk, grp, valid = _expert_tiles(counts, n)
    return _experts(blk, grp, valid, hs, w_up, w_down, layer)


def kernel(x, norm_mix, w_in, ssm_lam_re, ssm_lam_im, ssm_log_dt, ssm_b_re, ssm_b_im, ssm_c_re,
           ssm_c_im, ssm_d, w_ssm_glu_val, w_ssm_glu_gate, q_norm, k_norm, attn_sinks, w_attn_up,
           w_out, norm_ffn, w_coarse, w_fine, w_expert_up, w_expert_down):
    bsz, seq, d = x.shape
    n = bsz * seq
    depth = w_in.shape[0]
    x2 = x.reshape(n, d)
    s5p = jax.vmap(_s5_params)(ssm_lam_re, ssm_lam_im, ssm_log_dt, ssm_b_re, ssm_b_im, ssm_c_re,
                               ssm_c_im, ssm_d)
    w_in_b = w_in.astype(BF16)
    wglu = jnp.concatenate([w_ssm_glu_val, w_ssm_glu_gate], axis=2).astype(BF16)
    wup_b = w_attn_up.astype(BF16)
    wout_b = w_out.astype(BF16)
    wr_t = jnp.concatenate(
        [jnp.swapaxes(w_coarse, 1, 2), jnp.swapaxes(w_fine, 1, 2),
         jnp.zeros((depth, ROUTER_ROWS - N_GROUPS - N_EXPERTS, d), w_fine.dtype)], axis=1).astype(F32)
    wr_hi = wr_t.astype(BF16)
    wr = jnp.concatenate([wr_hi, (wr_t - wr_hi.astype(F32)).astype(BF16)], axis=1)
    wu_all = w_expert_up.reshape((depth * N_EXPERTS,) + w_expert_up.shape[2:])
    wd_all = w_expert_down.reshape((depth * N_EXPERTS,) + w_expert_down.shape[2:])
    pending = None
    for i in range(depth):
        if pending is None:
            u, q, k, v, gs, ga = _proj(x2, norm_mix[i].reshape(1, d), w_in_b, i)
        else:
            x2, u, q, k, v, gs, ga = _proj_moe(*pending, norm_mix[i].reshape(1, d), w_in_b, i)
        z = _s5(u.reshape(bsz, seq, SSM_WIDTH), *(p[i] for p in s5p))
        ya = _attn(q.reshape(bsz, seq, ATTN_WIDTH), k.reshape(bsz, seq, KV_WIDTH),
                   v.reshape(bsz, seq, KV_WIDTH), q_norm[i], k_norm[i], attn_sinks[i])
        x1, route, hs, starts, lens, counts = _mix(
            z.reshape(n, SSM_WIDTH), ya.reshape(n, ATTN_WIDTH), gs, ga, x2,
            wglu, wup_b, wout_b, norm_ffn[i].reshape(1, d), wr, i)
        ys = _moe_sorted(n, hs, counts, wu_all, wd_all, i)
        pending = (starts, lens, x1, route, ys)
    return _combine(*pending).reshape(bsz, seq, d)
```

```python
import functools
import math

import jax
import jax.numpy as jnp
from jax import lax
from jax.experimental import pallas as pl
from jax.experimental.pallas import tpu as pltpu

F32 = jnp.float32
BF16 = jnp.bfloat16

D_MODEL = 1024
SSM_WIDTH = 512
SSM_GROUP = 16
SSM_GROUPS = 32
SSM_STATE = 64
N_HEADS = 8
N_KV_HEADS = 2
HEAD_DIM = 64
ATTN_WIDTH = 512
KV_WIDTH = 128
BLOCK = 128
N_GROUPS = 4
EXPERTS_PER_GROUP = 4
N_EXPERTS = 16
D_EXPERT = 256
EPS = 1e-6
NEG = -1e30

LANES = 128
SLABS = SSM_WIDTH // LANES
GROUPS_PER_SLAB = LANES // SSM_GROUP
SLAB_STATE = GROUPS_PER_SLAB * SSM_STATE
ROUTER_LANES = 128
ROUTER_ROWS = 32
FINE_OFF = N_GROUPS
ROUTE_GROUP_OFF = 8

VMEM_LIMIT = 56 * 1024 * 1024


def _cparams(sem):
    return pltpu.CompilerParams(dimension_semantics=sem, vmem_limit_bytes=VMEM_LIMIT)


PROJ_TM = 1024


def _proj_body(x_ref, g_ref, w_ref, *out_refs):
    _project(x_ref[...], g_ref, w_ref, *out_refs)


def _proj_moe_body(starts_ref, lens_ref, x1_ref, route_ref, ys_hbm, g_ref, w_ref,
                   x_ref, *rest):
    out_refs, (buf_ref, sem) = rest[:-2], rest[-2:]
    moe = _moe_outputs(starts_ref, lens_ref, route_ref, ys_hbm, buf_ref, sem)
    x = x1_ref[...] + jnp.concatenate(moe, axis=0)
    x_ref[...] = x
    _project(x, g_ref, w_ref, *out_refs)


def _project(x, g_ref, w_ref, u_ref, q_ref, k_ref, v_ref, gs_ref, ga_ref):
    ms = jnp.mean(x * x, axis=-1, keepdims=True)
    h = (x * lax.rsqrt(ms + EPS) * g_ref[...]).astype(BF16)

    def seg(a, b):
        return jnp.dot(h, w_ref[:, a:b], preferred_element_type=F32)

    o = 0
    u_ref[...] = seg(o, o + SSM_WIDTH)
    o += SSM_WIDTH
    q_ref[...] = seg(o, o + ATTN_WIDTH).astype(BF16)
    o += ATTN_WIDTH
    kv = seg(o, o + 2 * KV_WIDTH).astype(BF16)
    k_ref[...] = kv[:, :KV_WIDTH]
    v_ref[...] = kv[:, KV_WIDTH:]
    o += 2 * KV_WIDTH
    gs_ref[...] = jax.nn.sigmoid(seg(o, o + D_MODEL)).astype(BF16)
    o += D_MODEL
    ga_ref[...] = jax.nn.sigmoid(seg(o, o + D_MODEL)).astype(BF16)


def _layer_spec(layer, *shape):
    return pl.BlockSpec((None,) + shape, lambda *_: (layer,) + (0,) * len(shape))


def _proj_out(n):
    row = lambda c: pl.BlockSpec((PROJ_TM, c), lambda i, *_: (i, 0))
    specs = [row(SSM_WIDTH), row(ATTN_WIDTH), row(KV_WIDTH), row(KV_WIDTH), row(D_MODEL),
             row(D_MODEL)]
    shapes = [jax.ShapeDtypeStruct((n, SSM_WIDTH), F32),
              jax.ShapeDtypeStruct((n, ATTN_WIDTH), BF16),
              jax.ShapeDtypeStruct((n, KV_WIDTH), BF16),
              jax.ShapeDtypeStruct((n, KV_WIDTH), BF16),
              jax.ShapeDtypeStruct((n, D_MODEL), BF16),
              jax.ShapeDtypeStruct((n, D_MODEL), BF16)]
    return specs, shapes


def _proj(x2, g, w, layer):
    n = x2.shape[0]
    cols = w.shape[2]
    row = lambda c: pl.BlockSpec((PROJ_TM, c), lambda i: (i, 0))
    full = lambda a, b: pl.BlockSpec((a, b), lambda i: (0, 0))
    out_specs, out_shape = _proj_out(n)
    return pl.pallas_call(
        _proj_body,
        grid=(n // PROJ_TM,),
        in_specs=[row(D_MODEL), full(1, D_MODEL), _layer_spec(layer, D_MODEL, cols)],
        out_specs=out_specs,
        out_shape=out_shape,
        compiler_params=_cparams(("arbitrary",)),
        name="proj",
    )(x2, g, w)


def _proj_moe(starts, lens, x1, route, ys, g, w, layer):
    n = x1.shape[0]
    cols = w.shape[2]
    subs = PROJ_TM // SORT_TM
    row = lambda c: pl.BlockSpec((PROJ_TM, c), lambda i, *_: (i, 0))
    out_specs, out_shape = _proj_out(n)
    return pl.pallas_call(
        _proj_moe_body,
        grid_spec=pltpu.PrefetchScalarGridSpec(
            num_scalar_prefetch=2,
            grid=(n // PROJ_TM,),
            in_specs=[row(D_MODEL), row(ROUTER_LANES), pl.BlockSpec(memory_space=pl.ANY),
                      pl.BlockSpec((1, D_MODEL), lambda i, *_: (0, 0)),
                      _layer_spec(layer, D_MODEL, cols)],
            out_specs=[row(D_MODEL)] + out_specs,
            scratch_shapes=[pltpu.VMEM((2 * PROJ_TM * TOK_ROWS, LANES), F32),
                            pltpu.SemaphoreType.DMA((2 * subs, N_GROUPS))],
        ),
        out_shape=[jax.ShapeDtypeStruct((n, D_MODEL), F32)] + out_shape,
        compiler_params=_cparams(("arbitrary",)),
        name="proj_moe",
    )(starts, lens, x1, route, ys, g, w)


S5_TT = 64
S5_PAIRS = S5_TT // 2


def _s5_body(u_ref, bblk_ref, cblk_ref, k0_ref, lam_ref, d_ref, z_ref,
             ut_ref, y_ref, st_ref, carry_ref, *, bsz):
    rows = S5_TT * bsz
    prow = S5_PAIRS * bsz

    @pl.when(pl.program_id(0) == 0)
    def _():
        carry_ref[...] = jnp.zeros_like(carry_ref)

    for b in range(bsz):
        for j in range(SLABS):
            ut_ref[j, pl.ds(b, S5_TT, stride=bsz), :] = u_ref[b, :, j * LANES:(j + 1) * LANES]

    for j in range(SLABS):
        u3 = ut_ref[j].reshape(S5_PAIRS, 2 * bsz, LANES)
        u0 = u3[:, :bsz, :].reshape(prow, LANES)
        u1 = u3[:, bsz:, :].reshape(prow, LANES)
        lhs = jnp.concatenate([u1, u0], axis=1).astype(BF16)
        st_ref[j, 0:bsz, :] = carry_ref[j]
        st_ref[j, bsz:, :] = jnp.dot(lhs, bblk_ref[j], preferred_element_type=F32)
        a = jnp.broadcast_to(lam_ref[j, 0:1, :], (bsz, SLAB_STATE))
        bb = jnp.broadcast_to(lam_ref[j, 1:2, :], (bsz, SLAB_STATE))

        cre = carry_ref[j, :, 0:SLAB_STATE]
        cim = carry_ref[j, :, SLAB_STATE:2 * SLAB_STATE]
        for k in range(S5_PAIRS):
            r0 = (k + 1) * bsz
            bre = st_ref[j, r0:r0 + bsz, 0:SLAB_STATE]
            bim = st_ref[j, r0:r0 + bsz, SLAB_STATE:2 * SLAB_STATE]
            cre, cim = a * cre - bb * cim + bre, a * cim + bb * cre + bim
            st_ref[j, r0:r0 + bsz, 0:SLAB_STATE] = cre
            st_ref[j, r0:r0 + bsz, SLAB_STATE:2 * SLAB_STATE] = cim
        carry_ref[j, :, 0:SLAB_STATE] = cre
        carry_ref[j, :, SLAB_STATE:2 * SLAB_STATE] = cim

        cs = jnp.dot(st_ref[j].astype(BF16), cblk_ref[j], preferred_element_type=F32)
        y1 = cs[bsz:, :LANES] + d_ref[j] * u1
        y0 = (cs[:prow, LANES:] + jnp.dot(u0.astype(BF16), k0_ref[j], preferred_element_type=F32)
              + d_ref[j] * u0)
        y = jnp.concatenate([y0.reshape(S5_PAIRS, bsz, LANES), y1.reshape(S5_PAIRS, bsz, LANES)],
                            axis=1).reshape(rows, LANES)
        y_ref[j] = jax.nn.gelu(y)

    for b in range(bsz):
        for j in range(SLABS):
            z_ref[b, :, j * LANES:(j + 1) * LANES] = (
                y_ref[j, pl.ds(b, S5_TT, stride=bsz), :].astype(BF16))


def _s5(u3, bblk, cblk, k0, lam, dskip):
    bsz, seq, _ = u3.shape
    rows = S5_TT * bsz
    full = lambda *s: pl.BlockSpec(s, lambda i: (0,) * len(s))
    return pl.pallas_call(
        functools.partial(_s5_body, bsz=bsz),
        grid=(seq // S5_TT,),
        in_specs=[pl.BlockSpec((bsz, S5_TT, SSM_WIDTH), lambda i: (0, i, 0)),
                  full(SLABS, 2 * LANES, 2 * SLAB_STATE),
                  full(SLABS, 2 * SLAB_STATE, 2 * LANES),
                  full(SLABS, LANES, LANES),
                  full(SLABS, 2, SLAB_STATE),
                  full(SLABS, 1, LANES)],
        out_specs=pl.BlockSpec((bsz, S5_TT, SSM_WIDTH), lambda i: (0, i, 0)),
        out_shape=jax.ShapeDtypeStruct((bsz, seq, SSM_WIDTH), BF16),
        scratch_shapes=[pltpu.VMEM((SLABS, rows, LANES), F32),
                        pltpu.VMEM((SLABS, rows, LANES), F32),
                        pltpu.VMEM((SLABS, bsz + S5_PAIRS * bsz, 2 * SLAB_STATE), F32),
                        pltpu.VMEM((SLABS, bsz, 2 * SLAB_STATE), F32)],
        compiler_params=_cparams(("arbitrary",)),
        name="s5",
    )(u3, bblk, cblk, k0, lam, dskip)


def _s5_params(lam_re, lam_im, log_dt, b_re, b_im, c_re, c_im, d_skip):
    lam = lax.complex(lam_re.astype(F32), lam_im.astype(F32))
    dt = jnp.exp(log_dt.astype(F32))[:, None]
    lam_bar = jnp.exp(lam * dt)
    b = lax.complex(b_re.astype(F32), b_im.astype(F32))
    b_bar = ((lam_bar - 1.0) / lam)[..., None] * b
    eye = jnp.eye(GROUPS_PER_SLAB, dtype=F32)

    def in_blk(part):
        p4 = part.reshape(SLABS, GROUPS_PER_SLAB, SSM_STATE, SSM_GROUP)
        m = jnp.einsum('jgph,gk->jghkp', p4, eye)
        return m.reshape(SLABS, LANES, SLAB_STATE)

    def in_cplx(z):
        return jnp.concatenate([in_blk(z.real), in_blk(z.imag)], axis=-1)

    bblk = jnp.concatenate([in_cplx(b_bar), in_cplx(lam_bar[..., None] * b_bar)],
                           axis=1).astype(BF16)

    def out_blk(part):
        p4 = part.reshape(SLABS, GROUPS_PER_SLAB, SSM_GROUP, SSM_STATE)
        m = jnp.einsum('jghp,gk->jgpkh', p4, eye)
        return m.reshape(SLABS, SLAB_STATE, LANES)

    def out_cplx(z):
        return jnp.concatenate([out_blk(z.real), -out_blk(z.imag)], axis=1)

    c = lax.complex(c_re.astype(F32), c_im.astype(F32))
    cblk = jnp.concatenate([out_cplx(c), out_cplx(c * lam_bar[:, None, :])], axis=-1).astype(BF16)
    k0 = jnp.einsum('ghp,gpk->gkh', c, b_bar).real.reshape(SLABS, GROUPS_PER_SLAB, SSM_GROUP,
                                                           SSM_GROUP)
    k0 = jnp.einsum('jgkh,gq->jgkqh', k0, eye).reshape(SLABS, LANES, LANES).astype(BF16)
    lam2 = lam_bar * lam_bar
    lam_k = jnp.stack([lam2.real.reshape(SLABS, SLAB_STATE),
                       lam2.imag.reshape(SLABS, SLAB_STATE)], axis=1)
    dsk = d_skip.astype(F32).reshape(SLABS, 1, LANES)
    return bblk, cblk, k0, lam_k, dsk


ATTN_QB = 8
ATTN_STACK = 16
HEADS_PER_TILE = LANES // HEAD_DIM


def _pair_norm(t, gain2, head_mean):
    ms = jnp.dot((t * t).astype(BF16), head_mean, preferred_element_type=F32)
    return t * lax.rsqrt(ms + EPS) * gain2


def _attn_body(sink_ref, q_ref, kc_ref, kp_ref, vc_ref, vp_ref, qg_ref, kg_ref, o_ref):
    n = pl.program_id(1)
    lane = lax.broadcasted_iota(jnp.int32, (1, LANES), 1)
    lo = lane < HEAD_DIM
    li = lax.broadcasted_iota(jnp.int32, (LANES, LANES), 0)
    lj = lax.broadcasted_iota(jnp.int32, (LANES, LANES), 1)
    head_mean = jnp.where((li < HEAD_DIM) == (lj < HEAD_DIM), 1.0 / HEAD_DIM, 0.0).astype(BF16)

    kall = jnp.concatenate([kp_ref[0], kc_ref[0]], axis=0).astype(F32)
    kn = _pair_norm(kall, kg_ref[...], head_mean)
    kroll = pltpu.roll(kn, HEAD_DIM, axis=1)
    vall = jnp.concatenate([vp_ref[0], vc_ref[0]], axis=0).astype(F32)
    vroll = pltpu.roll(vall, HEAD_DIM, axis=1)
    zero = jnp.zeros_like(kn)
    one = jnp.ones_like(vall)
    k_var = [[jnp.where(lo, kn, zero).astype(BF16), jnp.where(lo, zero, kroll).astype(BF16)],
             [jnp.where(lo, kroll, zero).astype(BF16), jnp.where(lo, zero, kn).astype(BF16)]]
    v_var = [[jnp.where(lo, vall, one).astype(BF16), jnp.where(lo, one, vroll).astype(BF16)],
             [jnp.where(lo, vroll, one).astype(BF16), jnp.where(lo, one, vall).astype(BF16)]]

    qn = []
    for i in range(ATTN_WIDTH // LANES):
        t = _pair_norm(q_ref[0, :, i * LANES:(i + 1) * LANES].astype(F32), qg_ref[...], head_mean)
        qn.append((t * (HEAD_DIM ** -0.5)).astype(BF16))

    qi = lax.broadcasted_iota(jnp.int32, (BLOCK, BLOCK), 0)
    si = lax.broadcasted_iota(jnp.int32, (BLOCK, BLOCK), 1)
    is_cur = si <= qi
    dist = jnp.where(is_cur, qi - si, qi - si + BLOCK).astype(F32)
    alibi = [(2.0 ** (-8.0 * (h + 1) / N_HEADS)) * dist for h in range(N_HEADS)]
    sink_tile = [jnp.full((BLOCK, LANES), sink_ref[h], F32) for h in range(N_HEADS)]

    units = [(b, h) for b in range(ATTN_QB) for h in range(N_HEADS)]
    for u0 in range(0, len(units), ATTN_STACK):
        chunk = units[u0:u0 + ATTN_STACK]
        s_all = []
        for b, h in chunk:
            i, slot = divmod(h, HEADS_PER_TILE)
            kvh = h // (N_HEADS // N_KV_HEADS)
            win = slice(b * BLOCK, (b + 2) * BLOCK)
            sc = lax.dot_general(qn[i][b * BLOCK:(b + 1) * BLOCK], k_var[kvh][slot][win],
                                 (((1,), (1,)), ((), ())), preferred_element_type=F32)
            prev = sc[:, :BLOCK]
            if b == 0:
                prev = jnp.where(n > 0, prev, NEG)
            s_all.append(jnp.where(is_cur, sc[:, BLOCK:], prev) - alibi[h])
        s = jnp.concatenate(s_all, axis=0)
        sink_rows = jnp.concatenate([sink_tile[h] for _, h in chunk], axis=0)
        m = jnp.maximum(jnp.max(s, axis=-1, keepdims=True), sink_rows)
        p = jnp.exp(s - m).astype(BF16)
        esink = jnp.exp(sink_rows - m)
        for k in range(0, len(chunk), HEADS_PER_TILE):
            b, h_even = chunk[k]
            i = h_even // HEADS_PER_TILE
            win = slice(b * BLOCK, (b + 2) * BLOCK)
            halves = []
            for slot in range(HEADS_PER_TILE):
                kvh = (h_even + slot) // (N_HEADS // N_KV_HEADS)
                hr = slice((k + slot) * BLOCK, (k + slot + 1) * BLOCK)
                pcat = jnp.concatenate([jnp.where(is_cur, 0.0, p[hr]).astype(BF16),
                                        jnp.where(is_cur, p[hr], 0.0).astype(BF16)], axis=1)
                pv = jnp.dot(pcat, v_var[kvh][slot][win], preferred_element_type=F32)
                halves.append(pv / (pltpu.roll(pv, HEAD_DIM, axis=1) + esink[hr]))
            o_ref[0, b * BLOCK:(b + 1) * BLOCK, i * LANES:(i + 1) * LANES] = jnp.where(
                lo, halves[0], halves[1]).astype(BF16)


def _attn(q3, k3, v3, q_gain, k_gain, sinks):
    bsz, seq, _ = q3.shape
    qrows = ATTN_QB * BLOCK
    cur = lambda b, n: (b, n, 0)
    prev = lambda b, n: (b, jnp.maximum(n * ATTN_QB - 1, 0), 0)
    qg2 = jnp.tile(q_gain.astype(F32), HEADS_PER_TILE).reshape(1, LANES)
    kg2 = jnp.tile(k_gain.astype(F32), HEADS_PER_TILE).reshape(1, LANES)
    gain_spec = pl.BlockSpec((1, LANES), lambda b, n: (0, 0))
    return pl.pallas_call(
        _attn_body,
        grid=(bsz, seq // qrows),
        in_specs=[pl.BlockSpec(memory_space=pltpu.SMEM),
                  pl.BlockSpec((1, qrows, ATTN_WIDTH), cur),
                  pl.BlockSpec((1, qrows, KV_WIDTH), cur),
                  pl.BlockSpec((1, BLOCK, KV_WIDTH), prev),
                  pl.BlockSpec((1, qrows, KV_WIDTH), cur),
                  pl.BlockSpec((1, BLOCK, KV_WIDTH), prev),
                  gain_spec, gain_spec],
        out_specs=pl.BlockSpec((1, qrows, ATTN_WIDTH), cur),
        out_shape=jax.ShapeDtypeStruct((bsz, seq, ATTN_WIDTH), BF16),
        compiler_params=_cparams(("arbitrary", "arbitrary")),
        name="attn",
    )(sinks.astype(F32), q3, k3, k3, v3, v3, qg2, kg2)


MIX_TM = 1024


def _mix_body(z_ref, ya_ref, gs_ref, ga_ref, x_ref, wglu_ref, wup_ref, wout_ref, g2_ref, wr_ref,
              x1_ref, gate_ref, hs_hbm, starts_ref, lens_ref, counts_ref,
              stage_ref, run_ref, pend_ref, sem):
    t = pl.program_id(0)
    sorter = _GroupSorter(t, MIX_TM // SORT_TM, hs_hbm, starts_ref, lens_ref, counts_ref,
                          stage_ref, run_ref, pend_ref, sem)

    z = z_ref[...]
    bv = jnp.dot(z, wglu_ref[:, :D_MODEL], preferred_element_type=F32)
    bg = jnp.dot(z, wglu_ref[:, D_MODEL:], preferred_element_type=F32)
    bs = bv * jax.nn.sigmoid(bg)
    ba = jnp.dot(ya_ref[...], wup_ref[...], preferred_element_type=F32)
    merged = gs_ref[...].astype(F32) * bs + ga_ref[...].astype(F32) * ba
    x1 = x_ref[...] + jnp.dot(merged.astype(BF16), wout_ref[...], preferred_element_type=F32)
    x1_ref[...] = x1

    ms = jnp.mean(x1 * x1, axis=-1, keepdims=True)
    h2 = x1 * lax.rsqrt(ms + EPS) * g2_ref[...]
    h2b = h2.astype(BF16)

    lt = lax.dot_general(wr_ref[...], h2b, (((1,), (1,)), ((), ())),
                         preferred_element_type=F32)
    logits = lt[:ROUTER_ROWS] + lt[ROUTER_ROWS:]
    row = lax.broadcasted_iota(jnp.int32, logits.shape, 0)
    ninf = -jnp.inf
    cm = jnp.where(row < N_GROUPS, logits, ninf)
    cmax = jnp.max(cm, axis=0, keepdims=True)
    g_prob = 1.0 / jnp.sum(jnp.exp(cm - cmax), axis=0, keepdims=True)
    g_idx = jnp.min(jnp.where(cm == cmax, row, ROUTER_ROWS), axis=0, keepdims=True)
    f0 = FINE_OFF + EXPERTS_PER_GROUP * g_idx
    fm = jnp.where((row >= f0) & (row < f0 + EXPERTS_PER_GROUP), logits, ninf)
    v1 = jnp.max(fm, axis=0, keepdims=True)
    i1 = jnp.min(jnp.where(fm == v1, row, ROUTER_ROWS), axis=0, keepdims=True)
    fm2 = jnp.where(row == i1, ninf, fm)
    v2 = jnp.max(fm2, axis=0, keepdims=True)
    i2 = jnp.min(jnp.where(fm2 == v2, row, ROUTER_ROWS), axis=0, keepdims=True)
    e21 = jnp.exp(v2 - v1)
    w1 = g_prob / (1.0 + e21)
    w2 = w1 * e21
    rt = jnp.where(row == i1 - f0, w1,
                   jnp.where(row == i2 - f0, w2,
                             jnp.where(row == ROUTE_GROUP_OFF + g_idx, 1.0, 0.0)))
    rt = jnp.concatenate([rt, jnp.zeros((ROUTER_LANES - ROUTER_ROWS, rt.shape[1]), F32)], axis=0)
    route = rt.T
    gate_ref[...] = route
    sorter.begin()
    sorter.issue(sorter.stage(h2b, route))

    @pl.when(t == pl.num_programs(0) - 1)
    def _():
        sorter.finish()


def _mix(z2, ya2, gs2, ga2, x2, wglu, wup, wout, g2, wr, layer):
    n = x2.shape[0]
    n_sort = n // SORT_TM
    subs = MIX_TM // SORT_TM
    cap = _region_cap(n)
    assert subs * SORT_TM >= ZFILL_TOK
    row = lambda c: pl.BlockSpec((MIX_TM, c), lambda i: (i, 0))
    full = lambda a, b: pl.BlockSpec((a, b), lambda i: (0, 0))
    smem = pl.BlockSpec(memory_space=pltpu.SMEM)
    return pl.pallas_call(
        _mix_body,
        grid=(n // MIX_TM,),
        in_specs=[row(SSM_WIDTH), row(ATTN_WIDTH), row(D_MODEL), row(D_MODEL), row(D_MODEL),
                  _layer_spec(layer, SSM_WIDTH, 2 * D_MODEL), _layer_spec(layer, ATTN_WIDTH, D_MODEL),
                  _layer_spec(layer, D_MODEL, D_MODEL), full(1, D_MODEL),
                  _layer_spec(layer, 2 * ROUTER_ROWS, D_MODEL)],
        out_specs=[row(D_MODEL), row(ROUTER_LANES), pl.BlockSpec(memory_space=pl.ANY),
                   smem, smem, smem],
        out_shape=[jax.ShapeDtypeStruct((n, D_MODEL), F32),
                   jax.ShapeDtypeStruct((n, ROUTER_LANES), F32),
                   jax.ShapeDtypeStruct((N_GROUPS * cap * TOK_ROWS, LANES), jnp.uint32),
                   jax.ShapeDtypeStruct((n_sort * N_GROUPS,), jnp.int32),
                   jax.ShapeDtypeStruct((n_sort * N_GROUPS,), jnp.int32),
                   jax.ShapeDtypeStruct((N_GROUPS,), jnp.int32)],
        scratch_shapes=[pltpu.VMEM((subs * SORT_TM * TOK_ROWS, LANES), jnp.uint32),
                        pltpu.SMEM((N_GROUPS,), jnp.int32),
                        pltpu.SMEM((subs * N_GROUPS,), jnp.int32),
                        pltpu.SemaphoreType.DMA((subs, N_GROUPS))],
        compiler_params=_cparams(("arbitrary",)),
        name="mix",
    )(z2, ya2, gs2, ga2, x2, wglu, wup, wout, g2, wr)


SORT_TM = 256
COMBINE_TM = 1024
MOE_TM = 512
TOK_ROWS = 8
H_WORDS = D_MODEL // 2
H_CHUNKS = H_WORDS // LANES
ROUTE_ROW = H_CHUNKS
ZFILL_TOK = MOE_TM
HI_MASK = 0xFFFF0000


def _region_cap(n):
    cap = n + ZFILL_TOK
    return -(-cap // MOE_TM) * MOE_TM


def _tile_positions(route):
    tm = route.shape[0]
    lane = lax.broadcasted_iota(jnp.int32, route.shape, 1)
    onehot = jnp.where((lane >= ROUTE_GROUP_OFF) & (lane < ROUTE_GROUP_OFF + N_GROUPS), route, 0.0)
    ci = lax.broadcasted_iota(jnp.int32, (tm, tm), 0)
    cj = lax.broadcasted_iota(jnp.int32, (tm, tm), 1)
    earlier = (cj < ci).astype(BF16)
    rank = jnp.dot(earlier, onehot.astype(BF16), preferred_element_type=F32)
    cnt = jnp.sum(onehot, axis=0, keepdims=True)
    li = lax.broadcasted_iota(jnp.int32, (LANES, LANES), 0)
    lj = lax.broadcasted_iota(jnp.int32, (LANES, LANES), 1)
    base = jnp.dot(jnp.broadcast_to(cnt, (8, LANES)).astype(BF16), (li < lj).astype(BF16),
                   preferred_element_type=F32)[0:1]
    return onehot, rank, base, cnt


class _GroupSorter:
    def __init__(self, t, subs, hs_hbm, starts_ref, lens_ref, counts_ref, stage_ref, run_ref,
                 pend_ref, sem):
        self.t, self.subs = t, subs
        self.hs_hbm, self.starts_ref, self.lens_ref, self.counts_ref = (
            hs_hbm, starts_ref, lens_ref, counts_ref)
        self.stage_ref, self.run_ref, self.pend_ref, self.sem = stage_ref, run_ref, pend_ref, sem
        self.slot_rows = SORT_TM * TOK_ROWS
        self.cap = hs_hbm.shape[0] // (N_GROUPS * TOK_ROWS)
        self.set0 = 0

    def _copy(self, src_row, dst_row, slot, g, tokens):
        return pltpu.make_async_copy(
            self.stage_ref.at[pl.ds(src_row, tokens * TOK_ROWS), :],
            self.hs_hbm.at[pl.ds(dst_row, tokens * TOK_ROWS), :], self.sem.at[slot, g])

    def _wait_set(self, set0):
        for sl in range(self.subs):
            for g in range(N_GROUPS):
                n = self.pend_ref[(set0 + sl) * N_GROUPS + g]

                @pl.when(n > 0)
                def _(sl=sl, g=g, n=n):
                    self._copy(0, 0, set0 + sl, g, n).wait()

    def begin(self):
        @pl.when(self.t == 0)
        def _():
            self.stage_ref[...] = jnp.zeros_like(self.stage_ref)
            for g in range(N_GROUPS):
                self.run_ref[g] = 0

        @pl.when(self.t > 0)
        def _():
            self._wait_set(self.set0)

    def stage(self, h, route):
        tm = SORT_TM
        runs = []
        for sl in range(self.subs):
            r = route[sl * tm:(sl + 1) * tm]
            onehot, rank, base, cnt = _tile_positions(r)
            w = (onehot * (rank + base)).astype(BF16)
            pos_row = lax.dot_general(jnp.ones((8, LANES), BF16), w, (((1,), (1,)), ((), ())),
                                      preferred_element_type=F32)[0:1]
            ri = lax.broadcasted_iota(jnp.int32, (tm, tm), 0).astype(F32)
            perm = (ri == pos_row).astype(BF16)
            r_hi = r.astype(BF16)
            r_mid = (r - r_hi.astype(F32)).astype(BF16)
            r_lo = (r - r_hi.astype(F32) - r_mid.astype(F32)).astype(BF16)
            sorted_all = jnp.dot(
                perm,
                jnp.concatenate([h[sl * tm:(sl + 1) * tm], r_hi, r_mid, r_lo], axis=1),
                preferred_element_type=F32)
            sh = sorted_all[:, :D_MODEL]
            sr = (sorted_all[:, D_MODEL:D_MODEL + LANES]
                  + sorted_all[:, D_MODEL + LANES:D_MODEL + 2 * LANES]
                  + sorted_all[:, D_MODEL + 2 * LANES:])
            hb = lax.bitcast_convert_type(sh, jnp.uint32)
            words = (hb[:, :H_WORDS] & jnp.uint32(HI_MASK)) | (hb[:, H_WORDS:] >> 16)

            slot0 = (self.set0 + sl) * self.slot_rows
            for s in range(H_CHUNKS):
                self.stage_ref[pl.ds(slot0 + s, tm, stride=TOK_ROWS), :] = (
                    words[:, s * LANES:(s + 1) * LANES])
            self.stage_ref[pl.ds(slot0 + ROUTE_ROW, tm, stride=TOK_ROWS), :] = (
                lax.bitcast_convert_type(sr, jnp.uint32))
            for g in range(N_GROUPS):
                runs.append((sl, g, slot0,
                             cnt[0, ROUTE_GROUP_OFF + g].astype(jnp.int32),
                             base[0, ROUTE_GROUP_OFF + g].astype(jnp.int32)))
        return runs

    def issue(self, runs):
        first_tile = self.t * self.subs
        for sl, g, slot0, c_g, b_g in runs:
            start = self.run_ref[g]
            self.starts_ref[(first_tile + sl) * N_GROUPS + g] = start
            self.lens_ref[(first_tile + sl) * N_GROUPS + g] = c_g
            self.pend_ref[(self.set0 + sl) * N_GROUPS + g] = c_g
            self.run_ref[g] = start + c_g
            src = pl.multiple_of(slot0 + b_g * TOK_ROWS, TOK_ROWS)
            dst = pl.multiple_of((g * self.cap + start) * TOK_ROWS, TOK_ROWS)

            @pl.when(c_g > 0)
            def _(sl=sl, g=g, src=src, dst=dst, c_g=c_g):
                self._copy(src, dst, self.set0 + sl, g, c_g).start()

    def finish(self):
        self._wait_set(self.set0)
        self.stage_ref[...] = jnp.zeros_like(self.stage_ref)
        zrows = ZFILL_TOK * TOK_ROWS
        for g in range(N_GROUPS):
            total = self.run_ref[g]
            self.counts_ref[g] = total
            dst = pl.multiple_of((g * self.cap + total) * TOK_ROWS, TOK_ROWS)
            pltpu.make_async_copy(self.stage_ref.at[pl.ds(0, zrows), :],
                                  self.hs_hbm.at[pl.ds(dst, zrows), :], self.sem.at[0, g]).start()
        for g in range(N_GROUPS):
            pltpu.make_async_copy(self.stage_ref.at[pl.ds(0, zrows), :],
                                  self.hs_hbm.at[pl.ds(0, zrows), :], self.sem.at[0, g]).wait()


def _experts_body(blk_ref, grp_ref, valid_ref, hs_ref, wu32_ref, wd32_ref, ys_ref, wu_ref, wd_ref):
    t = pl.program_id(0)
    tm = MOE_TM

    @pl.when((t == 0) | (grp_ref[t] != grp_ref[jnp.maximum(t - 1, 0)]))
    def _():
        wu_ref[...] = wu32_ref[...].astype(BF16)
        wd_ref[...] = wd32_ref[...].astype(BF16)

    @pl.when(valid_ref[t] > 0)
    def _():
        chunks = [hs_ref[pl.ds(s, tm, stride=TOK_ROWS), :] for s in range(H_CHUNKS + 1)]
        hi = [lax.bitcast_convert_type(c & jnp.uint32(HI_MASK), F32) for c in chunks[:H_CHUNKS]]
        lo = [lax.bitcast_convert_type(c << 16, F32) for c in chunks[:H_CHUNKS]]
        h = jnp.concatenate(hi + lo, axis=1).astype(BF16)
        route = lax.bitcast_convert_type(chunks[ROUTE_ROW], F32)
        acts = []
        for e in range(EXPERTS_PER_GROUP):
            gu = jnp.dot(h, wu_ref[e], preferred_element_type=F32)
            acts.append((jax.nn.silu(gu[:, :D_EXPERT]) * gu[:, D_EXPERT:]
                         * route[:, e:e + 1]).astype(BF16))
        act = jnp.concatenate(acts, axis=1)
        wd = wd_ref[...].reshape(EXPERTS_PER_GROUP * D_EXPERT, D_MODEL)
        out = jnp.dot(act, wd, preferred_element_type=F32)
        for j in range(D_MODEL // LANES):
            ys_ref[pl.ds(j, tm, stride=TOK_ROWS), :] = out[:, j * LANES:(j + 1) * LANES]


def _experts(blk, grp, valid, hs, wu, wd, layer):
    g0 = layer * N_GROUPS
    n_tiles = blk.shape[0]
    epg = EXPERTS_PER_GROUP
    return pl.pallas_call(
        _experts_body,
        grid_spec=pltpu.PrefetchScalarGridSpec(
            num_scalar_prefetch=3,
            grid=(n_tiles,),
            in_specs=[pl.BlockSpec((MOE_TM * TOK_ROWS, LANES), lambda t, b, g, v: (b[t], 0)),
                      pl.BlockSpec((epg, D_MODEL, 2 * D_EXPERT),
                                   lambda t, b, g, v: (g0 + g[t], 0, 0)),
                      pl.BlockSpec((epg, D_EXPERT, D_MODEL),
                                   lambda t, b, g, v: (g0 + g[t], 0, 0))],
            out_specs=pl.BlockSpec((MOE_TM * TOK_ROWS, LANES), lambda t, b, g, v: (b[t], 0)),
            scratch_shapes=[pltpu.VMEM((epg, D_MODEL, 2 * D_EXPERT), BF16),
                            pltpu.VMEM((epg, D_EXPERT, D_MODEL), BF16)],
        ),
        out_shape=jax.ShapeDtypeStruct(hs.shape, F32),
        compiler_params=_cparams(("arbitrary",)),
        name="moe_experts",
    )(blk, grp, valid, hs, wu, wd)


def _moe_outputs(starts_ref, lens_ref, route_ref, ys_hbm, buf_ref, sem):
    t = pl.program_id(0)
    nt = pl.num_programs(0)
    tm = SORT_TM
    subs = route_ref.shape[0] // tm
    slot_rows = tm * TOK_ROWS
    cap = ys_hbm.shape[0] // (N_GROUPS * TOK_ROWS)

    def fetches(step, sl):
        tile = step * subs + sl
        slot = (step % 2) * subs + sl
        copies = []
        first = 0
        for g in range(N_GROUPS):
            tokens = lens_ref[tile * N_GROUPS + g]
            src = pl.multiple_of((g * cap + starts_ref[tile * N_GROUPS + g]) * TOK_ROWS, TOK_ROWS)
            dst = pl.multiple_of(slot * slot_rows + first * TOK_ROWS, TOK_ROWS)
            copies.append((tokens, pltpu.make_async_copy(
                ys_hbm.at[pl.ds(src, tokens * TOK_ROWS), :],
                buf_ref.at[pl.ds(dst, tokens * TOK_ROWS), :], sem.at[slot, g])))
            first = first + tokens
        return copies

    def start_fetches(step):
        for sl in range(subs):
            for tokens, copy in fetches(step, sl):
                @pl.when(tokens > 0)
                def _(copy=copy):
                    copy.start()

    @pl.when(t == 0)
    def _():
        start_fetches(0)

    @pl.when(t + 1 < nt)
    def _():
        start_fetches(t + 1)

    for sl in range(subs):
        for tokens, copy in fetches(t, sl):
            @pl.when(tokens > 0)
            def _(copy=copy):
                copy.wait()

    outs = []
    for sl in range(subs):
        onehot, rank, base, _ = _tile_positions(route_ref[sl * tm:(sl + 1) * tm, :])
        pos = jnp.sum(onehot * (rank + base), axis=1, keepdims=True)
        ri = lax.broadcasted_iota(jnp.int32, (tm, tm), 1).astype(F32)
        sel = (ri == pos).astype(BF16)
        r0 = pl.multiple_of(((t % 2) * subs + sl) * slot_rows, TOK_ROWS)
        y = jnp.concatenate(
            [buf_ref[pl.ds(r0 + j, tm, stride=TOK_ROWS), :] for j in range(D_MODEL // LANES)],
            axis=1).astype(BF16)
        outs.append(jnp.dot(sel, y, preferred_element_type=F32))
    return outs


def _combine_body(starts_ref, lens_ref, x1_ref, route_ref, ys_hbm, o_ref, buf_ref, sem):
    outs = _moe_outputs(starts_ref, lens_ref, route_ref, ys_hbm, buf_ref, sem)
    for sl, y in enumerate(outs):
        rows = slice(sl * SORT_TM, (sl + 1) * SORT_TM)
        o_ref[rows, :] = x1_ref[rows, :] + y


def _combine(starts, lens, x1, route, ys):
    n = x1.shape[0]
    subs = COMBINE_TM // SORT_TM
    return pl.pallas_call(
        _combine_body,
        grid_spec=pltpu.PrefetchScalarGridSpec(
            num_scalar_prefetch=2,
            grid=(n // COMBINE_TM,),
            in_specs=[pl.BlockSpec((COMBINE_TM, D_MODEL), lambda i, s, c: (i, 0)),
                      pl.BlockSpec((COMBINE_TM, ROUTER_LANES), lambda i, s, c: (i, 0)),
                      pl.BlockSpec(memory_space=pl.ANY)],
            out_specs=pl.BlockSpec((COMBINE_TM, D_MODEL), lambda i, s, c: (i, 0)),
            scratch_shapes=[pltpu.VMEM((2 * COMBINE_TM * TOK_ROWS, LANES), F32),
                            pltpu.SemaphoreType.DMA((2 * subs, N_GROUPS))],
        ),
        out_shape=jax.ShapeDtypeStruct((n, D_MODEL), F32),
        compiler_params=_cparams(("arbitrary",)),
        name="moe_combine",
    )(starts, lens, x1, route, ys)


def _expert_tiles(counts, n):
    cap_blocks = _region_cap(n) // MOE_TM
    n_tiles = n // MOE_TM + N_GROUPS
    per_group = (counts + MOE_TM - 1) // MOE_TM
    ends = jnp.cumsum(per_group)
    total = ends[-1]
    t = jnp.minimum(jnp.arange(n_tiles, dtype=jnp.int32), total - 1)
    grp = jnp.sum((t[:, None] >= ends[None, :]).astype(jnp.int32), axis=1)
    first = ends - per_group
    blk = grp * cap_blocks + (t - first[grp])
    valid = (jnp.arange(n_tiles, dtype=jnp.int32) < total).astype(jnp.int32)
    return blk.astype(jnp.int32), grp.astype(jnp.int32), valid


def _moe_sorted(n, hs, counts, w_up, w_down, layer):
    blk, grp, valid = _expert_tiles(counts, n)
    return _experts(blk, grp, valid, hs, w_up, w_down, layer)


def kernel(x, norm_mix, w_in, ssm_lam_re, ssm_lam_im, ssm_log_dt, ssm_b_re, ssm_b_im, ssm_c_re,
           ssm_c_im, ssm_d, w_ssm_glu_val, w_ssm_glu_gate, q_norm, k_norm, attn_sinks, w_attn_up,
           w_out, norm_ffn, w_coarse, w_fine, w_expert_up, w_expert_down):
    bsz, seq, d = x.shape
    n = bsz * seq
    depth = w_in.shape[0]
    x2 = x.reshape(n, d)
    s5p = jax.vmap(_s5_params)(ssm_lam_re, ssm_lam_im, ssm_log_dt, ssm_b_re, ssm_b_im, ssm_c_re,
                               ssm_c_im, ssm_d)
    w_in_b = w_in.astype(BF16)
    wglu = jnp.concatenate([w_ssm_glu_val, w_ssm_glu_gate], axis=2).astype(BF16)
    wup_b = w_attn_up.astype(BF16)
    wout_b = w_out.astype(BF16)
    wr_t = jnp.concatenate(
        [jnp.swapaxes(w_coarse, 1, 2), jnp.swapaxes(w_fine, 1, 2),
         jnp.zeros((depth, ROUTER_ROWS - N_GROUPS - N_EXPERTS, d), w_fine.dtype)], axis=1).astype(F32)
    wr_hi = wr_t.astype(BF16)
    wr = jnp.concatenate([wr_hi, (wr_t - wr_hi.astype(F32)).astype(BF16)], axis=1)
    wu_all = w_expert_up.reshape((depth * N_EXPERTS,) + w_expert_up.shape[2:])
    wd_all = w_expert_down.reshape((depth * N_EXPERTS,) + w_expert_down.shape[2:])
    pending = None
    for i in range(depth):
        if pending is None:
            u, q, k, v, gs, ga = _proj(x2, norm_mix[i].reshape(1, d), w_in_b, i)
        else:
            x2, u, q, k, v, gs, ga = _proj_moe(*pending, norm_mix[i].reshape(1, d), w_in_b, i)
        z = _s5(u.reshape(bsz, seq, SSM_WIDTH), *(p[i] for p in s5p))
        ya = _attn(q.reshape(bsz, seq, ATTN_WIDTH), k.reshape(bsz, seq, KV_WIDTH),
                   v.reshape(bsz, seq, KV_WIDTH), q_norm[i], k_norm[i], attn_sinks[i])
        x1, route, hs, starts, lens, counts = _mix(
            z.reshape(n, SSM_WIDTH), ya.reshape(n, ATTN_WIDTH), gs, ga, x2,
            wglu, wup_b, wout_b, norm_ffn[i].reshape(1, d), wr, i)
        ys = _moe_sorted(n, hs, counts, wu_all, wd_all, i)
        pending = (starts, lens, x1, route, ys)
    return _combine(*pending).reshape(bsz, seq, d)
```

```python
import functools
import math

import jax
import jax.numpy as jnp
from jax import lax
from jax.experimental import pallas as pl
from jax.experimental.pallas import tpu as pltpu

F32 = jnp.float32
BF16 = jnp.bfloat16

D_MODEL = 1024
SSM_WIDTH = 512
SSM_GROUP = 16
SSM_GROUPS = 32
SSM_STATE = 64
N_HEADS = 8
N_KV_HEADS = 2
HEAD_DIM = 64
ATTN_WIDTH = 512
KV_WIDTH = 128
BLOCK = 128
N_GROUPS = 4
EXPERTS_PER_GROUP = 4
N_EXPERTS = 16
D_EXPERT = 256
EPS = 1e-6
NEG = -1e30

LANES = 128
SLABS = SSM_WIDTH // LANES
GROUPS_PER_SLAB = LANES // SSM_GROUP
SLAB_STATE = GROUPS_PER_SLAB * SSM_STATE
ROUTER_LANES = 128
ROUTER_ROWS = 32
FINE_OFF = N_GROUPS
ROUTE_GROUP_OFF = 8

VMEM_LIMIT = 56 * 1024 * 1024


def _cparams(sem):
    return pltpu.CompilerParams(dimension_semantics=sem, vmem_limit_bytes=VMEM_LIMIT)


PROJ_TM = 1024


def _proj_body(x_ref, g_ref, w_ref, *out_refs):
    _project(x_ref[...], g_ref, w_ref, *out_refs)


def _proj_moe_body(starts_ref, lens_ref, x1_ref, route_ref, ys_hbm, g_ref, w_ref,
                   x_ref, *rest):
    out_refs, (buf_ref, sem) = rest[:-2], rest[-2:]
    moe = _moe_outputs(starts_ref, lens_ref, route_ref, ys_hbm, buf_ref, sem)
    x = x1_ref[...] + jnp.concatenate(moe, axis=0)
    x_ref[...] = x
    _project(x, g_ref, w_ref, *out_refs)


def _project(x, g_ref, w_ref, u_ref, q_ref, k_ref, v_ref, gs_ref, ga_ref):
    ms = jnp.mean(x * x, axis=-1, keepdims=True)
    h = (x * lax.rsqrt(ms + EPS) * g_ref[...]).astype(BF16)

    def seg(a, b):
        return jnp.dot(h, w_ref[:, a:b], preferred_element_type=F32)

    o = 0
    u_ref[...] = seg(o, o + SSM_WIDTH)
    o += SSM_WIDTH
    q_ref[...] = seg(o, o + ATTN_WIDTH).astype(BF16)
    o += ATTN_WIDTH
    kv = seg(o, o + 2 * KV_WIDTH).astype(BF16)
    k_ref[...] = kv[:, :KV_WIDTH]
    v_ref[...] = kv[:, KV_WIDTH:]
    o += 2 * KV_WIDTH
    gs_ref[...] = jax.nn.sigmoid(seg(o, o + D_MODEL)).astype(BF16)
    o += D_MODEL
    ga_ref[...] = jax.nn.sigmoid(seg(o, o + D_MODEL)).astype(BF16)


def _layer_spec(layer, *shape):
    return pl.BlockSpec((None,) + shape, lambda *_: (layer,) + (0,) * len(shape))


def _proj_out(n):
    row = lambda c: pl.BlockSpec((PROJ_TM, c), lambda i, *_: (i, 0))
    specs = [row(SSM_WIDTH), row(ATTN_WIDTH), row(KV_WIDTH), row(KV_WIDTH), row(D_MODEL),
             row(D_MODEL)]
    shapes = [jax.ShapeDtypeStruct((n, SSM_WIDTH), F32),
              jax.ShapeDtypeStruct((n, ATTN_WIDTH), BF16),
              jax.ShapeDtypeStruct((n, KV_WIDTH), BF16),
              jax.ShapeDtypeStruct((n, KV_WIDTH), BF16),
              jax.ShapeDtypeStruct((n, D_MODEL), BF16),
              jax.ShapeDtypeStruct((n, D_MODEL), BF16)]
    return specs, shapes


def _proj(x2, g, w, layer):
    n = x2.shape[0]
    cols = w.shape[2]
    row = lambda c: pl.BlockSpec((PROJ_TM, c), lambda i: (i, 0))
    full = lambda a, b: pl.BlockSpec((a, b), lambda i: (0, 0))
    out_specs, out_shape = _proj_out(n)
    return pl.pallas_call(
        _proj_body,
        grid=(n // PROJ_TM,),
        in_specs=[row(D_MODEL), full(1, D_MODEL), _layer_spec(layer, D_MODEL, cols)],
        out_specs=out_specs,
        out_shape=out_shape,
        compiler_params=_cparams(("arbitrary",)),
        name="proj",
    )(x2, g, w)


def _proj_moe(starts, lens, x1, route, ys, g, w, layer):
    n = x1.shape[0]
    cols = w.shape[2]
    subs = PROJ_TM // SORT_TM
    row = lambda c: pl.BlockSpec((PROJ_TM, c), lambda i, *_: (i, 0))
    out_specs, out_shape = _proj_out(n)
    return pl.pallas_call(
        _proj_moe_body,
        grid_spec=pltpu.PrefetchScalarGridSpec(
            num_scalar_prefetch=2,
            grid=(n // PROJ_TM,),
            in_specs=[row(D_MODEL), row(ROUTER_LANES), pl.BlockSpec(memory_space=pl.ANY),
                      pl.BlockSpec((1, D_MODEL), lambda i, *_: (0, 0)),
                      _layer_spec(layer, D_MODEL, cols)],
            out_specs=[row(D_MODEL)] + out_specs,
            scratch_shapes=[pltpu.VMEM((2 * PROJ_TM * TOK_ROWS, LANES), F32),
                            pltpu.SemaphoreType.DMA((2 * subs, N_GROUPS))],
        ),
        out_shape=[jax.ShapeDtypeStruct((n, D_MODEL), F32)] + out_shape,
        compiler_params=_cparams(("arbitrary",)),
        name="proj_moe",
    )(starts, lens, x1, route, ys, g, w)


S5_TT = 64
S5_PAIRS = S5_TT // 2


def _s5_body(u_ref, bblk_ref, cblk_ref, k0_ref, lam_ref, d_ref, z_ref,
             ut_ref, y_ref, st_ref, carry_ref, *, bsz):
    rows = S5_TT * bsz
    prow = S5_PAIRS * bsz

    @pl.when(pl.program_id(0) == 0)
    def _():
        carry_ref[...] = jnp.zeros_like(carry_ref)

    for b in range(bsz):
        for j in range(SLABS):
            ut_ref[j, pl.ds(b, S5_TT, stride=bsz), :] = u_ref[b, :, j * LANES:(j + 1) * LANES]

    for j in range(SLABS):
        u3 = ut_ref[j].reshape(S5_PAIRS, 2 * bsz, LANES)
        u0 = u3[:, :bsz, :].reshape(prow, LANES)
        u1 = u3[:, bsz:, :].reshape(prow, LANES)
        lhs = jnp.concatenate([u1, u0], axis=1).astype(BF16)
        st_ref[j, 0:bsz, :] = carry_ref[j]
        st_ref[j, bsz:, :] = jnp.dot(lhs, bblk_ref[j], preferred_element_type=F32)
        a = jnp.broadcast_to(lam_ref[j, 0:1, :], (bsz, SLAB_STATE))
        bb = jnp.broadcast_to(lam_ref[j, 1:2, :], (bsz, SLAB_STATE))

        cre = carry_ref[j, :, 0:SLAB_STATE]
        cim = carry_ref[j, :, SLAB_STATE:2 * SLAB_STATE]
        for k in range(S5_PAIRS):
            r0 = (k + 1) * bsz
            bre = st_ref[j, r0:r0 + bsz, 0:SLAB_STATE]
            bim = st_ref[j, r0:r0 + bsz, SLAB_STATE:2 * SLAB_STATE]
            cre, cim = a * cre - bb * cim + bre, a * cim + bb * cre + bim
            st_ref[j, r0:r0 + bsz, 0:SLAB_STATE] = cre
            st_ref[j, r0:r0 + bsz, SLAB_STATE:2 * SLAB_STATE] = cim
        carry_ref[j, :, 0:SLAB_STATE] = cre
        carry_ref[j, :, SLAB_STATE:2 * SLAB_STATE] = cim

        cs = jnp.dot(st_ref[j].astype(BF16), cblk_ref[j], preferred_element_type=F32)
        y1 = cs[bsz:, :LANES] + d_ref[j] * u1
        y0 = (cs[:prow, LANES:] + jnp.dot(u0.astype(BF16), k0_ref[j], preferred_element_type=F32)
              + d_ref[j] * u0)
        y = jnp.concatenate([y0.reshape(S5_PAIRS, bsz, LANES), y1.reshape(S5_PAIRS, bsz, LANES)],
                            axis=1).reshape(rows, LANES)
        y_ref[j] = jax.nn.gelu(y)

    for b in range(bsz):
        for j in range(SLABS):
            z_ref[b, :, j * LANES:(j + 1) * LANES] = (
                y_ref[j, pl.ds(b, S5_TT, stride=bsz), :].astype(BF16))


def _s5(u3, bblk, cblk, k0, lam, dskip):
    bsz, seq, _ = u3.shape
    rows = S5_TT * bsz
    full = lambda *s: pl.BlockSpec(s, lambda i: (0,) * len(s))
    return pl.pallas_call(
        functools.partial(_s5_body, bsz=bsz),
        grid=(seq // S5_TT,),
        in_specs=[pl.BlockSpec((bsz, S5_TT, SSM_WIDTH), lambda i: (0, i, 0)),
                  full(SLABS, 2 * LANES, 2 * SLAB_STATE),
                  full(SLABS, 2 * SLAB_STATE, 2 * LANES),
                  full(SLABS, LANES, LANES),
                  full(SLABS, 2, SLAB_STATE),
                  full(SLABS, 1, LANES)],
        out_specs=pl.BlockSpec((bsz, S5_TT, SSM_WIDTH), lambda i: (0, i, 0)),
        out_shape=jax.ShapeDtypeStruct((bsz, seq, SSM_WIDTH), BF16),
        scratch_shapes=[pltpu.VMEM((SLABS, rows, LANES), F32),
                        pltpu.VMEM((SLABS, rows, LANES), F32),
                        pltpu.VMEM((SLABS, bsz + S5_PAIRS * bsz, 2 * SLAB_STATE), F32),
                        pltpu.VMEM((SLABS, bsz, 2 * SLAB_STATE), F32)],
        compiler_params=_cparams(("arbitrary",)),
        name="s5",
    )(u3, bblk, cblk, k0, lam, dskip)


def _s5_params(lam_re, lam_im, log_dt, b_re, b_im, c_re, c_im, d_skip):
    lam = lax.complex(lam_re.astype(F32), lam_im.astype(F32))
    dt = jnp.exp(log_dt.astype(F32))[:, None]
    lam_bar = jnp.exp(lam * dt)
    b = lax.complex(b_re.astype(F32), b_im.astype(F32))
    b_bar = ((lam_bar - 1.0) / lam)[..., None] * b
    eye = jnp.eye(GROUPS_PER_SLAB, dtype=F32)

    def in_blk(part):
        p4 = part.reshape(SLABS, GROUPS_PER_SLAB, SSM_STATE, SSM_GROUP)
        m = jnp.einsum('jgph,gk->jghkp', p4, eye)
        return m.reshape(SLABS, LANES, SLAB_STATE)

    def in_cplx(z):
        return jnp.concatenate([in_blk(z.real), in_blk(z.imag)], axis=-1)

    bblk = jnp.concatenate([in_cplx(b_bar), in_cplx(lam_bar[..., None] * b_bar)],
                           axis=1).astype(BF16)

    def out_blk(part):
        p4 = part.reshape(SLABS, GROUPS_PER_SLAB, SSM_GROUP, SSM_STATE)
        m = jnp.einsum('jghp,gk->jgpkh', p4, eye)
        return m.reshape(SLABS, SLAB_STATE, LANES)

    def out_cplx(z):
        return jnp.concatenate([out_blk(z.real), -out_blk(z.imag)], axis=1)

    c = lax.complex(c_re.astype(F32), c_im.astype(F32))
    cblk = jnp.concatenate([out_cplx(c), out_cplx(c * lam_bar[:, None, :])], axis=-1).astype(BF16)
    k0 = jnp.einsum('ghp,gpk->gkh', c, b_bar).real.reshape(SLABS, GROUPS_PER_SLAB, SSM_GROUP,
                                                           SSM_GROUP)
    k0 = jnp.einsum('jgkh,gq->jgkqh', k0, eye).reshape(SLABS, LANES, LANES).astype(BF16)
    lam2 = lam_bar * lam_bar
    lam_k = jnp.stack([lam2.real.reshape(SLABS, SLAB_STATE),
                       lam2.imag.reshape(SLABS, SLAB_STATE)], axis=1)
    dsk = d_skip.astype(F32).reshape(SLABS, 1, LANES)
    return bblk, cblk, k0, lam_k, dsk


ATTN_QB = 16
ATTN_STACK = 16
HEADS_PER_TILE = LANES // HEAD_DIM


def _pair_norm(t, gain2, head_mean):
    ms = jnp.dot((t * t).astype(BF16), head_mean, preferred_element_type=F32)
    return t * lax.rsqrt(ms + EPS) * gain2


def _attn_body(sink_ref, q_ref, kc_ref, kp_ref, vc_ref, vp_ref, qg_ref, kg_ref, o_ref):
    n = pl.program_id(1)
    lane = lax.broadcasted_iota(jnp.int32, (1, LANES), 1)
    lo = lane < HEAD_DIM
    li = lax.broadcasted_iota(jnp.int32, (LANES, LANES), 0)
    lj = lax.broadcasted_iota(jnp.int32, (LANES, LANES), 1)
    head_mean = jnp.where((li < HEAD_DIM) == (lj < HEAD_DIM), 1.0 / HEAD_DIM, 0.0).astype(BF16)

    kall = jnp.concatenate([kp_ref[0], kc_ref[0]], axis=0).astype(F32)
    kn = _pair_norm(kall, kg_ref[...], head_mean)
    kroll = pltpu.roll(kn, HEAD_DIM, axis=1)
    vall = jnp.concatenate([vp_ref[0], vc_ref[0]], axis=0).astype(F32)
    vroll = pltpu.roll(vall, HEAD_DIM, axis=1)
    zero = jnp.zeros_like(kn)
    one = jnp.ones_like(vall)
    k_var = [[jnp.where(lo, kn, zero).astype(BF16), jnp.where(lo, zero, kroll).astype(BF16)],
             [jnp.where(lo, kroll, zero).astype(BF16), jnp.where(lo, zero, kn).astype(BF16)]]
    v_var = [[jnp.where(lo, vall, one).astype(BF16), jnp.where(lo, one, vroll).astype(BF16)],
             [jnp.where(lo, vroll, one).astype(BF16), jnp.where(lo, one, vall).astype(BF16)]]

    qn = []
    for i in range(ATTN_WIDTH // LANES):
        t = _pair_norm(q_ref[0, :, i * LANES:(i + 1) * LANES].astype(F32), qg_ref[...], head_mean)
        qn.append((t * (HEAD_DIM ** -0.5)).astype(BF16))

    qi = lax.broadcasted_iota(jnp.int32, (BLOCK, BLOCK), 0)
    si = lax.broadcasted_iota(jnp.int32, (BLOCK, BLOCK), 1)
    is_cur = si <= qi
    dist = jnp.where(is_cur, qi - si, qi - si + BLOCK).astype(F32)
    alibi = [(2.0 ** (-8.0 * (h + 1) / N_HEADS)) * dist for h in range(N_HEADS)]
    sink_tile = [jnp.full((BLOCK, LANES), sink_ref[h], F32) for h in range(N_HEADS)]

    units = [(b, h) for b in range(ATTN_QB) for h in range(N_HEADS)]
    for u0 in range(0, len(units), ATTN_STACK):
        chunk = units[u0:u0 + ATTN_STACK]
        s_all = []
        for b, h in chunk:
            i, slot = divmod(h, HEADS_PER_TILE)
            kvh = h // (N_HEADS // N_KV_HEADS)
            win = slice(b * BLOCK, (b + 2) * BLOCK)
            sc = lax.dot_general(qn[i][b * BLOCK:(b + 1) * BLOCK], k_var[kvh][slot][win],
                                 (((1,), (1,)), ((), ())), preferred_element_type=F32)
            prev = sc[:, :BLOCK]
            if b == 0:
                prev = jnp.where(n > 0, prev, NEG)
            s_all.append(jnp.where(is_cur, sc[:, BLOCK:], prev) - alibi[h])
        s = jnp.concatenate(s_all, axis=0)
        sink_rows = jnp.concatenate([sink_tile[h] for _, h in chunk], axis=0)
        m = jnp.maximum(jnp.max(s, axis=-1, keepdims=True), sink_rows)
        p = jnp.exp(s - m).astype(BF16)
        esink = jnp.exp(sink_rows - m)
        for k in range(0, len(chunk), HEADS_PER_TILE):
            b, h_even = chunk[k]
            i = h_even // HEADS_PER_TILE
            win = slice(b * BLOCK, (b + 2) * BLOCK)
            halves = []
            for slot in range(HEADS_PER_TILE):
                kvh = (h_even + slot) // (N_HEADS // N_KV_HEADS)
                hr = slice((k + slot) * BLOCK, (k + slot + 1) * BLOCK)
                pcat = jnp.concatenate([jnp.where(is_cur, 0.0, p[hr]).astype(BF16),
                                        jnp.where(is_cur, p[hr], 0.0).astype(BF16)], axis=1)
                pv = jnp.dot(pcat, v_var[kvh][slot][win], preferred_element_type=F32)
                halves.append(pv / (pltpu.roll(pv, HEAD_DIM, axis=1) + esink[hr]))
            o_ref[0, b * BLOCK:(b + 1) * BLOCK, i * LANES:(i + 1) * LANES] = jnp.where(
                lo, halves[0], halves[1]).astype(BF16)


def _attn(q3, k3, v3, q_gain, k_gain, sinks):
    bsz, seq, _ = q3.shape
    qrows = ATTN_QB * BLOCK
    cur = lambda b, n: (b, n, 0)
    prev = lambda b, n: (b, jnp.maximum(n * ATTN_QB - 1, 0), 0)
    qg2 = jnp.tile(q_gain.astype(F32), HEADS_PER_TILE).reshape(1, LANES)
    kg2 = jnp.tile(k_gain.astype(F32), HEADS_PER_TILE).reshape(1, LANES)
    gain_spec = pl.BlockSpec((1, LANES), lambda b, n: (0, 0))
    return pl.pallas_call(
        _attn_body,
        grid=(bsz, seq // qrows),
        in_specs=[pl.BlockSpec(memory_space=pltpu.SMEM),
                  pl.BlockSpec((1, qrows, ATTN_WIDTH), cur),
                  pl.BlockSpec((1, qrows, KV_WIDTH), cur),
                  pl.BlockSpec((1, BLOCK, KV_WIDTH), prev),
                  pl.BlockSpec((1, qrows, KV_WIDTH), cur),
                  pl.BlockSpec((1, BLOCK, KV_WIDTH), prev),
                  gain_spec, gain_spec],
        out_specs=pl.BlockSpec((1, qrows, ATTN_WIDTH), cur),
        out_shape=jax.ShapeDtypeStruct((bsz, seq, ATTN_WIDTH), BF16),
        compiler_params=_cparams(("arbitrary", "arbitrary")),
        name="attn",
    )(sinks.astype(F32), q3, k3, k3, v3, v3, qg2, kg2)


MIX_TM = 1024


def _mix_body(z_ref, ya_ref, gs_ref, ga_ref, x_ref, wglu_ref, wup_ref, wout_ref, g2_ref, wr_ref,
              x1_ref, gate_ref, hs_hbm, starts_ref, lens_ref, counts_ref,
              stage_ref, run_ref, pend_ref, sem):
    t = pl.program_id(0)
    sorter = _GroupSorter(t, MIX_TM // SORT_TM, hs_hbm, starts_ref, lens_ref, counts_ref,
                          stage_ref, run_ref, pend_ref, sem)

    z = z_ref[...]
    bv = jnp.dot(z, wglu_ref[:, :D_MODEL], preferred_element_type=F32)
    bg = jnp.dot(z, wglu_ref[:, D_MODEL:], preferred_element_type=F32)
    bs = bv * jax.nn.sigmoid(bg)
    ba = jnp.dot(ya_ref[...], wup_ref[...], preferred_element_type=F32)
    merged = gs_ref[...].astype(F32) * bs + ga_ref[...].astype(F32) * ba
    x1 = x_ref[...] + jnp.dot(merged.astype(BF16), wout_ref[...], preferred_element_type=F32)
    x1_ref[...] = x1

    ms = jnp.mean(x1 * x1, axis=-1, keepdims=True)
    h2 = x1 * lax.rsqrt(ms + EPS) * g2_ref[...]
    h2b = h2.astype(BF16)

    lt = lax.dot_general(wr_ref[...], h2b, (((1,), (1,)), ((), ())),
                         preferred_element_type=F32)
    logits = lt[:ROUTER_ROWS] + lt[ROUTER_ROWS:]
    row = lax.broadcasted_iota(jnp.int32, logits.shape, 0)
    ninf = -jnp.inf
    cm = jnp.where(row < N_GROUPS, logits, ninf)
    cmax = jnp.max(cm, axis=0, keepdims=True)
    g_prob = 1.0 / jnp.sum(jnp.exp(cm - cmax), axis=0, keepdims=True)
    g_idx = jnp.min(jnp.where(cm == cmax, row, ROUTER_ROWS), axis=0, keepdims=True)
    f0 = FINE_OFF + EXPERTS_PER_GROUP * g_idx
    fm = jnp.where((row >= f0) & (row < f0 + EXPERTS_PER_GROUP), logits, ninf)
    v1 = jnp.max(fm, axis=0, keepdims=True)
    i1 = jnp.min(jnp.where(fm == v1, row, ROUTER_ROWS), axis=0, keepdims=True)
    fm2 = jnp.where(row == i1, ninf, fm)
    v2 = jnp.max(fm2, axis=0, keepdims=True)
    i2 = jnp.min(jnp.where(fm2 == v2, row, ROUTER_ROWS), axis=0, keepdims=True)
    e21 = jnp.exp(v2 - v1)
    w1 = g_prob / (1.0 + e21)
    w2 = w1 * e21
    rt = jnp.where(row == i1 - f0, w1,
                   jnp.where(row == i2 - f0, w2,
                             jnp.where(row == ROUTE_GROUP_OFF + g_idx, 1.0, 0.0)))
    rt = jnp.concatenate([rt, jnp.zeros((ROUTER_LANES - ROUTER_ROWS, rt.shape[1]), F32)], axis=0)
    route = rt.T
    gate_ref[...] = route
    sorter.begin()
    sorter.issue(sorter.stage(h2b, route))

    @pl.when(t == pl.num_programs(0) - 1)
    def _():
        sorter.finish()


def _mix(z2, ya2, gs2, ga2, x2, wglu, wup, wout, g2, wr, layer):
    n = x2.shape[0]
    n_sort = n // SORT_TM
    subs = MIX_TM // SORT_TM
    cap = _region_cap(n)
    assert subs * SORT_TM >= ZFILL_TOK
    row = lambda c: pl.BlockSpec((MIX_TM, c), lambda i: (i, 0))
    full = lambda a, b: pl.BlockSpec((a, b), lambda i: (0, 0))
    smem = pl.BlockSpec(memory_space=pltpu.SMEM)
    return pl.pallas_call(
        _mix_body,
        grid=(n // MIX_TM,),
        in_specs=[row(SSM_WIDTH), row(ATTN_WIDTH), row(D_MODEL), row(D_MODEL), row(D_MODEL),
                  _layer_spec(layer, SSM_WIDTH, 2 * D_MODEL), _layer_spec(layer, ATTN_WIDTH, D_MODEL),
                  _layer_spec(layer, D_MODEL, D_MODEL), full(1, D_MODEL),
                  _layer_spec(layer, 2 * ROUTER_ROWS, D_MODEL)],
        out_specs=[row(D_MODEL), row(ROUTER_LANES), pl.BlockSpec(memory_space=pl.ANY),
                   smem, smem, smem],
        out_shape=[jax.ShapeDtypeStruct((n, D_MODEL), F32),
                   jax.ShapeDtypeStruct((n, ROUTER_LANES), F32),
                   jax.ShapeDtypeStruct((N_GROUPS * cap * TOK_ROWS, LANES), jnp.uint32),
                   jax.ShapeDtypeStruct((n_sort * N_GROUPS,), jnp.int32),
                   jax.ShapeDtypeStruct((n_sort * N_GROUPS,), jnp.int32),
                   jax.ShapeDtypeStruct((N_GROUPS,), jnp.int32)],
        scratch_shapes=[pltpu.VMEM((subs * SORT_TM * TOK_ROWS, LANES), jnp.uint32),
                        pltpu.SMEM((N_GROUPS,), jnp.int32),
                        pltpu.SMEM((subs * N_GROUPS,), jnp.int32),
                        pltpu.SemaphoreType.DMA((subs, N_GROUPS))],
        compiler_params=_cparams(("arbitrary",)),
        name="mix",
    )(z2, ya2, gs2, ga2, x2, wglu, wup, wout, g2, wr)


SORT_TM = 256
COMBINE_TM = 1024
MOE_TM = 512
TOK_ROWS = 8
H_WORDS = D_MODEL // 2
H_CHUNKS = H_WORDS // LANES
ROUTE_ROW = H_CHUNKS
ZFILL_TOK = MOE_TM
HI_MASK = 0xFFFF0000


def _region_cap(n):
    cap = n + ZFILL_TOK
    return -(-cap // MOE_TM) * MOE_TM


def _tile_positions(route):
    tm = route.shape[0]
    lane = lax.broadcasted_iota(jnp.int32, route.shape, 1)
    onehot = jnp.where((lane >= ROUTE_GROUP_OFF) & (lane < ROUTE_GROUP_OFF + N_GROUPS), route, 0.0)
    ci = lax.broadcasted_iota(jnp.int32, (tm, tm), 0)
    cj = lax.broadcasted_iota(jnp.int32, (tm, tm), 1)
    earlier = (cj < ci).astype(BF16)
    rank = jnp.dot(earlier, onehot.astype(BF16), preferred_element_type=F32)
    cnt = jnp.sum(onehot, axis=0, keepdims=True)
    li = lax.broadcasted_iota(jnp.int32, (LANES, LANES), 0)
    lj = lax.broadcasted_iota(jnp.int32, (LANES, LANES), 1)
    base = jnp.dot(jnp.broadcast_to(cnt, (8, LANES)).astype(BF16), (li < lj).astype(BF16),
                   preferred_element_type=F32)[0:1]
    return onehot, rank, base, cnt


class _GroupSorter:
    def __init__(self, t, subs, hs_hbm, starts_ref, lens_ref, counts_ref, stage_ref, run_ref,
                 pend_ref, sem):
        self.t, self.subs = t, subs
        self.hs_hbm, self.starts_ref, self.lens_ref, self.counts_ref = (
            hs_hbm, starts_ref, lens_ref, counts_ref)
        self.stage_ref, self.run_ref, self.pend_ref, self.sem = stage_ref, run_ref, pend_ref, sem
        self.slot_rows = SORT_TM * TOK_ROWS
        self.cap = hs_hbm.shape[0] // (N_GROUPS * TOK_ROWS)
        self.set0 = 0

    def _copy(self, src_row, dst_row, slot, g, tokens):
        return pltpu.make_async_copy(
            self.stage_ref.at[pl.ds(src_row, tokens * TOK_ROWS), :],
            self.hs_hbm.at[pl.ds(dst_row, tokens * TOK_ROWS), :], self.sem.at[slot, g])

    def _wait_set(self, set0):
        for sl in range(self.subs):
            for g in range(N_GROUPS):
                n = self.pend_ref[(set0 + sl) * N_GROUPS + g]

                @pl.when(n > 0)
                def _(sl=sl, g=g, n=n):
                    self._copy(0, 0, set0 + sl, g, n).wait()

    def begin(self):
        @pl.when(self.t == 0)
        def _():
            self.stage_ref[...] = jnp.zeros_like(self.stage_ref)
            for g in range(N_GROUPS):
                self.run_ref[g] = 0

        @pl.when(self.t > 0)
        def _():
            self._wait_set(self.set0)

    def stage(self, h, route):
        tm = SORT_TM
        runs = []
        for sl in range(self.subs):
            r = route[sl * tm:(sl + 1) * tm]
            onehot, rank, base, cnt = _tile_positions(r)
            w = (onehot * (rank + base)).astype(BF16)
            pos_row = lax.dot_general(jnp.ones((8, LANES), BF16), w, (((1,), (1,)), ((), ())),
                                      preferred_element_type=F32)[0:1]
            ri = lax.broadcasted_iota(jnp.int32, (tm, tm), 0).astype(F32)
            perm = (ri == pos_row).astype(BF16)
            r_hi = r.astype(BF16)
            r_mid = (r - r_hi.astype(F32)).astype(BF16)
            r_lo = (r - r_hi.astype(F32) - r_mid.astype(F32)).astype(BF16)
            sorted_all = jnp.dot(
                perm,
                jnp.concatenate([h[sl * tm:(sl + 1) * tm], r_hi, r_mid, r_lo], axis=1),
                preferred_element_type=F32)
            sh = sorted_all[:, :D_MODEL]
            sr = (sorted_all[:, D_MODEL:D_MODEL + LANES]
                  + sorted_all[:, D_MODEL + LANES:D_MODEL + 2 * LANES]
                  + sorted_all[:, D_MODEL + 2 * LANES:])
            hb = lax.bitcast_convert_type(sh, jnp.uint32)
            words = (hb[:, :H_WORDS] & jnp.uint32(HI_MASK)) | (hb[:, H_WORDS:] >> 16)

            slot0 = (self.set0 + sl) * self.slot_rows
            for s in range(H_CHUNKS):
                self.stage_ref[pl.ds(slot0 + s, tm, stride=TOK_ROWS), :] = (
                    words[:, s * LANES:(s + 1) * LANES])
            self.stage_ref[pl.ds(slot0 + ROUTE_ROW, tm, stride=TOK_ROWS), :] = (
                lax.bitcast_convert_type(sr, jnp.uint32))
            for g in range(N_GROUPS):
                runs.append((sl, g, slot0,
                             cnt[0, ROUTE_GROUP_OFF + g].astype(jnp.int32),
                             base[0, ROUTE_GROUP_OFF + g].astype(jnp.int32)))
        return runs

    def issue(self, runs):
        first_tile = self.t * self.subs
        for sl, g, slot0, c_g, b_g in runs:
            start = self.run_ref[g]
            self.starts_ref[(first_tile + sl) * N_GROUPS + g] = start
            self.lens_ref[(first_tile + sl) * N_GROUPS + g] = c_g
            self.pend_ref[(self.set0 + sl) * N_GROUPS + g] = c_g
            self.run_ref[g] = start + c_g
            src = pl.multiple_of(slot0 + b_g * TOK_ROWS, TOK_ROWS)
            dst = pl.multiple_of((g * self.cap + start) * TOK_ROWS, TOK_ROWS)

            @pl.when(c_g > 0)
            def _(sl=sl, g=g, src=src, dst=dst, c_g=c_g):
                self._copy(src, dst, self.set0 + sl, g, c_g).start()

    def finish(self):
        self._wait_set(self.set0)
        self.stage_ref[...] = jnp.zeros_like(self.stage_ref)
        zrows = ZFILL_TOK * TOK_ROWS
        for g in range(N_GROUPS):
            total = self.run_ref[g]
            self.counts_ref[g] = total
            dst = pl.multiple_of((g * self.cap + total) * TOK_ROWS, TOK_ROWS)
            pltpu.make_async_copy(self.stage_ref.at[pl.ds(0, zrows), :],
                                  self.hs_hbm.at[pl.ds(dst, zrows), :], self.sem.at[0, g]).start()
        for g in range(N_GROUPS):
            pltpu.make_async_copy(self.stage_ref.at[pl.ds(0, zrows), :],
                                  self.hs_hbm.at[pl.ds(0, zrows), :], self.sem.at[0, g]).wait()


def _experts_body(blk_ref, grp_ref, valid_ref, hs_ref, wu32_ref, wd32_ref, ys_ref, wu_ref, wd_ref):
    t = pl.program_id(0)
    tm = MOE_TM

    @pl.when((t == 0) | (grp_ref[t] != grp_ref[jnp.maximum(t - 1, 0)]))
    def _():
        wu_ref[...] = wu32_ref[...].astype(BF16)
        wd_ref[...] = wd32_ref[...].astype(BF16)

    @pl.when(valid_ref[t] > 0)
    def _():
        chunks = [hs_ref[pl.ds(s, tm, stride=TOK_ROWS), :] for s in range(H_CHUNKS + 1)]
        hi = [lax.bitcast_convert_type(c & jnp.uint32(HI_MASK), F32) for c in chunks[:H_CHUNKS]]
        lo = [lax.bitcast_convert_type(c << 16, F32) for c in chunks[:H_CHUNKS]]
        h = jnp.concatenate(hi + lo, axis=1).astype(BF16)
        route = lax.bitcast_convert_type(chunks[ROUTE_ROW], F32)
        acts = []
        for e in range(EXPERTS_PER_GROUP):
            gu = jnp.dot(h, wu_ref[e], preferred_element_type=F32)
            acts.append((jax.nn.silu(gu[:, :D_EXPERT]) * gu[:, D_EXPERT:]
                         * route[:, e:e + 1]).astype(BF16))
        act = jnp.concatenate(acts, axis=1)
        wd = wd_ref[...].reshape(EXPERTS_PER_GROUP * D_EXPERT, D_MODEL)
        out = jnp.dot(act, wd, preferred_element_type=F32)
        for j in range(D_MODEL // LANES):
            ys_ref[pl.ds(j, tm, stride=TOK_ROWS), :] = out[:, j * LANES:(j + 1) * LANES]


def _experts(blk, grp, valid, hs, wu, wd, layer):
    g0 = layer * N_GROUPS
    n_tiles = blk.shape[0]
    epg = EXPERTS_PER_GROUP
    return pl.pallas_call(
        _experts_body,
        grid_spec=pltpu.PrefetchScalarGridSpec(
            num_scalar_prefetch=3,
            grid=(n_tiles,),
            in_specs=[pl.BlockSpec((MOE_TM * TOK_ROWS, LANES), lambda t, b, g, v: (b[t], 0)),
                      pl.BlockSpec((epg, D_MODEL, 2 * D_EXPERT),
                                   lambda t, b, g, v: (g0 + g[t], 0, 0)),
                      pl.BlockSpec((epg, D_EXPERT, D_MODEL),
                                   lambda t, b, g, v: (g0 + g[t], 0, 0))],
            out_specs=pl.BlockSpec((MOE_TM * TOK_ROWS, LANES), lambda t, b, g, v: (b[t], 0)),
            scratch_shapes=[pltpu.VMEM((epg, D_MODEL, 2 * D_EXPERT), BF16),
                            pltpu.VMEM((epg, D_EXPERT, D_MODEL), BF16)],
        ),
        out_shape=jax.ShapeDtypeStruct(hs.shape, F32),
        compiler_params=_cparams(("arbitrary",)),
        name="moe_experts",
    )(blk, grp, valid, hs, wu, wd)


def _moe_outputs(starts_ref, lens_ref, route_ref, ys_hbm, buf_ref, sem):
    t = pl.program_id(0)
    nt = pl.num_programs(0)
    tm = SORT_TM
    subs = route_ref.shape[0] // tm
    slot_rows = tm * TOK_ROWS
    cap = ys_hbm.shape[0] // (N_GROUPS * TOK_ROWS)

    def fetches(step, sl):
        tile = step * subs + sl
        slot = (step % 2) * subs + sl
        copies = []
        first = 0
        for g in range(N_GROUPS):
            tokens = lens_ref[tile * N_GROUPS + g]
            src = pl.multiple_of((g * cap + starts_ref[tile * N_GROUPS + g]) * TOK_ROWS, TOK_ROWS)
            dst = pl.multiple_of(slot * slot_rows + first * TOK_ROWS, TOK_ROWS)
            copies.append((tokens, pltpu.make_async_copy(
                ys_hbm.at[pl.ds(src, tokens * TOK_ROWS), :],
                buf_ref.at[pl.ds(dst, tokens * TOK_ROWS), :], sem.at[slot, g])))
            first = first + tokens
        return copies

    def start_fetches(step):
        for sl in range(subs):
            for tokens, copy in fetches(step, sl):
                @pl.when(tokens > 0)
                def _(copy=copy):
                    copy.start()

    @pl.when(t == 0)
    def _():
        start_fetches(0)

    @pl.when(t + 1 < nt)
    def _():
        start_fetches(t + 1)

    for sl in range(subs):
        for tokens, copy in fetches(t, sl):
            @pl.when(tokens > 0)
            def _(copy=copy):
                copy.wait()

    outs = []
    for sl in range(subs):
        onehot, rank, base, _ = _tile_positions(route_ref[sl * tm:(sl + 1) * tm, :])
        pos = jnp.sum(onehot * (rank + base), axis=1, keepdims=True)
        ri = lax.broadcasted_iota(jnp.int32, (tm, tm), 1).astype(F32)
        sel = (ri == pos).astype(BF16)
        r0 = pl.multiple_of(((t % 2) * subs + sl) * slot_rows, TOK_ROWS)
        y = jnp.concatenate(
            [buf_ref[pl.ds(r0 + j, tm, stride=TOK_ROWS), :] for j in range(D_MODEL // LANES)],
            axis=1).astype(BF16)
        outs.append(jnp.dot(sel, y, preferred_element_type=F32))
    return outs


def _combine_body(starts_ref, lens_ref, x1_ref, route_ref, ys_hbm, o_ref, buf_ref, sem):
    outs = _moe_outputs(starts_ref, lens_ref, route_ref, ys_hbm, buf_ref, sem)
    for sl, y in enumerate(outs):
        rows = slice(sl * SORT_TM, (sl + 1) * SORT_TM)
        o_ref[rows, :] = x1_ref[rows, :] + y


def _combine(starts, lens, x1, route, ys):
    n = x1.shape[0]
    subs = COMBINE_TM // SORT_TM
    return pl.pallas_call(
        _combine_body,
        grid_spec=pltpu.PrefetchScalarGridSpec(
            num_scalar_prefetch=2,
            grid=(n // COMBINE_TM,),
            in_specs=[pl.BlockSpec((COMBINE_TM, D_MODEL), lambda i, s, c: (i, 0)),
                      pl.BlockSpec((COMBINE_TM, ROUTER_LANES), lambda i, s, c: (i, 0)),
                      pl.BlockSpec(memory_space=pl.ANY)],
            out_specs=pl.BlockSpec((COMBINE_TM, D_MODEL), lambda i, s, c: (i, 0)),
            scratch_shapes=[pltpu.VMEM((2 * COMBINE_TM * TOK_ROWS, LANES), F32),
                            pltpu.SemaphoreType.DMA((2 * subs, N_GROUPS))],
        ),
        out_shape=jax.ShapeDtypeStruct((n, D_MODEL), F32),
        compiler_params=_cparams(("arbitrary",)),
        name="moe_combine",
    )(starts, lens, x1, route, ys)


def _expert_tiles(counts, n):
    cap_blocks = _region_cap(n) // MOE_TM
    n_tiles = n // MOE_TM + N_GROUPS
    per_group = (counts + MOE_TM - 1) // MOE_TM
    ends = jnp.cumsum(per_group)
    total = ends[-1]
    t = jnp.minimum(jnp.arange(n_tiles, dtype=jnp.int32), total - 1)
    grp = jnp.sum((t[:, None] >= ends[None, :]).astype(jnp.int32), axis=1)
    first = ends - per_group
    blk = grp * cap_blocks + (t - first[grp])
    valid = (jnp.arange(n_tiles, dtype=jnp.int32) < total).astype(jnp.int32)
    return blk.astype(jnp.int32), grp.astype(jnp.int32), valid


def _moe_sorted(n, hs, counts, w_up, w_down, layer):
    blk, grp, valid = _expert_tiles(counts, n)
    return _experts(blk, grp, valid, hs, w_up, w_down, layer)


def kernel(x, norm_mix, w_in, ssm_lam_re, ssm_lam_im, ssm_log_dt, ssm_b_re, ssm_b_im, ssm_c_re,
           ssm_c_im, ssm_d, w_ssm_glu_val, w_ssm_glu_gate, q_norm, k_norm, attn_sinks, w_attn_up,
           w_out, norm_ffn, w_coarse, w_fine, w_expert_up, w_expert_down):
    bsz, seq, d = x.shape
    n = bsz * seq
    depth = w_in.shape[0]
    x2 = x.reshape(n, d)
    s5p = jax.vmap(_s5_params)(ssm_lam_re, ssm_lam_im, ssm_log_dt, ssm_b_re, ssm_b_im, ssm_c_re,
                               ssm_c_im, ssm_d)
    w_in_b = w_in.astype(BF16)
    wglu = jnp.concatenate([w_ssm_glu_val, w_ssm_glu_gate], axis=2).astype(BF16)
    wup_b = w_attn_up.astype(BF16)
    wout_b = w_out.astype(BF16)
    wr_t = jnp.concatenate(
        [jnp.swapaxes(w_coarse, 1, 2), jnp.swapaxes(w_fine, 1, 2),
         jnp.zeros((depth, ROUTER_ROWS - N_GROUPS - N_EXPERTS, d), w_fine.dtype)], axis=1).astype(F32)
    wr_hi = wr_t.astype(BF16)
    wr = jnp.concatenate([wr_hi, (wr_t - wr_hi.astype(F32)).astype(BF16)], axis=1)
    wu_all = w_expert_up.reshape((depth * N_EXPERTS,) + w_expert_up.shape[2:])
    wd_all = w_expert_down.reshape((depth * N_EXPERTS,) + w_expert_down.shape[2:])
    pending = None
    for i in range(depth):
        if pending is None:
            u, q, k, v, gs, ga = _proj(x2, norm_mix[i].reshape(1, d), w_in_b, i)
        else:
            x2, u, q, k, v, gs, ga = _proj_moe(*pending, norm_mix[i].reshape(1, d), w_in_b, i)
        z = _s5(u.reshape(bsz, seq, SSM_WIDTH), *(p[i] for p in s5p))
        ya = _attn(q.reshape(bsz, seq, ATTN_WIDTH), k.reshape(bsz, seq, KV_WIDTH),
                   v.reshape(bsz, seq, KV_WIDTH), q_norm[i], k_norm[i], attn_sinks[i])
        x1, route, hs, starts, lens, counts = _mix(
            z.reshape(n, SSM_WIDTH), ya.reshape(n, ATTN_WIDTH), gs, ga, x2,
            wglu, wup_b, wout_b, norm_ffn[i].reshape(1, d), wr, i)
        ys = _moe_sorted(n, hs, counts, wu_all, wd_all, i)
        pending = (starts, lens, x1, route, ys)
    return _combine(*pending).reshape(bsz, seq, d)
```

```python
import functools

import jax
import jax.numpy as jnp
import numpy as np
from jax import lax
from jax.experimental import pallas as pl
from jax.experimental.pallas import tpu as pltpu

F32 = jnp.float32
BF16 = jnp.bfloat16

D_MODEL = 1024
SSM_WIDTH = 512
SSM_GROUP = 16
SSM_STATE = 64
N_HEADS = 8
N_KV_HEADS = 2
HEAD_DIM = 64
ATTN_WIDTH = 512
KV_WIDTH = 128
BLOCK = 128
N_GROUPS = 4
EXPERTS_PER_GROUP = 4
N_EXPERTS = 16
D_EXPERT = 256
EPS = 1e-6
NEG = -1e30

LANES = 128
SLABS = SSM_WIDTH // LANES
GROUPS_PER_SLAB = LANES // SSM_GROUP
SLAB_STATE = GROUPS_PER_SLAB * SSM_STATE
ROUTER_LANES = 128
ROUTER_ROWS = 32
FINE_OFF = N_GROUPS
ROUTE_GROUP_OFF = 8

VMEM_LIMIT = 56 * 1024 * 1024


def _cparams(sem):
    return pltpu.CompilerParams(dimension_semantics=sem, vmem_limit_bytes=VMEM_LIMIT)


PROJ_TM = 1024


def _proj_body(x_ref, g_ref, w_ref, *out_refs):
    _project(x_ref[...], g_ref, w_ref, *out_refs)


def _proj_moe_body(starts_ref, lens_ref, x1_ref, route_ref, ys_hbm, g_ref, w_ref,
                   x_ref, *rest):
    out_refs, (buf_ref, sem) = rest[:-2], rest[-2:]
    moe = _moe_outputs(starts_ref, lens_ref, route_ref, ys_hbm, buf_ref, sem)
    x = x1_ref[...] + jnp.concatenate(moe, axis=0)
    x_ref[...] = x
    _project(x, g_ref, w_ref, *out_refs)


def _project(x, g_ref, w_ref, u_ref, q_ref, k_ref, v_ref, gs_ref, ga_ref):
    ms = jnp.mean(x * x, axis=-1, keepdims=True)
    h = (x * lax.rsqrt(ms + EPS) * g_ref[...]).astype(BF16)

    def seg(a, b):
        return jnp.dot(h, w_ref[:, a:b], preferred_element_type=F32)

    o = 0
    u_ref[...] = seg(o, o + SSM_WIDTH)
    o += SSM_WIDTH
    q_ref[...] = seg(o, o + ATTN_WIDTH).astype(BF16)
    o += ATTN_WIDTH
    kv = seg(o, o + 2 * KV_WIDTH).astype(BF16)
    k_ref[...] = kv[:, :KV_WIDTH]
    v_ref[...] = kv[:, KV_WIDTH:]
    o += 2 * KV_WIDTH
    gs_ref[...] = jax.nn.sigmoid(seg(o, o + D_MODEL)).astype(BF16)
    o += D_MODEL
    ga_ref[...] = jax.nn.sigmoid(seg(o, o + D_MODEL)).astype(BF16)


def _layer_spec(layer, *shape):
    return pl.BlockSpec((None,) + shape, lambda *_: (layer,) + (0,) * len(shape))


def _proj_out(n):
    row = lambda c: pl.BlockSpec((PROJ_TM, c), lambda i, *_: (i, 0))
    specs = [row(SSM_WIDTH), row(ATTN_WIDTH), row(KV_WIDTH), row(KV_WIDTH), row(D_MODEL),
             row(D_MODEL)]
    shapes = [jax.ShapeDtypeStruct((n, SSM_WIDTH), F32),
              jax.ShapeDtypeStruct((n, ATTN_WIDTH), BF16),
              jax.ShapeDtypeStruct((n, KV_WIDTH), BF16),
              jax.ShapeDtypeStruct((n, KV_WIDTH), BF16),
              jax.ShapeDtypeStruct((n, D_MODEL), BF16),
              jax.ShapeDtypeStruct((n, D_MODEL), BF16)]
    return specs, shapes


def _proj(x2, g, w, layer):
    n = x2.shape[0]
    cols = w.shape[2]
    row = lambda c: pl.BlockSpec((PROJ_TM, c), lambda i: (i, 0))
    full = lambda a, b: pl.BlockSpec((a, b), lambda i: (0, 0))
    out_specs, out_shape = _proj_out(n)
    return pl.pallas_call(
        _proj_body,
        grid=(n // PROJ_TM,),
        in_specs=[row(D_MODEL), full(1, D_MODEL), _layer_spec(layer, D_MODEL, cols)],
        out_specs=out_specs,
        out_shape=out_shape,
        compiler_params=_cparams(("arbitrary",)),
        name="proj",
    )(x2, g, w)


def _proj_moe(starts, lens, x1, route, ys, g, w, layer):
    n = x1.shape[0]
    cols = w.shape[2]
    subs = PROJ_TM // SORT_TM
    row = lambda c: pl.BlockSpec((PROJ_TM, c), lambda i, *_: (i, 0))
    out_specs, out_shape = _proj_out(n)
    return pl.pallas_call(
        _proj_moe_body,
        grid_spec=pltpu.PrefetchScalarGridSpec(
            num_scalar_prefetch=2,
            grid=(n // PROJ_TM,),
            in_specs=[row(D_MODEL), row(ROUTER_LANES), pl.BlockSpec(memory_space=pl.ANY),
                      pl.BlockSpec((1, D_MODEL), lambda i, *_: (0, 0)),
                      _layer_spec(layer, D_MODEL, cols)],
            out_specs=[row(D_MODEL)] + out_specs,
            scratch_shapes=[pltpu.VMEM((2 * PROJ_TM * TOK_ROWS, LANES), F32),
                            pltpu.SemaphoreType.DMA((2 * subs, N_GROUPS))],
        ),
        out_shape=[jax.ShapeDtypeStruct((n, D_MODEL), F32)] + out_shape,
        compiler_params=_cparams(("arbitrary",)),
        name="proj_moe",
    )(starts, lens, x1, route, ys, g, w)


S5_TT = 64
S5_PAIRS = S5_TT // 2


def _s5_body(u_ref, bblk_ref, cblk_ref, k0_ref, lam_ref, d_ref, z_ref,
             ut_ref, y_ref, st_ref, carry_ref, *, bsz):
    rows = S5_TT * bsz
    prow = S5_PAIRS * bsz

    @pl.when(pl.program_id(0) == 0)
    def _():
        carry_ref[...] = jnp.zeros_like(carry_ref)

    for b in range(bsz):
        for j in range(SLABS):
            ut_ref[j, pl.ds(b, S5_TT, stride=bsz), :] = u_ref[b, :, j * LANES:(j + 1) * LANES]

    for j in range(SLABS):
        u3 = ut_ref[j].reshape(S5_PAIRS, 2 * bsz, LANES)
        u0 = u3[:, :bsz, :].reshape(prow, LANES)
        u1 = u3[:, bsz:, :].reshape(prow, LANES)
        lhs = jnp.concatenate([u1, u0], axis=1).astype(BF16)
        st_ref[j, 0:bsz, :] = carry_ref[j]
        st_ref[j, bsz:, :] = jnp.dot(lhs, bblk_ref[j], preferred_element_type=F32)
        a = jnp.broadcast_to(lam_ref[j, 0:1, :], (bsz, SLAB_STATE))
        bb = jnp.broadcast_to(lam_ref[j, 1:2, :], (bsz, SLAB_STATE))

        cre = carry_ref[j, :, 0:SLAB_STATE]
        cim = carry_ref[j, :, SLAB_STATE:2 * SLAB_STATE]
        for k in range(S5_PAIRS):
            r0 = (k + 1) * bsz
            bre = st_ref[j, r0:r0 + bsz, 0:SLAB_STATE]
            bim = st_ref[j, r0:r0 + bsz, SLAB_STATE:2 * SLAB_STATE]
            cre, cim = a * cre - bb * cim + bre, a * cim + bb * cre + bim
            st_ref[j, r0:r0 + bsz, 0:SLAB_STATE] = cre
            st_ref[j, r0:r0 + bsz, SLAB_STATE:2 * SLAB_STATE] = cim
        carry_ref[j, :, 0:SLAB_STATE] = cre
        carry_ref[j, :, SLAB_STATE:2 * SLAB_STATE] = cim

        cs = jnp.dot(st_ref[j].astype(BF16), cblk_ref[j], preferred_element_type=F32)
        y1 = cs[bsz:, :LANES] + d_ref[j] * u1
        y0 = (cs[:prow, LANES:] + jnp.dot(u0.astype(BF16), k0_ref[j], preferred_element_type=F32)
              + d_ref[j] * u0)
        y = jnp.concatenate([y0.reshape(S5_PAIRS, bsz, LANES), y1.reshape(S5_PAIRS, bsz, LANES)],
                            axis=1).reshape(rows, LANES)
        y_ref[j] = jax.nn.gelu(y)

    for b in range(bsz):
        for j in range(SLABS):
            z_ref[b, :, j * LANES:(j + 1) * LANES] = (
                y_ref[j, pl.ds(b, S5_TT, stride=bsz), :].astype(BF16))


def _s5(u3, bblk, cblk, k0, lam, dskip):
    bsz, seq, _ = u3.shape
    rows = S5_TT * bsz
    full = lambda *s: pl.BlockSpec(s, lambda i: (0,) * len(s))
    return pl.pallas_call(
        functools.partial(_s5_body, bsz=bsz),
        grid=(seq // S5_TT,),
        in_specs=[pl.BlockSpec((bsz, S5_TT, SSM_WIDTH), lambda i: (0, i, 0)),
                  full(SLABS, 2 * LANES, 2 * SLAB_STATE),
                  full(SLABS, 2 * SLAB_STATE, 2 * LANES),
                  full(SLABS, LANES, LANES),
                  full(SLABS, 2, SLAB_STATE),
                  full(SLABS, 1, LANES)],
        out_specs=pl.BlockSpec((bsz, S5_TT, SSM_WIDTH), lambda i: (0, i, 0)),
        out_shape=jax.ShapeDtypeStruct((bsz, seq, SSM_WIDTH), BF16),
        scratch_shapes=[pltpu.VMEM((SLABS, rows, LANES), F32),
                        pltpu.VMEM((SLABS, rows, LANES), F32),
                        pltpu.VMEM((SLABS, bsz + S5_PAIRS * bsz, 2 * SLAB_STATE), F32),
                        pltpu.VMEM((SLABS, bsz, 2 * SLAB_STATE), F32)],
        compiler_params=_cparams(("arbitrary",)),
        name="s5",
    )(u3, bblk, cblk, k0, lam, dskip)


def _s5_params(lam_re, lam_im, log_dt, b_re, b_im, c_re, c_im, d_skip):
    lam = lax.complex(lam_re.astype(F32), lam_im.astype(F32))
    dt = jnp.exp(log_dt.astype(F32))[:, None]
    lam_bar = jnp.exp(lam * dt)
    b = lax.complex(b_re.astype(F32), b_im.astype(F32))
    b_bar = ((lam_bar - 1.0) / lam)[..., None] * b

    def block_diag(blocks):
        _, g, r, c = blocks.shape
        mask = np.kron(np.eye(g, dtype=np.float32), np.ones((r, c), np.float32))
        return jnp.tile(blocks.reshape(SLABS, g * r, c), (1, 1, g)) * mask

    def in_blk(part):
        p4 = part.reshape(SLABS, GROUPS_PER_SLAB, SSM_STATE, SSM_GROUP)
        return block_diag(p4.transpose(0, 1, 3, 2))

    def in_cplx(z):
        return jnp.concatenate([in_blk(z.real), in_blk(z.imag)], axis=-1)

    bblk = jnp.concatenate([in_cplx(b_bar), in_cplx(lam_bar[..., None] * b_bar)],
                           axis=1).astype(BF16)

    def out_blk(part):
        p4 = part.reshape(SLABS, GROUPS_PER_SLAB, SSM_GROUP, SSM_STATE)
        return block_diag(p4.transpose(0, 1, 3, 2))

    def out_cplx(z):
        return jnp.concatenate([out_blk(z.real), -out_blk(z.imag)], axis=1)

    c = lax.complex(c_re.astype(F32), c_im.astype(F32))
    cblk = jnp.concatenate([out_cplx(c), out_cplx(c * lam_bar[:, None, :])], axis=-1).astype(BF16)
    k0 = jnp.einsum('ghp,gpk->gkh', c, b_bar).real.reshape(SLABS, GROUPS_PER_SLAB, SSM_GROUP,
                                                           SSM_GROUP)
    k0 = block_diag(k0).astype(BF16)
    lam2 = lam_bar * lam_bar
    lam_k = jnp.stack([lam2.real.reshape(SLABS, SLAB_STATE),
                       lam2.imag.reshape(SLABS, SLAB_STATE)], axis=1)
    dsk = d_skip.astype(F32).reshape(SLABS, 1, LANES)
    return bblk, cblk, k0, lam_k, dsk


ATTN_QB = 16
ATTN_STACK = 16
HEADS_PER_TILE = LANES // HEAD_DIM


def _pair_norm(t, gain2, head_mean):
    ms = jnp.dot((t * t).astype(BF16), head_mean, preferred_element_type=F32)
    return t * lax.rsqrt(ms + EPS) * gain2


def _attn_body(sink_ref, q_ref, kc_ref, kp_ref, vc_ref, vp_ref, qg_ref, kg_ref, o_ref):
    n = pl.program_id(1)
    lane = lax.broadcasted_iota(jnp.int32, (1, LANES), 1)
    lo = lane < HEAD_DIM
    li = lax.broadcasted_iota(jnp.int32, (LANES, LANES), 0)
    lj = lax.broadcasted_iota(jnp.int32, (LANES, LANES), 1)
    head_mean = jnp.where((li < HEAD_DIM) == (lj < HEAD_DIM), 1.0 / HEAD_DIM, 0.0).astype(BF16)

    kall = jnp.concatenate([kp_ref[0], kc_ref[0]], axis=0).astype(F32)
    kn = _pair_norm(kall, kg_ref[...], head_mean)
    kroll = pltpu.roll(kn, HEAD_DIM, axis=1)
    vall = jnp.concatenate([vp_ref[0], vc_ref[0]], axis=0).astype(F32)
    vroll = pltpu.roll(vall, HEAD_DIM, axis=1)
    zero = jnp.zeros_like(kn)
    one = jnp.ones_like(vall)
    k_var = [[jnp.where(lo, kn, zero).astype(BF16), jnp.where(lo, zero, kroll).astype(BF16)],
             [jnp.where(lo, kroll, zero).astype(BF16), jnp.where(lo, zero, kn).astype(BF16)]]
    v_var = [[jnp.where(lo, vall, one).astype(BF16), jnp.where(lo, one, vroll).astype(BF16)],
             [jnp.where(lo, vroll, one).astype(BF16), jnp.where(lo, one, vall).astype(BF16)]]

    qn = []
    for i in range(ATTN_WIDTH // LANES):
        t = _pair_norm(q_ref[0, :, i * LANES:(i + 1) * LANES].astype(F32), qg_ref[...], head_mean)
        qn.append((t * (HEAD_DIM ** -0.5)).astype(BF16))

    qi = lax.broadcasted_iota(jnp.int32, (BLOCK, BLOCK), 0)
    si = lax.broadcasted_iota(jnp.int32, (BLOCK, BLOCK), 1)
    is_cur = si <= qi
    dist = jnp.where(is_cur, qi - si, qi - si + BLOCK).astype(F32)
    alibi = [(2.0 ** (-8.0 * (h + 1) / N_HEADS)) * dist for h in range(N_HEADS)]
    sink_tile = [jnp.full((BLOCK, LANES), sink_ref[h], F32) for h in range(N_HEADS)]

    units = [(b, h) for b in range(ATTN_QB) for h in range(N_HEADS)]
    for u0 in range(0, len(units), ATTN_STACK):
        chunk = units[u0:u0 + ATTN_STACK]
        s_all = []
        for b, h in chunk:
            i, slot = divmod(h, HEADS_PER_TILE)
            kvh = h // (N_HEADS // N_KV_HEADS)
            win = slice(b * BLOCK, (b + 2) * BLOCK)
            sc = lax.dot_general(qn[i][b * BLOCK:(b + 1) * BLOCK], k_var[kvh][slot][win],
                                 (((1,), (1,)), ((), ())), preferred_element_type=F32)
            prev = sc[:, :BLOCK]
            if b == 0:
                prev = jnp.where(n > 0, prev, NEG)
            s_all.append(jnp.where(is_cur, sc[:, BLOCK:], prev) - alibi[h])
        s = jnp.concatenate(s_all, axis=0)
        sink_rows = jnp.concatenate([sink_tile[h] for _, h in chunk], axis=0)
        m = jnp.maximum(jnp.max(s, axis=-1, keepdims=True), sink_rows)
        p = jnp.exp(s - m).astype(BF16)
        esink = jnp.exp(sink_rows - m)
        for k in range(0, len(chunk), HEADS_PER_TILE):
            b, h_even = chunk[k]
            i = h_even // HEADS_PER_TILE
            win = slice(b * BLOCK, (b + 2) * BLOCK)
            halves = []
            for slot in range(HEADS_PER_TILE):
                kvh = (h_even + slot) // (N_HEADS // N_KV_HEADS)
                hr = slice((k + slot) * BLOCK, (k + slot + 1) * BLOCK)
                pcat = jnp.concatenate([jnp.where(is_cur, 0.0, p[hr]).astype(BF16),
                                        jnp.where(is_cur, p[hr], 0.0).astype(BF16)], axis=1)
                pv = jnp.dot(pcat, v_var[kvh][slot][win], preferred_element_type=F32)
                halves.append(pv / (pltpu.roll(pv, HEAD_DIM, axis=1) + esink[hr]))
            o_ref[0, b * BLOCK:(b + 1) * BLOCK, i * LANES:(i + 1) * LANES] = jnp.where(
                lo, halves[0], halves[1]).astype(BF16)


def _attn(q3, k3, v3, q_gain, k_gain, sinks):
    bsz, seq, _ = q3.shape
    qrows = ATTN_QB * BLOCK
    cur = lambda b, n: (b, n, 0)
    prev = lambda b, n: (b, jnp.maximum(n * ATTN_QB - 1, 0), 0)
    qg2 = jnp.tile(q_gain.astype(F32), HEADS_PER_TILE).reshape(1, LANES)
    kg2 = jnp.tile(k_gain.astype(F32), HEADS_PER_TILE).reshape(1, LANES)
    gain_spec = pl.BlockSpec((1, LANES), lambda b, n: (0, 0))
    return pl.pallas_call(
        _attn_body,
        grid=(bsz, seq // qrows),
        in_specs=[pl.BlockSpec(memory_space=pltpu.SMEM),
                  pl.BlockSpec((1, qrows, ATTN_WIDTH), cur),
                  pl.BlockSpec((1, qrows, KV_WIDTH), cur),
                  pl.BlockSpec((1, BLOCK, KV_WIDTH), prev),
                  pl.BlockSpec((1, qrows, KV_WIDTH), cur),
                  pl.BlockSpec((1, BLOCK, KV_WIDTH), prev),
                  gain_spec, gain_spec],
        out_specs=pl.BlockSpec((1, qrows, ATTN_WIDTH), cur),
        out_shape=jax.ShapeDtypeStruct((bsz, seq, ATTN_WIDTH), BF16),
        compiler_params=_cparams(("arbitrary", "arbitrary")),
        name="attn",
    )(sinks.astype(F32), q3, k3, k3, v3, v3, qg2, kg2)


MIX_TM = 1024


def _mix_body(z_ref, ya_ref, gs_ref, ga_ref, x_ref, wglu_ref, wup_ref, wout_ref, g2_ref, wr_ref,
              x1_ref, gate_ref, hs_hbm, starts_ref, lens_ref, counts_ref,
              stage_ref, run_ref, pend_ref, sem):
    t = pl.program_id(0)
    sorter = _GroupSorter(t, MIX_TM // SORT_TM, hs_hbm, starts_ref, lens_ref, counts_ref,
                          stage_ref, run_ref, pend_ref, sem)

    z = z_ref[...]
    bv = jnp.dot(z, wglu_ref[:, :D_MODEL], preferred_element_type=F32)
    bg = jnp.dot(z, wglu_ref[:, D_MODEL:], preferred_element_type=F32)
    bs = bv * jax.nn.sigmoid(bg)
    ba = jnp.dot(ya_ref[...], wup_ref[...], preferred_element_type=F32)
    merged = gs_ref[...].astype(F32) * bs + ga_ref[...].astype(F32) * ba
    x1 = x_ref[...] + jnp.dot(merged.astype(BF16), wout_ref[...], preferred_element_type=F32)
    x1_ref[...] = x1

    ms = jnp.mean(x1 * x1, axis=-1, keepdims=True)
    h2 = x1 * lax.rsqrt(ms + EPS) * g2_ref[...]
    h2b = h2.astype(BF16)

    lt = lax.dot_general(wr_ref[...], h2b, (((1,), (1,)), ((), ())),
                         preferred_element_type=F32)
    logits = lt[:ROUTER_ROWS] + lt[ROUTER_ROWS:]
    row = lax.broadcasted_iota(jnp.int32, logits.shape, 0)
    ninf = -jnp.inf
    cm = jnp.where(row < N_GROUPS, logits, ninf)
    cmax = jnp.max(cm, axis=0, keepdims=True)
    g_prob = 1.0 / jnp.sum(jnp.exp(cm - cmax), axis=0, keepdims=True)
    g_idx = jnp.min(jnp.where(cm == cmax, row, ROUTER_ROWS), axis=0, keepdims=True)
    f0 = FINE_OFF + EXPERTS_PER_GROUP * g_idx
    fm = jnp.where((row >= f0) & (row < f0 + EXPERTS_PER_GROUP), logits, ninf)
    v1 = jnp.max(fm, axis=0, keepdims=True)
    i1 = jnp.min(jnp.where(fm == v1, row, ROUTER_ROWS), axis=0, keepdims=True)
    fm2 = jnp.where(row == i1, ninf, fm)
    v2 = jnp.max(fm2, axis=0, keepdims=True)
    i2 = jnp.min(jnp.where(fm2 == v2, row, ROUTER_ROWS), axis=0, keepdims=True)
    e21 = jnp.exp(v2 - v1)
    w1 = g_prob / (1.0 + e21)
    w2 = w1 * e21
    rt = jnp.where(row == i1 - f0, w1,
                   jnp.where(row == i2 - f0, w2,
                             jnp.where(row == ROUTE_GROUP_OFF + g_idx, 1.0, 0.0)))
    rt = jnp.concatenate([rt, jnp.zeros((ROUTER_LANES - ROUTER_ROWS, rt.shape[1]), F32)], axis=0)
    route = rt.T
    gate_ref[...] = route
    sorter.begin()
    sorter.issue(sorter.stage(h2b, route))

    @pl.when(t == pl.num_programs(0) - 1)
    def _():
        sorter.finish()


def _mix(z2, ya2, gs2, ga2, x2, wglu, wup, wout, g2, wr, layer):
    n = x2.shape[0]
    n_sort = n // SORT_TM
    subs = MIX_TM // SORT_TM
    cap = _region_cap(n)
    assert subs * SORT_TM >= ZFILL_TOK
    row = lambda c: pl.BlockSpec((MIX_TM, c), lambda i: (i, 0))
    full = lambda a, b: pl.BlockSpec((a, b), lambda i: (0, 0))
    smem = pl.BlockSpec(memory_space=pltpu.SMEM)
    return pl.pallas_call(
        _mix_body,
        grid=(n // MIX_TM,),
        in_specs=[row(SSM_WIDTH), row(ATTN_WIDTH), row(D_MODEL), row(D_MODEL), row(D_MODEL),
                  _layer_spec(layer, SSM_WIDTH, 2 * D_MODEL), _layer_spec(layer, ATTN_WIDTH, D_MODEL),
                  _layer_spec(layer, D_MODEL, D_MODEL), full(1, D_MODEL),
                  _layer_spec(layer, 2 * ROUTER_ROWS, D_MODEL)],
        out_specs=[row(D_MODEL), row(ROUTER_LANES), pl.BlockSpec(memory_space=pl.ANY),
                   smem, smem, smem],
        out_shape=[jax.ShapeDtypeStruct((n, D_MODEL), F32),
                   jax.ShapeDtypeStruct((n, ROUTER_LANES), F32),
                   jax.ShapeDtypeStruct((N_GROUPS * cap * TOK_ROWS, LANES), jnp.uint32),
                   jax.ShapeDtypeStruct((n_sort * N_GROUPS,), jnp.int32),
                   jax.ShapeDtypeStruct((n_sort * N_GROUPS,), jnp.int32),
                   jax.ShapeDtypeStruct((N_GROUPS,), jnp.int32)],
        scratch_shapes=[pltpu.VMEM((subs * SORT_TM * TOK_ROWS, LANES), jnp.uint32),
                        pltpu.SMEM((N_GROUPS,), jnp.int32),
                        pltpu.SMEM((subs * N_GROUPS,), jnp.int32),
                        pltpu.SemaphoreType.DMA((subs, N_GROUPS))],
        compiler_params=_cparams(("arbitrary",)),
        name="mix",
    )(z2, ya2, gs2, ga2, x2, wglu, wup, wout, g2, wr)


SORT_TM = 256
COMBINE_TM = 1024
MOE_TM = 512
TOK_ROWS = 8
H_WORDS = D_MODEL // 2
H_CHUNKS = H_WORDS // LANES
ROUTE_ROW = H_CHUNKS
ZFILL_TOK = MOE_TM
HI_MASK = 0xFFFF0000


def _region_cap(n):
    cap = n + ZFILL_TOK
    return -(-cap // MOE_TM) * MOE_TM


def _tile_positions(route):
    tm = route.shape[0]
    lane = lax.broadcasted_iota(jnp.int32, route.shape, 1)
    onehot = jnp.where((lane >= ROUTE_GROUP_OFF) & (lane < ROUTE_GROUP_OFF + N_GROUPS), route, 0.0)
    ci = lax.broadcasted_iota(jnp.int32, (tm, tm), 0)
    cj = lax.broadcasted_iota(jnp.int32, (tm, tm), 1)
    earlier = (cj < ci).astype(BF16)
    rank = jnp.dot(earlier, onehot.astype(BF16), preferred_element_type=F32)
    cnt = jnp.sum(onehot, axis=0, keepdims=True)
    li = lax.broadcasted_iota(jnp.int32, (LANES, LANES), 0)
    lj = lax.broadcasted_iota(jnp.int32, (LANES, LANES), 1)
    base = jnp.dot(jnp.broadcast_to(cnt, (8, LANES)).astype(BF16), (li < lj).astype(BF16),
                   preferred_element_type=F32)[0:1]
    return onehot, rank, base, cnt


class _GroupSorter:
    def __init__(self, t, subs, hs_hbm, starts_ref, lens_ref, counts_ref, stage_ref, run_ref,
                 pend_ref, sem):
        self.t, self.subs = t, subs
        self.hs_hbm, self.starts_ref, self.lens_ref, self.counts_ref = (
            hs_hbm, starts_ref, lens_ref, counts_ref)
        self.stage_ref, self.run_ref, self.pend_ref, self.sem = stage_ref, run_ref, pend_ref, sem
        self.slot_rows = SORT_TM * TOK_ROWS
        self.cap = hs_hbm.shape[0] // (N_GROUPS * TOK_ROWS)
        self.set0 = 0

    def _copy(self, src_row, dst_row, slot, g, tokens):
        return pltpu.make_async_copy(
            self.stage_ref.at[pl.ds(src_row, tokens * TOK_ROWS), :],
            self.hs_hbm.at[pl.ds(dst_row, tokens * TOK_ROWS), :], self.sem.at[slot, g])

    def _wait_set(self, set0):
        for sl in range(self.subs):
            for g in range(N_GROUPS):
                n = self.pend_ref[(set0 + sl) * N_GROUPS + g]

                @pl.when(n > 0)
                def _(sl=sl, g=g, n=n):
                    self._copy(0, 0, set0 + sl, g, n).wait()

    def begin(self):
        @pl.when(self.t == 0)
        def _():
            self.stage_ref[...] = jnp.zeros_like(self.stage_ref)
            for g in range(N_GROUPS):
                self.run_ref[g] = 0

        @pl.when(self.t > 0)
        def _():
            self._wait_set(self.set0)

    def stage(self, h, route):
        tm = SORT_TM
        runs = []
        for sl in range(self.subs):
            r = route[sl * tm:(sl + 1) * tm]
            onehot, rank, base, cnt = _tile_positions(r)
            w = (onehot * (rank + base)).astype(BF16)
            pos_row = lax.dot_general(jnp.ones((8, LANES), BF16), w, (((1,), (1,)), ((), ())),
                                      preferred_element_type=F32)[0:1]
            ri = lax.broadcasted_iota(jnp.int32, (tm, tm), 0).astype(F32)
            perm = (ri == pos_row).astype(BF16)
            r_hi = r.astype(BF16)
            r_mid = (r - r_hi.astype(F32)).astype(BF16)
            r_lo = (r - r_hi.astype(F32) - r_mid.astype(F32)).astype(BF16)
            sorted_all = jnp.dot(
                perm,
                jnp.concatenate([h[sl * tm:(sl + 1) * tm], r_hi, r_mid, r_lo], axis=1),
                preferred_element_type=F32)
            sh = sorted_all[:, :D_MODEL]
            sr = (sorted_all[:, D_MODEL:D_MODEL + LANES]
                  + sorted_all[:, D_MODEL + LANES:D_MODEL + 2 * LANES]
                  + sorted_all[:, D_MODEL + 2 * LANES:])
            hb = lax.bitcast_convert_type(sh, jnp.uint32)
            words = (hb[:, :H_WORDS] & jnp.uint32(HI_MASK)) | (hb[:, H_WORDS:] >> 16)

            slot0 = (self.set0 + sl) * self.slot_rows
            for s in range(H_CHUNKS):
                self.stage_ref[pl.ds(slot0 + s, tm, stride=TOK_ROWS), :] = (
                    words[:, s * LANES:(s + 1) * LANES])
            self.stage_ref[pl.ds(slot0 + ROUTE_ROW, tm, stride=TOK_ROWS), :] = (
                lax.bitcast_convert_type(sr, jnp.uint32))
            for g in range(N_GROUPS):
                runs.append((sl, g, slot0,
                             cnt[0, ROUTE_GROUP_OFF + g].astype(jnp.int32),
                             base[0, ROUTE_GROUP_OFF + g].astype(jnp.int32)))
        return runs

    def issue(self, runs):
        first_tile = self.t * self.subs
        for sl, g, slot0, c_g, b_g in runs:
            start = self.run_ref[g]
            self.starts_ref[(first_tile + sl) * N_GROUPS + g] = start
            self.lens_ref[(first_tile + sl) * N_GROUPS + g] = c_g
            self.pend_ref[(self.set0 + sl) * N_GROUPS + g] = c_g
            self.run_ref[g] = start + c_g
            src = pl.multiple_of(slot0 + b_g * TOK_ROWS, TOK_ROWS)
            dst = pl.multiple_of((g * self.cap + start) * TOK_ROWS, TOK_ROWS)

            @pl.when(c_g > 0)
            def _(sl=sl, g=g, src=src, dst=dst, c_g=c_g):
                self._copy(src, dst, self.set0 + sl, g, c_g).start()

    def finish(self):
        self._wait_set(self.set0)
        self.stage_ref[...] = jnp.zeros_like(self.stage_ref)
        zrows = ZFILL_TOK * TOK_ROWS
        for g in range(N_GROUPS):
            total = self.run_ref[g]
            self.counts_ref[g] = total
            dst = pl.multiple_of((g * self.cap + total) * TOK_ROWS, TOK_ROWS)
            pltpu.make_async_copy(self.stage_ref.at[pl.ds(0, zrows), :],
                                  self.hs_hbm.at[pl.ds(dst, zrows), :], self.sem.at[0, g]).start()
        for g in range(N_GROUPS):
            pltpu.make_async_copy(self.stage_ref.at[pl.ds(0, zrows), :],
                                  self.hs_hbm.at[pl.ds(0, zrows), :], self.sem.at[0, g]).wait()


def _experts_body(blk_ref, grp_ref, valid_ref, hs_ref, wu32_ref, wd32_ref, ys_ref, wu_ref, wd_ref):
    t = pl.program_id(0)
    tm = MOE_TM

    @pl.when((t == 0) | (grp_ref[t] != grp_ref[jnp.maximum(t - 1, 0)]))
    def _():
        wu_ref[...] = wu32_ref[...].astype(BF16)
        wd_ref[...] = wd32_ref[...].astype(BF16)

    @pl.when(valid_ref[t] > 0)
    def _():
        chunks = [hs_ref[pl.ds(s, tm, stride=TOK_ROWS), :] for s in range(H_CHUNKS + 1)]
        hi = [lax.bitcast_convert_type(c & jnp.uint32(HI_MASK), F32) for c in chunks[:H_CHUNKS]]
        lo = [lax.bitcast_convert_type(c << 16, F32) for c in chunks[:H_CHUNKS]]
        h = jnp.concatenate(hi + lo, axis=1).astype(BF16)
        route = lax.bitcast_convert_type(chunks[ROUTE_ROW], F32)
        acts = []
        for e in range(EXPERTS_PER_GROUP):
            gu = jnp.dot(h, wu_ref[e], preferred_element_type=F32)
            acts.append((jax.nn.silu(gu[:, :D_EXPERT]) * gu[:, D_EXPERT:]
                         * route[:, e:e + 1]).astype(BF16))
        act = jnp.concatenate(acts, axis=1)
        wd = wd_ref[...].reshape(EXPERTS_PER_GROUP * D_EXPERT, D_MODEL)
        out = jnp.dot(act, wd, preferred_element_type=F32)
        for j in range(D_MODEL // LANES):
            ys_ref[pl.ds(j, tm, stride=TOK_ROWS), :] = out[:, j * LANES:(j + 1) * LANES]


def _experts(blk, grp, valid, hs, wu, wd, layer):
    g0 = layer * N_GROUPS
    n_tiles = blk.shape[0]
    epg = EXPERTS_PER_GROUP
    return pl.pallas_call(
        _experts_body,
        grid_spec=pltpu.PrefetchScalarGridSpec(
            num_scalar_prefetch=3,
            grid=(n_tiles,),
            in_specs=[pl.BlockSpec((MOE_TM * TOK_ROWS, LANES), lambda t, b, g, v: (b[t], 0)),
                      pl.BlockSpec((epg, D_MODEL, 2 * D_EXPERT),
                                   lambda t, b, g, v: (g0 + g[t], 0, 0)),
                      pl.BlockSpec((epg, D_EXPERT, D_MODEL),
                                   lambda t, b, g, v: (g0 + g[t], 0, 0))],
            out_specs=pl.BlockSpec((MOE_TM * TOK_ROWS, LANES), lambda t, b, g, v: (b[t], 0)),
            scratch_shapes=[pltpu.VMEM((epg, D_MODEL, 2 * D_EXPERT), BF16),
                            pltpu.VMEM((epg, D_EXPERT, D_MODEL), BF16)],
        ),
        out_shape=jax.ShapeDtypeStruct(hs.shape, F32),
        compiler_params=_cparams(("arbitrary",)),
        name="moe_experts",
    )(blk, grp, valid, hs, wu, wd)


def _moe_outputs(starts_ref, lens_ref, route_ref, ys_hbm, buf_ref, sem):
    t = pl.program_id(0)
    nt = pl.num_programs(0)
    tm = SORT_TM
    subs = route_ref.shape[0] // tm
    slot_rows = tm * TOK_ROWS
    cap = ys_hbm.shape[0] // (N_GROUPS * TOK_ROWS)

    def fetches(step, sl):
        tile = step * subs + sl
        slot = (step % 2) * subs + sl
        copies = []
        first = 0
        for g in range(N_GROUPS):
            tokens = lens_ref[tile * N_GROUPS + g]
            src = pl.multiple_of((g * cap + starts_ref[tile * N_GROUPS + g]) * TOK_ROWS, TOK_ROWS)
            dst = pl.multiple_of(slot * slot_rows + first * TOK_ROWS, TOK_ROWS)
            copies.append((tokens, pltpu.make_async_copy(
                ys_hbm.at[pl.ds(src, tokens * TOK_ROWS), :],
                buf_ref.at[pl.ds(dst, tokens * TOK_ROWS), :], sem.at[slot, g])))
            first = first + tokens
        return copies

    def start_fetches(step):
        for sl in range(subs):
            for tokens, copy in fetches(step, sl):
                @pl.when(tokens > 0)
                def _(copy=copy):
                    copy.start()

    @pl.when(t == 0)
    def _():
        start_fetches(0)

    @pl.when(t + 1 < nt)
    def _():
        start_fetches(t + 1)

    for sl in range(subs):
        for tokens, copy in fetches(t, sl):
            @pl.when(tokens > 0)
            def _(copy=copy):
                copy.wait()

    outs = []
    for sl in range(subs):
        onehot, rank, base, _ = _tile_positions(route_ref[sl * tm:(sl + 1) * tm, :])
        pos = jnp.sum(onehot * (rank + base), axis=1, keepdims=True)
        ri = lax.broadcasted_iota(jnp.int32, (tm, tm), 1).astype(F32)
        sel = (ri == pos).astype(BF16)
        r0 = pl.multiple_of(((t % 2) * subs + sl) * slot_rows, TOK_ROWS)
        y = jnp.concatenate(
            [buf_ref[pl.ds(r0 + j, tm, stride=TOK_ROWS), :] for j in range(D_MODEL // LANES)],
            axis=1).astype(BF16)
        outs.append(jnp.dot(sel, y, preferred_element_type=F32))
    return outs


def _combine_body(starts_ref, lens_ref, x1_ref, route_ref, ys_hbm, o_ref, buf_ref, sem):
    outs = _moe_outputs(starts_ref, lens_ref, route_ref, ys_hbm, buf_ref, sem)
    for sl, y in enumerate(outs):
        rows = slice(sl * SORT_TM, (sl + 1) * SORT_TM)
        o_ref[rows, :] = x1_ref[rows, :] + y


def _combine(starts, lens, x1, route, ys):
    n = x1.shape[0]
    subs = COMBINE_TM // SORT_TM
    return pl.pallas_call(
        _combine_body,
        grid_spec=pltpu.PrefetchScalarGridSpec(
            num_scalar_prefetch=2,
            grid=(n // COMBINE_TM,),
            in_specs=[pl.BlockSpec((COMBINE_TM, D_MODEL), lambda i, s, c: (i, 0)),
                      pl.BlockSpec((COMBINE_TM, ROUTER_LANES), lambda i, s, c: (i, 0)),
                      pl.BlockSpec(memory_space=pl.ANY)],
            out_specs=pl.BlockSpec((COMBINE_TM, D_MODEL), lambda i, s, c: (i, 0)),
            scratch_shapes=[pltpu.VMEM((2 * COMBINE_TM * TOK_ROWS, LANES), F32),
                            pltpu.SemaphoreType.DMA((2 * subs, N_GROUPS))],
        ),
        out_shape=jax.ShapeDtypeStruct((n, D_MODEL), F32),
        compiler_params=_cparams(("arbitrary",)),
        name="moe_combine",
    )(starts, lens, x1, route, ys)


def _expert_tiles(counts, n):
    cap_blocks = _region_cap(n) // MOE_TM
    n_tiles = n // MOE_TM + N_GROUPS
    per_group = (counts + MOE_TM - 1) // MOE_TM
    ends = jnp.cumsum(per_group)
    total = ends[-1]
    t = jnp.minimum(jnp.arange(n_tiles, dtype=jnp.int32), total - 1)
    grp = jnp.sum((t[:, None] >= ends[None, :]).astype(jnp.int32), axis=1)
    first = ends - per_group
    blk = grp * cap_blocks + (t - first[grp])
    valid = (jnp.arange(n_tiles, dtype=jnp.int32) < total).astype(jnp.int32)
    return blk.astype(jnp.int32), grp.astype(jnp.int32), valid


def _moe_sorted(n, hs, counts, w_up, w_down, layer):
    blk, grp, valid = _expert_tiles(counts, n)
    return _experts(blk, grp, valid, hs, w_up, w_down, layer)


def kernel(x, norm_mix, w_in, ssm_lam_re, ssm_lam_im, ssm_log_dt, ssm_b_re, ssm_b_im, ssm_c_re,
           ssm_c_im, ssm_d, w_ssm_glu_val, w_ssm_glu_gate, q_norm, k_norm, attn_sinks, w_attn_up,
           w_out, norm_ffn, w_coarse, w_fine, w_expert_up, w_expert_down):
    bsz, seq, d = x.shape
    n = bsz * seq
    depth = w_in.shape[0]
    x2 = x.reshape(n, d)
    s5p = jax.vmap(_s5_params)(ssm_lam_re, ssm_lam_im, ssm_log_dt, ssm_b_re, ssm_b_im, ssm_c_re,
                               ssm_c_im, ssm_d)
    w_in_b = w_in.astype(BF16)
    wglu = jnp.concatenate([w_ssm_glu_val, w_ssm_glu_gate], axis=2).astype(BF16)
    wup_b = w_attn_up.astype(BF16)
    wout_b = w_out.astype(BF16)
    wr_t = jnp.concatenate(
        [jnp.swapaxes(w_coarse, 1, 2), jnp.swapaxes(w_fine, 1, 2),
         jnp.zeros((depth, ROUTER_ROWS - N_GROUPS - N_EXPERTS, d), w_fine.dtype)], axis=1).astype(F32)
    wr_hi = wr_t.astype(BF16)
    wr = jnp.concatenate([wr_hi, (wr_t - wr_hi.astype(F32)).astype(BF16)], axis=1)
    wu_all = w_expert_up.reshape((depth * N_EXPERTS,) + w_expert_up.shape[2:])
    wd_all = w_expert_down.reshape((depth * N_EXPERTS,) + w_expert_down.shape[2:])
    pending = None
    for i in range(depth):
        if pending is None:
            u, q, k, v, gs, ga = _proj(x2, norm_mix[i].reshape(1, d), w_in_b, i)
        else:
            x2, u, q, k, v, gs, ga = _proj_moe(*pending, norm_mix[i].reshape(1, d), w_in_b, i)
        z = _s5(u.reshape(bsz, seq, SSM_WIDTH), *(p[i] for p in s5p))
        ya = _attn(q.reshape(bsz, seq, ATTN_WIDTH), k.reshape(bsz, seq, KV_WIDTH),
                   v.reshape(bsz, seq, KV_WIDTH), q_norm[i], k_norm[i], attn_sinks[i])
        x1, route, hs, starts, lens, counts = _mix(
            z.reshape(n, SSM_WIDTH), ya.reshape(n, ATTN_WIDTH), gs, ga, x2,
            wglu, wup_b, wout_b, norm_ffn[i].reshape(1, d), wr, i)
        ys = _moe_sorted(n, hs, counts, wu_all, wd_all, i)
        pending = (starts, lens, x1, route, ys)
    return _combine(*pending).reshape(bsz, seq, d)
```

```python
import functools

import jax
import jax.numpy as jnp
import numpy as np
from jax import lax
from jax.experimental import pallas as pl
from jax.experimental.pallas import tpu as pltpu

F32 = jnp.float32
BF16 = jnp.bfloat16

D_MODEL = 1024
SSM_WIDTH = 512
SSM_GROUP = 16
SSM_STATE = 64
N_HEADS = 8
N_KV_HEADS = 2
HEAD_DIM = 64
ATTN_WIDTH = 512
KV_WIDTH = 128
BLOCK = 128
N_GROUPS = 4
EXPERTS_PER_GROUP = 4
N_EXPERTS = 16
D_EXPERT = 256
EPS = 1e-6
NEG = -1e30

LANES = 128
SLABS = SSM_WIDTH // LANES
GROUPS_PER_SLAB = LANES // SSM_GROUP
SLAB_STATE = GROUPS_PER_SLAB * SSM_STATE
ROUTER_LANES = 128
ROUTER_ROWS = 32
FINE_OFF = N_GROUPS
ROUTE_GROUP_OFF = 8

VMEM_LIMIT = 56 * 1024 * 1024


def _cparams(sem):
    return pltpu.CompilerParams(dimension_semantics=sem, vmem_limit_bytes=VMEM_LIMIT)


PROJ_TM = 1024


def _proj_body(x_ref, g_ref, w_ref, *out_refs):
    _project(x_ref[...], g_ref, w_ref, *out_refs)


def _proj_moe_body(starts_ref, lens_ref, x1_ref, route_ref, ys_hbm, g_ref, w_ref,
                   x_ref, *rest):
    out_refs, (buf_ref, sem) = rest[:-2], rest[-2:]
    moe = _moe_outputs(starts_ref, lens_ref, route_ref, ys_hbm, buf_ref, sem)
    x = x1_ref[...] + jnp.concatenate(moe, axis=0)
    x_ref[...] = x
    _project(x, g_ref, w_ref, *out_refs)


def _project(x, g_ref, w_ref, u_ref, q_ref, k_ref, v_ref, gs_ref, ga_ref):
    ms = jnp.mean(x * x, axis=-1, keepdims=True)
    h = (x * lax.rsqrt(ms + EPS) * g_ref[...]).astype(BF16)

    def seg(a, b):
        return jnp.dot(h, w_ref[:, a:b], preferred_element_type=F32)

    o = 0
    u_ref[...] = seg(o, o + SSM_WIDTH)
    o += SSM_WIDTH
    q_ref[...] = seg(o, o + ATTN_WIDTH).astype(BF16)
    o += ATTN_WIDTH
    kv = seg(o, o + 2 * KV_WIDTH).astype(BF16)
    k_ref[...] = kv[:, :KV_WIDTH]
    v_ref[...] = kv[:, KV_WIDTH:]
    o += 2 * KV_WIDTH
    gs_ref[...] = jax.nn.sigmoid(seg(o, o + D_MODEL)).astype(BF16)
    o += D_MODEL
    ga_ref[...] = jax.nn.sigmoid(seg(o, o + D_MODEL)).astype(BF16)


def _layer_spec(layer, *shape):
    return pl.BlockSpec((None,) + shape, lambda *_: (layer,) + (0,) * len(shape))


def _proj_out(n):
    row = lambda c: pl.BlockSpec((PROJ_TM, c), lambda i, *_: (i, 0))
    specs = [row(SSM_WIDTH), row(ATTN_WIDTH), row(KV_WIDTH), row(KV_WIDTH), row(D_MODEL),
             row(D_MODEL)]
    shapes = [jax.ShapeDtypeStruct((n, SSM_WIDTH), F32),
              jax.ShapeDtypeStruct((n, ATTN_WIDTH), BF16),
              jax.ShapeDtypeStruct((n, KV_WIDTH), BF16),
              jax.ShapeDtypeStruct((n, KV_WIDTH), BF16),
              jax.ShapeDtypeStruct((n, D_MODEL), BF16),
              jax.ShapeDtypeStruct((n, D_MODEL), BF16)]
    return specs, shapes


def _proj(x2, g, w, layer):
    n = x2.shape[0]
    cols = w.shape[2]
    row = lambda c: pl.BlockSpec((PROJ_TM, c), lambda i: (i, 0))
    full = lambda a, b: pl.BlockSpec((a, b), lambda i: (0, 0))
    out_specs, out_shape = _proj_out(n)
    return pl.pallas_call(
        _proj_body,
        grid=(n // PROJ_TM,),
        in_specs=[row(D_MODEL), full(1, D_MODEL), _layer_spec(layer, D_MODEL, cols)],
        out_specs=out_specs,
        out_shape=out_shape,
        compiler_params=_cparams(("arbitrary",)),
        name="proj",
    )(x2, g, w)


def _proj_moe(starts, lens, x1, route, ys, g, w, layer):
    n = x1.shape[0]
    cols = w.shape[2]
    subs = PROJ_TM // SORT_TM
    row = lambda c: pl.BlockSpec((PROJ_TM, c), lambda i, *_: (i, 0))
    out_specs, out_shape = _proj_out(n)
    return pl.pallas_call(
        _proj_moe_body,
        grid_spec=pltpu.PrefetchScalarGridSpec(
            num_scalar_prefetch=2,
            grid=(n // PROJ_TM,),
            in_specs=[row(D_MODEL), row(ROUTER_LANES), pl.BlockSpec(memory_space=pl.ANY),
                      pl.BlockSpec((1, D_MODEL), lambda i, *_: (0, 0)),
                      _layer_spec(layer, D_MODEL, cols)],
            out_specs=[row(D_MODEL)] + out_specs,
            scratch_shapes=[pltpu.VMEM((2 * PROJ_TM * TOK_ROWS, LANES), F32),
                            pltpu.SemaphoreType.DMA((2 * subs, N_GROUPS))],
        ),
        out_shape=[jax.ShapeDtypeStruct((n, D_MODEL), F32)] + out_shape,
        compiler_params=_cparams(("arbitrary",)),
        name="proj_moe",
    )(starts, lens, x1, route, ys, g, w)


S5_TT = 128
S5_PAIRS = S5_TT // 2


def _s5_body(u_ref, bblk_ref, cblk_ref, k0_ref, lam_ref, d_ref, z_ref,
             ut_ref, y_ref, st_ref, carry_ref, *, bsz):
    rows = S5_TT * bsz
    prow = S5_PAIRS * bsz

    @pl.when(pl.program_id(0) == 0)
    def _():
        carry_ref[...] = jnp.zeros_like(carry_ref)

    for b in range(bsz):
        for j in range(SLABS):
            ut_ref[j, pl.ds(b, S5_TT, stride=bsz), :] = u_ref[b, :, j * LANES:(j + 1) * LANES]

    for j in range(SLABS):
        u3 = ut_ref[j].reshape(S5_PAIRS, 2 * bsz, LANES)
        u0 = u3[:, :bsz, :].reshape(prow, LANES)
        u1 = u3[:, bsz:, :].reshape(prow, LANES)
        lhs = jnp.concatenate([u1, u0], axis=1).astype(BF16)
        st_ref[j, 0:bsz, :] = carry_ref[j]
        st_ref[j, bsz:, :] = jnp.dot(lhs, bblk_ref[j], preferred_element_type=F32)
        a = jnp.broadcast_to(lam_ref[j, 0:1, :], (bsz, SLAB_STATE))
        bb = jnp.broadcast_to(lam_ref[j, 1:2, :], (bsz, SLAB_STATE))

        cre = carry_ref[j, :, 0:SLAB_STATE]
        cim = carry_ref[j, :, SLAB_STATE:2 * SLAB_STATE]
        for k in range(S5_PAIRS):
            r0 = (k + 1) * bsz
            bre = st_ref[j, r0:r0 + bsz, 0:SLAB_STATE]
            bim = st_ref[j, r0:r0 + bsz, SLAB_STATE:2 * SLAB_STATE]
            cre, cim = a * cre - bb * cim + bre, a * cim + bb * cre + bim
            st_ref[j, r0:r0 + bsz, 0:SLAB_STATE] = cre
            st_ref[j, r0:r0 + bsz, SLAB_STATE:2 * SLAB_STATE] = cim
        carry_ref[j, :, 0:SLAB_STATE] = cre
        carry_ref[j, :, SLAB_STATE:2 * SLAB_STATE] = cim

        cs = jnp.dot(st_ref[j].astype(BF16), cblk_ref[j], preferred_element_type=F32)
        y1 = cs[bsz:, :LANES] + d_ref[j] * u1
        y0 = (cs[:prow, LANES:] + jnp.dot(u0.astype(BF16), k0_ref[j], preferred_element_type=F32)
              + d_ref[j] * u0)
        y = jnp.concatenate([y0.reshape(S5_PAIRS, bsz, LANES), y1.reshape(S5_PAIRS, bsz, LANES)],
                            axis=1).reshape(rows, LANES)
        y_ref[j] = jax.nn.gelu(y)

    for b in range(bsz):
        for j in range(SLABS):
            z_ref[b, :, j * LANES:(j + 1) * LANES] = (
                y_ref[j, pl.ds(b, S5_TT, stride=bsz), :].astype(BF16))


def _s5(u3, bblk, cblk, k0, lam, dskip):
    bsz, seq, _ = u3.shape
    rows = S5_TT * bsz
    full = lambda *s: pl.BlockSpec(s, lambda i: (0,) * len(s))
    return pl.pallas_call(
        functools.partial(_s5_body, bsz=bsz),
        grid=(seq // S5_TT,),
        in_specs=[pl.BlockSpec((bsz, S5_TT, SSM_WIDTH), lambda i: (0, i, 0)),
                  full(SLABS, 2 * LANES, 2 * SLAB_STATE),
                  full(SLABS, 2 * SLAB_STATE, 2 * LANES),
                  full(SLABS, LANES, LANES),
                  full(SLABS, 2, SLAB_STATE),
                  full(SLABS, 1, LANES)],
        out_specs=pl.BlockSpec((bsz, S5_TT, SSM_WIDTH), lambda i: (0, i, 0)),
        out_shape=jax.ShapeDtypeStruct((bsz, seq, SSM_WIDTH), BF16),
        scratch_shapes=[pltpu.VMEM((SLABS, rows, LANES), F32),
                        pltpu.VMEM((SLABS, rows, LANES), F32),
                        pltpu.VMEM((SLABS, bsz + S5_PAIRS * bsz, 2 * SLAB_STATE), F32),
                        pltpu.VMEM((SLABS, bsz, 2 * SLAB_STATE), F32)],
        compiler_params=_cparams(("arbitrary",)),
        name="s5",
    )(u3, bblk, cblk, k0, lam, dskip)


def _s5_params(lam_re, lam_im, log_dt, b_re, b_im, c_re, c_im, d_skip):
    lam = lax.complex(lam_re.astype(F32), lam_im.astype(F32))
    dt = jnp.exp(log_dt.astype(F32))[:, None]
    lam_bar = jnp.exp(lam * dt)
    b = lax.complex(b_re.astype(F32), b_im.astype(F32))
    b_bar = ((lam_bar - 1.0) / lam)[..., None] * b

    def block_diag(blocks):
        _, g, r, c = blocks.shape
        mask = np.kron(np.eye(g, dtype=np.float32), np.ones((r, c), np.float32))
        return jnp.tile(blocks.reshape(SLABS, g * r, c), (1, 1, g)) * mask

    def in_blk(part):
        p4 = part.reshape(SLABS, GROUPS_PER_SLAB, SSM_STATE, SSM_GROUP)
        return block_diag(p4.transpose(0, 1, 3, 2))

    def in_cplx(z):
        return jnp.concatenate([in_blk(z.real), in_blk(z.imag)], axis=-1)

    bblk = jnp.concatenate([in_cplx(b_bar), in_cplx(lam_bar[..., None] * b_bar)],
                           axis=1).astype(BF16)

    def out_blk(part):
        p4 = part.reshape(SLABS, GROUPS_PER_SLAB, SSM_GROUP, SSM_STATE)
        return block_diag(p4.transpose(0, 1, 3, 2))

    def out_cplx(z):
        return jnp.concatenate([out_blk(z.real), -out_blk(z.imag)], axis=1)

    c = lax.complex(c_re.astype(F32), c_im.astype(F32))
    cblk = jnp.concatenate([out_cplx(c), out_cplx(c * lam_bar[:, None, :])], axis=-1).astype(BF16)
    k0 = jnp.einsum('ghp,gpk->gkh', c, b_bar).real.reshape(SLABS, GROUPS_PER_SLAB, SSM_GROUP,
                                                           SSM_GROUP)
    k0 = block_diag(k0).astype(BF16)
    lam2 = lam_bar * lam_bar
    lam_k = jnp.stack([lam2.real.reshape(SLABS, SLAB_STATE),
                       lam2.imag.reshape(SLABS, SLAB_STATE)], axis=1)
    dsk = d_skip.astype(F32).reshape(SLABS, 1, LANES)
    return bblk, cblk, k0, lam_k, dsk


ATTN_QB = 16
ATTN_STACK = 16
HEADS_PER_TILE = LANES // HEAD_DIM


def _pair_norm(t, gain2, head_mean):
    ms = jnp.dot((t * t).astype(BF16), head_mean, preferred_element_type=F32)
    return t * lax.rsqrt(ms + EPS) * gain2


def _attn_body(sink_ref, q_ref, kc_ref, kp_ref, vc_ref, vp_ref, qg_ref, kg_ref, o_ref):
    n = pl.program_id(1)
    lane = lax.broadcasted_iota(jnp.int32, (1, LANES), 1)
    lo = lane < HEAD_DIM
    li = lax.broadcasted_iota(jnp.int32, (LANES, LANES), 0)
    lj = lax.broadcasted_iota(jnp.int32, (LANES, LANES), 1)
    head_mean = jnp.where((li < HEAD_DIM) == (lj < HEAD_DIM), 1.0 / HEAD_DIM, 0.0).astype(BF16)

    kall = jnp.concatenate([kp_ref[0], kc_ref[0]], axis=0).astype(F32)
    kn = _pair_norm(kall, kg_ref[...], head_mean)
    kroll = pltpu.roll(kn, HEAD_DIM, axis=1)
    vall = jnp.concatenate([vp_ref[0], vc_ref[0]], axis=0).astype(F32)
    vroll = pltpu.roll(vall, HEAD_DIM, axis=1)
    zero = jnp.zeros_like(kn)
    one = jnp.ones_like(vall)
    k_var = [[jnp.where(lo, kn, zero).astype(BF16), jnp.where(lo, zero, kroll).astype(BF16)],
             [jnp.where(lo, kroll, zero).astype(BF16), jnp.where(lo, zero, kn).astype(BF16)]]
    v_var = [[jnp.where(lo, vall, one).astype(BF16), jnp.where(lo, one, vroll).astype(BF16)],
             [jnp.where(lo, vroll, one).astype(BF16), jnp.where(lo, one, vall).astype(BF16)]]

    qn = []
    for i in range(ATTN_WIDTH // LANES):
        t = _pair_norm(q_ref[0, :, i * LANES:(i + 1) * LANES].astype(F32), qg_ref[...], head_mean)
        qn.append((t * (HEAD_DIM ** -0.5)).astype(BF16))

    qi = lax.broadcasted_iota(jnp.int32, (BLOCK, BLOCK), 0)
    si = lax.broadcasted_iota(jnp.int32, (BLOCK, BLOCK), 1)
    is_cur = si <= qi
    dist = jnp.where(is_cur, qi - si, qi - si + BLOCK).astype(F32)
    alibi = [(2.0 ** (-8.0 * (h + 1) / N_HEADS)) * dist for h in range(N_HEADS)]
    sink_tile = [jnp.full((BLOCK, LANES), sink_ref[h], F32) for h in range(N_HEADS)]

    units = [(b, h) for b in range(ATTN_QB) for h in range(N_HEADS)]
    for u0 in range(0, len(units), ATTN_STACK):
        chunk = units[u0:u0 + ATTN_STACK]
        s_all = []
        for b, h in chunk:
            i, slot = divmod(h, HEADS_PER_TILE)
            kvh = h // (N_HEADS // N_KV_HEADS)
            win = slice(b * BLOCK, (b + 2) * BLOCK)
            sc = lax.dot_general(qn[i][b * BLOCK:(b + 1) * BLOCK], k_var[kvh][slot][win],
                                 (((1,), (1,)), ((), ())), preferred_element_type=F32)
            prev = sc[:, :BLOCK]
            if b == 0:
                prev = jnp.where(n > 0, prev, NEG)
            s_all.append(jnp.where(is_cur, sc[:, BLOCK:], prev) - alibi[h])
        s = jnp.concatenate(s_all, axis=0)
        sink_rows = jnp.concatenate([sink_tile[h] for _, h in chunk], axis=0)
        m = jnp.maximum(jnp.max(s, axis=-1, keepdims=True), sink_rows)
        p = jnp.exp(s - m).astype(BF16)
        esink = jnp.exp(sink_rows - m)
        for k in range(0, len(chunk), HEADS_PER_TILE):
            b, h_even = chunk[k]
            i = h_even // HEADS_PER_TILE
            win = slice(b * BLOCK, (b + 2) * BLOCK)
            halves = []
            for slot in range(HEADS_PER_TILE):
                kvh = (h_even + slot) // (N_HEADS // N_KV_HEADS)
                hr = slice((k + slot) * BLOCK, (k + slot + 1) * BLOCK)
                pcat = jnp.concatenate([jnp.where(is_cur, 0.0, p[hr]).astype(BF16),
                                        jnp.where(is_cur, p[hr], 0.0).astype(BF16)], axis=1)
                pv = jnp.dot(pcat, v_var[kvh][slot][win], preferred_element_type=F32)
                halves.append(pv / (pltpu.roll(pv, HEAD_DIM, axis=1) + esink[hr]))
            o_ref[0, b * BLOCK:(b + 1) * BLOCK, i * LANES:(i + 1) * LANES] = jnp.where(
                lo, halves[0], halves[1]).astype(BF16)


def _attn(q3, k3, v3, q_gain, k_gain, sinks):
    bsz, seq, _ = q3.shape
    qrows = ATTN_QB * BLOCK
    cur = lambda b, n: (b, n, 0)
    prev = lambda b, n: (b, jnp.maximum(n * ATTN_QB - 1, 0), 0)
    qg2 = jnp.tile(q_gain.astype(F32), HEADS_PER_TILE).reshape(1, LANES)
    kg2 = jnp.tile(k_gain.astype(F32), HEADS_PER_TILE).reshape(1, LANES)
    gain_spec = pl.BlockSpec((1, LANES), lambda b, n: (0, 0))
    return pl.pallas_call(
        _attn_body,
        grid=(bsz, seq // qrows),
        in_specs=[pl.BlockSpec(memory_space=pltpu.SMEM),
                  pl.BlockSpec((1, qrows, ATTN_WIDTH), cur),
                  pl.BlockSpec((1, qrows, KV_WIDTH), cur),
                  pl.BlockSpec((1, BLOCK, KV_WIDTH), prev),
                  pl.BlockSpec((1, qrows, KV_WIDTH), cur),
                  pl.BlockSpec((1, BLOCK, KV_WIDTH), prev),
                  gain_spec, gain_spec],
        out_specs=pl.BlockSpec((1, qrows, ATTN_WIDTH), cur),
        out_shape=jax.ShapeDtypeStruct((bsz, seq, ATTN_WIDTH), BF16),
        compiler_params=_cparams(("arbitrary", "arbitrary")),
        name="attn",
    )(sinks.astype(F32), q3, k3, k3, v3, v3, qg2, kg2)


MIX_TM = 1024


def _mix_body(z_ref, ya_ref, gs_ref, ga_ref, x_ref, wglu_ref, wup_ref, wout_ref, g2_ref, wr_ref,
              x1_ref, gate_ref, hs_hbm, starts_ref, lens_ref, counts_ref,
              stage_ref, run_ref, pend_ref, sem):
    t = pl.program_id(0)
    sorter = _GroupSorter(t, MIX_TM // SORT_TM, hs_hbm, starts_ref, lens_ref, counts_ref,
                          stage_ref, run_ref, pend_ref, sem)

    z = z_ref[...]
    bv = jnp.dot(z, wglu_ref[:, :D_MODEL], preferred_element_type=F32)
    bg = jnp.dot(z, wglu_ref[:, D_MODEL:], preferred_element_type=F32)
    bs = bv * jax.nn.sigmoid(bg)
    ba = jnp.dot(ya_ref[...], wup_ref[...], preferred_element_type=F32)
    merged = gs_ref[...].astype(F32) * bs + ga_ref[...].astype(F32) * ba
    x1 = x_ref[...] + jnp.dot(merged.astype(BF16), wout_ref[...], preferred_element_type=F32)
    x1_ref[...] = x1

    ms = jnp.mean(x1 * x1, axis=-1, keepdims=True)
    h2 = x1 * lax.rsqrt(ms + EPS) * g2_ref[...]
    h2b = h2.astype(BF16)

    lt = lax.dot_general(wr_ref[...], h2b, (((1,), (1,)), ((), ())),
                         preferred_element_type=F32)
    logits = lt[:ROUTER_ROWS] + lt[ROUTER_ROWS:]
    row = lax.broadcasted_iota(jnp.int32, logits.shape, 0)
    ninf = -jnp.inf
    cm = jnp.where(row < N_GROUPS, logits, ninf)
    cmax = jnp.max(cm, axis=0, keepdims=True)
    g_prob = 1.0 / jnp.sum(jnp.exp(cm - cmax), axis=0, keepdims=True)
    g_idx = jnp.min(jnp.where(cm == cmax, row, ROUTER_ROWS), axis=0, keepdims=True)
    f0 = FINE_OFF + EXPERTS_PER_GROUP * g_idx
    fm = jnp.where((row >= f0) & (row < f0 + EXPERTS_PER_GROUP), logits, ninf)
    v1 = jnp.max(fm, axis=0, keepdims=True)
    i1 = jnp.min(jnp.where(fm == v1, row, ROUTER_ROWS), axis=0, keepdims=True)
    fm2 = jnp.where(row == i1, ninf, fm)
    v2 = jnp.max(fm2, axis=0, keepdims=True)
    i2 = jnp.min(jnp.where(fm2 == v2, row, ROUTER_ROWS), axis=0, keepdims=True)
    e21 = jnp.exp(v2 - v1)
    w1 = g_prob / (1.0 + e21)
    w2 = w1 * e21
    rt = jnp.where(row == i1 - f0, w1,
                   jnp.where(row == i2 - f0, w2,
                             jnp.where(row == ROUTE_GROUP_OFF + g_idx, 1.0, 0.0)))
    rt = jnp.concatenate([rt, jnp.zeros((ROUTER_LANES - ROUTER_ROWS, rt.shape[1]), F32)], axis=0)
    route = rt.T
    gate_ref[...] = route
    sorter.begin()
    sorter.issue(sorter.stage(h2b, route))

    @pl.when(t == pl.num_programs(0) - 1)
    def _():
        sorter.finish()


def _mix(z2, ya2, gs2, ga2, x2, wglu, wup, wout, g2, wr, layer):
    n = x2.shape[0]
    n_sort = n // SORT_TM
    subs = MIX_TM // SORT_TM
    cap = _region_cap(n)
    assert subs * SORT_TM >= ZFILL_TOK
    row = lambda c: pl.BlockSpec((MIX_TM, c), lambda i: (i, 0))
    full = lambda a, b: pl.BlockSpec((a, b), lambda i: (0, 0))
    smem = pl.BlockSpec(memory_space=pltpu.SMEM)
    return pl.pallas_call(
        _mix_body,
        grid=(n // MIX_TM,),
        in_specs=[row(SSM_WIDTH), row(ATTN_WIDTH), row(D_MODEL), row(D_MODEL), row(D_MODEL),
                  _layer_spec(layer, SSM_WIDTH, 2 * D_MODEL), _layer_spec(layer, ATTN_WIDTH, D_MODEL),
                  _layer_spec(layer, D_MODEL, D_MODEL), full(1, D_MODEL),
                  _layer_spec(layer, 2 * ROUTER_ROWS, D_MODEL)],
        out_specs=[row(D_MODEL), row(ROUTER_LANES), pl.BlockSpec(memory_space=pl.ANY),
                   smem, smem, smem],
        out_shape=[jax.ShapeDtypeStruct((n, D_MODEL), F32),
                   jax.ShapeDtypeStruct((n, ROUTER_LANES), F32),
                   jax.ShapeDtypeStruct((N_GROUPS * cap * TOK_ROWS, LANES), jnp.uint32),
                   jax.ShapeDtypeStruct((n_sort * N_GROUPS,), jnp.int32),
                   jax.ShapeDtypeStruct((n_sort * N_GROUPS,), jnp.int32),
                   jax.ShapeDtypeStruct((N_GROUPS,), jnp.int32)],
        scratch_shapes=[pltpu.VMEM((subs * SORT_TM * TOK_ROWS, LANES), jnp.uint32),
                        pltpu.SMEM((N_GROUPS,), jnp.int32),
                        pltpu.SMEM((subs * N_GROUPS,), jnp.int32),
                        pltpu.SemaphoreType.DMA((subs, N_GROUPS))],
        compiler_params=_cparams(("arbitrary",)),
        name="mix",
    )(z2, ya2, gs2, ga2, x2, wglu, wup, wout, g2, wr)


SORT_TM = 256
COMBINE_TM = 1024
MOE_TM = 512
TOK_ROWS = 8
H_WORDS = D_MODEL // 2
H_CHUNKS = H_WORDS // LANES
ROUTE_ROW = H_CHUNKS
ZFILL_TOK = MOE_TM
HI_MASK = 0xFFFF0000


def _region_cap(n):
    cap = n + ZFILL_TOK
    return -(-cap // MOE_TM) * MOE_TM


def _tile_positions(route):
    tm = route.shape[0]
    lane = lax.broadcasted_iota(jnp.int32, route.shape, 1)
    onehot = jnp.where((lane >= ROUTE_GROUP_OFF) & (lane < ROUTE_GROUP_OFF + N_GROUPS), route, 0.0)
    ci = lax.broadcasted_iota(jnp.int32, (tm, tm), 0)
    cj = lax.broadcasted_iota(jnp.int32, (tm, tm), 1)
    earlier = (cj < ci).astype(BF16)
    rank = jnp.dot(earlier, onehot.astype(BF16), preferred_element_type=F32)
    cnt = jnp.sum(onehot, axis=0, keepdims=True)
    li = lax.broadcasted_iota(jnp.int32, (LANES, LANES), 0)
    lj = lax.broadcasted_iota(jnp.int32, (LANES, LANES), 1)
    base = jnp.dot(jnp.broadcast_to(cnt, (8, LANES)).astype(BF16), (li < lj).astype(BF16),
                   preferred_element_type=F32)[0:1]
    return onehot, rank, base, cnt


class _GroupSorter:
    def __init__(self, t, subs, hs_hbm, starts_ref, lens_ref, counts_ref, stage_ref, run_ref,
                 pend_ref, sem):
        self.t, self.subs = t, subs
        self.hs_hbm, self.starts_ref, self.lens_ref, self.counts_ref = (
            hs_hbm, starts_ref, lens_ref, counts_ref)
        self.stage_ref, self.run_ref, self.pend_ref, self.sem = stage_ref, run_ref, pend_ref, sem
        self.slot_rows = SORT_TM * TOK_ROWS
        self.cap = hs_hbm.shape[0] // (N_GROUPS * TOK_ROWS)
        self.set0 = 0

    def _copy(self, src_row, dst_row, slot, g, tokens):
        return pltpu.make_async_copy(
            self.stage_ref.at[pl.ds(src_row, tokens * TOK_ROWS), :],
            self.hs_hbm.at[pl.ds(dst_row, tokens * TOK_ROWS), :], self.sem.at[slot, g])

    def _wait_set(self, set0):
        for sl in range(self.subs):
            for g in range(N_GROUPS):
                n = self.pend_ref[(set0 + sl) * N_GROUPS + g]

                @pl.when(n > 0)
                def _(sl=sl, g=g, n=n):
                    self._copy(0, 0, set0 + sl, g, n).wait()

    def begin(self):
        @pl.when(self.t == 0)
        def _():
            self.stage_ref[...] = jnp.zeros_like(self.stage_ref)
            for g in range(N_GROUPS):
                self.run_ref[g] = 0

        @pl.when(self.t > 0)
        def _():
            self._wait_set(self.set0)

    def stage(self, h, route):
        tm = SORT_TM
        runs = []
        for sl in range(self.subs):
            r = route[sl * tm:(sl + 1) * tm]
            onehot, rank, base, cnt = _tile_positions(r)
            w = (onehot * (rank + base)).astype(BF16)
            pos_row = lax.dot_general(jnp.ones((8, LANES), BF16), w, (((1,), (1,)), ((), ())),
                                      preferred_element_type=F32)[0:1]
            ri = lax.broadcasted_iota(jnp.int32, (tm, tm), 0).astype(F32)
            perm = (ri == pos_row).astype(BF16)
            r_hi = r.astype(BF16)
            r_mid = (r - r_hi.astype(F32)).astype(BF16)
            r_lo = (r - r_hi.astype(F32) - r_mid.astype(F32)).astype(BF16)
            sorted_all = jnp.dot(
                perm,
                jnp.concatenate([h[sl * tm:(sl + 1) * tm], r_hi, r_mid, r_lo], axis=1),
                preferred_element_type=F32)
            sh = sorted_all[:, :D_MODEL]
            sr = (sorted_all[:, D_MODEL:D_MODEL + LANES]
                  + sorted_all[:, D_MODEL + LANES:D_MODEL + 2 * LANES]
                  + sorted_all[:, D_MODEL + 2 * LANES:])
            hb = lax.bitcast_convert_type(sh, jnp.uint32)
            words = (hb[:, :H_WORDS] & jnp.uint32(HI_MASK)) | (hb[:, H_WORDS:] >> 16)

            slot0 = (self.set0 + sl) * self.slot_rows
            for s in range(H_CHUNKS):
                self.stage_ref[pl.ds(slot0 + s, tm, stride=TOK_ROWS), :] = (
                    words[:, s * LANES:(s + 1) * LANES])
            self.stage_ref[pl.ds(slot0 + ROUTE_ROW, tm, stride=TOK_ROWS), :] = (
                lax.bitcast_convert_type(sr, jnp.uint32))
            for g in range(N_GROUPS):
                runs.append((sl, g, slot0,
                             cnt[0, ROUTE_GROUP_OFF + g].astype(jnp.int32),
                             base[0, ROUTE_GROUP_OFF + g].astype(jnp.int32)))
        return runs

    def issue(self, runs):
        first_tile = self.t * self.subs
        for sl, g, slot0, c_g, b_g in runs:
            start = self.run_ref[g]
            self.starts_ref[(first_tile + sl) * N_GROUPS + g] = start
            self.lens_ref[(first_tile + sl) * N_GROUPS + g] = c_g
            self.pend_ref[(self.set0 + sl) * N_GROUPS + g] = c_g
            self.run_ref[g] = start + c_g
            src = pl.multiple_of(slot0 + b_g * TOK_ROWS, TOK_ROWS)
            dst = pl.multiple_of((g * self.cap + start) * TOK_ROWS, TOK_ROWS)

            @pl.when(c_g > 0)
            def _(sl=sl, g=g, src=src, dst=dst, c_g=c_g):
                self._copy(src, dst, self.set0 + sl, g, c_g).start()

    def finish(self):
        self._wait_set(self.set0)
        self.stage_ref[...] = jnp.zeros_like(self.stage_ref)
        zrows = ZFILL_TOK * TOK_ROWS
        for g in range(N_GROUPS):
            total = self.run_ref[g]
            self.counts_ref[g] = total
            dst = pl.multiple_of((g * self.cap + total) * TOK_ROWS, TOK_ROWS)
            pltpu.make_async_copy(self.stage_ref.at[pl.ds(0, zrows), :],
                                  self.hs_hbm.at[pl.ds(dst, zrows), :], self.sem.at[0, g]).start()
        for g in range(N_GROUPS):
            pltpu.make_async_copy(self.stage_ref.at[pl.ds(0, zrows), :],
                                  self.hs_hbm.at[pl.ds(0, zrows), :], self.sem.at[0, g]).wait()


def _experts_body(blk_ref, grp_ref, valid_ref, hs_ref, wu32_ref, wd32_ref, ys_ref, wu_ref, wd_ref):
    t = pl.program_id(0)
    tm = MOE_TM

    @pl.when((t == 0) | (grp_ref[t] != grp_ref[jnp.maximum(t - 1, 0)]))
    def _():
        wu_ref[...] = wu32_ref[...].astype(BF16)
        wd_ref[...] = wd32_ref[...].astype(BF16)

    @pl.when(valid_ref[t] > 0)
    def _():
        chunks = [hs_ref[pl.ds(s, tm, stride=TOK_ROWS), :] for s in range(H_CHUNKS + 1)]
        hi = [lax.bitcast_convert_type(c & jnp.uint32(HI_MASK), F32) for c in chunks[:H_CHUNKS]]
        lo = [lax.bitcast_convert_type(c << 16, F32) for c in chunks[:H_CHUNKS]]
        h = jnp.concatenate(hi + lo, axis=1).astype(BF16)
        route = lax.bitcast_convert_type(chunks[ROUTE_ROW], F32)
        acts = []
        for e in range(EXPERTS_PER_GROUP):
            gu = jnp.dot(h, wu_ref[e], preferred_element_type=F32)
            acts.append((jax.nn.silu(gu[:, :D_EXPERT]) * gu[:, D_EXPERT:]
                         * route[:, e:e + 1]).astype(BF16))
        act = jnp.concatenate(acts, axis=1)
        wd = wd_ref[...].reshape(EXPERTS_PER_GROUP * D_EXPERT, D_MODEL)
        out = jnp.dot(act, wd, preferred_element_type=F32)
        for j in range(D_MODEL // LANES):
            ys_ref[pl.ds(j, tm, stride=TOK_ROWS), :] = out[:, j * LANES:(j + 1) * LANES]


def _experts(blk, grp, valid, hs, wu, wd, layer):
    g0 = layer * N_GROUPS
    n_tiles = blk.shape[0]
    epg = EXPERTS_PER_GROUP
    return pl.pallas_call(
        _experts_body,
        grid_spec=pltpu.PrefetchScalarGridSpec(
            num_scalar_prefetch=3,
            grid=(n_tiles,),
            in_specs=[pl.BlockSpec((MOE_TM * TOK_ROWS, LANES), lambda t, b, g, v: (b[t], 0)),
                      pl.BlockSpec((epg, D_MODEL, 2 * D_EXPERT),
                                   lambda t, b, g, v: (g0 + g[t], 0, 0)),
                      pl.BlockSpec((epg, D_EXPERT, D_MODEL),
                                   lambda t, b, g, v: (g0 + g[t], 0, 0))],
            out_specs=pl.BlockSpec((MOE_TM * TOK_ROWS, LANES), lambda t, b, g, v: (b[t], 0)),
            scratch_shapes=[pltpu.VMEM((epg, D_MODEL, 2 * D_EXPERT), BF16),
                            pltpu.VMEM((epg, D_EXPERT, D_MODEL), BF16)],
        ),
        out_shape=jax.ShapeDtypeStruct(hs.shape, F32),
        compiler_params=_cparams(("arbitrary",)),
        name="moe_experts",
    )(blk, grp, valid, hs, wu, wd)


def _moe_outputs(starts_ref, lens_ref, route_ref, ys_hbm, buf_ref, sem):
    t = pl.program_id(0)
    nt = pl.num_programs(0)
    tm = SORT_TM
    subs = route_ref.shape[0] // tm
    slot_rows = tm * TOK_ROWS
    cap = ys_hbm.shape[0] // (N_GROUPS * TOK_ROWS)

    def fetches(step, sl):
        tile = step * subs + sl
        slot = (step % 2) * subs + sl
        copies = []
        first = 0
        for g in range(N_GROUPS):
            tokens = lens_ref[tile * N_GROUPS + g]
            src = pl.multiple_of((g * cap + starts_ref[tile * N_GROUPS + g]) * TOK_ROWS, TOK_ROWS)
            dst = pl.multiple_of(slot * slot_rows + first * TOK_ROWS, TOK_ROWS)
            copies.append((tokens, pltpu.make_async_copy(
                ys_hbm.at[pl.ds(src, tokens * TOK_ROWS), :],
                buf_ref.at[pl.ds(dst, tokens * TOK_ROWS), :], sem.at[slot, g])))
            first = first + tokens
        return copies

    def start_fetches(step):
        for sl in range(subs):
            for tokens, copy in fetches(step, sl):
                @pl.when(tokens > 0)
                def _(copy=copy):
                    copy.start()

    @pl.when(t == 0)
    def _():
        start_fetches(0)

    @pl.when(t + 1 < nt)
    def _():
        start_fetches(t + 1)

    for sl in range(subs):
        for tokens, copy in fetches(t, sl):
            @pl.when(tokens > 0)
            def _(copy=copy):
                copy.wait()

    outs = []
    for sl in range(subs):
        onehot, rank, base, _ = _tile_positions(route_ref[sl * tm:(sl + 1) * tm, :])
        pos = jnp.sum(onehot * (rank + base), axis=1, keepdims=True)
        ri = lax.broadcasted_iota(jnp.int32, (tm, tm), 1).astype(F32)
        sel = (ri == pos).astype(BF16)
        r0 = pl.multiple_of(((t % 2) * subs + sl) * slot_rows, TOK_ROWS)
        y = jnp.concatenate(
            [buf_ref[pl.ds(r0 + j, tm, stride=TOK_ROWS), :] for j in range(D_MODEL // LANES)],
            axis=1).astype(BF16)
        outs.append(jnp.dot(sel, y, preferred_element_type=F32))
    return outs


def _combine_body(starts_ref, lens_ref, x1_ref, route_ref, ys_hbm, o_ref, buf_ref, sem):
    outs = _moe_outputs(starts_ref, lens_ref, route_ref, ys_hbm, buf_ref, sem)
    for sl, y in enumerate(outs):
        rows = slice(sl * SORT_TM, (sl + 1) * SORT_TM)
        o_ref[rows, :] = x1_ref[rows, :] + y


def _combine(starts, lens, x1, route, ys):
    n = x1.shape[0]
    subs = COMBINE_TM // SORT_TM
    return pl.pallas_call(
        _combine_body,
        grid_spec=pltpu.PrefetchScalarGridSpec(
            num_scalar_prefetch=2,
            grid=(n // COMBINE_TM,),
            in_specs=[pl.BlockSpec((COMBINE_TM, D_MODEL), lambda i, s, c: (i, 0)),
                      pl.BlockSpec((COMBINE_TM, ROUTER_LANES), lambda i, s, c: (i, 0)),
                      pl.BlockSpec(memory_space=pl.ANY)],
            out_specs=pl.BlockSpec((COMBINE_TM, D_MODEL), lambda i, s, c: (i, 0)),
            scratch_shapes=[pltpu.VMEM((2 * COMBINE_TM * TOK_ROWS, LANES), F32),
                            pltpu.SemaphoreType.DMA((2 * subs, N_GROUPS))],
        ),
        out_shape=jax.ShapeDtypeStruct((n, D_MODEL), F32),
        compiler_params=_cparams(("arbitrary",)),
        name="moe_combine",
    )(starts, lens, x1, route, ys)


def _expert_tiles(counts, n):
    cap_blocks = _region_cap(n) // MOE_TM
    n_tiles = n // MOE_TM + N_GROUPS
    per_group = (counts + MOE_TM - 1) // MOE_TM
    ends = jnp.cumsum(per_group)
    total = ends[-1]
    t = jnp.minimum(jnp.arange(n_tiles, dtype=jnp.int32), total - 1)
    grp = jnp.sum((t[:, None] >= ends[None, :]).astype(jnp.int32), axis=1)
    first = ends - per_group
    blk = grp * cap_blocks + (t - first[grp])
    valid = (jnp.arange(n_tiles, dtype=jnp.int32) < total).astype(jnp.int32)
    return blk.astype(jnp.int32), grp.astype(jnp.int32), valid


def _moe_sorted(n, hs, counts, w_up, w_down, layer):
    blk, grp, valid = _expert_tiles(counts, n)
    return _experts(blk, grp, valid, hs, w_up, w_down, layer)


def kernel(x, norm_mix, w_in, ssm_lam_re, ssm_lam_im, ssm_log_dt, ssm_b_re, ssm_b_im, ssm_c_re,
           ssm_c_im, ssm_d, w_ssm_glu_val, w_ssm_glu_gate, q_norm, k_norm, attn_sinks, w_attn_up,
           w_out, norm_ffn, w_coarse, w_fine, w_expert_up, w_expert_down):
    bsz, seq, d = x.shape
    n = bsz * seq
    depth = w_in.shape[0]
    x2 = x.reshape(n, d)
    s5p = jax.vmap(_s5_params)(ssm_lam_re, ssm_lam_im, ssm_log_dt, ssm_b_re, ssm_b_im, ssm_c_re,
                               ssm_c_im, ssm_d)
    w_in_b = w_in.astype(BF16)
    wglu = jnp.concatenate([w_ssm_glu_val, w_ssm_glu_gate], axis=2).astype(BF16)
    wup_b = w_attn_up.astype(BF16)
    wout_b = w_out.astype(BF16)
    wr_t = jnp.concatenate(
        [jnp.swapaxes(w_coarse, 1, 2), jnp.swapaxes(w_fine, 1, 2),
         jnp.zeros((depth, ROUTER_ROWS - N_GROUPS - N_EXPERTS, d), w_fine.dtype)], axis=1).astype(F32)
    wr_hi = wr_t.astype(BF16)
    wr = jnp.concatenate([wr_hi, (wr_t - wr_hi.astype(F32)).astype(BF16)], axis=1)
    wu_all = w_expert_up.reshape((depth * N_EXPERTS,) + w_expert_up.shape[2:])
    wd_all = w_expert_down.reshape((depth * N_EXPERTS,) + w_expert_down.shape[2:])
    pending = None
    for i in range(depth):
        if pending is None:
            u, q, k, v, gs, ga = _proj(x2, norm_mix[i].reshape(1, d), w_in_b, i)
        else:
            x2, u, q, k, v, gs, ga = _proj_moe(*pending, norm_mix[i].reshape(1, d), w_in_b, i)
        z = _s5(u.reshape(bsz, seq, SSM_WIDTH), *(p[i] for p in s5p))
        ya = _attn(q.reshape(bsz, seq, ATTN_WIDTH), k.reshape(bsz, seq, KV_WIDTH),
                   v.reshape(bsz, seq, KV_WIDTH), q_norm[i], k_norm[i], attn_sinks[i])
        x1, route, hs, starts, lens, counts = _mix(
            z.reshape(n, SSM_WIDTH), ya.reshape(n, ATTN_WIDTH), gs, ga, x2,
            wglu, wup_b, wout_b, norm_ffn[i].reshape(1, d), wr, i)
        ys = _moe_sorted(n, hs, counts, wu_all, wd_all, i)
        pending = (starts, lens, x1, route, ys)
    return _combine(*pending).reshape(bsz, seq, d)
```
